```python
import math
import jax
import jax.numpy as jnp
from jax import lax
import numpy as np

D_MODEL = 2048
BATCH = 2
SEQ = 4096
DEPTH = 2
DEC_BATCH = 8
DEC_SEQ = 1
PAST_LEN = 16384
PAGE_SIZE = 128

MEM_LEN = 256
N_MEM_HEADS = 4
MEM_HEAD_DIM = 128
MIX_W = D_MODEL
MEM_W = N_MEM_HEADS * MEM_HEAD_DIM
TOK_W = MIX_W - MEM_W
D_FF = 5632
NORM_EPS = 1e-6
GLA_HEADS = 4
GLA_DV = TOK_W // GLA_HEADS
GLA_DK = GLA_DV // 2
GLA_GATE_RANK = 16
GLA_TAU = 16.0
GLA_CHUNK = 64
NSA_HEAD_DIM = 128
NSA_HEADS = TOK_W // NSA_HEAD_DIM
NSA_GROUPS = 3
NSA_REP = NSA_HEADS // NSA_GROUPS
NSA_BLOCK = 64
NSA_TOPN = 16
NSA_WINDOW = 512
NSA_CMP_HIDDEN = 256
NSA_QBLK = 64
NSA_KV_W = NSA_GROUPS * NSA_HEAD_DIM
REL_BUCKETS = 32
REL_MAX_EXACT = 16
REL_MAX_DIST = 128
N_GLA = (DEPTH + 1) // 2
N_NSA = DEPTH // 2
GLA_COLS = 2 * GLA_HEADS * GLA_DK + 2 * TOK_W + GLA_GATE_RANK + MEM_W
NSA_COLS = TOK_W + 6 * NSA_KV_W + 3 * NSA_HEADS + MEM_W

kernel_name = 'hybrid_gla_nsa_decoder_step'


def rmsnorm(x, g):
    xf = x.astype(jnp.float32)
    y = xf * lax.rsqrt(jnp.mean(xf * xf, axis=-1, keepdims=True) + NORM_EPS)
    return (y * g.astype(jnp.float32)).astype(x.dtype)


def swiglu(x, w_gate, w_up, w_down):
    return (jax.nn.silu(x @ w_gate) * (x @ w_up)) @ w_down


def masked_softmax(s, valid):
    s = jnp.where(valid, s.astype(jnp.float32), -jnp.inf)
    m = jnp.max(s, axis=-1, keepdims=True)
    m = jnp.where(jnp.isfinite(m), m, 0.0)
    e = jnp.exp(s - m)
    return e / jnp.maximum(jnp.sum(e, axis=-1, keepdims=True), 1e-30)


def t5_bucket(dist):
    n = jnp.maximum(dist, 0)
    nf = jnp.maximum(n, REL_MAX_EXACT).astype(jnp.float32)
    large = REL_MAX_EXACT + (jnp.log(nf / REL_MAX_EXACT) / math.log(REL_MAX_DIST / REL_MAX_EXACT)
                             * (REL_BUCKETS - REL_MAX_EXACT)).astype(jnp.int32)
    return jnp.where(n < REL_MAX_EXACT, n, jnp.minimum(large, REL_BUCKETS - 1))


def mem_attend(qm, mem_kv):
    s = jnp.einsum('blhd,bmhd->bhlm', qm, mem_kv[:, :, 0]) * (MEM_HEAD_DIM ** -0.5)
    p = jax.nn.softmax(s.astype(jnp.float32), axis=-1).astype(mem_kv.dtype)
    o = jnp.einsum('bhlm,bmhd->blhd', p, mem_kv[:, :, 1])
    return o.reshape(qm.shape[0], qm.shape[1], MEM_W)


def gla_recurrence(q, k, v, log_a, s0):
    B, L, H = q.shape[:3]
    c = min(GLA_CHUNK, L)
    n = -(-L // c)
    pad = n * c - L

    def prep(a):
        a = jnp.pad(a.astype(jnp.float32), ((0, 0), (0, pad), (0, 0), (0, 0)))
        return jnp.moveaxis(a.reshape(B, n, c, a.shape[2], a.shape[3]), 1, 0)

    causal = jnp.tril(jnp.ones((c, c), dtype=bool))

    def step(S, xs):
        qc, kc, vc, ac = xs
        b = jnp.cumsum(ac, axis=1)
        qe = qc * jnp.exp(b)
        ke = kc * jnp.exp(-b)
        o = jnp.einsum('bchk,bhkv->bchv', qe, S)
        A = jnp.where(causal, jnp.einsum('bchk,bshk->bhcs', qe, ke), 0.0)
        o = o + jnp.einsum('bhcs,bshv->bchv', A, vc)
        bl = b[:, -1]
        S = jnp.exp(bl)[..., None] * S + jnp.einsum('bshk,bshv->bhkv', kc * jnp.exp(bl[:, None] - b), vc)
        return S, o

    S, o = lax.scan(step, s0.astype(jnp.float32), (prep(q), prep(k), prep(v), prep(log_a)))
    o = jnp.moveaxis(o, 0, 1).reshape(B, n * c, H, v.shape[3])[:, :L]
    return o, S


def gla_mixer(tp, s0, w_a2, b_a, onorm_g):
    B, L, _ = tp.shape
    dkw = GLA_HEADS * GLA_DK
    q = tp[..., :dkw].reshape(B, L, GLA_HEADS, GLA_DK) * (GLA_DK ** -0.5)
    k = tp[..., dkw:2 * dkw].reshape(B, L, GLA_HEADS, GLA_DK)
    v = tp[..., 2 * dkw:2 * dkw + TOK_W].reshape(B, L, GLA_HEADS, GLA_DV)
    r = tp[..., 2 * dkw + TOK_W:2 * dkw + 2 * TOK_W]
    a = tp[..., 2 * dkw + 2 * TOK_W:]
    log_a = (jax.nn.log_sigmoid((a @ w_a2 + b_a).astype(jnp.float32)) / GLA_TAU).reshape(B, L, GLA_HEADS, GLA_DK)
    o, s_new = gla_recurrence(q, k, v, log_a, s0)
    o = rmsnorm(o.astype(tp.dtype), onorm_g).reshape(B, L, TOK_W)
    return o * jax.nn.silu(r), s_new.astype(tp.dtype)


def compress_blocks(rows, pe, w1, b1, w2):
    B, Tp, G, dh = rows.shape
    nb = Tp // NSA_BLOCK
    blk = rows.reshape(B, nb, NSA_BLOCK, G, dh) + pe[:, None, :]
    blk = jnp.swapaxes(blk, 2, 3).reshape(B, nb, G, NSA_BLOCK * dh)
    return jax.nn.silu(blk @ w1 + b1) @ w2


def nsa_core(q, gates, cmp_all, slc_all, win_all, off, rel_bias, cmp_pe, cmp_w1, cmp_b1, cmp_w2):
    B, Lq = q.shape[:2]
    G, R, dh = NSA_GROUPS, NSA_REP, NSA_HEAD_DIM
    T = off + Lq
    nb = -(-T // NSA_BLOCK)
    Tp = nb * NSA_BLOCK

    def pad_t(a):
        return jnp.pad(a, ((0, 0), (0, Tp - T), (0, 0), (0, 0)))

    k_cmp = compress_blocks(pad_t(cmp_all[:, :, 0]), cmp_pe[0], cmp_w1[0], cmp_b1[0], cmp_w2[0])
    v_cmp = compress_blocks(pad_t(cmp_all[:, :, 1]), cmp_pe[1], cmp_w1[1], cmp_b1[1], cmp_w2[1])
    k_slc = jnp.swapaxes(pad_t(slc_all[:, :, 0]), 1, 2)
    v_slc = jnp.swapaxes(pad_t(slc_all[:, :, 1]), 1, 2)
    k_win, v_win = win_all[:, :, 0], win_all[:, :, 1]
    blk_idx = jnp.arange(nb)
    blk_end = (blk_idx + 1) * NSA_BLOCK - 1
    n_sel = min(NSA_TOPN, nb)
    n_keys_sel = n_sel * NSA_BLOCK
    qb = math.gcd(Lq, NSA_QBLK)
    nq = Lq // qb
    n_keys_win = NSA_WINDOW + qb
    bias_grp = jnp.swapaxes(rel_bias.reshape(REL_BUCKETS, G, R), 0, 1)
    q_blocks = jnp.moveaxis((q * (dh ** -0.5)).reshape(B, nq, qb, G, R, dh), 1, 0)
    g_blocks = jnp.moveaxis(gates.reshape(B, nq, qb, G, R, 3), 1, 0)
    starts = jnp.arange(nq, dtype=jnp.int32) * qb
    b_ix = jnp.arange(B)[:, None, None]
    g_ix = jnp.arange(G)[None, :, None]

    def one_block(args):
        qg, gg, qs = args
        t = off + qs + jnp.arange(qb)
        dist_c = t[:, None] - blk_end[None, :]
        bias_c = jnp.transpose(rel_bias[t5_bucket(dist_c)].reshape(qb, nb, G, R), (0, 2, 3, 1))
        s_c = jnp.einsum('bqgrd,bngd->bqgrn', qg, k_cmp) + bias_c
        p_c = masked_softmax(s_c, (dist_c >= 0)[:, None, None, :])
        o_c = jnp.einsum('bqgrn,bngd->bqgrd', p_c.astype(v_cmp.dtype), v_cmp)
        imp = jnp.sum(p_c, axis=3)
        tb = (t // NSA_BLOCK)[:, None]
        forced = (blk_idx[None, :] == 0) | (blk_idx[None, :] == tb) | (blk_idx[None, :] == tb - 1)
        future = blk_idx[None, :] * NSA_BLOCK > t[:, None]
        score = jnp.where(forced[:, None, :], jnp.inf, jnp.where(future[:, None, :], -jnp.inf, imp))
        _, sel = lax.top_k(score, n_sel)
        tok = (sel[..., None] * NSA_BLOCK + jnp.arange(NSA_BLOCK)).reshape(B, qb, G, n_keys_sel)
        tok_bg = jnp.swapaxes(tok, 1, 2).reshape(B, G, qb * n_keys_sel)
        ks_g = k_slc[b_ix, g_ix, tok_bg].reshape(B, G, qb, n_keys_sel, dh)
        vs_g = v_slc[b_ix, g_ix, tok_bg].reshape(B, G, qb, n_keys_sel, dh)
        dist_s = t[None, :, None, None] - tok
        bias_s = jnp.moveaxis(bias_grp[jnp.arange(G)[None, None, :, None], t5_bucket(dist_s)], -1, 3)
        s_s = jnp.einsum('bqgrd,bgqld->bqgrl', qg, ks_g) + bias_s
        p_s = masked_softmax(s_s, (dist_s >= 0)[:, :, :, None, :])
        o_s = jnp.einsum('bqgrl,bgqld->bqgrd', p_s.astype(vs_g.dtype), vs_g)
        kw = lax.dynamic_slice_in_dim(k_win, qs, n_keys_win, axis=1)
        vw = lax.dynamic_slice_in_dim(v_win, qs, n_keys_win, axis=1)
        s_pos = off - NSA_WINDOW + qs + jnp.arange(n_keys_win)
        dist_w = t[:, None] - s_pos[None, :]
        valid_w = (dist_w >= 0) & (dist_w <= NSA_WINDOW) & (s_pos[None, :] >= 0)
        bias_w = jnp.transpose(rel_bias[t5_bucket(dist_w)].reshape(qb, n_keys_win, G, R), (0, 2, 3, 1))
        s_w = jnp.einsum('bqgrd,bkgd->bqgrk', qg, kw) + bias_w
        p_w = masked_softmax(s_w, valid_w[:, None, None, :])
        o_w = jnp.einsum('bqgrk,bkgd->bqgrd', p_w.astype(vw.dtype), vw)
        o = gg[..., 0:1] * o_c + gg[..., 1:2] * o_s + gg[..., 2:3] * o_w
        return o.reshape(B, qb, TOK_W)

    out = lax.map(one_block, (q_blocks, g_blocks, starts))
    return jnp.moveaxis(out, 0, 1).reshape(B, Lq, TOK_W)


def nsa_mixer(tp, past, off, win_buf, rel_bias, gate_b, cmp_pe, cmp_w1, cmp_b1, cmp_w2):
    B, L, _ = tp.shape
    q = tp[..., :TOK_W].reshape(B, L, NSA_HEADS, NSA_HEAD_DIM)
    kv = tp[..., TOK_W:TOK_W + 6 * NSA_KV_W].reshape(B, L, 6, NSA_GROUPS, NSA_HEAD_DIM)
    gates = jax.nn.sigmoid(tp[..., TOK_W + 6 * NSA_KV_W:] + gate_b).reshape(B, L, NSA_HEADS, 3)
    new_cmp, new_slc, new_win = kv[:, :, 0:2], kv[:, :, 2:4], kv[:, :, 4:6]
    if past is None:
        cmp_all, slc_all = new_cmp, new_slc
        win_all = jnp.pad(new_win, ((0, 0), (NSA_WINDOW, 0), (0, 0), (0, 0), (0, 0)))
    else:
        cmp_past, slc_past, win_past = past
        cmp_all = jnp.concatenate([cmp_past, new_cmp], axis=1)
        slc_all = jnp.concatenate([slc_past, new_slc], axis=1)
        win_past = jnp.pad(win_past, ((0, 0), (NSA_WINDOW - win_past.shape[1], 0), (0, 0), (0, 0), (0, 0)))
        win_all = jnp.concatenate([win_past, new_win], axis=1)
    out = nsa_core(q, gates, cmp_all, slc_all, win_all, off, rel_bias, cmp_pe, cmp_w1, cmp_b1, cmp_w2)
    return out, (new_cmp, new_slc, win_all[:, -win_buf:])


def setup_inputs(seed: int = 0) -> dict:
    key = jax.random.key(seed)
    ks = jax.random.split(key, 32)

    def nrm(k, shape, scale=1.0):
        return jax.random.normal(k, shape, jnp.float32) * scale

    n_pages = PAST_LEN // PAGE_SIZE
    n_used = DEC_BATCH * n_pages
    n_pool = (5 * n_used + 3) // 4
    win_buf = min(NSA_WINDOW, PAST_LEN)
    page_table = jax.random.permutation(ks[0], n_pool)[:n_used].reshape(DEC_BATCH, n_pages).astype(jnp.int32)
    kv_shape = (N_NSA, n_pool, PAGE_SIZE, 2, NSA_GROUPS, NSA_HEAD_DIM)
    return {
        'x_prompt': nrm(ks[1], (BATCH, SEQ, D_MODEL)),
        'x_sample': nrm(ks[2], (DEC_BATCH, DEC_SEQ, D_MODEL)),
        'mem_prompt': nrm(ks[3], (BATCH, MEM_LEN, D_MODEL)),
        'cache_cmp_kv': nrm(ks[4], kv_shape),
        'cache_slc_kv': nrm(ks[5], kv_shape),
        'state_win_kv': nrm(ks[6], (N_NSA, DEC_BATCH, win_buf, 2, NSA_GROUPS, NSA_HEAD_DIM)),
        'state_gla': nrm(ks[7], (N_GLA, DEC_BATCH, GLA_HEADS, GLA_DK, GLA_DV)),
        'cache_mem_kv': nrm(ks[8], (DEPTH, DEC_BATCH, MEM_LEN, 2, N_MEM_HEADS, MEM_HEAD_DIM)),
        'page_table': page_table,
        'norm_g': 1.0 + nrm(ks[9], (DEPTH, 6, D_MODEL), 0.02),
        'w_ffn_gate': nrm(ks[10], (DEPTH, 2, D_MODEL, D_FF), D_MODEL ** -0.5),
        'w_ffn_up': nrm(ks[11], (DEPTH, 2, D_MODEL, D_FF), D_MODEL ** -0.5),
        'w_ffn_down': nrm(ks[12], (DEPTH, 2, D_FF, D_MODEL), D_FF ** -0.5),
        'w_in_gla': nrm(ks[13], (N_GLA, D_MODEL, GLA_COLS), D_MODEL ** -0.5),
        'w_in_nsa': nrm(ks[14], (N_NSA, D_MODEL, NSA_COLS), D_MODEL ** -0.5),
        'w_out': nrm(ks[15], (DEPTH, MIX_W, D_MODEL), MIX_W ** -0.5),
        'mem_norm_g': 1.0 + nrm(ks[16], (DEPTH, D_MODEL), 0.02),
        'w_mem_kv': nrm(ks[17], (DEPTH, D_MODEL, 2 * MEM_W), D_MODEL ** -0.5),
        'w_gla_a2': nrm(ks[18], (N_GLA, GLA_GATE_RANK, GLA_HEADS * GLA_DK), GLA_GATE_RANK ** -0.5),
        'b_gla_a': nrm(ks[19], (N_GLA, GLA_HEADS * GLA_DK), 0.1),
        'gla_onorm_g': 1.0 + nrm(ks[20], (N_GLA, GLA_DV), 0.02),
        'nsa_gate_b': nrm(ks[21], (N_NSA, 3 * NSA_HEADS), 0.1),
        'cmp_pe': nrm(ks[22], (N_NSA, 2, NSA_BLOCK, NSA_HEAD_DIM), 0.02),
        'cmp_w1': nrm(ks[23], (N_NSA, 2, NSA_BLOCK * NSA_HEAD_DIM, NSA_CMP_HIDDEN), (NSA_BLOCK * NSA_HEAD_DIM) ** -0.5),
        'cmp_b1': nrm(ks[24], (N_NSA, 2, NSA_CMP_HIDDEN), 0.02),
        'cmp_w2': nrm(ks[25], (N_NSA, 2, NSA_CMP_HIDDEN, NSA_HEAD_DIM), NSA_CMP_HIDDEN ** -0.5),
        'rel_bias': nrm(ks[26], (REL_BUCKETS, NSA_HEADS), 0.2),
    }


def reference(x_prompt, x_sample, mem_prompt, cache_cmp_kv, cache_slc_kv, state_win_kv, state_gla, cache_mem_kv,
              page_table, norm_g, w_ffn_gate, w_ffn_up, w_ffn_down, w_in_gla, w_in_nsa, w_out, mem_norm_g, w_mem_kv,
              w_gla_a2, b_gla_a, gla_onorm_g, nsa_gate_b, cmp_pe, cmp_w1, cmp_b1, cmp_w2, rel_bias):
    past_len = page_table.shape[1] * cache_cmp_kv.shape[2]
    win_buf = state_win_kv.shape[2]
    n_dec = x_sample.shape[0]

    def ffn_half(x, i, j):
        h = rmsnorm(x, norm_g[i, 4 * j])
        y = swiglu(h, w_ffn_gate[i, j], w_ffn_up[i, j], w_ffn_down[i, j])
        return x + 0.5 * rmsnorm(y, norm_g[i, 4 * j + 1])

    def mixing(x, i, mem_kv, gla_s0, nsa_past, off):
        B, L, _ = x.shape
        li = i // 2
        h = rmsnorm(x, norm_g[i, 2])
        if i % 2 == 0:
            proj = h @ w_in_gla[li]
            tok, s_new = gla_mixer(proj[..., :GLA_COLS - MEM_W], gla_s0, w_gla_a2[li], b_gla_a[li], gla_onorm_g[li])
            new = (s_new,)
        else:
            proj = h @ w_in_nsa[li]
            tok, new = nsa_mixer(proj[..., :NSA_COLS - MEM_W], nsa_past, off, win_buf, rel_bias, nsa_gate_b[li],
                                 cmp_pe[li], cmp_w1[li], cmp_b1[li], cmp_w2[li])
        mem_o = mem_attend(proj[..., -MEM_W:].reshape(B, L, N_MEM_HEADS, MEM_HEAD_DIM), mem_kv)
        y = jnp.concatenate([tok, mem_o], axis=-1) @ w_out[i]
        return x + rmsnorm(y, norm_g[i, 3]), new

    def to_pages(a):
        return a.reshape(a.shape[0], a.shape[1] // PAGE_SIZE, PAGE_SIZE, *a.shape[2:])

    xp, xs = x_prompt, x_sample
    gla_p, gla_s, cmp_p, cmp_s, slc_p, slc_s, win_p, win_s, mem_p = [], [], [], [], [], [], [], [], []
    for i in range(DEPTH):
        li = i // 2
        mem_kv_p = (rmsnorm(mem_prompt, mem_norm_g[i]) @ w_mem_kv[i]).reshape(
            mem_prompt.shape[0], mem_prompt.shape[1], 2, N_MEM_HEADS, MEM_HEAD_DIM)
        mem_p.append(mem_kv_p)
        xp = ffn_half(xp, i, 0)
        xs = ffn_half(xs, i, 0)
        if i % 2 == 0:
            s0 = jnp.zeros((xp.shape[0], GLA_HEADS, GLA_DK, GLA_DV), xp.dtype)
            xp, (sp,) = mixing(xp, i, mem_kv_p, s0, None, 0)
            xs, (ss,) = mixing(xs, i, cache_mem_kv[i], state_gla[li], None, past_len)
            gla_p.append(sp)
            gla_s.append(ss)
        else:
            past = (cache_cmp_kv[li][page_table].reshape(n_dec, past_len, 2, NSA_GROUPS, NSA_HEAD_DIM),
                    cache_slc_kv[li][page_table].reshape(n_dec, past_len, 2, NSA_GROUPS, NSA_HEAD_DIM),
                    state_win_kv[li])
            xp, (cp, sp2, wp) = mixing(xp, i, mem_kv_p, None, None, 0)
            xs, (cs, ss2, ws) = mixing(xs, i, cache_mem_kv[i], None, past, past_len)
            cmp_p.append(to_pages(cp))
            slc_p.append(to_pages(sp2))
            win_p.append(wp)
            cmp_s.append(cs)
            slc_s.append(ss2)
            win_s.append(ws)
        xp = ffn_half(xp, i, 1)
        xs = ffn_half(xs, i, 1)
    y_prompt, y_sample = xp, xs
    return (y_prompt, y_sample, jnp.stack(gla_p), jnp.stack(cmp_p), jnp.stack(slc_p), jnp.stack(win_p),
            jnp.stack(mem_p), jnp.stack(gla_s), jnp.stack(cmp_s), jnp.stack(slc_s), jnp.stack(win_s))
```

```python
import functools
import math

import jax
import jax.numpy as jnp
from jax import lax
from jax.experimental import pallas as pl
from jax.experimental.pallas import tpu as pltpu

F32 = jnp.float32
BF16 = jnp.bfloat16
HI = lax.Precision.HIGHEST

D_MODEL = 2048
D_FF = 5632
EPS = 1e-6
MEM_LEN = 256
N_MEM_HEADS = 4
MEM_HEAD_DIM = 128
MEM_W = N_MEM_HEADS * MEM_HEAD_DIM
TOK_W = D_MODEL - MEM_W
GLA_HEADS = 4
GLA_DV = TOK_W // GLA_HEADS
GLA_DK = GLA_DV // 2
GLA_RANK = 16
GLA_TAU = 16.0
GLA_CHUNK = 64
GLA_PAIR_W = 2 * GLA_DK
DH = 128
NSA_HEADS = TOK_W // DH
NSA_G = 3
NSA_R = NSA_HEADS // NSA_G
BLK = 64
TOPN = 16
WINDOW = 512
CMP_HID = 256
QBLK = 64
KV_W = NSA_G * DH
REL_BUCKETS = 32
REL_MAX_EXACT = 16
REL_MAX_DIST = 128
PAGE = 128
LANE = 128
VMEM_LIMIT = 56 * 1024 * 1024

GLA_A_COL = 2 * GLA_HEADS * GLA_DK + 2 * TOK_W + MEM_W
GLA_N = 5376
NSA_MEM_COL = TOK_W
NSA_KV_COL = NSA_MEM_COL + MEM_W
NSA_GATE_COL = NSA_KV_COL + 6 * KV_W
NSA_N = 4608
SAMPLE_ROWS = 16
SLC_PAD = 192
WIN_PAD = 576
WIN_KEYS = WIN_PAD + QBLK
SLC_NEAR = SLC_PAD + QBLK
FAR_CHUNK = 512


def _cparams(sem, vmem=VMEM_LIMIT):
    return pltpu.CompilerParams(dimension_semantics=sem, vmem_limit_bytes=vmem)


def _sigmoid(x):
    return 1.0 / (1.0 + jnp.exp(-x))


def _rms(x, g):
    ms = jnp.mean(x * x, axis=-1, keepdims=True)
    return x * lax.rsqrt(ms + EPS) * g


def _nt(a, b, precision=None):
    return lax.dot_general(a, b, (((1,), (1,)), ((), ())), precision=precision,
                           preferred_element_type=F32)


def _tn(a, b, precision=None):
    return lax.dot_general(a, b, (((0,), (0,)), ((), ())), precision=precision,
                           preferred_element_type=F32)


def _dot(a, b, precision=None):
    return jnp.dot(a, b, precision=precision, preferred_element_type=F32)


def _ffn_kernel(x_ref, g1_ref, wg_ref, wu_ref, wd_ref, g2_ref, o_ref, xn_ref, acc_ref):
    j = pl.program_id(1)

    @pl.when(j == 0)
    def _():
        xn_ref[...] = _rms(x_ref[...], g1_ref[...]).astype(BF16)
        acc_ref[...] = jnp.zeros_like(acc_ref)

    xn = xn_ref[...]
    gate = _dot(xn, wg_ref[...])
    up = _dot(xn, wu_ref[...])
    h = (gate * _sigmoid(gate) * up).astype(BF16)
    acc_ref[...] += _dot(h, wd_ref[...])

    @pl.when(j == pl.num_programs(1) - 1)
    def _():
        o_ref[...] = x_ref[...] + 0.5 * _rms(acc_ref[...], g2_ref[...])


def _ffn_half(x, g1, wg, wu, wd, g2, tm, tf=512):
    m = x.shape[0]
    return pl.pallas_call(
        _ffn_kernel,
        out_shape=jax.ShapeDtypeStruct((m, D_MODEL), F32),
        grid=(m // tm, D_FF // tf),
        in_specs=[
            pl.BlockSpec((tm, D_MODEL), lambda i, j: (i, 0)),
            pl.BlockSpec((1, D_MODEL), lambda i, j: (0, 0)),
            pl.BlockSpec((D_MODEL, tf), lambda i, j: (0, j)),
            pl.BlockSpec((D_MODEL, tf), lambda i, j: (0, j)),
            pl.BlockSpec((tf, D_MODEL), lambda i, j: (j, 0)),
            pl.BlockSpec((1, D_MODEL), lambda i, j: (0, 0)),
        ],
        out_specs=pl.BlockSpec((tm, D_MODEL), lambda i, j: (i, 0)),
        scratch_shapes=[pltpu.VMEM((tm, D_MODEL), BF16), pltpu.VMEM((tm, D_MODEL), F32)],
        compiler_params=_cparams(("parallel", "arbitrary")),
        name="ffn_half",
    )(x, g1, wg, wu, wd, g2)


def _norm_matmul_kernel(x_ref, g_ref, w_ref, o_ref, xn_ref):
    @pl.when(pl.program_id(1) == 0)
    def _():
        xn_ref[...] = _rms(x_ref[...], g_ref[...]).astype(BF16)

    o_ref[...] = _dot(xn_ref[...], w_ref[...])


def _norm_matmul(x, g, w, tm, tn):
    m, n = x.shape[0], w.shape[1]
    return pl.pallas_call(
        _norm_matmul_kernel,
        out_shape=jax.ShapeDtypeStruct((m, n), F32),
        grid=(m // tm, n // tn),
        in_specs=[
            pl.BlockSpec((tm, D_MODEL), lambda i, j: (i, 0)),
            pl.BlockSpec((1, D_MODEL), lambda i, j: (0, 0)),
            pl.BlockSpec((D_MODEL, tn), lambda i, j: (0, j)),
        ],
        out_specs=pl.BlockSpec((tm, tn), lambda i, j: (i, j)),
        scratch_shapes=[pltpu.VMEM((tm, D_MODEL), BF16)],
        compiler_params=_cparams(("parallel", "arbitrary")),
        name="norm_matmul",
    )(x, g, w)


def _out_proj_kernel(x_ref, tok_ref, mem_ref, wt_ref, wm_ref, g_ref, o_ref):
    y = _dot(tok_ref[...], wt_ref[...]) + _dot(mem_ref[...], wm_ref[...])
    o_ref[...] = x_ref[...] + _rms(y, g_ref[...])


def _out_proj(x, tok, mem_o, w_tok, w_mem, g, tm):
    m = x.shape[0]
    return pl.pallas_call(
        _out_proj_kernel,
        out_shape=jax.ShapeDtypeStruct((m, D_MODEL), F32),
        grid=(m // tm,),
        in_specs=[
            pl.BlockSpec((tm, D_MODEL), lambda i: (i, 0)),
            pl.BlockSpec((tm, TOK_W), lambda i: (i, 0)),
            pl.BlockSpec((tm, MEM_W), lambda i: (i, 0)),
            pl.BlockSpec((TOK_W, D_MODEL), lambda i: (0, 0)),
            pl.BlockSpec((MEM_W, D_MODEL), lambda i: (0, 0)),
            pl.BlockSpec((1, D_MODEL), lambda i: (0, 0)),
        ],
        out_specs=pl.BlockSpec((tm, D_MODEL), lambda i: (i, 0)),
        compiler_params=_cparams(("parallel",)),
        name="out_proj",
    )(x, tok, mem_o, w_tok, w_mem, g)


def _mem_attn_kernel(q_ref, kv_ref, o_ref):
    for h in range(N_MEM_HEADS):
        q = (q_ref[:, h * DH:(h + 1) * DH] * (MEM_HEAD_DIM ** -0.5)).astype(BF16)
        k = kv_ref[:, h * DH:(h + 1) * DH].astype(BF16)
        v = kv_ref[:, MEM_W + h * DH:MEM_W + (h + 1) * DH].astype(BF16)
        s = _nt(q, k)
        e = jnp.exp(s - jnp.max(s, axis=-1, keepdims=True))
        p = e / jnp.sum(e, axis=-1, keepdims=True)
        o_ref[:, h * DH:(h + 1) * DH] = _dot(p.astype(BF16), v).astype(o_ref.dtype)


def _mem_attn(q_arr, q_col_block, mem_kv, rows_per_batch, tm):
    nb = mem_kv.shape[0]
    per = rows_per_batch // tm
    return pl.pallas_call(
        _mem_attn_kernel,
        out_shape=jax.ShapeDtypeStruct((nb * rows_per_batch, MEM_W), BF16),
        grid=(nb, per),
        in_specs=[
            pl.BlockSpec((tm, MEM_W), lambda b, i: (b * per + i, q_col_block)),
            pl.BlockSpec((None, MEM_LEN, 2 * MEM_W), lambda b, i: (b, 0, 0)),
        ],
        out_specs=pl.BlockSpec((tm, MEM_W), lambda b, i: (b * per + i, 0)),
        compiler_params=_cparams(("parallel", "parallel")),
        name="mem_attn",
    )(q_arr, mem_kv)


def _gla_kernel(q_ref, k_ref, v0_ref, v1_ref, r0_ref, r1_ref, a_ref, s0_ref, wa_ref, ba_ref, gn_ref,
                tok_ref, s_out_ref, s_ref, *, chunk, n_valid):
    l = pl.program_id(2)
    tl = q_ref.shape[0]

    @pl.when(l == 0)
    def _():
        s_ref[...] = s0_ref[...].reshape(GLA_PAIR_W, GLA_DV)

    lane = lax.broadcasted_iota(jnp.int32, (1, GLA_PAIR_W), 1)
    head_mask = [(lane < GLA_DK).astype(F32), (lane >= GLA_DK).astype(F32)]
    ti = lax.broadcasted_iota(jnp.int32, (chunk, chunk), 0)
    si = lax.broadcasted_iota(jnp.int32, (chunk, chunk), 1)
    causal = si <= ti
    tri = causal.astype(F32)
    ones = jnp.ones((chunk, LANE), F32)
    v_refs = (v0_ref, v1_ref)
    r_refs = (r0_ref, r1_ref)

    def step(ci, carry):
        r0 = pl.multiple_of(ci * chunk, chunk)
        rows = pl.ds(r0, chunk)
        z = _dot(a_ref[rows, :], wa_ref[...], HI) + ba_ref[...]
        la = -(jnp.maximum(-z, 0.0) + jnp.log1p(jnp.exp(-jnp.abs(z)))) / GLA_TAU
        pos = l * tl + r0 + lax.broadcasted_iota(jnp.int32, (chunk, 1), 0)
        la = jnp.where(pos < n_valid, la, 0.0)
        b = _dot(tri, la, HI)
        bl = b[chunk - 1:chunk, :]
        q = q_ref[rows, :] * (GLA_DK ** -0.5)
        k = k_ref[rows, :]
        qe = q * jnp.exp(b)
        ke = (k * jnp.exp(-b)).astype(BF16)
        kd = k * jnp.exp(bl - b)
        s_old = s_ref[...]
        s_bf = s_old.astype(BF16)
        upd = None
        for h in range(2):
            v = v_refs[h][rows, :].astype(BF16)
            qm = (qe * head_mask[h]).astype(BF16)
            att = jnp.where(causal, _nt(qm, ke), 0.0)
            o = _dot(qm, s_bf) + _dot(att.astype(BF16), v)
            o = _rms(o, gn_ref[...])
            r = r_refs[h][rows, :]
            tok_ref[rows, h * GLA_DV:(h + 1) * GLA_DV] = (o * (r * _sigmoid(r))).astype(tok_ref.dtype)
            u = _tn((kd * head_mask[h]).astype(BF16), v)
            upd = u if upd is None else upd + u
        blc = _tn(la, ones, HI)
        dec = jnp.exp(jnp.concatenate([blc] * (GLA_DV // LANE), axis=1))
        s_ref[...] = dec * s_old + upd
        return carry

    lax.fori_loop(0, tl // chunk, step, 0)

    @pl.when(l == pl.num_programs(2) - 1)
    def _():
        s_out_ref[...] = s_ref[...].reshape(2, GLA_DK, GLA_DV)


def _gla(proj, s0, wa_pad, b_a, gn, seq, tl, chunk, n_valid):
    nb = s0.shape[0]
    per = seq // tl
    w = GLA_PAIR_W

    def col(base, stride=1):
        return lambda b, p, l: (b * per + l, base + stride * p)

    return pl.pallas_call(
        functools.partial(_gla_kernel, chunk=chunk, n_valid=n_valid),
        out_shape=(jax.ShapeDtypeStruct((nb * seq, TOK_W), BF16),
                   jax.ShapeDtypeStruct((nb, GLA_HEADS, GLA_DK, GLA_DV), F32)),
        grid=(nb, 2, per),
        in_specs=[
            pl.BlockSpec((tl, w), col(0)),
            pl.BlockSpec((tl, w), col(2)),
            pl.BlockSpec((tl, w), col(4, 2)),
            pl.BlockSpec((tl, w), col(5, 2)),
            pl.BlockSpec((tl, w), col(8, 2)),
            pl.BlockSpec((tl, w), col(9, 2)),
            pl.BlockSpec((tl, LANE), lambda b, p, l: (b * per + l, GLA_A_COL // LANE)),
            pl.BlockSpec((None, 2, GLA_DK, GLA_DV), lambda b, p, l: (b, p, 0, 0)),
            pl.BlockSpec((LANE, w), lambda b, p, l: (0, p)),
            pl.BlockSpec((1, w), lambda b, p, l: (0, p)),
            pl.BlockSpec((1, GLA_DV), lambda b, p, l: (0, 0)),
        ],
        out_specs=(pl.BlockSpec((tl, 2 * GLA_DV), lambda b, p, l: (b * per + l, p)),
                   pl.BlockSpec((None, 2, GLA_DK, GLA_DV), lambda b, p, l: (b, p, 0, 0))),
        scratch_shapes=[pltpu.VMEM((GLA_PAIR_W, GLA_DV), F32)],
        compiler_params=_cparams(("parallel", "parallel", "arbitrary")),
        name="gla",
    )(proj, proj, proj, proj, proj, proj, proj, s0, wa_pad, b_a, gn)


def _masked_softmax(s, valid):
    s = jnp.where(valid, s, -jnp.inf)
    m = jnp.max(s, axis=-1, keepdims=True)
    m = jnp.where(m > -jnp.inf, m, 0.0)
    e = jnp.exp(s - m)
    return e / jnp.maximum(jnp.sum(e, axis=-1, keepdims=True), 1e-30)


def _online_update(state, s, v):
    m, l, acc = state
    r, nq, w = s.shape
    m_new = jnp.maximum(m, jnp.max(s, axis=-1, keepdims=True))
    m_safe = jnp.where(m_new > -jnp.inf, m_new, 0.0)
    alpha = jnp.exp(m - m_safe)
    p = jnp.exp(s - m_safe)
    l = alpha * l + jnp.sum(p, axis=-1, keepdims=True)
    pv = _dot(p.reshape(r * nq, w).astype(BF16), v).reshape(r, nq, DH)
    return m_new, l, alpha * acc + pv


def _compress_tail(xflat, w1_ref, b1_ref, w2_ref):
    h = _dot(xflat, w1_ref[...]) + b1_ref[...]
    h = h * _sigmoid(h)
    return _dot(h.astype(BF16), w2_ref[...])


def _cmp_prompt_kernel(x0_ref, x1_ref, x2_ref, pe_ref, w1_ref, b1_ref, w2_ref, o_ref, xflat_ref):
    x_refs = (x0_ref, x1_ref, x2_ref)
    nb, seq = x0_ref.shape[0], x0_ref.shape[1]
    nblk = seq // BLK
    for j in range(BLK):
        pe_j = pe_ref[j:j + 1, :]
        for b in range(nb):
            for g in range(NSA_G):
                xj = x_refs[g][b, pl.ds(j, nblk, stride=BLK), :]
                row = (b * NSA_G + g) * nblk
                xflat_ref[row:row + nblk, j * DH:(j + 1) * DH] = (xj + pe_j).astype(BF16)
    out = _compress_tail(xflat_ref[...], w1_ref, b1_ref, w2_ref)
    o_ref[...] = out.reshape(nb, NSA_G, nblk, DH)


def _cmp_prompt(proj3, pe, w1, b1, w2):
    nb, seq, _ = proj3.shape
    nblk = seq // BLK
    return pl.pallas_call(
        _cmp_prompt_kernel,
        out_shape=jax.ShapeDtypeStruct((2, nb, NSA_G, nblk, DH), F32),
        grid=(2,),
        in_specs=[
            pl.BlockSpec((nb, seq, DH), lambda kv: (0, 0, NSA_KV_COL // DH + NSA_G * kv)),
            pl.BlockSpec((nb, seq, DH), lambda kv: (0, 0, NSA_KV_COL // DH + NSA_G * kv + 1)),
            pl.BlockSpec((nb, seq, DH), lambda kv: (0, 0, NSA_KV_COL // DH + NSA_G * kv + 2)),
            pl.BlockSpec((None, BLK, DH), lambda kv: (kv, 0, 0)),
            pl.BlockSpec((None, BLK * DH, CMP_HID), lambda kv: (kv, 0, 0)),
            pl.BlockSpec((None, 1, CMP_HID), lambda kv: (kv, 0, 0)),
            pl.BlockSpec((None, CMP_HID, DH), lambda kv: (kv, 0, 0)),
        ],
        out_specs=pl.BlockSpec((None, nb, NSA_G, nblk, DH), lambda kv: (kv, 0, 0, 0, 0)),
        scratch_shapes=[pltpu.VMEM((nb * NSA_G * nblk, BLK * DH), BF16)],
        compiler_params=_cparams(("arbitrary",)),
        name="cmp_prompt",
    )(proj3, proj3, proj3, pe, w1, b1, w2)


def _nsa_prompt_kernel(rb_ref, q_ref, gl_ref, gb_ref, kc_ref, vc_ref, ks_ref, vs_ref, kw_ref, vw_ref,
                       bct_ref, bsn_ref, bw_ref, o_ref, ksb, vsb, kwb, vwb):
    g = pl.program_id(1)
    qi = pl.program_id(2)
    seq = ks_ref.shape[0]
    nblk = seq // BLK
    R = NSA_R

    @pl.when(qi == 0)
    def _():
        for src, dst, pad in ((ks_ref, ksb, SLC_PAD), (vs_ref, vsb, SLC_PAD),
                              (kw_ref, kwb, WIN_PAD), (vw_ref, vwb, WIN_PAD)):
            dst[0:pad, :] = jnp.zeros((pad, DH), BF16)
            dst[pad:pad + seq, :] = src[...].astype(BF16)

    q = q_ref[...] * (DH ** -0.5)
    q_all = jnp.concatenate([q[:, r * DH:(r + 1) * DH] for r in range(R)], axis=0)
    q_bf = q_all.astype(BF16)
    i_col = lax.broadcasted_iota(jnp.int32, (QBLK, 1), 0)
    t_col = qi * QBLK + i_col
    n_row = lax.broadcasted_iota(jnp.int32, (1, nblk), 1)
    n_col = lax.broadcasted_iota(jnp.int32, (nblk, 1), 0)
    r_idx = lax.broadcasted_iota(jnp.int32, (R, 1, 1), 0)
    c_far = jnp.zeros((R, 1, 1), F32)
    for r in range(R):
        c_far = jnp.where(r_idx == r, rb_ref[REL_BUCKETS - 1, g * R + r], c_far)

    s_c = _nt(q_all, kc_ref[...], HI).reshape(R, QBLK, nblk)
    rel = qi - n_row
    bias_c = []
    for r in range(R):
        tab = bct_ref[r]
        bias_c.append(jnp.where(rel == 0, tab[:, 0:1],
                      jnp.where(rel == 1, tab[:, 1:2],
                      jnp.where(rel == 2, tab[:, 2:3], tab[:, 3:4]))))
    s_c = s_c + jnp.stack(bias_c, axis=0)
    valid_c = (t_col - (n_row * BLK + (BLK - 1))) >= 0
    p_c = _masked_softmax(s_c, valid_c[None])
    o_c = _dot(p_c.reshape(R * QBLK, nblk).astype(BF16), vc_ref[...].astype(BF16))

    imp = jnp.sum(p_c, axis=0)
    forced = (n_row == 0) | (n_row == qi) | (n_row == qi - 1)
    future = n_row * BLK > t_col
    score = jnp.where(forced, jnp.inf, jnp.where(future, -jnp.inf, imp))
    rank = jnp.zeros((QBLK, nblk), F32)
    for i in range(nblk):
        col = score[:, i:i + 1]
        ahead = (col > score) | ((col == score) & (n_row > i))
        rank = rank + jnp.where(ahead, 1.0, 0.0)
    sel = rank < float(min(TOPN, nblk))
    n_far = qi - 3
    sel_far = jnp.where(sel & (n_row < n_far), 1.0, 0.0).astype(BF16)
    sel_all = jnp.where(sel, 1.0, 0.0).astype(BF16)

    def far_body(kc_i, state):
        start = pl.multiple_of(SLC_PAD + kc_i * FAR_CHUNK, FAR_CHUNK // 8)
        k = ksb[pl.ds(start, FAR_CHUNK), :]
        v = vsb[pl.ds(start, FAR_CHUNK), :]
        s = _nt(q_bf, k).reshape(R, QBLK, FAR_CHUNK) + c_far
        blk_of_key = kc_i * (FAR_CHUNK // BLK) + lax.shift_right_logical(
            lax.broadcasted_iota(jnp.int32, (1, FAR_CHUNK), 1), 6)
        expand = jnp.where(n_col == blk_of_key, 1.0, 0.0).astype(BF16)
        keep = _dot(sel_far, expand) > 0.5
        s = jnp.where(keep[None], s, -jnp.inf)
        return _online_update(state, s, v)

    n_chunks = (jnp.maximum(n_far, 0) + (FAR_CHUNK // BLK - 1)) // (FAR_CHUNK // BLK)
    state = (jnp.full((R, QBLK, 1), -jnp.inf, F32), jnp.zeros((R, QBLK, 1), F32), jnp.zeros((R, QBLK, DH), F32))
    state = lax.fori_loop(0, n_chunks, far_body, state)

    row0 = pl.multiple_of(qi * QBLK, QBLK)
    k = ksb[pl.ds(row0, SLC_NEAR), :]
    v = vsb[pl.ds(row0, SLC_NEAR), :]
    c_near = lax.broadcasted_iota(jnp.int32, (1, SLC_NEAR), 1)
    dist = SLC_PAD + i_col - c_near
    pos = qi * QBLK - SLC_PAD + c_near
    blk_of_key = qi - 3 + lax.shift_right_logical(c_near, 6)
    expand = jnp.where(n_col == blk_of_key, 1.0, 0.0).astype(BF16)
    keep = (_dot(sel_all, expand) > 0.5) & (dist >= 0) & (pos >= 0)
    s = _nt(q_bf, k).reshape(R, QBLK, SLC_NEAR) + bsn_ref[...]
    s = jnp.where(keep[None], s, -jnp.inf)
    _, l_s, acc_s = _online_update(state, s, v)
    o_s = (acc_s / jnp.maximum(l_s, 1e-30)).reshape(R * QBLK, DH)

    k = kwb[pl.ds(row0, WIN_KEYS), :]
    v = vwb[pl.ds(row0, WIN_KEYS), :]
    c_win = lax.broadcasted_iota(jnp.int32, (1, WIN_KEYS), 1)
    dist = WIN_PAD + i_col - c_win
    pos = qi * QBLK - WIN_PAD + c_win
    valid_w = (dist >= 0) & (dist <= WINDOW) & (pos >= 0)
    s = _nt(q_bf, k).reshape(R, QBLK, WIN_KEYS) + bw_ref[...]
    p_w = _masked_softmax(s, valid_w[None])
    o_w = _dot(p_w.reshape(R * QBLK, WIN_KEYS).astype(BF16), v)

    gates = _sigmoid(gl_ref[...] + gb_ref[...])
    src = lax.broadcasted_iota(jnp.int32, (LANE, LANE), 0)
    dst = lax.broadcasted_iota(jnp.int32, (LANE, LANE), 1)
    pick = jnp.where((src == g * (3 * R) + dst) & (dst < 3 * R), 1.0, 0.0)
    gsel = _dot(gates, pick, HI)
    for r in range(R):
        rows = slice(r * QBLK, (r + 1) * QBLK)
        o = (gsel[:, 3 * r:3 * r + 1] * o_c[rows] + gsel[:, 3 * r + 1:3 * r + 2] * o_s[rows]
             + gsel[:, 3 * r + 2:3 * r + 3] * o_w[rows])
        o_ref[:, r * DH:(r + 1) * DH] = o.astype(o_ref.dtype)


def _nsa_prompt(proj2, cmp_kv, rel_bias, gate_b_pad, bct, bsn, bw, nb, seq):
    nq = seq // QBLK
    nblk = seq // BLK
    proj3 = proj2.reshape(nb, seq, NSA_N)

    def kv_spec(col):
        return pl.BlockSpec((None, seq, DH), lambda b, g, qi: (b, 0, col // DH + g))

    return pl.pallas_call(
        _nsa_prompt_kernel,
        out_shape=jax.ShapeDtypeStruct((nb * seq, TOK_W), BF16),
        grid=(nb, NSA_G, nq),
        in_specs=[
            pl.BlockSpec(memory_space=pltpu.SMEM),
            pl.BlockSpec((QBLK, NSA_R * DH), lambda b, g, qi: (b * nq + qi, g)),
            pl.BlockSpec((QBLK, LANE), lambda b, g, qi: (b * nq + qi, NSA_GATE_COL // LANE)),
            pl.BlockSpec((1, LANE), lambda b, g, qi: (0, 0)),
            pl.BlockSpec((None, None, None, nblk, DH), lambda b, g, qi: (0, b, g, 0, 0)),
            pl.BlockSpec((None, None, None, nblk, DH), lambda b, g, qi: (1, b, g, 0, 0)),
            kv_spec(NSA_KV_COL + 2 * KV_W), kv_spec(NSA_KV_COL + 3 * KV_W),
            kv_spec(NSA_KV_COL + 4 * KV_W), kv_spec(NSA_KV_COL + 5 * KV_W),
            pl.BlockSpec((None, NSA_R, QBLK, LANE), lambda b, g, qi: (g, 0, 0, 0)),
            pl.BlockSpec((None, NSA_R, QBLK, SLC_NEAR), lambda b, g, qi: (g, 0, 0, 0)),
            pl.BlockSpec((None, NSA_R, QBLK, WIN_KEYS), lambda b, g, qi: (g, 0, 0, 0)),
        ],
        out_specs=pl.BlockSpec((QBLK, NSA_R * DH), lambda b, g, qi: (b * nq + qi, g)),
        scratch_shapes=[pltpu.VMEM((SLC_PAD + seq, DH), BF16), pltpu.VMEM((SLC_PAD + seq, DH), BF16),
                        pltpu.VMEM((WIN_PAD + seq, DH), BF16), pltpu.VMEM((WIN_PAD + seq, DH), BF16)],
        compiler_params=_cparams(("parallel", "parallel", "arbitrary")),
        name="nsa_prompt",
    )(rel_bias, proj2, proj2, gate_b_pad, cmp_kv, cmp_kv, proj3, proj3, proj3, proj3, bct, bsn, bw)


CMP_PAGES = 64


def _cmp_decode_kernel(pt_ref, cache_ref, new_ref, pe_ref, w1_ref, b1_ref, w2_ref, o_ref, last_ref,
                       buf, sem, xflat_ref, xlast_ref):
    kv, b, half = pl.program_id(0), pl.program_id(1), pl.program_id(2)
    n_seq, n_half = pl.num_programs(1), pl.num_programs(2)
    lin = (kv * n_seq + b) * n_half + half
    total = pl.num_programs(0) * n_seq * n_half
    pages = buf.shape[2] // PAGE
    blk_per_step = pages * PAGE // BLK

    def page_copy(kv_, b_, half_, slot, p, g):
        page = pt_ref[b_, half_ * pages + p]
        col = pl.multiple_of(kv_ * KV_W + g * DH, DH)
        return pltpu.make_async_copy(cache_ref.at[page, :, pl.ds(col, DH)],
                                     buf.at[slot, g, pl.ds(pl.multiple_of(p * PAGE, PAGE), PAGE), :],
                                     sem.at[slot])

    def for_each_copy(lin_, fn):
        kv_ = lin_ // (n_seq * n_half)
        b_ = (lin_ // n_half) % n_seq
        half_ = lin_ % n_half
        slot_ = lin_ % 2

        def body(p, c):
            for g in range(NSA_G):
                fn(page_copy(kv_, b_, half_, slot_, p, g))
            return c

        lax.fori_loop(0, pages, body, 0)

    @pl.when(lin == 0)
    def _():
        for_each_copy(lin, lambda cp: cp.start())

    @pl.when(lin + 1 < total)
    def _():
        for_each_copy(lin + 1, lambda cp: cp.start())

    for_each_copy(lin, lambda cp: cp.wait())
    slot = lin % 2

    for j in range(BLK):
        pe_j = pe_ref[j:j + 1, :]
        for g in range(NSA_G):
            xj = buf[slot, g, pl.ds(j, blk_per_step, stride=BLK), :]
            xflat_ref[g * blk_per_step:(g + 1) * blk_per_step, j * DH:(j + 1) * DH] = (xj + pe_j).astype(BF16)
    out = _compress_tail(xflat_ref[...], w1_ref, b1_ref, w2_ref)
    o_ref[...] = out.reshape(NSA_G, blk_per_step, DH)

    @pl.when((b == 0) & (half == 0))
    def _():
        rows = new_ref.shape[0]
        for j in range(BLK):
            pe_j = jnp.broadcast_to(pe_ref[j:j + 1, :], (rows, DH))
            xj = new_ref[...] + pe_j if j == 0 else pe_j
            xlast_ref[:, j * DH:(j + 1) * DH] = xj.astype(BF16)
        last_ref[...] = _compress_tail(xlast_ref[...], w1_ref, b1_ref, w2_ref)


def _cmp_decode(page_table, cache, new_rows, pe, w1, b1, w2):
    nd, n_pages = page_table.shape
    pages = min(CMP_PAGES, n_pages)
    n_half = n_pages // pages
    blk_per_step = pages * PAGE // BLK
    rows = new_rows.shape[1]
    grid_spec = pltpu.PrefetchScalarGridSpec(
        num_scalar_prefetch=1,
        grid=(2, nd, n_half),
        in_specs=[
            pl.BlockSpec(memory_space=pl.ANY),
            pl.BlockSpec((None, rows, DH), lambda kv, b, h, pt: (kv, 0, 0)),
            pl.BlockSpec((None, BLK, DH), lambda kv, b, h, pt: (kv, 0, 0)),
            pl.BlockSpec((None, BLK * DH, CMP_HID), lambda kv, b, h, pt: (kv, 0, 0)),
            pl.BlockSpec((None, 1, CMP_HID), lambda kv, b, h, pt: (kv, 0, 0)),
            pl.BlockSpec((None, CMP_HID, DH), lambda kv, b, h, pt: (kv, 0, 0)),
        ],
        out_specs=(pl.BlockSpec((None, None, NSA_G, blk_per_step, DH), lambda kv, b, h, pt: (kv, b, 0, h, 0)),
                   pl.BlockSpec((None, rows, DH), lambda kv, b, h, pt: (kv, 0, 0))),
        scratch_shapes=[pltpu.VMEM((2, NSA_G, pages * PAGE, DH), F32),
                        pltpu.SemaphoreType.DMA((2,)),
                        pltpu.VMEM((NSA_G * blk_per_step, BLK * DH), BF16),
                        pltpu.VMEM((rows, BLK * DH), BF16)],
    )
    return pl.pallas_call(
        _cmp_decode_kernel,
        out_shape=(jax.ShapeDtypeStruct((2, nd, NSA_G, n_pages * PAGE // BLK, DH), F32),
                   jax.ShapeDtypeStruct((2, rows, DH), F32)),
        grid_spec=grid_spec,
        compiler_params=_cparams(("arbitrary", "arbitrary", "arbitrary")),
        name="cmp_decode",
    )(page_table, cache, new_rows, pe, w1, b1, w2)


SEL_LANES = 384


def _sel_decode_kernel(q_ref, kc_ref, vc_ref, last_ref, bias_ref, oc_ref, sel_ref, *, t_pos):
    b = pl.program_id(0)
    n_past = kc_ref.shape[1]
    n_blocks = n_past + 1
    n_lane = lax.broadcasted_iota(jnp.int32, (1, SEL_LANES), 1)
    n_lane_f = n_lane.astype(F32)
    head_row = lax.broadcasted_iota(jnp.int32, (8, 1), 0) < NSA_R
    tb = t_pos // BLK
    for g in range(NSA_G):
        q = q_ref[g] * (DH ** -0.5)
        bias = bias_ref[g]
        s_p = _nt(q, kc_ref[g], HI) + bias[:, :n_past]
        row = b * NSA_G + g
        k_last = last_ref[0, pl.ds(row, 1), :]
        v_last = last_ref[1, pl.ds(row, 1), :]
        s_l = jnp.sum(q * k_last, axis=-1, keepdims=True) + bias[:, n_past:n_past + 1]
        valid_p = (t_pos - (n_lane[:, :n_past] * BLK + (BLK - 1))) >= 0
        valid_l = (t_pos - (n_past * BLK + (BLK - 1))) >= 0
        s_p = jnp.where(valid_p, s_p, -jnp.inf)
        s_l = jnp.where(valid_l, s_l, -jnp.inf)
        m = jnp.maximum(jnp.max(s_p, axis=-1, keepdims=True), s_l)
        m = jnp.where(m > -jnp.inf, m, 0.0)
        e_p = jnp.exp(s_p - m)
        e_l = jnp.exp(s_l - m)
        den = jnp.maximum(jnp.sum(e_p, axis=-1, keepdims=True) + e_l, 1e-30)
        p_p = e_p / den
        p_l = e_l / den
        oc_ref[g] = _dot(p_p, vc_ref[g], HI) + p_l * v_last
        imp_p = jnp.sum(jnp.where(head_row, p_p, 0.0), axis=0, keepdims=True)
        imp_l = jnp.sum(jnp.where(head_row, p_l, 0.0), axis=0, keepdims=True)
        imp = jnp.concatenate([imp_p, jnp.broadcast_to(imp_l, (1, SEL_LANES - n_past))], axis=1)
        forced = (n_lane == 0) | (n_lane == tb) | (n_lane == tb - 1)
        future = n_lane * BLK > t_pos
        score = jnp.where(forced, jnp.inf, jnp.where(future, -jnp.inf, imp))
        cand = n_lane < n_blocks
        sel = jnp.zeros((1, LANE), jnp.int32)
        k_lane = lax.broadcasted_iota(jnp.int32, (1, LANE), 1)
        for k in range(min(TOPN, n_blocks)):
            best = jnp.max(jnp.where(cand, score, -jnp.inf), axis=-1, keepdims=True)
            idx_f = jnp.min(jnp.where(cand & (score == best), n_lane_f, float(SEL_LANES)), axis=-1, keepdims=True)
            idx = idx_f.astype(jnp.int32)
            sel = jnp.where(k_lane == k, idx, sel)
            cand = cand & (n_lane != idx)
        sel_ref[g] = jnp.broadcast_to(sel, (8, LANE))


def _sel_decode(q8, cmp_kv, cmp_last, bias_cd, t_pos):
    nd = q8.shape[0]
    n_past = cmp_kv.shape[3]
    return pl.pallas_call(
        functools.partial(_sel_decode_kernel, t_pos=t_pos),
        out_shape=(jax.ShapeDtypeStruct((nd, NSA_G, 8, DH), F32),
                   jax.ShapeDtypeStruct((nd, NSA_G, 8, LANE), jnp.int32)),
        grid=(nd,),
        in_specs=[
            pl.BlockSpec((None, NSA_G, 8, DH), lambda b: (b, 0, 0, 0)),
            pl.BlockSpec((None, None, NSA_G, n_past, DH), lambda b: (0, b, 0, 0, 0)),
            pl.BlockSpec((None, None, NSA_G, n_past, DH), lambda b: (1, b, 0, 0, 0)),
            pl.BlockSpec(cmp_last.shape, lambda b: (0, 0, 0)),
            pl.BlockSpec(bias_cd.shape, lambda b: (0, 0, 0)),
        ],
        out_specs=(pl.BlockSpec((None, NSA_G, 8, DH), lambda b: (b, 0, 0, 0)),
                   pl.BlockSpec((None, NSA_G, 8, LANE), lambda b: (b, 0, 0, 0))),
        compiler_params=_cparams(("parallel",)),
        name="sel_decode",
    )(q8, cmp_kv, cmp_kv, cmp_last, bias_cd)


def _slc_decode_kernel(sel_ref, pt_ref, q_ref, oc_ref, gl_ref, gb_ref, k_ref, v_ref, bias_ref, nk_ref, nv_ref,
                       kw_ref, vw_ref, nwk_ref, nwv_ref, bw_ref, o_ref, m_ref, l_ref, acc_ref, *, t_pos, n_past):
    b, g, ki = pl.program_id(0), pl.program_id(1), pl.program_id(2)
    n = sel_ref[(b * NSA_G + g) * TOPN + ki]
    q = q_ref[...] * (DH ** -0.5)

    @pl.when(ki == 0)
    def _():
        m_ref[...] = jnp.full_like(m_ref, -jnp.inf)
        l_ref[...] = jnp.zeros_like(l_ref)
        acc_ref[...] = jnp.zeros_like(acc_ref)

    is_new = (n == n_past) & (lax.broadcasted_iota(jnp.int32, (BLK, 1), 0) == 0)
    kb = jnp.where(is_new, nk_ref[0:1, :], k_ref[...])
    vb = jnp.where(is_new, nv_ref[0:1, :], v_ref[...])
    pos = n * BLK + lax.broadcasted_iota(jnp.int32, (1, BLK), 1)
    s = _nt(q, kb, HI) + bias_ref[...]
    s = jnp.where(t_pos - pos >= 0, s, -jnp.inf)
    m_old = m_ref[...]
    m_new = jnp.maximum(m_old, jnp.max(s, axis=-1, keepdims=True))
    m_safe = jnp.where(m_new > -jnp.inf, m_new, 0.0)
    alpha = jnp.exp(m_old - m_safe)
    p = jnp.exp(s - m_safe)
    m_ref[...] = m_new
    l_ref[...] = alpha * l_ref[...] + jnp.sum(p, axis=-1, keepdims=True)
    acc_ref[...] = alpha * acc_ref[...] + _dot(p, vb, HI)

    @pl.when(ki == pl.num_programs(2) - 1)
    def _():
        o_s = acc_ref[...] / jnp.maximum(l_ref[...], 1e-30)
        n_win = kw_ref.shape[0]
        c = lax.broadcasted_iota(jnp.int32, (1, n_win), 1)
        dist = n_win - c
        valid = (dist >= 0) & (dist <= WINDOW) & (t_pos - dist >= 0)
        bw = bw_ref[...]
        s_w = jnp.where(valid, _nt(q, kw_ref[...], HI) + bw[:, :n_win], -jnp.inf)
        s_n = jnp.sum(q * nwk_ref[0:1, :], axis=-1, keepdims=True) + bw[:, n_win:n_win + 1]
        m_w = jnp.maximum(jnp.max(s_w, axis=-1, keepdims=True), s_n)
        e_w = jnp.exp(s_w - m_w)
        e_n = jnp.exp(s_n - m_w)
        den = jnp.maximum(jnp.sum(e_w, axis=-1, keepdims=True) + e_n, 1e-30)
        o_w = (_dot(e_w, vw_ref[...], HI) + e_n * nwv_ref[0:1, :]) / den
        gates = _sigmoid(gl_ref[...] + gb_ref[...])
        o_ref[...] = gates[:, 0:1] * oc_ref[...] + gates[:, 1:2] * o_s + gates[:, 2:3] * o_w


def _slc_decode(sel_flat, page_table, q8, o_c, gate_l, gate_b, slc_cache, bias_sd, new_k, new_v,
                win_cache, new_wk, new_wv, bias_wd, t_pos):
    nd = q8.shape[0]
    n_past = page_table.shape[1] * (PAGE // BLK)
    n_win = win_cache.shape[1]

    def blk_row(b, g, ki, sel, pt):
        n = jnp.minimum(sel[(b * NSA_G + g) * TOPN + ki], n_past - 1)
        return pt[b, n // 2] * 2 + n % 2

    def per_bg(b, g, ki, sel, pt):
        return (b, g, 0, 0)

    bg_spec = pl.BlockSpec((None, None, 8, DH), per_bg)
    grid_spec = pltpu.PrefetchScalarGridSpec(
        num_scalar_prefetch=2,
        grid=(nd, NSA_G, TOPN),
        in_specs=[
            bg_spec, bg_spec, bg_spec,
            pl.BlockSpec((None, 8, LANE), lambda b, g, ki, sel, pt: (g, 0, 0)),
            pl.BlockSpec((None, BLK, DH), lambda b, g, ki, sel, pt: (blk_row(b, g, ki, sel, pt), 0, g)),
            pl.BlockSpec((None, BLK, DH), lambda b, g, ki, sel, pt: (blk_row(b, g, ki, sel, pt), 0, NSA_G + g)),
            pl.BlockSpec((None, None, 8, BLK),
                         lambda b, g, ki, sel, pt: (g, sel[(b * NSA_G + g) * TOPN + ki], 0, 0)),
            bg_spec, bg_spec,
            pl.BlockSpec((None, n_win, DH), lambda b, g, ki, sel, pt: (b, 0, g)),
            pl.BlockSpec((None, n_win, DH), lambda b, g, ki, sel, pt: (b, 0, NSA_G + g)),
            bg_spec, bg_spec,
            pl.BlockSpec((None, 8, bias_wd.shape[2]), lambda b, g, ki, sel, pt: (g, 0, 0)),
        ],
        out_specs=bg_spec,
        scratch_shapes=[pltpu.VMEM((8, 1), F32), pltpu.VMEM((8, 1), F32), pltpu.VMEM((8, DH), F32)],
    )
    return pl.pallas_call(
        functools.partial(_slc_decode_kernel, t_pos=t_pos, n_past=n_past),
        out_shape=jax.ShapeDtypeStruct((nd, NSA_G, 8, DH), F32),
        grid_spec=grid_spec,
        compiler_params=_cparams(("parallel", "parallel", "arbitrary")),
        name="slc_decode",
    )(sel_flat, page_table, q8, o_c, gate_l, gate_b, slc_cache, slc_cache, bias_sd, new_k, new_v,
      win_cache, win_cache, new_wk, new_wv, bias_wd)


def _t5_bucket(dist):
    n = jnp.maximum(dist, 0)
    nf = jnp.maximum(n, REL_MAX_EXACT).astype(F32)
    large = REL_MAX_EXACT + (jnp.log(nf / REL_MAX_EXACT) / math.log(REL_MAX_DIST / REL_MAX_EXACT)
                             * (REL_BUCKETS - REL_MAX_EXACT)).astype(jnp.int32)
    return jnp.where(n < REL_MAX_EXACT, n, jnp.minimum(large, REL_BUCKETS - 1))


def _bias_table(rel_bias, dist):
    b = rel_bias[_t5_bucket(dist)]
    b = jnp.moveaxis(b, -1, 0)
    return b.reshape((NSA_G, NSA_R) + dist.shape)


def _prompt_bias_tables(rel_bias):
    i = jnp.arange(QBLK)[:, None]
    rel = jnp.arange(LANE)[None, :]
    bct = _bias_table(rel_bias, BLK * (rel - 1) + i + 1)
    bsn = _bias_table(rel_bias, SLC_PAD + i - jnp.arange(SLC_NEAR)[None, :])
    bw = _bias_table(rel_bias, WIN_PAD + i - jnp.arange(WIN_KEYS)[None, :])
    return bct, bsn, bw


def _pad_rows(a, rows, axis):
    pad = [(0, 0)] * a.ndim
    pad[axis] = (0, rows - a.shape[axis])
    return jnp.pad(a, pad)


def _decode_bias_tables(rel_bias, t_pos, n_past, n_win):
    n = jnp.arange(SEL_LANES)
    bias_cd = _pad_rows(_bias_table(rel_bias, t_pos - (n * BLK + BLK - 1)), 8, 1)
    tok = jnp.arange(n_past + 1)[:, None] * BLK + jnp.arange(BLK)[None, :]
    bias_sd = _pad_rows(jnp.swapaxes(_bias_table(rel_bias, t_pos - tok), 1, 2), 8, 2)
    c = jnp.arange(n_win + LANE)
    bias_wd = _pad_rows(_bias_table(rel_bias, n_win - c), 8, 1)
    return bias_cd, bias_sd, bias_wd


def kernel(x_prompt, x_sample, mem_prompt, cache_cmp_kv, cache_slc_kv, state_win_kv, state_gla, cache_mem_kv,
           page_table, norm_g, w_ffn_gate, w_ffn_up, w_ffn_down, w_in_gla, w_in_nsa, w_out, mem_norm_g, w_mem_kv,
           w_gla_a2, b_gla_a, gla_onorm_g, nsa_gate_b, cmp_pe, cmp_w1, cmp_b1, cmp_w2, rel_bias):
    nb, seq, _ = x_prompt.shape
    nd = x_sample.shape[0]
    depth = norm_g.shape[0]
    n_pages = page_table.shape[1]
    past_len = n_pages * PAGE
    n_past = past_len // BLK
    n_win = state_win_kv.shape[2]
    sr = SAMPLE_ROWS

    xp = x_prompt.reshape(nb * seq, D_MODEL)
    xs = _pad_rows(x_sample.reshape(nd, D_MODEL), sr, 0)
    mem_x = mem_prompt.reshape(nb * MEM_LEN, D_MODEL)

    wg, wu, wd = w_ffn_gate.astype(BF16), w_ffn_up.astype(BF16), w_ffn_down.astype(BF16)
    w_o = w_out.astype(BF16)
    w_mkv = w_mem_kv.astype(BF16)
    qkvr = 2 * GLA_HEADS * GLA_DK + 2 * TOK_W
    w_gla = jnp.concatenate(
        [w_in_gla[..., :qkvr], w_in_gla[..., qkvr + GLA_RANK:], w_in_gla[..., qkvr:qkvr + GLA_RANK],
         jnp.zeros(w_in_gla.shape[:2] + (GLA_N - w_in_gla.shape[2],), F32)], axis=-1).astype(BF16)
    n_gate = 3 * NSA_HEADS
    w_nsa = jnp.concatenate(
        [w_in_nsa[..., :TOK_W], w_in_nsa[..., TOK_W + 6 * KV_W + n_gate:], w_in_nsa[..., TOK_W:TOK_W + 6 * KV_W + n_gate],
         jnp.zeros(w_in_nsa.shape[:2] + (NSA_N - w_in_nsa.shape[2],), F32)], axis=-1).astype(BF16)
    wa_pad = _pad_rows(w_gla_a2, LANE, 1)
    w1 = cmp_w1.astype(BF16)
    w2 = cmp_w2.astype(BF16)

    def ffn(x, i, j, tm):
        return _ffn_half(x, norm_g[i, 4 * j][None], wg[i, j], wu[i, j], wd[i, j], norm_g[i, 4 * j + 1][None], tm)

    def every(a, step):
        return a.reshape(nd, step, a.shape[-1])[:, 0]

    def per_seq(a):
        return _pad_rows(a[:, None, :], sr, 1).reshape(nd * sr, a.shape[-1])

    outs = dict(gla_p=[], gla_s=[], cmp_p=[], cmp_s=[], slc_p=[], slc_s=[], win_p=[], win_s=[], mem_p=[])
    for i in range(depth):
        li = i // 2
        mem_kv_p = _norm_matmul(mem_x, mem_norm_g[i][None], w_mkv[i], 512, 512)
        outs["mem_p"].append(mem_kv_p.reshape(nb, MEM_LEN, 2, N_MEM_HEADS, MEM_HEAD_DIM))
        mem_kv_p = mem_kv_p.reshape(nb, MEM_LEN, 2 * MEM_W)
        mem_kv_s = cache_mem_kv[i].reshape(nd, MEM_LEN, 2 * MEM_W)
        xp = ffn(xp, i, 0, 512)
        xs = ffn(xs, i, 0, sr)
        g_mix = norm_g[i, 2][None]
        if i % 2 == 0:
            proj_p = _norm_matmul(xp, g_mix, w_gla[li], 1024, 768)
            proj_s = _norm_matmul(xs, g_mix, w_gla[li], sr, 768)
            mem_col = (qkvr) // MEM_W
            b_a = b_gla_a[li][None]
            gn = gla_onorm_g[li][None]
            s0 = jnp.zeros((nb, GLA_HEADS, GLA_DK, GLA_DV), F32)
            tok_p, sp = _gla(proj_p, s0, wa_pad[li], b_a, gn, seq, 512, GLA_CHUNK, seq)
            tok_s, ss = _gla(per_seq(proj_s[:nd]), state_gla[li], wa_pad[li], b_a, gn, sr, sr, sr, 1)
            tok_s = _pad_rows(every(tok_s, sr), sr, 0)
            outs["gla_p"].append(sp)
            outs["gla_s"].append(ss)
        else:
            proj_p = _norm_matmul(xp, g_mix, w_nsa[li], 1024, 768)
            proj_s = _norm_matmul(xs, g_mix, w_nsa[li], sr, 768)
            mem_col = NSA_MEM_COL // MEM_W
            b1 = cmp_b1[li][:, None, :]
            gate_b = nsa_gate_b[li]
            bct, bsn, bw = _prompt_bias_tables(rel_bias)
            proj3 = proj_p.reshape(nb, seq, NSA_N)
            cmp_kv = _cmp_prompt(proj3, cmp_pe[li], w1[li], b1, w2[li])
            tok_p = _nsa_prompt(proj_p, cmp_kv, rel_bias, _pad_rows(gate_b[None], LANE, 1), bct, bsn, bw, nb, seq)
            kv_p = proj3[:, :, NSA_KV_COL:NSA_GATE_COL].reshape(nb, seq, 3, 2, NSA_G, DH)
            outs["cmp_p"].append(kv_p[:, :, 0].reshape(nb, seq // PAGE, PAGE, 2, NSA_G, DH))
            outs["slc_p"].append(kv_p[:, :, 1].reshape(nb, seq // PAGE, PAGE, 2, NSA_G, DH))
            outs["win_p"].append(kv_p[:, seq - n_win:, 2])
            t_pos = past_len
            bias_cd, bias_sd, bias_wd = _decode_bias_tables(rel_bias, t_pos, n_past, n_win)
            kv_s = proj_s[:nd, NSA_KV_COL:NSA_GATE_COL].reshape(nd, 3, 2, NSA_G, DH)
            outs["cmp_s"].append(kv_s[:, None, 0])
            outs["slc_s"].append(kv_s[:, None, 1])
            outs["win_s"].append(jnp.concatenate([state_win_kv[li][:, 1:], kv_s[:, None, 2]], axis=1))
            q8 = _pad_rows(proj_s[:nd, :TOK_W].reshape(nd, NSA_G, NSA_R, DH), 8, 2)
            new_cmp = _pad_rows(jnp.moveaxis(kv_s[:, 0], 1, 0).reshape(2, nd * NSA_G, DH), 32, 1)
            cmp_kv_s, cmp_last = _cmp_decode(page_table, cache_cmp_kv[li].reshape(-1, PAGE, 2 * KV_W), new_cmp,
                                             cmp_pe[li], w1[li], b1, w2[li])
            o_c, sel = _sel_decode(q8, cmp_kv_s, cmp_last, bias_cd, t_pos)
            sel_flat = sel[:, :, 0, :TOPN].reshape(-1)

            def row8(a):
                return _pad_rows(a[:, :, None, :], 8, 2)

            gate_l = _pad_rows(_pad_rows(proj_s[:nd, NSA_GATE_COL:NSA_GATE_COL + n_gate].reshape(nd, NSA_G, NSA_R, 3),
                                         8, 2), LANE, 3)
            gate_b8 = _pad_rows(_pad_rows(gate_b.reshape(NSA_G, NSA_R, 3), 8, 1), LANE, 2)
            tok_s = _slc_decode(sel_flat, page_table, q8, o_c, gate_l, gate_b8,
                                cache_slc_kv[li].reshape(-1, BLK, 2 * KV_W), bias_sd,
                                row8(kv_s[:, 1, 0]), row8(kv_s[:, 1, 1]),
                                state_win_kv[li].reshape(nd, n_win, 2 * KV_W),
                                row8(kv_s[:, 2, 0]), row8(kv_s[:, 2, 1]), bias_wd, t_pos)
            tok_s = _pad_rows(tok_s[:, :, :NSA_R].reshape(nd, TOK_W), sr, 0).astype(BF16)
        mem_o_p = _mem_attn(proj_p, mem_col, mem_kv_p, seq, 512)
        q_s = jnp.broadcast_to(proj_s[:nd, None, mem_col * MEM_W:(mem_col + 1) * MEM_W],
                               (nd, sr, MEM_W)).reshape(nd * sr, MEM_W)
        mem_o_s = _pad_rows(every(_mem_attn(q_s, 0, mem_kv_s, sr, sr), sr), sr, 0)
        xp = _out_proj(xp, tok_p, mem_o_p, w_o[i, :TOK_W], w_o[i, TOK_W:], norm_g[i, 3][None], 256)
        xs = _out_proj(xs, tok_s, mem_o_s, w_o[i, :TOK_W], w_o[i, TOK_W:], norm_g[i, 3][None], sr)
        xp = ffn(xp, i, 1, 512)
        xs = ffn(xs, i, 1, sr)

    y_prompt = xp.reshape(nb, seq, D_MODEL)
    y_sample = xs[:nd].reshape(nd, 1, D_MODEL)
    st = lambda k: jnp.stack(outs[k])
    return (y_prompt, y_sample, st("gla_p"), st("cmp_p"), st("slc_p"), st("win_p"), st("mem_p"),
            st("gla_s"), st("cmp_s"), st("slc_s"), st("win_s"))
```

```python
import functools
import math

import jax
import jax.numpy as jnp
from jax import lax
from jax.experimental import pallas as pl
from jax.experimental.pallas import tpu as pltpu

F32 = jnp.float32
BF16 = jnp.bfloat16
HI = lax.Precision.HIGHEST

D_MODEL = 2048
D_FF = 5632
EPS = 1e-6
MEM_LEN = 256
N_MEM_HEADS = 4
MEM_HEAD_DIM = 128
MEM_W = N_MEM_HEADS * MEM_HEAD_DIM
TOK_W = D_MODEL - MEM_W
GLA_HEADS = 4
GLA_DV = TOK_W // GLA_HEADS
GLA_DK = GLA_DV // 2
GLA_RANK = 16
GLA_TAU = 16.0
GLA_CHUNK = 64
GLA_PAIR_W = 2 * GLA_DK
DH = 128
NSA_HEADS = TOK_W // DH
NSA_G = 3
NSA_R = NSA_HEADS // NSA_G
BLK = 64
TOPN = 16
WINDOW = 512
CMP_HID = 256
QBLK = 64
KV_W = NSA_G * DH
REL_BUCKETS = 32
REL_MAX_EXACT = 16
REL_MAX_DIST = 128
PAGE = 128
LANE = 128
VMEM_LIMIT = 56 * 1024 * 1024

GLA_A_COL = 2 * GLA_HEADS * GLA_DK + 2 * TOK_W + MEM_W
GLA_N = 5376
NSA_MEM_COL = TOK_W
NSA_KV_COL = NSA_MEM_COL + MEM_W
NSA_GATE_COL = NSA_KV_COL + 6 * KV_W
NSA_N = 4608
SAMPLE_ROWS = 16
SLC_PAD = 192
WIN_PAD = 576
WIN_KEYS = WIN_PAD + QBLK
SLC_NEAR = SLC_PAD + QBLK
FAR_CHUNK = 512


def _cparams(sem, vmem=VMEM_LIMIT):
    return pltpu.CompilerParams(dimension_semantics=sem, vmem_limit_bytes=vmem)


def _sigmoid(x):
    return 1.0 / (1.0 + jnp.exp(-x))


def _rms(x, g):
    ms = jnp.mean(x * x, axis=-1, keepdims=True)
    return x * lax.rsqrt(ms + EPS) * g


def _nt(a, b, precision=None):
    return lax.dot_general(a, b, (((1,), (1,)), ((), ())), precision=precision,
                           preferred_element_type=F32)


def _tn(a, b, precision=None):
    return lax.dot_general(a, b, (((0,), (0,)), ((), ())), precision=precision,
                           preferred_element_type=F32)


def _dot(a, b, precision=None):
    return jnp.dot(a, b, precision=precision, preferred_element_type=F32)


def _ffn_kernel(x_ref, g1_ref, wg_ref, wu_ref, wd_ref, g2_ref, o_ref, xn_ref, acc_ref):
    j = pl.program_id(1)

    @pl.when(j == 0)
    def _():
        xn_ref[...] = _rms(x_ref[...], g1_ref[...]).astype(BF16)
        acc_ref[...] = jnp.zeros_like(acc_ref)

    xn = xn_ref[...]
    gate = _dot(xn, wg_ref[...])
    up = _dot(xn, wu_ref[...])
    h = (gate * _sigmoid(gate) * up).astype(BF16)
    acc_ref[...] += _dot(h, wd_ref[...])

    @pl.when(j == pl.num_programs(1) - 1)
    def _():
        o_ref[...] = x_ref[...] + 0.5 * _rms(acc_ref[...], g2_ref[...])


def _ffn_half(x, g1, wg, wu, wd, g2, layer, half, tm, tf=512):
    m = x.shape[0]
    return pl.pallas_call(
        _ffn_kernel,
        out_shape=jax.ShapeDtypeStruct((m, D_MODEL), F32),
        grid=(m // tm, D_FF // tf),
        in_specs=[
            pl.BlockSpec((tm, D_MODEL), lambda i, j: (i, 0)),
            pl.BlockSpec((1, D_MODEL), lambda i, j: (0, 0)),
            pl.BlockSpec((None, None, D_MODEL, tf), lambda i, j: (layer, half, 0, j)),
            pl.BlockSpec((None, None, D_MODEL, tf), lambda i, j: (layer, half, 0, j)),
            pl.BlockSpec((None, None, tf, D_MODEL), lambda i, j: (layer, half, j, 0)),
            pl.BlockSpec((1, D_MODEL), lambda i, j: (0, 0)),
        ],
        out_specs=pl.BlockSpec((tm, D_MODEL), lambda i, j: (i, 0)),
        scratch_shapes=[pltpu.VMEM((tm, D_MODEL), BF16), pltpu.VMEM((tm, D_MODEL), F32)],
        compiler_params=_cparams(("parallel", "arbitrary")),
        name="ffn_half",
    )(x, g1, wg, wu, wd, g2)


def _norm_matmul_kernel(x_ref, g_ref, w_ref, o_ref, xn_ref):
    @pl.when(pl.program_id(1) == 0)
    def _():
        xn_ref[...] = _rms(x_ref[...], g_ref[...]).astype(BF16)

    o_ref[...] = _dot(xn_ref[...], w_ref[...])


def _norm_matmul(x, g, w, layer, tm, tn):
    m, n = x.shape[0], w.shape[2]
    return pl.pallas_call(
        _norm_matmul_kernel,
        out_shape=jax.ShapeDtypeStruct((m, n), F32),
        grid=(m // tm, n // tn),
        in_specs=[
            pl.BlockSpec((tm, D_MODEL), lambda i, j: (i, 0)),
            pl.BlockSpec((1, D_MODEL), lambda i, j: (0, 0)),
            pl.BlockSpec((None, D_MODEL, tn), lambda i, j: (layer, 0, j)),
        ],
        out_specs=pl.BlockSpec((tm, tn), lambda i, j: (i, j)),
        scratch_shapes=[pltpu.VMEM((tm, D_MODEL), BF16)],
        compiler_params=_cparams(("parallel", "arbitrary")),
        name="norm_matmul",
    )(x, g, w)


def _out_proj_kernel(x_ref, tok_ref, mem_ref, wt_ref, wm_ref, g_ref, o_ref):
    y = _dot(tok_ref[...], wt_ref[...]) + _dot(mem_ref[...], wm_ref[...])
    o_ref[...] = x_ref[...] + _rms(y, g_ref[...])


def _out_proj(x, tok, mem_o, w_o, layer, g, tm):
    m = x.shape[0]
    return pl.pallas_call(
        _out_proj_kernel,
        out_shape=jax.ShapeDtypeStruct((m, D_MODEL), F32),
        grid=(m // tm,),
        in_specs=[
            pl.BlockSpec((tm, D_MODEL), lambda i: (i, 0)),
            pl.BlockSpec((tm, TOK_W), lambda i: (i, 0)),
            pl.BlockSpec((tm, MEM_W), lambda i: (i, 0)),
            pl.BlockSpec((None, TOK_W, D_MODEL), lambda i: (layer, 0, 0), pipeline_mode=pl.Buffered(1)),
            pl.BlockSpec((None, MEM_W, D_MODEL), lambda i: (layer, TOK_W // MEM_W, 0), pipeline_mode=pl.Buffered(1)),
            pl.BlockSpec((1, D_MODEL), lambda i: (0, 0)),
        ],
        out_specs=pl.BlockSpec((tm, D_MODEL), lambda i: (i, 0)),
        compiler_params=_cparams(("parallel",)),
        name="out_proj",
    )(x, tok, mem_o, w_o, w_o, g)


def _mem_attn_kernel(q_ref, kv_ref, o_ref):
    for h in range(N_MEM_HEADS):
        q = (q_ref[:, h * DH:(h + 1) * DH] * (MEM_HEAD_DIM ** -0.5)).astype(BF16)
        k = kv_ref[:, h * DH:(h + 1) * DH].astype(BF16)
        v = kv_ref[:, MEM_W + h * DH:MEM_W + (h + 1) * DH].astype(BF16)
        s = _nt(q, k)
        e = jnp.exp(s - jnp.max(s, axis=-1, keepdims=True))
        p = e / jnp.sum(e, axis=-1, keepdims=True)
        o_ref[:, h * DH:(h + 1) * DH] = _dot(p.astype(BF16), v).astype(o_ref.dtype)


def _mem_attn(q_arr, q_col_block, mem_kv, rows_per_batch, tm):
    nb = mem_kv.shape[0]
    per = rows_per_batch // tm
    return pl.pallas_call(
        _mem_attn_kernel,
        out_shape=jax.ShapeDtypeStruct((nb * rows_per_batch, MEM_W), BF16),
        grid=(nb, per),
        in_specs=[
            pl.BlockSpec((tm, MEM_W), lambda b, i: (b * per + i, q_col_block)),
            pl.BlockSpec((None, MEM_LEN, 2 * MEM_W), lambda b, i: (b, 0, 0)),
        ],
        out_specs=pl.BlockSpec((tm, MEM_W), lambda b, i: (b * per + i, 0)),
        compiler_params=_cparams(("parallel", "parallel")),
        name="mem_attn",
    )(q_arr, mem_kv)


def _gla_kernel(q_ref, k_ref, v0_ref, v1_ref, r0_ref, r1_ref, a_ref, s0_ref, wa_ref, ba_ref, gn_ref,
                tok_ref, s_out_ref, s_ref, *, chunk, n_valid):
    l = pl.program_id(2)
    tl = q_ref.shape[0]

    @pl.when(l == 0)
    def _():
        s_ref[...] = s0_ref[...].reshape(GLA_PAIR_W, GLA_DV)

    lane = lax.broadcasted_iota(jnp.int32, (1, GLA_PAIR_W), 1)
    head_mask = [(lane < GLA_DK).astype(F32), (lane >= GLA_DK).astype(F32)]
    ti = lax.broadcasted_iota(jnp.int32, (chunk, chunk), 0)
    si = lax.broadcasted_iota(jnp.int32, (chunk, chunk), 1)
    causal = si <= ti
    tri = causal.astype(F32)
    ones = jnp.ones((chunk, LANE), F32)
    v_refs = (v0_ref, v1_ref)
    r_refs = (r0_ref, r1_ref)

    def step(ci, carry):
        r0 = pl.multiple_of(ci * chunk, chunk)
        rows = pl.ds(r0, chunk)
        z = _dot(a_ref[rows, :], wa_ref[...], HI) + ba_ref[...]
        la = -(jnp.maximum(-z, 0.0) + jnp.log1p(jnp.exp(-jnp.abs(z)))) / GLA_TAU
        pos = l * tl + r0 + lax.broadcasted_iota(jnp.int32, (chunk, 1), 0)
        la = jnp.where(pos < n_valid, la, 0.0)
        b = _dot(tri, la, HI)
        bl = b[chunk - 1:chunk, :]
        q = q_ref[rows, :] * (GLA_DK ** -0.5)
        k = k_ref[rows, :]
        qe = q * jnp.exp(b)
        ke = (k * jnp.exp(-b)).astype(BF16)
        kd = k * jnp.exp(bl - b)
        s_old = s_ref[...]
        s_bf = s_old.astype(BF16)
        upd = None
        for h in range(2):
            v = v_refs[h][rows, :].astype(BF16)
            qm = (qe * head_mask[h]).astype(BF16)
            att = jnp.where(causal, _nt(qm, ke), 0.0)
            o = _dot(qm, s_bf) + _dot(att.astype(BF16), v)
            o = _rms(o, gn_ref[...])
            r = r_refs[h][rows, :]
            tok_ref[rows, h * GLA_DV:(h + 1) * GLA_DV] = (o * (r * _sigmoid(r))).astype(tok_ref.dtype)
            u = _tn((kd * head_mask[h]).astype(BF16), v)
            upd = u if upd is None else upd + u
        blc = _tn(la, ones, HI)
        dec = jnp.exp(jnp.concatenate([blc] * (GLA_DV // LANE), axis=1))
        s_ref[...] = dec * s_old + upd
        return carry

    lax.fori_loop(0, tl // chunk, step, 0)

    @pl.when(l == pl.num_programs(2) - 1)
    def _():
        s_out_ref[...] = s_ref[...].reshape(2, GLA_DK, GLA_DV)


def _gla(proj, s0, wa_pad, b_a, gn, seq, tl, chunk, n_valid):
    nb = s0.shape[0]
    per = seq // tl
    w = GLA_PAIR_W

    def col(base, stride=1):
        return lambda b, p, l: (b * per + l, base + stride * p)

    return pl.pallas_call(
        functools.partial(_gla_kernel, chunk=chunk, n_valid=n_valid),
        out_shape=(jax.ShapeDtypeStruct((nb * seq, TOK_W), BF16),
                   jax.ShapeDtypeStruct((nb, GLA_HEADS, GLA_DK, GLA_DV), F32)),
        grid=(nb, 2, per),
        in_specs=[
            pl.BlockSpec((tl, w), col(0)),
            pl.BlockSpec((tl, w), col(2)),
            pl.BlockSpec((tl, w), col(4, 2)),
            pl.BlockSpec((tl, w), col(5, 2)),
            pl.BlockSpec((tl, w), col(8, 2)),
            pl.BlockSpec((tl, w), col(9, 2)),
            pl.BlockSpec((tl, LANE), lambda b, p, l: (b * per + l, GLA_A_COL // LANE)),
            pl.BlockSpec((None, 2, GLA_DK, GLA_DV), lambda b, p, l: (b, p, 0, 0)),
            pl.BlockSpec((LANE, w), lambda b, p, l: (0, p)),
            pl.BlockSpec((1, w), lambda b, p, l: (0, p)),
            pl.BlockSpec((1, GLA_DV), lambda b, p, l: (0, 0)),
        ],
        out_specs=(pl.BlockSpec((tl, 2 * GLA_DV), lambda b, p, l: (b * per + l, p)),
                   pl.BlockSpec((None, 2, GLA_DK, GLA_DV), lambda b, p, l: (b, p, 0, 0))),
        scratch_shapes=[pltpu.VMEM((GLA_PAIR_W, GLA_DV), F32)],
        compiler_params=_cparams(("parallel", "parallel", "arbitrary")),
        name="gla",
    )(proj, proj, proj, proj, proj, proj, proj, s0, wa_pad, b_a, gn)


def _masked_softmax(s, valid):
    s = jnp.where(valid, s, -jnp.inf)
    m = jnp.max(s, axis=-1, keepdims=True)
    m = jnp.where(m > -jnp.inf, m, 0.0)
    e = jnp.exp(s - m)
    return e / jnp.maximum(jnp.sum(e, axis=-1, keepdims=True), 1e-30)


def _online_update(state, s, v):
    m, l, acc = state
    r, nq, w = s.shape
    m_new = jnp.maximum(m, jnp.max(s, axis=-1, keepdims=True))
    m_safe = jnp.where(m_new > -jnp.inf, m_new, 0.0)
    alpha = jnp.exp(m - m_safe)
    p = jnp.exp(s - m_safe)
    l = alpha * l + jnp.sum(p, axis=-1, keepdims=True)
    pv = _dot(p.reshape(r * nq, w).astype(BF16), v).reshape(r, nq, DH)
    return m_new, l, alpha * acc + pv


def _compress_tail(xflat, w1, b1, w2):
    h = _dot(xflat, w1) + b1
    h = h * _sigmoid(h)
    return _dot(h.astype(BF16), w2)


def _cmp_prompt_kernel(x0_ref, x1_ref, x2_ref, pe_ref, w1_ref, b1_ref, w2_ref, o_ref, xflat_ref):
    x_refs = (x0_ref, x1_ref, x2_ref)
    nb, seq = x0_ref.shape[0], x0_ref.shape[1]
    nblk = seq // BLK
    for j in range(BLK):
        pe_j = pe_ref[j:j + 1, :]
        for b in range(nb):
            for g in range(NSA_G):
                xj = x_refs[g][b, pl.ds(j, nblk, stride=BLK), :]
                row = (b * NSA_G + g) * nblk
                xflat_ref[row:row + nblk, j * DH:(j + 1) * DH] = (xj + pe_j).astype(BF16)
    out = _compress_tail(xflat_ref[...], w1_ref[...], b1_ref[...], w2_ref[...])
    o_ref[...] = out.reshape(nb, NSA_G, nblk, DH)


def _cmp_prompt(proj3, pe, w1, b1, w2):
    nb, seq, _ = proj3.shape
    nblk = seq // BLK
    return pl.pallas_call(
        _cmp_prompt_kernel,
        out_shape=jax.ShapeDtypeStruct((2, nb, NSA_G, nblk, DH), F32),
        grid=(2,),
        in_specs=[
            pl.BlockSpec((nb, seq, DH), lambda kv: (0, 0, NSA_KV_COL // DH + NSA_G * kv)),
            pl.BlockSpec((nb, seq, DH), lambda kv: (0, 0, NSA_KV_COL // DH + NSA_G * kv + 1)),
            pl.BlockSpec((nb, seq, DH), lambda kv: (0, 0, NSA_KV_COL // DH + NSA_G * kv + 2)),
            pl.BlockSpec((None, BLK, DH), lambda kv: (kv, 0, 0)),
            pl.BlockSpec((None, BLK * DH, CMP_HID), lambda kv: (kv, 0, 0)),
            pl.BlockSpec((None, 1, CMP_HID), lambda kv: (kv, 0, 0)),
            pl.BlockSpec((None, CMP_HID, DH), lambda kv: (kv, 0, 0)),
        ],
        out_specs=pl.BlockSpec((None, nb, NSA_G, nblk, DH), lambda kv: (kv, 0, 0, 0, 0)),
        scratch_shapes=[pltpu.VMEM((nb * NSA_G * nblk, BLK * DH), BF16)],
        compiler_params=_cparams(("arbitrary",)),
        name="cmp_prompt",
    )(proj3, proj3, proj3, pe, w1, b1, w2)


def _nsa_prompt_kernel(rb_ref, q_ref, gl_ref, gb_ref, kc_ref, vc_ref, ks_ref, vs_ref, kw_ref, vw_ref,
                       bct_ref, bsn_ref, bw_ref, o_ref, ksb, vsb, kwb, vwb):
    g = pl.program_id(1)
    qi = pl.program_id(2)
    seq = ks_ref.shape[0]
    nblk = seq // BLK
    R = NSA_R

    @pl.when(qi == 0)
    def _():
        for src, dst, pad in ((ks_ref, ksb, SLC_PAD), (vs_ref, vsb, SLC_PAD),
                              (kw_ref, kwb, WIN_PAD), (vw_ref, vwb, WIN_PAD)):
            dst[0:pad, :] = jnp.zeros((pad, DH), BF16)
            dst[pad:pad + seq, :] = src[...].astype(BF16)

    q = q_ref[...] * (DH ** -0.5)
    q_all = jnp.concatenate([q[:, r * DH:(r + 1) * DH] for r in range(R)], axis=0)
    q_bf = q_all.astype(BF16)
    i_col = lax.broadcasted_iota(jnp.int32, (QBLK, 1), 0)
    t_col = qi * QBLK + i_col
    n_row = lax.broadcasted_iota(jnp.int32, (1, nblk), 1)
    n_col = lax.broadcasted_iota(jnp.int32, (nblk, 1), 0)
    r_idx = lax.broadcasted_iota(jnp.int32, (R, 1, 1), 0)
    c_far = jnp.zeros((R, 1, 1), F32)
    for r in range(R):
        c_far = jnp.where(r_idx == r, rb_ref[REL_BUCKETS - 1, g * R + r], c_far)

    s_c = _nt(q_all, kc_ref[...], HI).reshape(R, QBLK, nblk)
    rel = qi - n_row
    bias_c = []
    for r in range(R):
        tab = bct_ref[r]
        bias_c.append(jnp.where(rel == 0, tab[:, 0:1],
                      jnp.where(rel == 1, tab[:, 1:2],
                      jnp.where(rel == 2, tab[:, 2:3], tab[:, 3:4]))))
    s_c = s_c + jnp.stack(bias_c, axis=0)
    valid_c = (t_col - (n_row * BLK + (BLK - 1))) >= 0
    p_c = _masked_softmax(s_c, valid_c[None])
    o_c = _dot(p_c.reshape(R * QBLK, nblk).astype(BF16), vc_ref[...].astype(BF16))

    imp = jnp.sum(p_c, axis=0)
    forced = (n_row == 0) | (n_row == qi) | (n_row == qi - 1)
    future = n_row * BLK > t_col
    score = jnp.where(forced, jnp.inf, jnp.where(future, -jnp.inf, imp))
    half = nblk // 2
    score2 = jnp.concatenate([score, score], axis=1)
    lane2 = lax.broadcasted_iota(jnp.int32, (1, 2 * nblk), 1)
    left = lane2 < nblk
    n2 = jnp.where(left, lane2, lane2 - nblk)
    rank2 = jnp.zeros((QBLK, 2 * nblk), F32)
    for i in range(half):
        col = jnp.where(left, score[:, i:i + 1], score[:, i + half:i + half + 1])
        wins_tie = jnp.where(n2 > jnp.where(left, i, i + half), 1.0, 0.0)
        rank2 = rank2 + jnp.where(col > score2, 1.0, jnp.where(col == score2, wins_tie, 0.0))
    rank = rank2[:, :nblk] + rank2[:, nblk:]
    sel = rank < float(min(TOPN, nblk))
    n_far = qi - 3
    sel_far = jnp.where(sel & (n_row < n_far), 1.0, 0.0).astype(BF16)
    sel_all = jnp.where(sel, 1.0, 0.0).astype(BF16)

    def far_body(kc_i, state):
        start = pl.multiple_of(SLC_PAD + kc_i * FAR_CHUNK, FAR_CHUNK // 8)
        k = ksb[pl.ds(start, FAR_CHUNK), :]
        v = vsb[pl.ds(start, FAR_CHUNK), :]
        s = _nt(q_bf, k).reshape(R, QBLK, FAR_CHUNK) + c_far
        blk_of_key = kc_i * (FAR_CHUNK // BLK) + lax.shift_right_logical(
            lax.broadcasted_iota(jnp.int32, (1, FAR_CHUNK), 1), 6)
        expand = jnp.where(n_col == blk_of_key, 1.0, 0.0).astype(BF16)
        keep = _dot(sel_far, expand) > 0.5
        s = jnp.where(keep[None], s, -jnp.inf)
        return _online_update(state, s, v)

    n_chunks = (jnp.maximum(n_far, 0) + (FAR_CHUNK // BLK - 1)) // (FAR_CHUNK // BLK)
    state = (jnp.full((R, QBLK, 1), -jnp.inf, F32), jnp.zeros((R, QBLK, 1), F32), jnp.zeros((R, QBLK, DH), F32))
    state = lax.fori_loop(0, n_chunks, far_body, state)

    row0 = pl.multiple_of(qi * QBLK, QBLK)
    k = ksb[pl.ds(row0, SLC_NEAR), :]
    v = vsb[pl.ds(row0, SLC_NEAR), :]
    c_near = lax.broadcasted_iota(jnp.int32, (1, SLC_NEAR), 1)
    dist = SLC_PAD + i_col - c_near
    pos = qi * QBLK - SLC_PAD + c_near
    blk_of_key = qi - 3 + lax.shift_right_logical(c_near, 6)
    expand = jnp.where(n_col == blk_of_key, 1.0, 0.0).astype(BF16)
    keep = (_dot(sel_all, expand) > 0.5) & (dist >= 0) & (pos >= 0)
    s = _nt(q_bf, k).reshape(R, QBLK, SLC_NEAR) + bsn_ref[...]
    s = jnp.where(keep[None], s, -jnp.inf)
    _, l_s, acc_s = _online_update(state, s, v)
    o_s = (acc_s / jnp.maximum(l_s, 1e-30)).reshape(R * QBLK, DH)

    k = kwb[pl.ds(row0, WIN_KEYS), :]
    v = vwb[pl.ds(row0, WIN_KEYS), :]
    c_win = lax.broadcasted_iota(jnp.int32, (1, WIN_KEYS), 1)
    dist = WIN_PAD + i_col - c_win
    pos = qi * QBLK - WIN_PAD + c_win
    valid_w = (dist >= 0) & (dist <= WINDOW) & (pos >= 0)
    s = _nt(q_bf, k).reshape(R, QBLK, WIN_KEYS) + bw_ref[...]
    p_w = _masked_softmax(s, valid_w[None])
    o_w = _dot(p_w.reshape(R * QBLK, WIN_KEYS).astype(BF16), v)

    gates = _sigmoid(gl_ref[...] + gb_ref[...])
    src = lax.broadcasted_iota(jnp.int32, (LANE, LANE), 0)
    dst = lax.broadcasted_iota(jnp.int32, (LANE, LANE), 1)
    pick = jnp.where((src == g * (3 * R) + dst) & (dst < 3 * R), 1.0, 0.0)
    gsel = _dot(gates, pick, HI)
    for r in range(R):
        rows = slice(r * QBLK, (r + 1) * QBLK)
        o = (gsel[:, 3 * r:3 * r + 1] * o_c[rows] + gsel[:, 3 * r + 1:3 * r + 2] * o_s[rows]
             + gsel[:, 3 * r + 2:3 * r + 3] * o_w[rows])
        o_ref[:, r * DH:(r + 1) * DH] = o.astype(o_ref.dtype)


def _nsa_prompt(proj2, cmp_kv, rel_bias, gate_b_pad, bct, bsn, bw, nb, seq):
    nq = seq // QBLK
    nblk = seq // BLK
    proj3 = proj2.reshape(nb, seq, NSA_N)

    def kv_spec(col):
        return pl.BlockSpec((None, seq, DH), lambda b, g, qi: (b, 0, col // DH + g))

    return pl.pallas_call(
        _nsa_prompt_kernel,
        out_shape=jax.ShapeDtypeStruct((nb * seq, TOK_W), BF16),
        grid=(nb, NSA_G, nq),
        in_specs=[
            pl.BlockSpec(memory_space=pltpu.SMEM),
            pl.BlockSpec((QBLK, NSA_R * DH), lambda b, g, qi: (b * nq + qi, g)),
            pl.BlockSpec((QBLK, LANE), lambda b, g, qi: (b * nq + qi, NSA_GATE_COL // LANE)),
            pl.BlockSpec((1, LANE), lambda b, g, qi: (0, 0)),
            pl.BlockSpec((None, None, None, nblk, DH), lambda b, g, qi: (0, b, g, 0, 0)),
            pl.BlockSpec((None, None, None, nblk, DH), lambda b, g, qi: (1, b, g, 0, 0)),
            kv_spec(NSA_KV_COL + 2 * KV_W), kv_spec(NSA_KV_COL + 3 * KV_W),
            kv_spec(NSA_KV_COL + 4 * KV_W), kv_spec(NSA_KV_COL + 5 * KV_W),
            pl.BlockSpec((None, NSA_R, QBLK, LANE), lambda b, g, qi: (g, 0, 0, 0)),
            pl.BlockSpec((None, NSA_R, QBLK, SLC_NEAR), lambda b, g, qi: (g, 0, 0, 0)),
            pl.BlockSpec((None, NSA_R, QBLK, WIN_KEYS), lambda b, g, qi: (g, 0, 0, 0)),
        ],
        out_specs=pl.BlockSpec((QBLK, NSA_R * DH), lambda b, g, qi: (b * nq + qi, g)),
        scratch_shapes=[pltpu.VMEM((SLC_PAD + seq, DH), BF16), pltpu.VMEM((SLC_PAD + seq, DH), BF16),
                        pltpu.VMEM((WIN_PAD + seq, DH), BF16), pltpu.VMEM((WIN_PAD + seq, DH), BF16)],
        compiler_params=_cparams(("parallel", "parallel", "arbitrary")),
        name="nsa_prompt",
    )(rel_bias, proj2, proj2, gate_b_pad, cmp_kv, cmp_kv, proj3, proj3, proj3, proj3, bct, bsn, bw)


SUB_PAGES = 8
RING = 4
ROW_W = 2 * NSA_G
PAGE_ROWS = PAGE * ROW_W


def _linear_cache(cache):
    return jnp.transpose(cache, (0, 1, 3, 2, 4)).reshape(-1, DH)


def _cmp_decode_kernel(pt_ref, cache_ref, new_ref, pe_ref, w1_ref, b1_ref, w2_ref, o_ref, last_ref,
                       buf, sem, xflat_ref, xlast_ref, *, n_sub):
    b = pl.program_id(0)
    total = pl.num_programs(0) * n_sub
    sub_blk = SUB_PAGES * PAGE // BLK
    seq_blk = n_sub * sub_blk

    def page_copy(s, p):
        page = pt_ref[s // n_sub, (s % n_sub) * SUB_PAGES + p]
        return pltpu.make_async_copy(cache_ref.at[pl.ds(pl.multiple_of(page * PAGE_ROWS, PAGE_ROWS), PAGE_ROWS), :],
                                     buf.at[s % RING, pl.ds(p * PAGE_ROWS, PAGE_ROWS), :], sem.at[s % RING])

    def start_sub(s):
        for p in range(SUB_PAGES):
            page_copy(s, p).start()

    @pl.when(b == 0)
    def _():
        for s in range(RING):
            start_sub(s)

    def body(i, c):
        s = b * n_sub + i
        for p in range(SUB_PAGES):
            page_copy(s, p).wait()
        slot = s % RING
        row0 = pl.multiple_of(i * sub_blk, sub_blk)
        for kv in range(2):
            for j in range(BLK):
                pe_j = pe_ref[kv, j:j + 1, :]
                for g in range(NSA_G):
                    xj = buf[slot, pl.ds(j * ROW_W + g * 2 + kv, sub_blk, stride=BLK * ROW_W), :]
                    xflat_ref[kv, pl.ds(g * seq_blk + row0, sub_blk), j * DH:(j + 1) * DH] = (xj + pe_j).astype(BF16)

        @pl.when(s + RING < total)
        def _():
            start_sub(s + RING)

        return c

    lax.fori_loop(0, n_sub, body, 0)
    for kv in range(2):
        out = _compress_tail(xflat_ref[kv], w1_ref[kv], b1_ref[kv], w2_ref[kv])
        o_ref[kv] = out.reshape(NSA_G, seq_blk, DH)

    @pl.when(b == 0)
    def _():
        rows = new_ref.shape[1]
        for kv in range(2):
            for j in range(BLK):
                pe_j = jnp.broadcast_to(pe_ref[kv, j:j + 1, :], (rows, DH))
                xj = new_ref[kv] + pe_j if j == 0 else pe_j
                xlast_ref[:, j * DH:(j + 1) * DH] = xj.astype(BF16)
            last_ref[kv] = _compress_tail(xlast_ref[...], w1_ref[kv], b1_ref[kv], w2_ref[kv])


def _cmp_decode(page_table, cache, new_rows, pe, w1, b1, w2):
    nd, n_pages = page_table.shape
    n_sub = n_pages // SUB_PAGES
    seq_blk = n_pages * PAGE // BLK
    rows = new_rows.shape[1]

    def whole(shape):
        return pl.BlockSpec(shape, lambda b, pt: (0,) * len(shape), pipeline_mode=pl.Buffered(1))

    grid_spec = pltpu.PrefetchScalarGridSpec(
        num_scalar_prefetch=1,
        grid=(nd,),
        in_specs=[
            pl.BlockSpec(memory_space=pl.ANY),
            whole(new_rows.shape), whole(pe.shape), whole(w1.shape), whole(b1.shape), whole(w2.shape),
        ],
        out_specs=(pl.BlockSpec((2, None, NSA_G, seq_blk, DH), lambda b, pt: (0, b, 0, 0, 0)),
                   pl.BlockSpec((2, rows, DH), lambda b, pt: (0, 0, 0))),
        scratch_shapes=[pltpu.VMEM((RING, SUB_PAGES * PAGE_ROWS, DH), F32),
                        pltpu.SemaphoreType.DMA((RING,)),
                        pltpu.VMEM((2, NSA_G * seq_blk, BLK * DH), BF16),
                        pltpu.VMEM((rows, BLK * DH), BF16)],
    )
    return pl.pallas_call(
        functools.partial(_cmp_decode_kernel, n_sub=n_sub),
        out_shape=(jax.ShapeDtypeStruct((2, nd, NSA_G, seq_blk, DH), F32),
                   jax.ShapeDtypeStruct((2, rows, DH), F32)),
        grid_spec=grid_spec,
        compiler_params=_cparams(("arbitrary",)),
        name="cmp_decode",
    )(page_table, cache, new_rows, pe, w1, b1, w2)


SEL_LANES = 384


def _sel_decode_kernel(q_ref, kc_ref, vc_ref, last_ref, bias_ref, oc_ref, sel_ref, *, t_pos):
    b = pl.program_id(0)
    n_past = kc_ref.shape[1]
    n_blocks = n_past + 1
    n_lane = lax.broadcasted_iota(jnp.int32, (1, SEL_LANES), 1)
    n_lane_f = n_lane.astype(F32)
    head_row = lax.broadcasted_iota(jnp.int32, (8, 1), 0) < NSA_R
    tb = t_pos // BLK
    for g in range(NSA_G):
        q = q_ref[g] * (DH ** -0.5)
        bias = bias_ref[g]
        s_p = _nt(q, kc_ref[g], HI) + bias[:, :n_past]
        row = b * NSA_G + g
        k_last = last_ref[0, pl.ds(row, 1), :]
        v_last = last_ref[1, pl.ds(row, 1), :]
        s_l = jnp.sum(q * k_last, axis=-1, keepdims=True) + bias[:, n_past:n_past + 1]
        valid_p = (t_pos - (n_lane[:, :n_past] * BLK + (BLK - 1))) >= 0
        valid_l = (t_pos - (n_past * BLK + (BLK - 1))) >= 0
        s_p = jnp.where(valid_p, s_p, -jnp.inf)
        s_l = jnp.where(valid_l, s_l, -jnp.inf)
        m = jnp.maximum(jnp.max(s_p, axis=-1, keepdims=True), s_l)
        m = jnp.where(m > -jnp.inf, m, 0.0)
        e_p = jnp.exp(s_p - m)
        e_l = jnp.exp(s_l - m)
        den = jnp.maximum(jnp.sum(e_p, axis=-1, keepdims=True) + e_l, 1e-30)
        p_p = e_p / den
        p_l = e_l / den
        oc_ref[g] = _dot(p_p, vc_ref[g], HI) + p_l * v_last
        imp_p = jnp.sum(jnp.where(head_row, p_p, 0.0), axis=0, keepdims=True)
        imp_l = jnp.sum(jnp.where(head_row, p_l, 0.0), axis=0, keepdims=True)
        imp = jnp.concatenate([imp_p, jnp.broadcast_to(imp_l, (1, SEL_LANES - n_past))], axis=1)
        forced = (n_lane == 0) | (n_lane == tb) | (n_lane == tb - 1)
        future = n_lane * BLK > t_pos
        score = jnp.where(forced, jnp.inf, jnp.where(future, -jnp.inf, imp))
        cand = n_lane < n_blocks
        sel = jnp.zeros((1, LANE), jnp.int32)
        k_lane = lax.broadcasted_iota(jnp.int32, (1, LANE), 1)
        for k in range(min(TOPN, n_blocks)):
            best = jnp.max(jnp.where(cand, score, -jnp.inf), axis=-1, keepdims=True)
            idx_f = jnp.min(jnp.where(cand & (score == best), n_lane_f, float(SEL_LANES)), axis=-1, keepdims=True)
            idx = idx_f.astype(jnp.int32)
            sel = jnp.where(k_lane == k, idx, sel)
            cand = cand & (n_lane != idx)
        sel_ref[g] = jnp.broadcast_to(sel, (8, LANE))


def _sel_decode(q8, cmp_kv, cmp_last, bias_cd, t_pos):
    nd = q8.shape[0]
    n_past = cmp_kv.shape[3]
    return pl.pallas_call(
        functools.partial(_sel_decode_kernel, t_pos=t_pos),
        out_shape=(jax.ShapeDtypeStruct((nd, NSA_G, 8, DH), F32),
                   jax.ShapeDtypeStruct((nd, NSA_G, 8, LANE), jnp.int32)),
        grid=(nd,),
        in_specs=[
            pl.BlockSpec((None, NSA_G, 8, DH), lambda b: (b, 0, 0, 0)),
            pl.BlockSpec((None, None, NSA_G, n_past, DH), lambda b: (0, b, 0, 0, 0)),
            pl.BlockSpec((None, None, NSA_G, n_past, DH), lambda b: (1, b, 0, 0, 0)),
            pl.BlockSpec(cmp_last.shape, lambda b: (0, 0, 0)),
            pl.BlockSpec(bias_cd.shape, lambda b: (0, 0, 0)),
        ],
        out_specs=(pl.BlockSpec((None, NSA_G, 8, DH), lambda b: (b, 0, 0, 0)),
                   pl.BlockSpec((None, NSA_G, 8, LANE), lambda b: (b, 0, 0, 0))),
        compiler_params=_cparams(("parallel",)),
        name="sel_decode",
    )(q8, cmp_kv, cmp_kv, cmp_last, bias_cd)


HALF_ROWS = BLK * ROW_W


def _slc_decode_kernel(sel_ref, pt_ref, q_ref, oc_ref, gl_ref, gb_ref, slc_ref, bias_ref, nk_ref, nv_ref,
                       win_ref, nwk_ref, nwv_ref, bw_ref, o_ref, gbuf, sem, *, t_pos, n_past):
    b = pl.program_id(0)
    n_win = win_ref.shape[0] // ROW_W

    def block_copy(g, k):
        n = jnp.minimum(sel_ref[(b * NSA_G + g) * TOPN + k], n_past - 1)
        half = pt_ref[b, n // 2] * 2 + n % 2
        return pltpu.make_async_copy(slc_ref.at[pl.ds(pl.multiple_of(half * HALF_ROWS, HALF_ROWS), HALF_ROWS), :],
                                     gbuf.at[g * TOPN + k], sem)

    for g in range(NSA_G):
        for k in range(TOPN):
            block_copy(g, k).start()

    c = lax.broadcasted_iota(jnp.int32, (1, n_win), 1)
    dist = n_win - c
    valid = (dist >= 0) & (dist <= WINDOW) & (t_pos - dist >= 0)
    qs, o_w = [], []
    for g in range(NSA_G):
        q = q_ref[g] * (DH ** -0.5)
        q_bf = q.astype(BF16)
        qs.append((q, q_bf))
        kw = win_ref[pl.ds(2 * g, n_win, stride=ROW_W), :].astype(BF16)
        vw = win_ref[pl.ds(2 * g + 1, n_win, stride=ROW_W), :].astype(BF16)
        bw = bw_ref[g]
        s_w = jnp.where(valid, _nt(q_bf, kw) + bw[:, :n_win], -jnp.inf)
        s_n = jnp.sum(q * nwk_ref[g, 0:1, :], axis=-1, keepdims=True) + bw[:, n_win:n_win + 1]
        m_w = jnp.maximum(jnp.max(s_w, axis=-1, keepdims=True), s_n)
        e_w = jnp.exp(s_w - m_w)
        e_n = jnp.exp(s_n - m_w)
        den = jnp.maximum(jnp.sum(e_w, axis=-1, keepdims=True) + e_n, 1e-30)
        o_w.append((_dot(e_w.astype(BF16), vw) + e_n * nwv_ref[g, 0:1, :]) / den)

    for g in range(NSA_G):
        for k in range(TOPN):
            block_copy(g, k).wait()

    row0 = lax.broadcasted_iota(jnp.int32, (BLK, 1), 0) == 0
    j_row = lax.broadcasted_iota(jnp.int32, (1, BLK), 1)
    for g in range(NSA_G):
        q, q_bf = qs[g]
        scores, values = [], []
        for k in range(TOPN):
            n = sel_ref[(b * NSA_G + g) * TOPN + k]
            is_new = (n == n_past) & row0
            kb = jnp.where(is_new, nk_ref[g, 0:1, :], gbuf[g * TOPN + k, pl.ds(2 * g, BLK, stride=ROW_W), :])
            vb = jnp.where(is_new, nv_ref[g, 0:1, :], gbuf[g * TOPN + k, pl.ds(2 * g + 1, BLK, stride=ROW_W), :])
            s = _nt(q_bf, kb.astype(BF16)) + bias_ref[g, n]
            scores.append(jnp.where(t_pos - (n * BLK + j_row) >= 0, s, -jnp.inf))
            values.append(vb.astype(BF16))
        m = functools.reduce(jnp.maximum, [jnp.max(s, axis=-1, keepdims=True) for s in scores])
        m = jnp.where(m > -jnp.inf, m, 0.0)
        probs = [jnp.exp(s - m) for s in scores]
        l = functools.reduce(jnp.add, [jnp.sum(p, axis=-1, keepdims=True) for p in probs])
        acc = functools.reduce(jnp.add, [_dot(p.astype(BF16), v) for p, v in zip(probs, values)])
        o_s = acc / jnp.maximum(l, 1e-30)
        gates = _sigmoid(gl_ref[g] + gb_ref[g])
        o_ref[g] = gates[:, 0:1] * oc_ref[g] + gates[:, 1:2] * o_s + gates[:, 2:3] * o_w[g]


def _slc_decode(sel_flat, page_table, q8, o_c, gate_l, gate_b, slc_cache, bias_sd, new_k, new_v,
                win_cache, new_wk, new_wv, bias_wd, t_pos):
    nd = q8.shape[0]
    n_past = page_table.shape[1] * (PAGE // BLK)
    win_rows = win_cache.shape[0] // nd

    def per_b(b, sel, pt):
        return (b, 0, 0, 0)

    def whole(shape):
        return pl.BlockSpec(shape, lambda b, sel, pt: (0,) * len(shape), pipeline_mode=pl.Buffered(1))

    b_spec = pl.BlockSpec((None, NSA_G, 8, DH), per_b)
    grid_spec = pltpu.PrefetchScalarGridSpec(
        num_scalar_prefetch=2,
        grid=(nd,),
        in_specs=[
            b_spec, b_spec, b_spec, whole(gate_b.shape),
            pl.BlockSpec(memory_space=pl.ANY),
            whole(bias_sd.shape),
            b_spec, b_spec,
            pl.BlockSpec((win_rows, DH), lambda b, sel, pt: (b, 0)),
            b_spec, b_spec,
            whole(bias_wd.shape),
        ],
        out_specs=b_spec,
        scratch_shapes=[pltpu.VMEM((NSA_G * TOPN, HALF_ROWS, DH), F32), pltpu.SemaphoreType.DMA(())],
    )
    return pl.pallas_call(
        functools.partial(_slc_decode_kernel, t_pos=t_pos, n_past=n_past),
        out_shape=jax.ShapeDtypeStruct((nd, NSA_G, 8, DH), F32),
        grid_spec=grid_spec,
        compiler_params=_cparams(("arbitrary",)),
        name="slc_decode",
    )(sel_flat, page_table, q8, o_c, gate_l, gate_b, slc_cache, bias_sd, new_k, new_v,
      win_cache, new_wk, new_wv, bias_wd)


def _t5_bucket(dist):
    n = jnp.maximum(dist, 0)
    nf = jnp.maximum(n, REL_MAX_EXACT).astype(F32)
    large = REL_MAX_EXACT + (jnp.log(nf / REL_MAX_EXACT) / math.log(REL_MAX_DIST / REL_MAX_EXACT)
                             * (REL_BUCKETS - REL_MAX_EXACT)).astype(jnp.int32)
    return jnp.where(n < REL_MAX_EXACT, n, jnp.minimum(large, REL_BUCKETS - 1))


def _bias_table(rel_bias, dist):
    onehot = (_t5_bucket(dist)[..., None] == jnp.arange(REL_BUCKETS)).astype(F32)
    b = jnp.einsum("...k,kh->...h", onehot, rel_bias, precision=HI)
    b = jnp.moveaxis(b, -1, 0)
    return b.reshape((NSA_G, NSA_R) + dist.shape)


def _prompt_bias_tables(rel_bias):
    i = jnp.arange(QBLK)[:, None]
    rel = jnp.arange(LANE)[None, :]
    bct = _bias_table(rel_bias, BLK * (rel - 1) + i + 1)
    bsn = _bias_table(rel_bias, SLC_PAD + i - jnp.arange(SLC_NEAR)[None, :])
    bw = _bias_table(rel_bias, WIN_PAD + i - jnp.arange(WIN_KEYS)[None, :])
    return bct, bsn, bw


def _pad_rows(a, rows, axis):
    pad = [(0, 0)] * a.ndim
    pad[axis] = (0, rows - a.shape[axis])
    return jnp.pad(a, pad)


def _decode_bias_tables(rel_bias, t_pos, n_past, n_win):
    n = jnp.arange(SEL_LANES)
    bias_cd = _pad_rows(_bias_table(rel_bias, t_pos - (n * BLK + BLK - 1)), 8, 1)
    tok = jnp.arange(n_past + 1)[:, None] * BLK + jnp.arange(BLK)[None, :]
    bias_sd = _pad_rows(jnp.swapaxes(_bias_table(rel_bias, t_pos - tok), 1, 2), 8, 2)
    c = jnp.arange(n_win + LANE)
    bias_wd = _pad_rows(_bias_table(rel_bias, n_win - c), 8, 1)
    return bias_cd, bias_sd, bias_wd


def kernel(x_prompt, x_sample, mem_prompt, cache_cmp_kv, cache_slc_kv, state_win_kv, state_gla, cache_mem_kv,
           page_table, norm_g, w_ffn_gate, w_ffn_up, w_ffn_down, w_in_gla, w_in_nsa, w_out, mem_norm_g, w_mem_kv,
           w_gla_a2, b_gla_a, gla_onorm_g, nsa_gate_b, cmp_pe, cmp_w1, cmp_b1, cmp_w2, rel_bias):
    nb, seq, _ = x_prompt.shape
    nd = x_sample.shape[0]
    depth = norm_g.shape[0]
    n_pages = page_table.shape[1]
    past_len = n_pages * PAGE
    n_past = past_len // BLK
    n_win = state_win_kv.shape[2]
    sr = SAMPLE_ROWS

    xp = x_prompt.reshape(nb * seq, D_MODEL)
    xs = _pad_rows(x_sample.reshape(nd, D_MODEL), sr, 0)
    mem_x = mem_prompt.reshape(nb * MEM_LEN, D_MODEL)

    wg, wu, wd = w_ffn_gate.astype(BF16), w_ffn_up.astype(BF16), w_ffn_down.astype(BF16)
    w_o = w_out.astype(BF16)
    w_mkv = w_mem_kv.astype(BF16)
    qkvr = 2 * GLA_HEADS * GLA_DK + 2 * TOK_W
    w_gla = jnp.concatenate(
        [w_in_gla[..., :qkvr], w_in_gla[..., qkvr + GLA_RANK:], w_in_gla[..., qkvr:qkvr + GLA_RANK],
         jnp.zeros(w_in_gla.shape[:2] + (GLA_N - w_in_gla.shape[2],), F32)], axis=-1).astype(BF16)
    n_gate = 3 * NSA_HEADS
    w_nsa = jnp.concatenate(
        [w_in_nsa[..., :TOK_W], w_in_nsa[..., TOK_W + 6 * KV_W + n_gate:], w_in_nsa[..., TOK_W:TOK_W + 6 * KV_W + n_gate],
         jnp.zeros(w_in_nsa.shape[:2] + (NSA_N - w_in_nsa.shape[2],), F32)], axis=-1).astype(BF16)
    wa_pad = _pad_rows(w_gla_a2, LANE, 1)
    w1 = cmp_w1.astype(BF16)
    w2 = cmp_w2.astype(BF16)

    def ffn(x, i, j, tm):
        return _ffn_half(x, norm_g[i, 4 * j][None], wg, wu, wd, norm_g[i, 4 * j + 1][None], i, j, tm)

    def every(a, step):
        return a.reshape(nd, step, a.shape[-1])[:, 0]

    def per_seq(a):
        return _pad_rows(a[:, None, :], sr, 1).reshape(nd * sr, a.shape[-1])

    outs = dict(gla_p=[], gla_s=[], cmp_p=[], cmp_s=[], slc_p=[], slc_s=[], win_p=[], win_s=[], mem_p=[])
    for i in range(depth):
        li = i // 2
        mem_kv_p = _norm_matmul(mem_x, mem_norm_g[i][None], w_mkv, i, 512, 512)
        outs["mem_p"].append(mem_kv_p.reshape(nb, MEM_LEN, 2, N_MEM_HEADS, MEM_HEAD_DIM))
        mem_kv_p = mem_kv_p.reshape(nb, MEM_LEN, 2 * MEM_W)
        mem_kv_s = cache_mem_kv[i].reshape(nd, MEM_LEN, 2 * MEM_W)
        xp = ffn(xp, i, 0, 512)
        xs = ffn(xs, i, 0, sr)
        g_mix = norm_g[i, 2][None]
        if i % 2 == 0:
            proj_p = _norm_matmul(xp, g_mix, w_gla, li, 1024, 768)
            proj_s = _norm_matmul(xs, g_mix, w_gla, li, sr, 768)
            mem_col = (qkvr) // MEM_W
            b_a = b_gla_a[li][None]
            gn = gla_onorm_g[li][None]
            s0 = jnp.zeros((nb, GLA_HEADS, GLA_DK, GLA_DV), F32)
            tok_p, sp = _gla(proj_p, s0, wa_pad[li], b_a, gn, seq, 512, GLA_CHUNK, seq)
            tok_s, ss = _gla(per_seq(proj_s[:nd]), state_gla[li], wa_pad[li], b_a, gn, sr, sr, sr, 1)
            tok_s = _pad_rows(every(tok_s, sr), sr, 0)
            outs["gla_p"].append(sp)
            outs["gla_s"].append(ss)
        else:
            proj_p = _norm_matmul(xp, g_mix, w_nsa, li, 1024, 768)
            proj_s = _norm_matmul(xs, g_mix, w_nsa, li, sr, 768)
            mem_col = NSA_MEM_COL // MEM_W
            b1 = cmp_b1[li][:, None, :]
            gate_b = nsa_gate_b[li]
            bct, bsn, bw = _prompt_bias_tables(rel_bias)
            proj3 = proj_p.reshape(nb, seq, NSA_N)
            cmp_kv = _cmp_prompt(proj3, cmp_pe[li], w1[li], b1, w2[li])
            tok_p = _nsa_prompt(proj_p, cmp_kv, rel_bias, _pad_rows(gate_b[None], LANE, 1), bct, bsn, bw, nb, seq)
            kv_p = proj3[:, :, NSA_KV_COL:NSA_GATE_COL].reshape(nb, seq, 3, 2, NSA_G, DH)
            outs["cmp_p"].append(kv_p[:, :, 0].reshape(nb, seq // PAGE, PAGE, 2, NSA_G, DH))
            outs["slc_p"].append(kv_p[:, :, 1].reshape(nb, seq // PAGE, PAGE, 2, NSA_G, DH))
            outs["win_p"].append(kv_p[:, seq - n_win:, 2])
            t_pos = past_len
            bias_cd, bias_sd, bias_wd = _decode_bias_tables(rel_bias, t_pos, n_past, n_win)
            kv_s = proj_s[:nd, NSA_KV_COL:NSA_GATE_COL].reshape(nd, 3, 2, NSA_G, DH)
            outs["cmp_s"].append(kv_s[:, None, 0])
            outs["slc_s"].append(kv_s[:, None, 1])
            outs["win_s"].append(jnp.concatenate([state_win_kv[li][:, 1:], kv_s[:, None, 2]], axis=1))
            q8 = _pad_rows(proj_s[:nd, :TOK_W].reshape(nd, NSA_G, NSA_R, DH), 8, 2)
            new_cmp = _pad_rows(jnp.moveaxis(kv_s[:, 0], 1, 0).reshape(2, nd * NSA_G, DH), 32, 1)
            cmp_kv_s, cmp_last = _cmp_decode(page_table, _linear_cache(cache_cmp_kv[li]), new_cmp,
                                             cmp_pe[li], w1[li], b1, w2[li])
            o_c, sel = _sel_decode(q8, cmp_kv_s, cmp_last, bias_cd, t_pos)
            sel_flat = sel[:, :, 0, :TOPN].reshape(-1)

            def row8(a):
                return _pad_rows(a[:, :, None, :], 8, 2)

            gate_l = _pad_rows(_pad_rows(proj_s[:nd, NSA_GATE_COL:NSA_GATE_COL + n_gate].reshape(nd, NSA_G, NSA_R, 3),
                                         8, 2), LANE, 3)
            gate_b8 = _pad_rows(_pad_rows(gate_b.reshape(NSA_G, NSA_R, 3), 8, 1), LANE, 2)
            tok_s = _slc_decode(sel_flat, page_table, q8, o_c, gate_l, gate_b8,
                                _linear_cache(cache_slc_kv[li]), bias_sd,
                                row8(kv_s[:, 1, 0]), row8(kv_s[:, 1, 1]),
                                _linear_cache(state_win_kv[li]),
                                row8(kv_s[:, 2, 0]), row8(kv_s[:, 2, 1]), bias_wd, t_pos)
            tok_s = _pad_rows(tok_s[:, :, :NSA_R].reshape(nd, TOK_W), sr, 0).astype(BF16)
        mem_o_p = _mem_attn(proj_p, mem_col, mem_kv_p, seq, 512)
        q_s = jnp.broadcast_to(proj_s[:nd, None, mem_col * MEM_W:(mem_col + 1) * MEM_W],
                               (nd, sr, MEM_W)).reshape(nd * sr, MEM_W)
        mem_o_s = _pad_rows(every(_mem_attn(q_s, 0, mem_kv_s, sr, sr), sr), sr, 0)
        xp = _out_proj(xp, tok_p, mem_o_p, w_o, i, norm_g[i, 3][None], 256)
        xs = _out_proj(xs, tok_s, mem_o_s, w_o, i, norm_g[i, 3][None], sr)
        xp = ffn(xp, i, 1, 512)
        xs = ffn(xs, i, 1, sr)

    y_prompt = xp.reshape(nb, seq, D_MODEL)
    y_sample = xs[:nd].reshape(nd, 1, D_MODEL)
    st = lambda k: jnp.stack(outs[k])
    return (y_prompt, y_sample, st("gla_p"), st("cmp_p"), st("slc_p"), st("win_p"), st("mem_p"),
            st("gla_s"), st("cmp_s"), st("slc_s"), st("win_s"))
```

```python
import functools
import math

import jax
import jax.numpy as jnp
from jax import lax
from jax.experimental import pallas as pl
from jax.experimental.pallas import tpu as pltpu

F32 = jnp.float32
BF16 = jnp.bfloat16
HI = lax.Precision.HIGHEST

D_MODEL = 2048
D_FF = 5632
EPS = 1e-6
MEM_LEN = 256
N_MEM_HEADS = 4
MEM_HEAD_DIM = 128
MEM_W = N_MEM_HEADS * MEM_HEAD_DIM
TOK_W = D_MODEL - MEM_W
GLA_HEADS = 4
GLA_DV = TOK_W // GLA_HEADS
GLA_DK = GLA_DV // 2
GLA_RANK = 16
GLA_TAU = 16.0
GLA_CHUNK = 64
GLA_PAIR_W = 2 * GLA_DK
DH = 128
NSA_HEADS = TOK_W // DH
NSA_G = 3
NSA_R = NSA_HEADS // NSA_G
BLK = 64
TOPN = 16
WINDOW = 512
CMP_HID = 256
QBLK = 64
KV_W = NSA_G * DH
REL_BUCKETS = 32
REL_MAX_EXACT = 16
REL_MAX_DIST = 128
PAGE = 128
LANE = 128
HALF_LANE = LANE // 2
BLK_SHIFT = BLK.bit_length() - 1
VMEM_LIMIT = 56 * 1024 * 1024

GLA_A_COL = 2 * GLA_HEADS * GLA_DK + 2 * TOK_W + MEM_W
GLA_N = 5376
NSA_MEM_COL = TOK_W
NSA_KV_COL = NSA_MEM_COL + MEM_W
NSA_GATE_COL = NSA_KV_COL + 6 * KV_W
NSA_N = 4608
SAMPLE_ROWS = 16
SLC_PAD = 192
WIN_PAD = 576
WIN_KEYS = WIN_PAD + QBLK
SLC_NEAR = SLC_PAD + QBLK
FAR_CHUNK = 1024
NEG = -1e30
M_FLOOR = -1e29


def _cparams(sem, vmem=VMEM_LIMIT):
    return pltpu.CompilerParams(dimension_semantics=sem, vmem_limit_bytes=vmem)


def _sigmoid(x):
    return 1.0 / (1.0 + jnp.exp(-x))


def _rms(x, g):
    ms = jnp.mean(x * x, axis=-1, keepdims=True)
    return x * lax.rsqrt(ms + EPS) * g


def _nt(a, b, precision=None):
    return lax.dot_general(a, b, (((1,), (1,)), ((), ())), precision=precision,
                           preferred_element_type=F32)


def _tn(a, b, precision=None):
    return lax.dot_general(a, b, (((0,), (0,)), ((), ())), precision=precision,
                           preferred_element_type=F32)


def _dot(a, b, precision=None):
    return jnp.dot(a, b, precision=precision, preferred_element_type=F32)


def _split3(x):
    hi = x.astype(BF16)
    r1 = x - hi.astype(F32)
    mid = r1.astype(BF16)
    lo = (r1 - mid.astype(F32)).astype(BF16)
    return hi, mid, lo


def _ffn_kernel(x_ref, g1_ref, wg_ref, wu_ref, wd_ref, g2_ref, o_ref, xn_ref, acc_ref):
    j = pl.program_id(1)

    @pl.when(j == 0)
    def _():
        xn_ref[...] = _rms(x_ref[...], g1_ref[...]).astype(BF16)
        acc_ref[...] = jnp.zeros_like(acc_ref)

    xn = xn_ref[...]
    gate = _dot(xn, wg_ref[...])
    up = _dot(xn, wu_ref[...])
    h = (gate * _sigmoid(gate) * up).astype(BF16)
    acc_ref[...] += _dot(h, wd_ref[...])

    @pl.when(j == pl.num_programs(1) - 1)
    def _():
        o_ref[...] = x_ref[...] + 0.5 * _rms(acc_ref[...], g2_ref[...])


def _ffn_half(x, g1, wg, wu, wd, g2, layer, half, tm, tf=512):
    m = x.shape[0]
    return pl.pallas_call(
        _ffn_kernel,
        out_shape=jax.ShapeDtypeStruct((m, D_MODEL), F32),
        grid=(m // tm, D_FF // tf),
        in_specs=[
            pl.BlockSpec((tm, D_MODEL), lambda i, j: (i, 0)),
            pl.BlockSpec((1, D_MODEL), lambda i, j: (0, 0)),
            pl.BlockSpec((None, None, D_MODEL, tf), lambda i, j: (layer, half, 0, j)),
            pl.BlockSpec((None, None, D_MODEL, tf), lambda i, j: (layer, half, 0, j)),
            pl.BlockSpec((None, None, tf, D_MODEL), lambda i, j: (layer, half, j, 0)),
            pl.BlockSpec((1, D_MODEL), lambda i, j: (0, 0)),
        ],
        out_specs=pl.BlockSpec((tm, D_MODEL), lambda i, j: (i, 0)),
        scratch_shapes=[pltpu.VMEM((tm, D_MODEL), BF16), pltpu.VMEM((tm, D_MODEL), F32)],
        compiler_params=_cparams(("parallel", "arbitrary")),
        name="ffn_half",
    )(x, g1, wg, wu, wd, g2)


def _norm_matmul_kernel(x_ref, g_ref, w_ref, o_ref, xn_ref):
    @pl.when(pl.program_id(1) == 0)
    def _():
        xn_ref[...] = _rms(x_ref[...], g_ref[...]).astype(BF16)

    o_ref[...] = _dot(xn_ref[...], w_ref[...])


def _norm_matmul(x, g, w, layer, tm, tn):
    m, n = x.shape[0], w.shape[2]
    return pl.pallas_call(
        _norm_matmul_kernel,
        out_shape=jax.ShapeDtypeStruct((m, n), F32),
        grid=(m // tm, n // tn),
        in_specs=[
            pl.BlockSpec((tm, D_MODEL), lambda i, j: (i, 0)),
            pl.BlockSpec((1, D_MODEL), lambda i, j: (0, 0)),
            pl.BlockSpec((None, D_MODEL, tn), lambda i, j: (layer, 0, j)),
        ],
        out_specs=pl.BlockSpec((tm, tn), lambda i, j: (i, j)),
        scratch_shapes=[pltpu.VMEM((tm, D_MODEL), BF16)],
        compiler_params=_cparams(("parallel", "arbitrary")),
        name="norm_matmul",
    )(x, g, w)


def _out_proj_kernel(x_ref, tok_ref, mem_ref, wt_ref, wm_ref, g_ref, o_ref):
    y = _dot(tok_ref[...], wt_ref[...]) + _dot(mem_ref[...], wm_ref[...])
    o_ref[...] = x_ref[...] + _rms(y, g_ref[...])


def _out_proj(x, tok, mem_o, w_o, layer, g, tm):
    m = x.shape[0]
    return pl.pallas_call(
        _out_proj_kernel,
        out_shape=jax.ShapeDtypeStruct((m, D_MODEL), F32),
        grid=(m // tm,),
        in_specs=[
            pl.BlockSpec((tm, D_MODEL), lambda i: (i, 0)),
            pl.BlockSpec((tm, TOK_W), lambda i: (i, 0)),
            pl.BlockSpec((tm, MEM_W), lambda i: (i, 0)),
            pl.BlockSpec((None, TOK_W, D_MODEL), lambda i: (layer, 0, 0), pipeline_mode=pl.Buffered(1)),
            pl.BlockSpec((None, MEM_W, D_MODEL), lambda i: (layer, TOK_W // MEM_W, 0), pipeline_mode=pl.Buffered(1)),
            pl.BlockSpec((1, D_MODEL), lambda i: (0, 0)),
        ],
        out_specs=pl.BlockSpec((tm, D_MODEL), lambda i: (i, 0)),
        compiler_params=_cparams(("parallel",)),
        name="out_proj",
    )(x, tok, mem_o, w_o, w_o, g)


def _mem_attn_kernel(q_ref, kv_ref, o_ref):
    for h in range(N_MEM_HEADS):
        q = (q_ref[:, h * DH:(h + 1) * DH] * (MEM_HEAD_DIM ** -0.5)).astype(BF16)
        k = kv_ref[:, h * DH:(h + 1) * DH].astype(BF16)
        v = kv_ref[:, MEM_W + h * DH:MEM_W + (h + 1) * DH].astype(BF16)
        s = _nt(q, k)
        e = jnp.exp(s - jnp.max(s, axis=-1, keepdims=True))
        p = e / jnp.sum(e, axis=-1, keepdims=True)
        o_ref[:, h * DH:(h + 1) * DH] = _dot(p.astype(BF16), v).astype(o_ref.dtype)


def _mem_attn(q_arr, q_col_block, mem_kv, rows_per_batch, tm):
    nb = mem_kv.shape[0]
    per = rows_per_batch // tm
    return pl.pallas_call(
        _mem_attn_kernel,
        out_shape=jax.ShapeDtypeStruct((nb * rows_per_batch, MEM_W), BF16),
        grid=(nb, per),
        in_specs=[
            pl.BlockSpec((tm, MEM_W), lambda b, i: (b * per + i, q_col_block)),
            pl.BlockSpec((None, MEM_LEN, 2 * MEM_W), lambda b, i: (b, 0, 0)),
        ],
        out_specs=pl.BlockSpec((tm, MEM_W), lambda b, i: (b * per + i, 0)),
        compiler_params=_cparams(("parallel", "parallel")),
        name="mem_attn",
    )(q_arr, mem_kv)


def _gla_kernel(q_ref, k_ref, v0_ref, v1_ref, r0_ref, r1_ref, a_ref, s0_ref, wa_ref, ba_ref, gn_ref,
                tok_ref, s_out_ref, s_ref, la_ref, *, chunk, n_valid):
    l = pl.program_id(2)
    tl = q_ref.shape[0]

    @pl.when(l == 0)
    def _():
        s_ref[...] = s0_ref[...].reshape(GLA_PAIR_W, GLA_DV).T

    lane = lax.broadcasted_iota(jnp.int32, (1, GLA_PAIR_W), 1)
    head_mask = [(lane < GLA_DK).astype(F32), (lane >= GLA_DK).astype(F32)]
    ti = lax.broadcasted_iota(jnp.int32, (chunk, chunk), 0)
    si = lax.broadcasted_iota(jnp.int32, (chunk, chunk), 1)
    causal = si <= ti
    tri = jnp.where(causal, 1.0, 0.0).astype(BF16)
    v_refs = (v0_ref, v1_ref)
    r_refs = (r0_ref, r1_ref)

    a_hi, a_lo, _ = _split3(a_ref[...])
    w_hi, w_lo, _ = _split3(wa_ref[...])
    z = _dot(a_hi, w_hi) + _dot(a_lo, w_hi) + _dot(a_hi, w_lo) + ba_ref[...]
    la_all = -(jnp.maximum(-z, 0.0) + jnp.log1p(jnp.exp(-jnp.abs(z)))) / GLA_TAU
    pos = l * tl + lax.broadcasted_iota(jnp.int32, (tl, 1), 0)
    la_ref[...] = jnp.where(pos < n_valid, la_all, 0.0)

    def step(ci, carry):
        r0 = pl.multiple_of(ci * chunk, chunk)
        rows = pl.ds(r0, chunk)
        b = functools.reduce(jnp.add, [_dot(tri, piece) for piece in _split3(la_ref[rows, :])])
        bl = b[chunk - 1:chunk, :]
        q = q_ref[rows, :] * (GLA_DK ** -0.5)
        k = k_ref[rows, :]
        qe = q * jnp.exp(b)
        ke = (k * jnp.exp(-b)).astype(BF16)
        kd = k * jnp.exp(bl - b)
        st_old = s_ref[...]
        st_bf = st_old.astype(BF16)
        upd = None
        for h in range(2):
            v = v_refs[h][rows, :].astype(BF16)
            qm = (qe * head_mask[h]).astype(BF16)
            att = jnp.where(causal, _nt(qm, ke), 0.0)
            o = _nt(qm, st_bf) + _dot(att.astype(BF16), v)
            o = _rms(o, gn_ref[...])
            r = r_refs[h][rows, :]
            tok_ref[rows, h * GLA_DV:(h + 1) * GLA_DV] = (o * (r * _sigmoid(r))).astype(tok_ref.dtype)
            u = _tn(v, (kd * head_mask[h]).astype(BF16))
            upd = u if upd is None else upd + u
        s_ref[...] = jnp.exp(bl) * st_old + upd
        return carry

    lax.fori_loop(0, tl // chunk, step, 0)

    @pl.when(l == pl.num_programs(2) - 1)
    def _():
        s_out_ref[...] = s_ref[...].T.reshape(2, GLA_DK, GLA_DV)


def _gla(proj, s0, wa_pad, b_a, gn, seq, tl, chunk, n_valid):
    nb = s0.shape[0]
    per = seq // tl
    w = GLA_PAIR_W

    def col(base, stride=1):
        return lambda b, p, l: (b * per + l, base + stride * p)

    return pl.pallas_call(
        functools.partial(_gla_kernel, chunk=chunk, n_valid=n_valid),
        out_shape=(jax.ShapeDtypeStruct((nb * seq, TOK_W), BF16),
                   jax.ShapeDtypeStruct((nb, GLA_HEADS, GLA_DK, GLA_DV), F32)),
        grid=(nb, 2, per),
        in_specs=[
            pl.BlockSpec((tl, w), col(0)),
            pl.BlockSpec((tl, w), col(2)),
            pl.BlockSpec((tl, w), col(4, 2)),
            pl.BlockSpec((tl, w), col(5, 2)),
            pl.BlockSpec((tl, w), col(8, 2)),
            pl.BlockSpec((tl, w), col(9, 2)),
            pl.BlockSpec((tl, LANE), lambda b, p, l: (b * per + l, GLA_A_COL // LANE)),
            pl.BlockSpec((None, 2, GLA_DK, GLA_DV), lambda b, p, l: (b, p, 0, 0)),
            pl.BlockSpec((LANE, w), lambda b, p, l: (0, p)),
            pl.BlockSpec((1, w), lambda b, p, l: (0, p)),
            pl.BlockSpec((1, GLA_DV), lambda b, p, l: (0, 0)),
        ],
        out_specs=(pl.BlockSpec((tl, 2 * GLA_DV), lambda b, p, l: (b * per + l, p)),
                   pl.BlockSpec((None, 2, GLA_DK, GLA_DV), lambda b, p, l: (b, p, 0, 0))),
        scratch_shapes=[pltpu.VMEM((GLA_DV, GLA_PAIR_W), F32), pltpu.VMEM((tl, GLA_PAIR_W), F32)],
        compiler_params=_cparams(("parallel", "parallel", "arbitrary")),
        name="gla",
    )(proj, proj, proj, proj, proj, proj, proj, s0, wa_pad, b_a, gn)


def _masked_softmax(s, valid):
    s = jnp.where(valid, s, -jnp.inf)
    m = jnp.max(s, axis=-1, keepdims=True)
    m = jnp.where(m > -jnp.inf, m, 0.0)
    e = jnp.exp(s - m)
    return e / jnp.maximum(jnp.sum(e, axis=-1, keepdims=True), 1e-30)


def _online_update(state, s, v):
    m, l, acc = state
    r, nq, w = s.shape
    m_new = jnp.maximum(m, jnp.max(s, axis=-1, keepdims=True))
    alpha = jnp.exp(m - m_new)
    p = jnp.exp(s - m_new)
    l = alpha * l + jnp.sum(p, axis=-1, keepdims=True)
    pv = _dot(p.reshape(r * nq, w).astype(BF16), v).reshape(r, nq, DH)
    return m_new, l, alpha * acc + pv


def _compress_tail(xflat, w1, b1, w2):
    h = _dot(xflat, w1) + b1
    h = h * _sigmoid(h)
    return _dot(h.astype(BF16), w2)


def _cmp_prompt_kernel(x0_ref, x1_ref, x2_ref, pe_ref, w1_ref, b1_ref, w2_ref, o_ref, xflat_ref):
    x_refs = (x0_ref, x1_ref, x2_ref)
    nb, seq = x0_ref.shape[0], x0_ref.shape[1]
    nblk = seq // BLK
    for j in range(BLK):
        pe_j = pe_ref[j:j + 1, :]
        for b in range(nb):
            for g in range(NSA_G):
                xj = x_refs[g][b, pl.ds(j, nblk, stride=BLK), :]
                row = (b * NSA_G + g) * nblk
                xflat_ref[row:row + nblk, j * DH:(j + 1) * DH] = (xj + pe_j).astype(BF16)
    out = _compress_tail(xflat_ref[...], w1_ref[...], b1_ref[...], w2_ref[...])
    o_ref[...] = out.reshape(nb, NSA_G, nblk, DH)


def _cmp_prompt(proj3, pe, w1, b1, w2):
    nb, seq, _ = proj3.shape
    nblk = seq // BLK
    return pl.pallas_call(
        _cmp_prompt_kernel,
        out_shape=jax.ShapeDtypeStruct((2, nb, NSA_G, nblk, DH), F32),
        grid=(2,),
        in_specs=[
            pl.BlockSpec((nb, seq, DH), lambda kv: (0, 0, NSA_KV_COL // DH + NSA_G * kv)),
            pl.BlockSpec((nb, seq, DH), lambda kv: (0, 0, NSA_KV_COL // DH + NSA_G * kv + 1)),
            pl.BlockSpec((nb, seq, DH), lambda kv: (0, 0, NSA_KV_COL // DH + NSA_G * kv + 2)),
            pl.BlockSpec((None, BLK, DH), lambda kv: (kv, 0, 0)),
            pl.BlockSpec((None, BLK * DH, CMP_HID), lambda kv: (kv, 0, 0)),
            pl.BlockSpec((None, 1, CMP_HID), lambda kv: (kv, 0, 0)),
            pl.BlockSpec((None, CMP_HID, DH), lambda kv: (kv, 0, 0)),
        ],
        out_specs=pl.BlockSpec((None, nb, NSA_G, nblk, DH), lambda kv: (kv, 0, 0, 0, 0)),
        scratch_shapes=[pltpu.VMEM((nb * NSA_G * nblk, BLK * DH), BF16)],
        compiler_params=_cparams(("arbitrary",)),
        name="cmp_prompt",
    )(proj3, proj3, proj3, pe, w1, b1, w2)


def _nsa_prompt_kernel(q_ref, gl_ref, gb_ref, kc_ref, vc_ref, ks_ref, vs_ref, kw_ref, vw_ref,
                       bct_ref, bsn_ref, bw_ref, o_ref, ksb, vsb, kwb, vwb):
    g = pl.program_id(1)
    qi = pl.program_id(2)
    seq = ks_ref.shape[0]
    nblk = seq // BLK
    R = NSA_R

    @pl.when(qi == 0)
    def _():
        for src, dst, pad in ((vs_ref, vsb, SLC_PAD), (vw_ref, vwb, WIN_PAD)):
            dst[0:pad, :] = jnp.zeros((pad, DH), BF16)
            dst[pad:pad + seq, :] = src[...].astype(BF16)
        for src, dst, pad, per_block in ((ks_ref, ksb, SLC_PAD, True), (kw_ref, kwb, WIN_PAD, False)):
            dst[0:pad, 0:DH] = jnp.zeros((pad, DH), BF16)
            dst[pad:pad + seq, 0:DH] = src[...].astype(BF16)
            pos = lax.broadcasted_iota(jnp.int32, (pad + seq, 1), 0) - pad
            feat = lax.broadcasted_iota(jnp.int32, (1, DH), 1)
            masked = (feat == HALF_LANE) & (pos < 0)
            if per_block:
                masked = masked | ((pos >= 0) & (lax.shift_right_arithmetic(pos, BLK_SHIFT) == feat))
            dst[:, DH:2 * DH] = jnp.where(masked, NEG, 0.0).astype(BF16)

    q = q_ref[...] * (DH ** -0.5)
    q_all = jnp.concatenate([q[:, r * DH:(r + 1) * DH] for r in range(R)], axis=0)
    q_bf = q_all.astype(BF16)
    i_col = lax.broadcasted_iota(jnp.int32, (QBLK, 1), 0)
    t_col = qi * QBLK + i_col
    n_row = lax.broadcasted_iota(jnp.int32, (1, nblk), 1)
    row0 = pl.multiple_of(qi * QBLK, QBLK)
    lane2 = lax.broadcasted_iota(jnp.int32, (1, LANE), 1)

    s_c = _nt(q_all, kc_ref[...], HI).reshape(R, QBLK, nblk)
    rel = qi - n_row
    bias_c = []
    for r in range(R):
        tab = bct_ref[r]
        bias_c.append(jnp.where(rel == 0, tab[:, 0:1],
                      jnp.where(rel == 1, tab[:, 1:2],
                      jnp.where(rel == 2, tab[:, 2:3], tab[:, 3:4]))))
    s_c = s_c + jnp.stack(bias_c, axis=0)
    valid_c = (t_col - (n_row * BLK + (BLK - 1))) >= 0
    p_c = _masked_softmax(s_c, valid_c[None])
    o_c = _dot(p_c.reshape(R * QBLK, nblk).astype(BF16), vc_ref[...].astype(BF16))

    imp = jnp.sum(p_c, axis=0)
    forced = (n_row == 0) | (n_row == qi) | (n_row == qi - 1)
    future = n_row * BLK > t_col
    score = jnp.where(forced, jnp.inf, jnp.where(future, -jnp.inf, imp))
    if nblk < HALF_LANE:
        score = jnp.concatenate([score, jnp.full((QBLK, HALF_LANE - nblk), -jnp.inf, F32)], axis=1)
    half = HALF_LANE // 2
    score2 = jnp.concatenate([score, score], axis=1)
    left = lane2 < HALF_LANE
    n2 = jnp.where(left, lane2, lane2 - HALF_LANE)
    rank2 = jnp.zeros((QBLK, LANE), F32)
    for i in range(half):
        col = jnp.where(left, score[:, i:i + 1], score[:, i + half:i + half + 1])
        wins_tie = jnp.where(n2 > jnp.where(left, i, i + half), 1.0, 0.0)
        rank2 = rank2 + jnp.where(col > score2, 1.0, jnp.where(col == score2, wins_tie, 0.0))
    rank = rank2 + jnp.concatenate([rank2[:, HALF_LANE:], rank2[:, :HALF_LANE]], axis=1)
    n_far = qi - 3
    dropped = rank >= float(min(TOPN, nblk))
    flag_all = jnp.where(left, jnp.where(dropped, 1.0, 0.0), jnp.where(lane2 == HALF_LANE, 1.0, 0.0))
    flag_far = jnp.where(left & (lane2 >= n_far), 1.0, flag_all)
    q_far = jnp.concatenate([q_bf, jnp.concatenate([flag_far.astype(BF16)] * R, axis=0)], axis=1)
    q_near = jnp.concatenate([q_bf, jnp.concatenate([flag_all.astype(BF16)] * R, axis=0)], axis=1)

    def far_body(kc_i, state):
        start = pl.multiple_of(SLC_PAD + kc_i * FAR_CHUNK, BLK)
        s = _nt(q_far, ksb[pl.ds(start, FAR_CHUNK), :]).reshape(R, QBLK, FAR_CHUNK)
        return _online_update(state, s, vsb[pl.ds(start, FAR_CHUNK), :])

    blk_per_chunk = FAR_CHUNK // BLK
    n_chunks = (jnp.maximum(n_far, 0) + (blk_per_chunk - 1)) // blk_per_chunk
    state = (jnp.full((R, QBLK, 1), M_FLOOR, F32), jnp.zeros((R, QBLK, 1), F32), jnp.zeros((R, QBLK, DH), F32))
    state = lax.fori_loop(0, n_chunks, far_body, state)

    s = _nt(q_near, ksb[pl.ds(row0, SLC_NEAR), :]).reshape(R, QBLK, SLC_NEAR) + bsn_ref[...]
    _, l_s, acc_s = _online_update(state, s, vsb[pl.ds(row0, SLC_NEAR), :])
    o_s = (acc_s / jnp.maximum(l_s, 1e-30)).reshape(R * QBLK, DH)

    s = _nt(q_near, kwb[pl.ds(row0, WIN_KEYS), :]).reshape(R, QBLK, WIN_KEYS) + bw_ref[...]
    e_w = jnp.exp(s - jnp.max(s, axis=-1, keepdims=True))
    p_w = e_w / jnp.maximum(jnp.sum(e_w, axis=-1, keepdims=True), 1e-30)
    o_w = _dot(p_w.reshape(R * QBLK, WIN_KEYS).astype(BF16), vwb[pl.ds(row0, WIN_KEYS), :])

    gates = _sigmoid(gl_ref[...] + gb_ref[...])
    src = lax.broadcasted_iota(jnp.int32, (LANE, LANE), 0)
    dst = lax.broadcasted_iota(jnp.int32, (LANE, LANE), 1)
    pick = jnp.where((src == g * (3 * R) + dst) & (dst < 3 * R), 1.0, 0.0)
    gsel = _dot(gates, pick, HI)
    for r in range(R):
        rows = slice(r * QBLK, (r + 1) * QBLK)
        o = (gsel[:, 3 * r:3 * r + 1] * o_c[rows] + gsel[:, 3 * r + 1:3 * r + 2] * o_s[rows]
             + gsel[:, 3 * r + 2:3 * r + 3] * o_w[rows])
        o_ref[:, r * DH:(r + 1) * DH] = o.astype(o_ref.dtype)


def _nsa_prompt(proj2, cmp_kv, gate_b_pad, bct, bsn, bw, nb, seq):
    nq = seq // QBLK
    nblk = seq // BLK
    assert nblk <= HALF_LANE, "mask features of the selected branch hold at most 64 key blocks"
    assert seq % FAR_CHUNK == 0
    proj3 = proj2.reshape(nb, seq, NSA_N)

    def kv_spec(col):
        return pl.BlockSpec((None, seq, DH), lambda b, g, qi: (b, 0, col // DH + g))

    return pl.pallas_call(
        _nsa_prompt_kernel,
        out_shape=jax.ShapeDtypeStruct((nb * seq, TOK_W), BF16),
        grid=(nb, NSA_G, nq),
        in_specs=[
            pl.BlockSpec((QBLK, NSA_R * DH), lambda b, g, qi: (b * nq + qi, g)),
            pl.BlockSpec((QBLK, LANE), lambda b, g, qi: (b * nq + qi, NSA_GATE_COL // LANE)),
            pl.BlockSpec((1, LANE), lambda b, g, qi: (0, 0)),
            pl.BlockSpec((None, None, None, nblk, DH), lambda b, g, qi: (0, b, g, 0, 0)),
            pl.BlockSpec((None, None, None, nblk, DH), lambda b, g, qi: (1, b, g, 0, 0)),
            kv_spec(NSA_KV_COL + 2 * KV_W), kv_spec(NSA_KV_COL + 3 * KV_W),
            kv_spec(NSA_KV_COL + 4 * KV_W), kv_spec(NSA_KV_COL + 5 * KV_W),
            pl.BlockSpec((None, NSA_R, QBLK, LANE), lambda b, g, qi: (g, 0, 0, 0)),
            pl.BlockSpec((None, NSA_R, QBLK, SLC_NEAR), lambda b, g, qi: (g, 0, 0, 0)),
            pl.BlockSpec((None, NSA_R, QBLK, WIN_KEYS), lambda b, g, qi: (g, 0, 0, 0)),
        ],
        out_specs=pl.BlockSpec((QBLK, NSA_R * DH), lambda b, g, qi: (b * nq + qi, g)),
        scratch_shapes=[pltpu.VMEM((SLC_PAD + seq, 2 * DH), BF16), pltpu.VMEM((SLC_PAD + seq, DH), BF16),
                        pltpu.VMEM((WIN_PAD + seq, 2 * DH), BF16), pltpu.VMEM((WIN_PAD + seq, DH), BF16)],
        compiler_params=_cparams(("parallel", "parallel", "arbitrary")),
        name="nsa_prompt",
    )(proj2, proj2, gate_b_pad, cmp_kv, cmp_kv, proj3, proj3, proj3, proj3, bct, bsn, bw)


SUB_PAGES = 8
RING = 4
ROW_W = 2 * NSA_G
PAGE_ROWS = PAGE * ROW_W


def _linear_cache(cache):
    return jnp.transpose(cache, (0, 1, 3, 2, 4)).reshape(-1, DH)


def _cmp_decode_kernel(pt_ref, cache_ref, new_ref, pe_ref, w1_ref, b1_ref, w2_ref, o_ref, last_ref,
                       buf, sem, xflat_ref, xlast_ref, *, n_sub):
    b = pl.program_id(0)
    total = pl.num_programs(0) * n_sub
    sub_blk = SUB_PAGES * PAGE // BLK
    seq_blk = n_sub * sub_blk

    def page_copy(s, p):
        page = pt_ref[s // n_sub, (s % n_sub) * SUB_PAGES + p]
        return pltpu.make_async_copy(cache_ref.at[pl.ds(pl.multiple_of(page * PAGE_ROWS, PAGE_ROWS), PAGE_ROWS), :],
                                     buf.at[s % RING, pl.ds(p * PAGE_ROWS, PAGE_ROWS), :], sem.at[s % RING])

    def start_sub(s):
        for p in range(SUB_PAGES):
            page_copy(s, p).start()

    @pl.when(b == 0)
    def _():
        for s in range(RING):
            start_sub(s)

    def body(i, c):
        s = b * n_sub + i
        for p in range(SUB_PAGES):
            page_copy(s, p).wait()
        slot = s % RING
        row0 = pl.multiple_of(i * sub_blk, sub_blk)
        for kv in range(2):
            for j in range(BLK):
                pe_j = pe_ref[kv, j:j + 1, :]
                for g in range(NSA_G):
                    xj = buf[slot, pl.ds(j * ROW_W + g * 2 + kv, sub_blk, stride=BLK * ROW_W), :]
                    xflat_ref[kv, pl.ds(g * seq_blk + row0, sub_blk), j * DH:(j + 1) * DH] = (xj + pe_j).astype(BF16)

        @pl.when(s + RING < total)
        def _():
            start_sub(s + RING)

        return c

    lax.fori_loop(0, n_sub, body, 0)
    for kv in range(2):
        out = _compress_tail(xflat_ref[kv], w1_ref[kv], b1_ref[kv], w2_ref[kv])
        o_ref[kv] = out.reshape(NSA_G, seq_blk, DH)

    @pl.when(b == 0)
    def _():
        rows = new_ref.shape[1]
        for kv in range(2):
            for j in range(BLK):
                pe_j = jnp.broadcast_to(pe_ref[kv, j:j + 1, :], (rows, DH))
                xj = new_ref[kv] + pe_j if j == 0 else pe_j
                xlast_ref[:, j * DH:(j + 1) * DH] = xj.astype(BF16)
            last_ref[kv] = _compress_tail(xlast_ref[...], w1_ref[kv], b1_ref[kv], w2_ref[kv])


def _cmp_decode(page_table, cache, new_rows, pe, w1, b1, w2):
    nd, n_pages = page_table.shape
    n_sub = n_pages // SUB_PAGES
    seq_blk = n_pages * PAGE // BLK
    rows = new_rows.shape[1]

    def whole(shape):
        return pl.BlockSpec(shape, lambda b, pt: (0,) * len(shape), pipeline_mode=pl.Buffered(1))

    grid_spec = pltpu.PrefetchScalarGridSpec(
        num_scalar_prefetch=1,
        grid=(nd,),
        in_specs=[
            pl.BlockSpec(memory_space=pl.ANY),
            whole(new_rows.shape), whole(pe.shape), whole(w1.shape), whole(b1.shape), whole(w2.shape),
        ],
        out_specs=(pl.BlockSpec((2, None, NSA_G, seq_blk, DH), lambda b, pt: (0, b, 0, 0, 0)),
                   pl.BlockSpec((2, rows, DH), lambda b, pt: (0, 0, 0))),
        scratch_shapes=[pltpu.VMEM((RING, SUB_PAGES * PAGE_ROWS, DH), F32),
                        pltpu.SemaphoreType.DMA((RING,)),
                        pltpu.VMEM((2, NSA_G * seq_blk, BLK * DH), BF16),
                        pltpu.VMEM((rows, BLK * DH), BF16)],
    )
    return pl.pallas_call(
        functools.partial(_cmp_decode_kernel, n_sub=n_sub),
        out_shape=(jax.ShapeDtypeStruct((2, nd, NSA_G, seq_blk, DH), F32),
                   jax.ShapeDtypeStruct((2, rows, DH), F32)),
        grid_spec=grid_spec,
        compiler_params=_cparams(("arbitrary",)),
        name="cmp_decode",
    )(page_table, cache, new_rows, pe, w1, b1, w2)


SEL_LANES = 384


def _sel_decode_kernel(q_ref, kc_ref, vc_ref, last_ref, bias_ref, oc_ref, sel_ref, *, t_pos):
    b = pl.program_id(0)
    n_past = kc_ref.shape[1]
    n_blocks = n_past + 1
    n_lane = lax.broadcasted_iota(jnp.int32, (1, SEL_LANES), 1)
    n_lane_f = n_lane.astype(F32)
    head_row = lax.broadcasted_iota(jnp.int32, (8, 1), 0) < NSA_R
    tb = t_pos // BLK
    for g in range(NSA_G):
        q = q_ref[g] * (DH ** -0.5)
        bias = bias_ref[g]
        s_p = _nt(q, kc_ref[g], HI) + bias[:, :n_past]
        row = b * NSA_G + g
        k_last = last_ref[0, pl.ds(row, 1), :]
        v_last = last_ref[1, pl.ds(row, 1), :]
        s_l = jnp.sum(q * k_last, axis=-1, keepdims=True) + bias[:, n_past:n_past + 1]
        valid_p = (t_pos - (n_lane[:, :n_past] * BLK + (BLK - 1))) >= 0
        valid_l = (t_pos - (n_past * BLK + (BLK - 1))) >= 0
        s_p = jnp.where(valid_p, s_p, -jnp.inf)
        s_l = jnp.where(valid_l, s_l, -jnp.inf)
        m = jnp.maximum(jnp.max(s_p, axis=-1, keepdims=True), s_l)
        m = jnp.where(m > -jnp.inf, m, 0.0)
        e_p = jnp.exp(s_p - m)
        e_l = jnp.exp(s_l - m)
        den = jnp.maximum(jnp.sum(e_p, axis=-1, keepdims=True) + e_l, 1e-30)
        p_p = e_p / den
        p_l = e_l / den
        oc_ref[g] = _dot(p_p, vc_ref[g], HI) + p_l * v_last
        imp_p = jnp.sum(jnp.where(head_row, p_p, 0.0), axis=0, keepdims=True)
        imp_l = jnp.sum(jnp.where(head_row, p_l, 0.0), axis=0, keepdims=True)
        imp = jnp.concatenate([imp_p, jnp.broadcast_to(imp_l, (1, SEL_LANES - n_past))], axis=1)
        forced = (n_lane == 0) | (n_lane == tb) | (n_lane == tb - 1)
        future = n_lane * BLK > t_pos
        score = jnp.where(forced, jnp.inf, jnp.where(future, -jnp.inf, imp))
        cand = n_lane < n_blocks
        sel = jnp.zeros((1, LANE), jnp.int32)
        k_lane = lax.broadcasted_iota(jnp.int32, (1, LANE), 1)
        for k in range(min(TOPN, n_blocks)):
            best = jnp.max(jnp.where(cand, score, -jnp.inf), axis=-1, keepdims=True)
            idx_f = jnp.min(jnp.where(cand & (score == best), n_lane_f, float(SEL_LANES)), axis=-1, keepdims=True)
            idx = idx_f.astype(jnp.int32)
            sel = jnp.where(k_lane == k, idx, sel)
            cand = cand & (n_lane != idx)
        sel_ref[g] = jnp.broadcast_to(sel, (8, LANE))


def _sel_decode(q8, cmp_kv, cmp_last, bias_cd, t_pos):
    nd = q8.shape[0]
    n_past = cmp_kv.shape[3]
    return pl.pallas_call(
        functools.partial(_sel_decode_kernel, t_pos=t_pos),
        out_shape=(jax.ShapeDtypeStruct((nd, NSA_G, 8, DH), F32),
                   jax.ShapeDtypeStruct((nd, NSA_G, 8, LANE), jnp.int32)),
        grid=(nd,),
        in_specs=[
            pl.BlockSpec((None, NSA_G, 8, DH), lambda b: (b, 0, 0, 0)),
            pl.BlockSpec((None, None, NSA_G, n_past, DH), lambda b: (0, b, 0, 0, 0)),
            pl.BlockSpec((None, None, NSA_G, n_past, DH), lambda b: (1, b, 0, 0, 0)),
            pl.BlockSpec(cmp_last.shape, lambda b: (0, 0, 0)),
            pl.BlockSpec(bias_cd.shape, lambda b: (0, 0, 0)),
        ],
        out_specs=(pl.BlockSpec((None, NSA_G, 8, DH), lambda b: (b, 0, 0, 0)),
                   pl.BlockSpec((None, NSA_G, 8, LANE), lambda b: (b, 0, 0, 0))),
        compiler_params=_cparams(("parallel",)),
        name="sel_decode",
    )(q8, cmp_kv, cmp_kv, cmp_last, bias_cd)


HALF_ROWS = BLK * ROW_W


def _slc_decode_kernel(sel_ref, pt_ref, q_ref, oc_ref, gl_ref, gb_ref, slc_ref, bias_ref, nk_ref, nv_ref,
                       win_ref, nwk_ref, nwv_ref, bw_ref, o_ref, gbuf, sem, *, t_pos, n_past):
    b = pl.program_id(0)
    n_win = win_ref.shape[0] // ROW_W

    def block_copy(g, k):
        n = jnp.minimum(sel_ref[(b * NSA_G + g) * TOPN + k], n_past - 1)
        half = pt_ref[b, n // 2] * 2 + n % 2
        return pltpu.make_async_copy(slc_ref.at[pl.ds(pl.multiple_of(half * HALF_ROWS, HALF_ROWS), HALF_ROWS), :],
                                     gbuf.at[g * TOPN + k], sem)

    for g in range(NSA_G):
        for k in range(TOPN):
            block_copy(g, k).start()

    c = lax.broadcasted_iota(jnp.int32, (1, n_win), 1)
    dist = n_win - c
    valid = (dist >= 0) & (dist <= WINDOW) & (t_pos - dist >= 0)
    qs, o_w = [], []
    for g in range(NSA_G):
        q = q_ref[g] * (DH ** -0.5)
        q_bf = q.astype(BF16)
        qs.append((q, q_bf))
        kw = win_ref[pl.ds(2 * g, n_win, stride=ROW_W), :].astype(BF16)
        vw = win_ref[pl.ds(2 * g + 1, n_win, stride=ROW_W), :].astype(BF16)
        bw = bw_ref[g]
        s_w = jnp.where(valid, _nt(q_bf, kw) + bw[:, :n_win], -jnp.inf)
        s_n = jnp.sum(q * nwk_ref[g, 0:1, :], axis=-1, keepdims=True) + bw[:, n_win:n_win + 1]
        m_w = jnp.maximum(jnp.max(s_w, axis=-1, keepdims=True), s_n)
        e_w = jnp.exp(s_w - m_w)
        e_n = jnp.exp(s_n - m_w)
        den = jnp.maximum(jnp.sum(e_w, axis=-1, keepdims=True) + e_n, 1e-30)
        o_w.append((_dot(e_w.astype(BF16), vw) + e_n * nwv_ref[g, 0:1, :]) / den)

    for g in range(NSA_G):
        for k in range(TOPN):
            block_copy(g, k).wait()

    row0 = lax.broadcasted_iota(jnp.int32, (BLK, 1), 0) == 0
    j_row = lax.broadcasted_iota(jnp.int32, (1, BLK), 1)
    for g in range(NSA_G):
        q, q_bf = qs[g]
        scores, values = [], []
        for k in range(TOPN):
            n = sel_ref[(b * NSA_G + g) * TOPN + k]
            is_new = (n == n_past) & row0
            kb = jnp.where(is_new, nk_ref[g, 0:1, :], gbuf[g * TOPN + k, pl.ds(2 * g, BLK, stride=ROW_W), :])
            vb = jnp.where(is_new, nv_ref[g, 0:1, :], gbuf[g * TOPN + k, pl.ds(2 * g + 1, BLK, stride=ROW_W), :])
            s = _nt(q_bf, kb.astype(BF16)) + bias_ref[g, n]
            scores.append(jnp.where(t_pos - (n * BLK + j_row) >= 0, s, -jnp.inf))
            values.append(vb.astype(BF16))
        m = functools.reduce(jnp.maximum, [jnp.max(s, axis=-1, keepdims=True) for s in scores])
        m = jnp.where(m > -jnp.inf, m, 0.0)
        probs = [jnp.exp(s - m) for s in scores]
        l = functools.reduce(jnp.add, [jnp.sum(p, axis=-1, keepdims=True) for p in probs])
        acc = functools.reduce(jnp.add, [_dot(p.astype(BF16), v) for p, v in zip(probs, values)])
        o_s = acc / jnp.maximum(l, 1e-30)
        gates = _sigmoid(gl_ref[g] + gb_ref[g])
        o_ref[g] = gates[:, 0:1] * oc_ref[g] + gates[:, 1:2] * o_s + gates[:, 2:3] * o_w[g]


def _slc_decode(sel_flat, page_table, q8, o_c, gate_l, gate_b, slc_cache, bias_sd, new_k, new_v,
                win_cache, new_wk, new_wv, bias_wd, t_pos):
    nd = q8.shape[0]
    n_past = page_table.shape[1] * (PAGE // BLK)
    win_rows = win_cache.shape[0] // nd

    def per_b(b, sel, pt):
        return (b, 0, 0, 0)

    def whole(shape):
        return pl.BlockSpec(shape, lambda b, sel, pt: (0,) * len(shape), pipeline_mode=pl.Buffered(1))

    b_spec = pl.BlockSpec((None, NSA_G, 8, DH), per_b)
    grid_spec = pltpu.PrefetchScalarGridSpec(
        num_scalar_prefetch=2,
        grid=(nd,),
        in_specs=[
            b_spec, b_spec, b_spec, whole(gate_b.shape),
            pl.BlockSpec(memory_space=pl.ANY),
            whole(bias_sd.shape),
            b_spec, b_spec,
            pl.BlockSpec((win_rows, DH), lambda b, sel, pt: (b, 0)),
            b_spec, b_spec,
            whole(bias_wd.shape),
        ],
        out_specs=b_spec,
        scratch_shapes=[pltpu.VMEM((NSA_G * TOPN, HALF_ROWS, DH), F32), pltpu.SemaphoreType.DMA(())],
    )
    return pl.pallas_call(
        functools.partial(_slc_decode_kernel, t_pos=t_pos, n_past=n_past),
        out_shape=jax.ShapeDtypeStruct((nd, NSA_G, 8, DH), F32),
        grid_spec=grid_spec,
        compiler_params=_cparams(("arbitrary",)),
        name="slc_decode",
    )(sel_flat, page_table, q8, o_c, gate_l, gate_b, slc_cache, bias_sd, new_k, new_v,
      win_cache, new_wk, new_wv, bias_wd)


def _t5_bucket(dist):
    n = jnp.maximum(dist, 0)
    nf = jnp.maximum(n, REL_MAX_EXACT).astype(F32)
    large = REL_MAX_EXACT + (jnp.log(nf / REL_MAX_EXACT) / math.log(REL_MAX_DIST / REL_MAX_EXACT)
                             * (REL_BUCKETS - REL_MAX_EXACT)).astype(jnp.int32)
    return jnp.where(n < REL_MAX_EXACT, n, jnp.minimum(large, REL_BUCKETS - 1))


def _bias_table(rel_bias, dist):
    onehot = (_t5_bucket(dist)[..., None] == jnp.arange(REL_BUCKETS)).astype(F32)
    b = jnp.einsum("...k,kh->...h", onehot, rel_bias, precision=HI)
    b = jnp.moveaxis(b, -1, 0)
    return b.reshape((NSA_G, NSA_R) + dist.shape)


def _prompt_bias_tables(rel_bias):
    i = jnp.arange(QBLK)[:, None]
    rel = jnp.arange(LANE)[None, :]
    bct = _bias_table(rel_bias, BLK * (rel - 1) + i + 1)
    d_near = SLC_PAD + i - jnp.arange(SLC_NEAR)[None, :]
    far = rel_bias[REL_BUCKETS - 1].reshape(NSA_G, NSA_R, 1, 1)
    bsn = _bias_table(rel_bias, d_near) - far + jnp.where(d_near >= 0, 0.0, NEG)
    d_win = WIN_PAD + i - jnp.arange(WIN_KEYS)[None, :]
    bw = _bias_table(rel_bias, d_win) + jnp.where((d_win >= 0) & (d_win <= WINDOW), 0.0, NEG)
    return bct, bsn, bw


def _pad_rows(a, rows, axis):
    pad = [(0, 0)] * a.ndim
    pad[axis] = (0, rows - a.shape[axis])
    return jnp.pad(a, pad)


def _decode_bias_tables(rel_bias, t_pos, n_past, n_win):
    n = jnp.arange(SEL_LANES)
    bias_cd = _pad_rows(_bias_table(rel_bias, t_pos - (n * BLK + BLK - 1)), 8, 1)
    tok = jnp.arange(n_past + 1)[:, None] * BLK + jnp.arange(BLK)[None, :]
    bias_sd = _pad_rows(jnp.swapaxes(_bias_table(rel_bias, t_pos - tok), 1, 2), 8, 2)
    c = jnp.arange(n_win + LANE)
    bias_wd = _pad_rows(_bias_table(rel_bias, n_win - c), 8, 1)
    return bias_cd, bias_sd, bias_wd


def kernel(x_prompt, x_sample, mem_prompt, cache_cmp_kv, cache_slc_kv, state_win_kv, state_gla, cache_mem_kv,
           page_table, norm_g, w_ffn_gate, w_ffn_up, w_ffn_down, w_in_gla, w_in_nsa, w_out, mem_norm_g, w_mem_kv,
           w_gla_a2, b_gla_a, gla_onorm_g, nsa_gate_b, cmp_pe, cmp_w1, cmp_b1, cmp_w2, rel_bias):
    nb, seq, _ = x_prompt.shape
    nd = x_sample.shape[0]
    depth = norm_g.shape[0]
    n_pages = page_table.shape[1]
    past_len = n_pages * PAGE
    n_past = past_len // BLK
    n_win = state_win_kv.shape[2]
    sr = SAMPLE_ROWS

    xp = x_prompt.reshape(nb * seq, D_MODEL)
    xs = _pad_rows(x_sample.reshape(nd, D_MODEL), sr, 0)
    mem_x = mem_prompt.reshape(nb * MEM_LEN, D_MODEL)

    wg, wu, wd = w_ffn_gate.astype(BF16), w_ffn_up.astype(BF16), w_ffn_down.astype(BF16)
    w_o = w_out.astype(BF16)
    w_mkv = w_mem_kv.astype(BF16)
    qkvr = 2 * GLA_HEADS * GLA_DK + 2 * TOK_W
    w_gla = jnp.concatenate(
        [w_in_gla[..., :qkvr], w_in_gla[..., qkvr + GLA_RANK:], w_in_gla[..., qkvr:qkvr + GLA_RANK],
         jnp.zeros(w_in_gla.shape[:2] + (GLA_N - w_in_gla.shape[2],), F32)], axis=-1).astype(BF16)
    n_gate = 3 * NSA_HEADS
    w_nsa = jnp.concatenate(
        [w_in_nsa[..., :TOK_W], w_in_nsa[..., TOK_W + 6 * KV_W + n_gate:], w_in_nsa[..., TOK_W:TOK_W + 6 * KV_W + n_gate],
         jnp.zeros(w_in_nsa.shape[:2] + (NSA_N - w_in_nsa.shape[2],), F32)], axis=-1).astype(BF16)
    wa_pad = _pad_rows(w_gla_a2, LANE, 1)
    w1 = cmp_w1.astype(BF16)
    w2 = cmp_w2.astype(BF16)

    def ffn(x, i, j, tm):
        return _ffn_half(x, norm_g[i, 4 * j][None], wg, wu, wd, norm_g[i, 4 * j + 1][None], i, j, tm)

    def every(a, step):
        return a.reshape(nd, step, a.shape[-1])[:, 0]

    def per_seq(a):
        return _pad_rows(a[:, None, :], sr, 1).reshape(nd * sr, a.shape[-1])

    outs = dict(gla_p=[], gla_s=[], cmp_p=[], cmp_s=[], slc_p=[], slc_s=[], win_p=[], win_s=[], mem_p=[])
    for i in range(depth):
        li = i // 2
        mem_kv_p = _norm_matmul(mem_x, mem_norm_g[i][None], w_mkv, i, 512, 512)
        outs["mem_p"].append(mem_kv_p.reshape(nb, MEM_LEN, 2, N_MEM_HEADS, MEM_HEAD_DIM))
        mem_kv_p = mem_kv_p.reshape(nb, MEM_LEN, 2 * MEM_W)
        mem_kv_s = cache_mem_kv[i].reshape(nd, MEM_LEN, 2 * MEM_W)
        xp = ffn(xp, i, 0, 512)
        xs = ffn(xs, i, 0, sr)
        g_mix = norm_g[i, 2][None]
        if i % 2 == 0:
            proj_p = _norm_matmul(xp, g_mix, w_gla, li, 1024, 768)
            proj_s = _norm_matmul(xs, g_mix, w_gla, li, sr, 768)
            mem_col = (qkvr) // MEM_W
            b_a = b_gla_a[li][None]
            gn = gla_onorm_g[li][None]
            s0 = jnp.zeros((nb, GLA_HEADS, GLA_DK, GLA_DV), F32)
            tok_p, sp = _gla(proj_p, s0, wa_pad[li], b_a, gn, seq, 512, GLA_CHUNK, seq)
            tok_s, ss = _gla(per_seq(proj_s[:nd]), state_gla[li], wa_pad[li], b_a, gn, sr, sr, sr, 1)
            tok_s = _pad_rows(every(tok_s, sr), sr, 0)
            outs["gla_p"].append(sp)
            outs["gla_s"].append(ss)
        else:
            proj_p = _norm_matmul(xp, g_mix, w_nsa, li, 1024, 768)
            proj_s = _norm_matmul(xs, g_mix, w_nsa, li, sr, 768)
            mem_col = NSA_MEM_COL // MEM_W
            b1 = cmp_b1[li][:, None, :]
            gate_b = nsa_gate_b[li]
            bct, bsn, bw = _prompt_bias_tables(rel_bias)
            proj3 = proj_p.reshape(nb, seq, NSA_N)
            cmp_kv = _cmp_prompt(proj3, cmp_pe[li], w1[li], b1, w2[li])
            tok_p = _nsa_prompt(proj_p, cmp_kv, _pad_rows(gate_b[None], LANE, 1), bct, bsn, bw, nb, seq)
            kv_p = proj3[:, :, NSA_KV_COL:NSA_GATE_COL].reshape(nb, seq, 3, 2, NSA_G, DH)
            outs["cmp_p"].append(kv_p[:, :, 0].reshape(nb, seq // PAGE, PAGE, 2, NSA_G, DH))
            outs["slc_p"].append(kv_p[:, :, 1].reshape(nb, seq // PAGE, PAGE, 2, NSA_G, DH))
            outs["win_p"].append(kv_p[:, seq - n_win:, 2])
            t_pos = past_len
            bias_cd, bias_sd, bias_wd = _decode_bias_tables(rel_bias, t_pos, n_past, n_win)
            kv_s = proj_s[:nd, NSA_KV_COL:NSA_GATE_COL].reshape(nd, 3, 2, NSA_G, DH)
            outs["cmp_s"].append(kv_s[:, None, 0])
            outs["slc_s"].append(kv_s[:, None, 1])
            outs["win_s"].append(jnp.concatenate([state_win_kv[li][:, 1:], kv_s[:, None, 2]], axis=1))
            q8 = _pad_rows(proj_s[:nd, :TOK_W].reshape(nd, NSA_G, NSA_R, DH), 8, 2)
            new_cmp = _pad_rows(jnp.moveaxis(kv_s[:, 0], 1, 0).reshape(2, nd * NSA_G, DH), 32, 1)
            cmp_kv_s, cmp_last = _cmp_decode(page_table, _linear_cache(cache_cmp_kv[li]), new_cmp,
                                             cmp_pe[li], w1[li], b1, w2[li])
            o_c, sel = _sel_decode(q8, cmp_kv_s, cmp_last, bias_cd, t_pos)
            sel_flat = sel[:, :, 0, :TOPN].reshape(-1)

            def row8(a):
                return _pad_rows(a[:, :, None, :], 8, 2)

            gate_l = _pad_rows(_pad_rows(proj_s[:nd, NSA_GATE_COL:NSA_GATE_COL + n_gate].reshape(nd, NSA_G, NSA_R, 3),
                                         8, 2), LANE, 3)
            gate_b8 = _pad_rows(_pad_rows(gate_b.reshape(NSA_G, NSA_R, 3), 8, 1), LANE, 2)
            tok_s = _slc_decode(sel_flat, page_table, q8, o_c, gate_l, gate_b8,
                                _linear_cache(cache_slc_kv[li]), bias_sd,
                                row8(kv_s[:, 1, 0]), row8(kv_s[:, 1, 1]),
                                _linear_cache(state_win_kv[li]),
                                row8(kv_s[:, 2, 0]), row8(kv_s[:, 2, 1]), bias_wd, t_pos)
            tok_s = _pad_rows(tok_s[:, :, :NSA_R].reshape(nd, TOK_W), sr, 0).astype(BF16)
        mem_o_p = _mem_attn(proj_p, mem_col, mem_kv_p, seq, 512)
        q_s = jnp.broadcast_to(proj_s[:nd, None, mem_col * MEM_W:(mem_col + 1) * MEM_W],
                               (nd, sr, MEM_W)).reshape(nd * sr, MEM_W)
        mem_o_s = _pad_rows(every(_mem_attn(q_s, 0, mem_kv_s, sr, sr), sr), sr, 0)
        xp = _out_proj(xp, tok_p, mem_o_p, w_o, i, norm_g[i, 3][None], 256)
        xs = _out_proj(xs, tok_s, mem_o_s, w_o, i, norm_g[i, 3][None], sr)
        xp = ffn(xp, i, 1, 512)
        xs = ffn(xs, i, 1, sr)

    y_prompt = xp.reshape(nb, seq, D_MODEL)
    y_sample = xs[:nd].reshape(nd, 1, D_MODEL)
    st = lambda k: jnp.stack(outs[k])
    return (y_prompt, y_sample, st("gla_p"), st("cmp_p"), st("slc_p"), st("win_p"), st("mem_p"),
            st("gla_s"), st("cmp_s"), st("slc_s"), st("win_s"))
```

```python
import functools
import math

import jax
import jax.numpy as jnp
from jax import lax
from jax.experimental import pallas as pl
from jax.experimental.pallas import tpu as pltpu

F32 = jnp.float32
BF16 = jnp.bfloat16
HI = lax.Precision.HIGHEST

D_MODEL = 2048
D_FF = 5632
EPS = 1e-6
MEM_LEN = 256
N_MEM_HEADS = 4
MEM_HEAD_DIM = 128
MEM_W = N_MEM_HEADS * MEM_HEAD_DIM
TOK_W = D_MODEL - MEM_W
GLA_HEADS = 4
GLA_DV = TOK_W // GLA_HEADS
GLA_DK = GLA_DV // 2
GLA_RANK = 16
GLA_TAU = 16.0
GLA_CHUNK = 64
GLA_PAIR_W = 2 * GLA_DK
DH = 128
NSA_HEADS = TOK_W // DH
NSA_G = 3
NSA_R = NSA_HEADS // NSA_G
BLK = 64
TOPN = 16
WINDOW = 512
CMP_HID = 256
QBLK = 128
KV_W = NSA_G * DH
REL_BUCKETS = 32
REL_MAX_EXACT = 16
REL_MAX_DIST = 128
PAGE = 128
LANE = 128
HALF_LANE = LANE // 2
BLK_SHIFT = BLK.bit_length() - 1
VMEM_LIMIT = 56 * 1024 * 1024

GLA_A_COL = 2 * GLA_HEADS * GLA_DK + 2 * TOK_W + MEM_W
GLA_N = 5376
NSA_MEM_COL = TOK_W
NSA_KV_COL = NSA_MEM_COL + MEM_W
NSA_GATE_COL = NSA_KV_COL + 6 * KV_W
NSA_N = 4608
SAMPLE_ROWS = 16
TM_SAMPLE = SAMPLE_ROWS
TM_FFN = 512
TM_PROJ, TN_PROJ = 1024, 768
TM_MEMKV = 512
TM_OUT = 256
TM_MEM_ATTN = 512
TL_GLA = 512
SLC_PAD = 256
WIN_PAD = WINDOW
WIN_KEYS = WIN_PAD + QBLK
SLC_NEAR = SLC_PAD + QBLK
FAR_CHUNK = 1024
NEG = -1e30
M_FLOOR = -1e29


def _cparams(sem, vmem=VMEM_LIMIT):
    return pltpu.CompilerParams(dimension_semantics=sem, vmem_limit_bytes=vmem)


def _sigmoid(x):
    return 1.0 / (1.0 + jnp.exp(-x))


def _rms(x, g):
    ms = jnp.mean(x * x, axis=-1, keepdims=True)
    return x * lax.rsqrt(ms + EPS) * g


def _nt(a, b, precision=None):
    return lax.dot_general(a, b, (((1,), (1,)), ((), ())), precision=precision,
                           preferred_element_type=F32)


def _tn(a, b, precision=None):
    return lax.dot_general(a, b, (((0,), (0,)), ((), ())), precision=precision,
                           preferred_element_type=F32)


def _dot(a, b, precision=None):
    return jnp.dot(a, b, precision=precision, preferred_element_type=F32)


def _split3(x):
    hi = x.astype(BF16)
    r1 = x - hi.astype(F32)
    mid = r1.astype(BF16)
    lo = (r1 - mid.astype(F32)).astype(BF16)
    return hi, mid, lo


def _ffn_kernel(x_ref, g1_ref, wg_ref, wu_ref, wd_ref, g2_ref, o_ref, *rest):
    bf_out, (xn_ref, acc_ref) = rest[:-2], rest[-2:]
    j = pl.program_id(1)

    @pl.when(j == 0)
    def _():
        xn_ref[...] = _rms(x_ref[...], g1_ref[...]).astype(BF16)
        acc_ref[...] = jnp.zeros_like(acc_ref)

    wg, wu, wd = (w_ref[...].astype(BF16) for w_ref in (wg_ref, wu_ref, wd_ref))
    for out_ref, w in zip(bf_out, (wg, wu, wd)):
        out_ref[...] = w
    xn = xn_ref[...]
    gate = _dot(xn, wg)
    up = _dot(xn, wu)
    h = (gate * _sigmoid(gate) * up).astype(BF16)
    acc_ref[...] += _dot(h, wd)

    @pl.when(j == pl.num_programs(1) - 1)
    def _():
        o_ref[...] = x_ref[...] + 0.5 * _rms(acc_ref[...], g2_ref[...])


def _ffn_half(x, g1, wg, wu, wd, g2, tm, tf=512, f32_weights_at=None):
    m = x.shape[0]
    y_shape = jax.ShapeDtypeStruct((m, D_MODEL), F32)
    y_spec = pl.BlockSpec((tm, D_MODEL), lambda i, j: (i, 0))
    col_spec = pl.BlockSpec((D_MODEL, tf), lambda i, j: (0, j))
    row_spec = pl.BlockSpec((tf, D_MODEL), lambda i, j: (j, 0))
    if f32_weights_at is None:
        w_specs = [col_spec, col_spec, row_spec]
        out_shape, out_specs = y_shape, y_spec
    else:
        assert m == tm, "the bf16 copies are written once, by a single row tile"
        layer, half = f32_weights_at
        w_specs = [pl.BlockSpec((None, None, D_MODEL, tf), lambda i, j: (layer, half, 0, j)),
                   pl.BlockSpec((None, None, D_MODEL, tf), lambda i, j: (layer, half, 0, j)),
                   pl.BlockSpec((None, None, tf, D_MODEL), lambda i, j: (layer, half, j, 0))]
        out_shape = (y_shape, jax.ShapeDtypeStruct((D_MODEL, D_FF), BF16),
                     jax.ShapeDtypeStruct((D_MODEL, D_FF), BF16), jax.ShapeDtypeStruct((D_FF, D_MODEL), BF16))
        out_specs = (y_spec, col_spec, col_spec, row_spec)
    return pl.pallas_call(
        _ffn_kernel,
        out_shape=out_shape,
        grid=(m // tm, D_FF // tf),
        in_specs=[y_spec, pl.BlockSpec((1, D_MODEL), lambda i, j: (0, 0))] + w_specs
                 + [pl.BlockSpec((1, D_MODEL), lambda i, j: (0, 0))],
        out_specs=out_specs,
        scratch_shapes=[pltpu.VMEM((tm, D_MODEL), BF16), pltpu.VMEM((tm, D_MODEL), F32)],
        compiler_params=_cparams(("parallel", "arbitrary")),
        name="ffn_half",
    )(x, g1, wg, wu, wd, g2)


def _norm_matmul_kernel(x_ref, g_ref, w_ref, o_ref, xn_ref, *, w_transposed):
    @pl.when(pl.program_id(1) == 0)
    def _():
        xn_ref[...] = _rms(x_ref[...], g_ref[...]).astype(BF16)

    o_ref[...] = _nt(xn_ref[...], w_ref[...]) if w_transposed else _dot(xn_ref[...], w_ref[...])


def _norm_matmul(x, g, w, layer, tm, tn, w_transposed=False):
    m = x.shape[0]
    if w_transposed:
        n = w.shape[1]
        w_spec = pl.BlockSpec((None, tn, D_MODEL), lambda i, j: (layer, j, 0))
    else:
        n = w.shape[2]
        w_spec = pl.BlockSpec((None, D_MODEL, tn), lambda i, j: (layer, 0, j))
    return pl.pallas_call(
        functools.partial(_norm_matmul_kernel, w_transposed=w_transposed),
        out_shape=jax.ShapeDtypeStruct((m, n), F32),
        grid=(m // tm, n // tn),
        in_specs=[
            pl.BlockSpec((tm, D_MODEL), lambda i, j: (i, 0)),
            pl.BlockSpec((1, D_MODEL), lambda i, j: (0, 0)),
            w_spec,
        ],
        out_specs=pl.BlockSpec((tm, tn), lambda i, j: (i, j)),
        scratch_shapes=[pltpu.VMEM((tm, D_MODEL), BF16)],
        compiler_params=_cparams(("parallel", "arbitrary")),
        name="norm_matmul",
    )(x, g, w)


def _out_proj_kernel(x_ref, tok_ref, mem_ref, wt_ref, wm_ref, g_ref, o_ref):
    y = _dot(tok_ref[...], wt_ref[...]) + _dot(mem_ref[...], wm_ref[...])
    o_ref[...] = x_ref[...] + _rms(y, g_ref[...])


def _out_proj(x, tok, mem_o, w_o, layer, g, tm):
    m = x.shape[0]
    return pl.pallas_call(
        _out_proj_kernel,
        out_shape=jax.ShapeDtypeStruct((m, D_MODEL), F32),
        grid=(m // tm,),
        in_specs=[
            pl.BlockSpec((tm, D_MODEL), lambda i: (i, 0)),
            pl.BlockSpec((tm, TOK_W), lambda i: (i, 0)),
            pl.BlockSpec((tm, MEM_W), lambda i: (i, 0)),
            pl.BlockSpec((None, TOK_W, D_MODEL), lambda i: (layer, 0, 0), pipeline_mode=pl.Buffered(1)),
            pl.BlockSpec((None, MEM_W, D_MODEL), lambda i: (layer, TOK_W // MEM_W, 0), pipeline_mode=pl.Buffered(1)),
            pl.BlockSpec((1, D_MODEL), lambda i: (0, 0)),
        ],
        out_specs=pl.BlockSpec((tm, D_MODEL), lambda i: (i, 0)),
        compiler_params=_cparams(("parallel",)),
        name="out_proj",
    )(x, tok, mem_o, w_o, w_o, g)


def _mem_attn_kernel(q_ref, kv_ref, o_ref):
    for h in range(N_MEM_HEADS):
        q = (q_ref[:, h * DH:(h + 1) * DH] * (MEM_HEAD_DIM ** -0.5)).astype(BF16)
        k = kv_ref[:, h * DH:(h + 1) * DH].astype(BF16)
        v = kv_ref[:, MEM_W + h * DH:MEM_W + (h + 1) * DH].astype(BF16)
        s = _nt(q, k)
        e = jnp.exp(s - jnp.max(s, axis=-1, keepdims=True))
        p = e / jnp.sum(e, axis=-1, keepdims=True)
        o_ref[:, h * DH:(h + 1) * DH] = _dot(p.astype(BF16), v).astype(o_ref.dtype)


def _mem_attn(q_arr, q_col_block, mem_kv, rows_per_batch, tm):
    nb = mem_kv.shape[0]
    per = rows_per_batch // tm
    return pl.pallas_call(
        _mem_attn_kernel,
        out_shape=jax.ShapeDtypeStruct((nb * rows_per_batch, MEM_W), BF16),
        grid=(nb, per),
        in_specs=[
            pl.BlockSpec((tm, MEM_W), lambda b, i: (b * per + i, q_col_block)),
            pl.BlockSpec((None, MEM_LEN, 2 * MEM_W), lambda b, i: (b, 0, 0)),
        ],
        out_specs=pl.BlockSpec((tm, MEM_W), lambda b, i: (b * per + i, 0)),
        compiler_params=_cparams(("parallel", "parallel")),
        name="mem_attn",
    )(q_arr, mem_kv)


def _gla_kernel(q_ref, k_ref, v0_ref, v1_ref, r0_ref, r1_ref, a_ref, s0_ref, wa_ref, ba_ref, gn_ref,
                tok_ref, s_out_ref, s_ref, la_ref, *, chunk, n_valid):
    l = pl.program_id(2)
    tl = q_ref.shape[0]

    @pl.when(l == 0)
    def _():
        s_ref[...] = s0_ref[...].reshape(GLA_PAIR_W, GLA_DV).T

    lane = lax.broadcasted_iota(jnp.int32, (1, GLA_PAIR_W), 1)
    head_mask = [(lane < GLA_DK).astype(F32), (lane >= GLA_DK).astype(F32)]
    ti = lax.broadcasted_iota(jnp.int32, (chunk, chunk), 0)
    si = lax.broadcasted_iota(jnp.int32, (chunk, chunk), 1)
    causal = si <= ti
    tri = jnp.where(causal, 1.0, 0.0).astype(BF16)
    v_refs = (v0_ref, v1_ref)
    r_refs = (r0_ref, r1_ref)

    a_hi, a_lo, _ = _split3(a_ref[...])
    w_hi, w_lo, _ = _split3(wa_ref[...])
    z = _dot(a_hi, w_hi) + _dot(a_lo, w_hi) + _dot(a_hi, w_lo) + ba_ref[...]
    la_all = -(jnp.maximum(-z, 0.0) + jnp.log1p(jnp.exp(-jnp.abs(z)))) / GLA_TAU
    pos = l * tl + lax.broadcasted_iota(jnp.int32, (tl, 1), 0)
    la_ref[...] = jnp.where(pos < n_valid, la_all, 0.0)

    def step(ci, carry):
        r0 = pl.multiple_of(ci * chunk, chunk)
        rows = pl.ds(r0, chunk)
        b = functools.reduce(jnp.add, [_dot(tri, piece) for piece in _split3(la_ref[rows, :])])
        bl = b[chunk - 1:chunk, :]
        q = q_ref[rows, :] * (GLA_DK ** -0.5)
        k = k_ref[rows, :]
        qe = q * jnp.exp(b)
        ke = (k * jnp.exp(-b)).astype(BF16)
        kd = k * jnp.exp(bl - b)
        st_old = s_ref[...]
        st_bf = st_old.astype(BF16)
        upd = None
        for h in range(2):
            v = v_refs[h][rows, :].astype(BF16)
            qm = (qe * head_mask[h]).astype(BF16)
            att = jnp.where(causal, _nt(qm, ke), 0.0)
            o = _nt(qm, st_bf) + _dot(att.astype(BF16), v)
            o = _rms(o, gn_ref[...])
            r = r_refs[h][rows, :]
            tok_ref[rows, h * GLA_DV:(h + 1) * GLA_DV] = (o * (r * _sigmoid(r))).astype(tok_ref.dtype)
            u = _tn(v, (kd * head_mask[h]).astype(BF16))
            upd = u if upd is None else upd + u
        s_ref[...] = jnp.exp(bl) * st_old + upd
        return carry

    lax.fori_loop(0, tl // chunk, step, 0)

    @pl.when(l == pl.num_programs(2) - 1)
    def _():
        s_out_ref[...] = s_ref[...].T.reshape(2, GLA_DK, GLA_DV)


def _gla(proj, s0, wa_pad, b_a, gn, seq, tl, chunk, n_valid):
    nb = s0.shape[0]
    per = seq // tl
    w = GLA_PAIR_W

    def col(base, stride=1):
        return lambda b, p, l: (b * per + l, base + stride * p)

    return pl.pallas_call(
        functools.partial(_gla_kernel, chunk=chunk, n_valid=n_valid),
        out_shape=(jax.ShapeDtypeStruct((nb * seq, TOK_W), BF16),
                   jax.ShapeDtypeStruct((nb, GLA_HEADS, GLA_DK, GLA_DV), F32)),
        grid=(nb, 2, per),
        in_specs=[
            pl.BlockSpec((tl, w), col(0)),
            pl.BlockSpec((tl, w), col(2)),
            pl.BlockSpec((tl, w), col(4, 2)),
            pl.BlockSpec((tl, w), col(5, 2)),
            pl.BlockSpec((tl, w), col(8, 2)),
            pl.BlockSpec((tl, w), col(9, 2)),
            pl.BlockSpec((tl, LANE), lambda b, p, l: (b * per + l, GLA_A_COL // LANE)),
            pl.BlockSpec((None, 2, GLA_DK, GLA_DV), lambda b, p, l: (b, p, 0, 0)),
            pl.BlockSpec((LANE, w), lambda b, p, l: (0, p)),
            pl.BlockSpec((1, w), lambda b, p, l: (0, p)),
            pl.BlockSpec((1, GLA_DV), lambda b, p, l: (0, 0)),
        ],
        out_specs=(pl.BlockSpec((tl, 2 * GLA_DV), lambda b, p, l: (b * per + l, p)),
                   pl.BlockSpec((None, 2, GLA_DK, GLA_DV), lambda b, p, l: (b, p, 0, 0))),
        scratch_shapes=[pltpu.VMEM((GLA_DV, GLA_PAIR_W), F32), pltpu.VMEM((tl, GLA_PAIR_W), F32)],
        compiler_params=_cparams(("parallel", "parallel", "arbitrary")),
        name="gla",
    )(proj, proj, proj, proj, proj, proj, proj, s0, wa_pad, b_a, gn)


def _masked_softmax(s, valid):
    s = jnp.where(valid, s, -jnp.inf)
    m = jnp.max(s, axis=-1, keepdims=True)
    m = jnp.where(m > -jnp.inf, m, 0.0)
    e = jnp.exp(s - m)
    return e / jnp.maximum(jnp.sum(e, axis=-1, keepdims=True), 1e-30)


def _online_update(state, s, v):
    m, l, acc = state
    r, nq, w = s.shape
    m_new = jnp.maximum(m, jnp.max(s, axis=-1, keepdims=True))
    alpha = jnp.exp(m - m_new)
    p = jnp.exp(s - m_new)
    l = alpha * l + jnp.sum(p, axis=-1, keepdims=True)
    pv = _dot(p.reshape(r * nq, w).astype(BF16), v).reshape(r, nq, DH)
    return m_new, l, alpha * acc + pv


def _compress_tail(xflat, w1, b1, w2):
    h = _dot(xflat, w1) + b1
    h = h * _sigmoid(h)
    return _dot(h.astype(BF16), w2)


def _cmp_prompt_kernel(x0_ref, x1_ref, x2_ref, pe_ref, w1_ref, b1_ref, w2_ref, o_ref, xflat_ref):
    x_refs = (x0_ref, x1_ref, x2_ref)
    nb, seq = x0_ref.shape[0], x0_ref.shape[1]
    nblk = seq // BLK
    for j in range(BLK):
        pe_j = pe_ref[j:j + 1, :]
        for b in range(nb):
            for g in range(NSA_G):
                xj = x_refs[g][b, pl.ds(j, nblk, stride=BLK), :]
                row = (b * NSA_G + g) * nblk
                xflat_ref[row:row + nblk, j * DH:(j + 1) * DH] = (xj + pe_j).astype(BF16)
    out = _compress_tail(xflat_ref[...], w1_ref[...], b1_ref[...], w2_ref[...])
    o_ref[...] = out.reshape(nb, NSA_G, nblk, DH)


def _cmp_prompt(proj3, pe, w1, b1, w2):
    nb, seq, _ = proj3.shape
    nblk = seq // BLK
    return pl.pallas_call(
        _cmp_prompt_kernel,
        out_shape=jax.ShapeDtypeStruct((2, nb, NSA_G, nblk, DH), F32),
        grid=(2,),
        in_specs=[
            pl.BlockSpec((nb, seq, DH), lambda kv: (0, 0, NSA_KV_COL // DH + NSA_G * kv)),
            pl.BlockSpec((nb, seq, DH), lambda kv: (0, 0, NSA_KV_COL // DH + NSA_G * kv + 1)),
            pl.BlockSpec((nb, seq, DH), lambda kv: (0, 0, NSA_KV_COL // DH + NSA_G * kv + 2)),
            pl.BlockSpec((None, BLK, DH), lambda kv: (kv, 0, 0)),
            pl.BlockSpec((None, BLK * DH, CMP_HID), lambda kv: (kv, 0, 0)),
            pl.BlockSpec((None, 1, CMP_HID), lambda kv: (kv, 0, 0)),
            pl.BlockSpec((None, CMP_HID, DH), lambda kv: (kv, 0, 0)),
        ],
        out_specs=pl.BlockSpec((None, nb, NSA_G, nblk, DH), lambda kv: (kv, 0, 0, 0, 0)),
        scratch_shapes=[pltpu.VMEM((nb * NSA_G * nblk, BLK * DH), BF16)],
        compiler_params=_cparams(("arbitrary",)),
        name="cmp_prompt",
    )(proj3, proj3, proj3, pe, w1, b1, w2)


def _nsa_prompt_kernel(q_ref, gl_ref, gb_ref, kc_ref, vc_ref, ks_ref, vs_ref, kw_ref, vw_ref,
                       bct_ref, bsn_ref, bw_ref, o_ref, ksb, vsb, kwb, vwb):
    g = pl.program_id(1)
    qi = pl.program_id(2)
    seq = ks_ref.shape[0]
    nblk = seq // BLK
    R = NSA_R

    @pl.when(qi == 0)
    def _():
        for src, dst, pad in ((vs_ref, vsb, SLC_PAD), (vw_ref, vwb, WIN_PAD)):
            dst[0:pad, :] = jnp.zeros((pad, DH), BF16)
            dst[pad:pad + seq, :] = src[...].astype(BF16)
        for src, dst, pad, per_block in ((ks_ref, ksb, SLC_PAD, True), (kw_ref, kwb, WIN_PAD, False)):
            dst[0:pad, 0:DH] = jnp.zeros((pad, DH), BF16)
            dst[pad:pad + seq, 0:DH] = src[...].astype(BF16)
            pos = lax.broadcasted_iota(jnp.int32, (pad + seq, 1), 0) - pad
            feat = lax.broadcasted_iota(jnp.int32, (1, DH), 1)
            masked = (feat == HALF_LANE) & (pos < 0)
            if per_block:
                masked = masked | ((pos >= 0) & (lax.shift_right_arithmetic(pos, BLK_SHIFT) == feat))
            dst[:, DH:2 * DH] = jnp.where(masked, NEG, 0.0).astype(BF16)

    q = q_ref[...] * (DH ** -0.5)
    q_all = jnp.concatenate([q[:, r * DH:(r + 1) * DH] for r in range(R)], axis=0)
    q_bf = q_all.astype(BF16)
    i_col = lax.broadcasted_iota(jnp.int32, (QBLK, 1), 0)
    t_col = qi * QBLK + i_col
    n_row = lax.broadcasted_iota(jnp.int32, (1, nblk), 1)
    row0 = pl.multiple_of(qi * QBLK, QBLK)
    lane2 = lax.broadcasted_iota(jnp.int32, (1, LANE), 1)

    s_c = _nt(q_all, kc_ref[...], HI).reshape(R, QBLK, nblk)
    tb_col = lax.shift_right_arithmetic(t_col, BLK_SHIFT)
    rel = tb_col - n_row
    bias_c = []
    for r in range(R):
        tab = bct_ref[r]
        bias_c.append(jnp.where(rel == 0, tab[:, 0:1],
                      jnp.where(rel == 1, tab[:, 1:2],
                      jnp.where(rel == 2, tab[:, 2:3], tab[:, 3:4]))))
    s_c = s_c + jnp.stack(bias_c, axis=0)
    valid_c = (t_col - (n_row * BLK + (BLK - 1))) >= 0
    p_c = _masked_softmax(s_c, valid_c[None])
    o_c = _dot(p_c.reshape(R * QBLK, nblk).astype(BF16), vc_ref[...].astype(BF16))

    imp = jnp.sum(p_c, axis=0)
    forced = (n_row == 0) | (n_row == tb_col) | (n_row == tb_col - 1)
    future = n_row * BLK > t_col
    score = jnp.where(forced, jnp.inf, jnp.where(future, -jnp.inf, imp))
    if nblk < HALF_LANE:
        score = jnp.concatenate([score, jnp.full((QBLK, HALF_LANE - nblk), -jnp.inf, F32)], axis=1)
    half = HALF_LANE // 2
    score2 = jnp.concatenate([score, score], axis=1)
    left = lane2 < HALF_LANE
    n2 = jnp.where(left, lane2, lane2 - HALF_LANE)
    rank2 = jnp.zeros((QBLK, LANE), F32)
    for i in range(half):
        col = jnp.where(left, score[:, i:i + 1], score[:, i + half:i + half + 1])
        wins_tie = jnp.where(n2 > jnp.where(left, i, i + half), 1.0, 0.0)
        rank2 = rank2 + jnp.where(col > score2, 1.0, jnp.where(col == score2, wins_tie, 0.0))
    rank = rank2 + jnp.concatenate([rank2[:, HALF_LANE:], rank2[:, :HALF_LANE]], axis=1)
    n_far = (qi * QBLK - SLC_PAD) // BLK
    dropped = rank >= float(min(TOPN, nblk))
    flag_all = jnp.where(left, jnp.where(dropped, 1.0, 0.0), jnp.where(lane2 == HALF_LANE, 1.0, 0.0))
    flag_far = jnp.where(left & (lane2 >= n_far), 1.0, flag_all)
    q_far = jnp.concatenate([q_bf, jnp.concatenate([flag_far.astype(BF16)] * R, axis=0)], axis=1)
    q_near = jnp.concatenate([q_bf, jnp.concatenate([flag_all.astype(BF16)] * R, axis=0)], axis=1)

    def far_body(kc_i, state):
        start = pl.multiple_of(SLC_PAD + kc_i * FAR_CHUNK, BLK)
        s = _nt(q_far, ksb[pl.ds(start, FAR_CHUNK), :]).reshape(R, QBLK, FAR_CHUNK)
        return _online_update(state, s, vsb[pl.ds(start, FAR_CHUNK), :])

    blk_per_chunk = FAR_CHUNK // BLK
    n_chunks = (jnp.maximum(n_far, 0) + (blk_per_chunk - 1)) // blk_per_chunk
    state = (jnp.full((R, QBLK, 1), M_FLOOR, F32), jnp.zeros((R, QBLK, 1), F32), jnp.zeros((R, QBLK, DH), F32))
    state = lax.fori_loop(0, n_chunks, far_body, state)

    s = _nt(q_near, ksb[pl.ds(row0, SLC_NEAR), :]).reshape(R, QBLK, SLC_NEAR) + bsn_ref[...]
    _, l_s, acc_s = _online_update(state, s, vsb[pl.ds(row0, SLC_NEAR), :])
    o_s = (acc_s / jnp.maximum(l_s, 1e-30)).reshape(R * QBLK, DH)

    s = _nt(q_near, kwb[pl.ds(row0, WIN_KEYS), :]).reshape(R, QBLK, WIN_KEYS) + bw_ref[...]
    e_w = jnp.exp(s - jnp.max(s, axis=-1, keepdims=True))
    p_w = e_w / jnp.maximum(jnp.sum(e_w, axis=-1, keepdims=True), 1e-30)
    o_w = _dot(p_w.reshape(R * QBLK, WIN_KEYS).astype(BF16), vwb[pl.ds(row0, WIN_KEYS), :])

    gates = _sigmoid(gl_ref[...] + gb_ref[...])
    src = lax.broadcasted_iota(jnp.int32, (LANE, LANE), 0)
    dst = lax.broadcasted_iota(jnp.int32, (LANE, LANE), 1)
    pick = jnp.where((src == g * (3 * R) + dst) & (dst < 3 * R), 1.0, 0.0)
    gsel = _dot(gates, pick, HI)
    for r in range(R):
        rows = slice(r * QBLK, (r + 1) * QBLK)
        o = (gsel[:, 3 * r:3 * r + 1] * o_c[rows] + gsel[:, 3 * r + 1:3 * r + 2] * o_s[rows]
             + gsel[:, 3 * r + 2:3 * r + 3] * o_w[rows])
        o_ref[:, r * DH:(r + 1) * DH] = o.astype(o_ref.dtype)


def _nsa_prompt(proj2, cmp_kv, gate_b_pad, bct, bsn, bw, nb, seq):
    nq = seq // QBLK
    nblk = seq // BLK
    assert nblk <= HALF_LANE, "mask features of the selected branch hold at most 64 key blocks"
    assert seq % FAR_CHUNK == 0
    proj3 = proj2.reshape(nb, seq, NSA_N)

    def kv_spec(col):
        return pl.BlockSpec((None, seq, DH), lambda b, g, qi: (b, 0, col // DH + g))

    return pl.pallas_call(
        _nsa_prompt_kernel,
        out_shape=jax.ShapeDtypeStruct((nb * seq, TOK_W), BF16),
        grid=(nb, NSA_G, nq),
        in_specs=[
            pl.BlockSpec((QBLK, NSA_R * DH), lambda b, g, qi: (b * nq + qi, g)),
            pl.BlockSpec((QBLK, LANE), lambda b, g, qi: (b * nq + qi, NSA_GATE_COL // LANE)),
            pl.BlockSpec((1, LANE), lambda b, g, qi: (0, 0)),
            pl.BlockSpec((None, None, None, nblk, DH), lambda b, g, qi: (0, b, g, 0, 0)),
            pl.BlockSpec((None, None, None, nblk, DH), lambda b, g, qi: (1, b, g, 0, 0)),
            kv_spec(NSA_KV_COL + 2 * KV_W), kv_spec(NSA_KV_COL + 3 * KV_W),
            kv_spec(NSA_KV_COL + 4 * KV_W), kv_spec(NSA_KV_COL + 5 * KV_W),
            pl.BlockSpec((None, NSA_R, QBLK, LANE), lambda b, g, qi: (g, 0, 0, 0)),
            pl.BlockSpec((None, NSA_R, QBLK, SLC_NEAR), lambda b, g, qi: (g, 0, 0, 0)),
            pl.BlockSpec((None, NSA_R, QBLK, WIN_KEYS), lambda b, g, qi: (g, 0, 0, 0)),
        ],
        out_specs=pl.BlockSpec((QBLK, NSA_R * DH), lambda b, g, qi: (b * nq + qi, g)),
        scratch_shapes=[pltpu.VMEM((SLC_PAD + seq, 2 * DH), BF16), pltpu.VMEM((SLC_PAD + seq, DH), BF16),
                        pltpu.VMEM((WIN_PAD + seq, 2 * DH), BF16), pltpu.VMEM((WIN_PAD + seq, DH), BF16)],
        compiler_params=_cparams(("parallel", "parallel", "arbitrary")),
        name="nsa_prompt",
    )(proj2, proj2, gate_b_pad, cmp_kv, cmp_kv, proj3, proj3, proj3, proj3, bct, bsn, bw)


SUB_PAGES = 8
RING = 4
ROW_W = 2 * NSA_G
PAGE_ROWS = PAGE * ROW_W


def _linear_cache(cache):
    return jnp.transpose(cache, (0, 1, 3, 2, 4)).reshape(-1, DH)


def _cmp_decode_kernel(pt_ref, cache_ref, new_ref, pe_ref, w1_ref, b1_ref, w2_ref, o_ref, last_ref,
                       buf, sem, xflat_ref, xlast_ref, *, n_sub):
    b = pl.program_id(0)
    total = pl.num_programs(0) * n_sub
    sub_blk = SUB_PAGES * PAGE // BLK
    seq_blk = n_sub * sub_blk

    def page_copy(s, p):
        page = pt_ref[s // n_sub, (s % n_sub) * SUB_PAGES + p]
        return pltpu.make_async_copy(cache_ref.at[pl.ds(pl.multiple_of(page * PAGE_ROWS, PAGE_ROWS), PAGE_ROWS), :],
                                     buf.at[s % RING, pl.ds(p * PAGE_ROWS, PAGE_ROWS), :], sem.at[s % RING])

    def start_sub(s):
        for p in range(SUB_PAGES):
            page_copy(s, p).start()

    @pl.when(b == 0)
    def _():
        for s in range(RING):
            start_sub(s)

    def body(i, c):
        s = b * n_sub + i
        for p in range(SUB_PAGES):
            page_copy(s, p).wait()
        slot = s % RING
        row0 = pl.multiple_of(i * sub_blk, sub_blk)
        for kv in range(2):
            for j in range(BLK):
                pe_j = pe_ref[kv, j:j + 1, :]
                for g in range(NSA_G):
                    xj = buf[slot, pl.ds(j * ROW_W + g * 2 + kv, sub_blk, stride=BLK * ROW_W), :]
                    xflat_ref[kv, pl.ds(g * seq_blk + row0, sub_blk), j * DH:(j + 1) * DH] = (xj + pe_j).astype(BF16)

        @pl.when(s + RING < total)
        def _():
            start_sub(s + RING)

        return c

    lax.fori_loop(0, n_sub, body, 0)
    for kv in range(2):
        out = _compress_tail(xflat_ref[kv], w1_ref[kv], b1_ref[kv], w2_ref[kv])
        o_ref[kv] = out.reshape(NSA_G, seq_blk, DH)

    @pl.when(b == 0)
    def _():
        rows = new_ref.shape[1]
        for kv in range(2):
            for j in range(BLK):
                pe_j = jnp.broadcast_to(pe_ref[kv, j:j + 1, :], (rows, DH))
                xj = new_ref[kv] + pe_j if j == 0 else pe_j
                xlast_ref[:, j * DH:(j + 1) * DH] = xj.astype(BF16)
            last_ref[kv] = _compress_tail(xlast_ref[...], w1_ref[kv], b1_ref[kv], w2_ref[kv])


def _cmp_decode(page_table, cache, new_rows, pe, w1, b1, w2):
    nd, n_pages = page_table.shape
    n_sub = n_pages // SUB_PAGES
    seq_blk = n_pages * PAGE // BLK
    rows = new_rows.shape[1]

    def whole(shape):
        return pl.BlockSpec(shape, lambda b, pt: (0,) * len(shape), pipeline_mode=pl.Buffered(1))

    grid_spec = pltpu.PrefetchScalarGridSpec(
        num_scalar_prefetch=1,
        grid=(nd,),
        in_specs=[
            pl.BlockSpec(memory_space=pl.ANY),
            whole(new_rows.shape), whole(pe.shape), whole(w1.shape), whole(b1.shape), whole(w2.shape),
        ],
        out_specs=(pl.BlockSpec((2, None, NSA_G, seq_blk, DH), lambda b, pt: (0, b, 0, 0, 0)),
                   pl.BlockSpec((2, rows, DH), lambda b, pt: (0, 0, 0))),
        scratch_shapes=[pltpu.VMEM((RING, SUB_PAGES * PAGE_ROWS, DH), F32),
                        pltpu.SemaphoreType.DMA((RING,)),
                        pltpu.VMEM((2, NSA_G * seq_blk, BLK * DH), BF16),
                        pltpu.VMEM((rows, BLK * DH), BF16)],
    )
    return pl.pallas_call(
        functools.partial(_cmp_decode_kernel, n_sub=n_sub),
        out_shape=(jax.ShapeDtypeStruct((2, nd, NSA_G, seq_blk, DH), F32),
                   jax.ShapeDtypeStruct((2, rows, DH), F32)),
        grid_spec=grid_spec,
        compiler_params=_cparams(("arbitrary",)),
        name="cmp_decode",
    )(page_table, cache, new_rows, pe, w1, b1, w2)


SEL_LANES = 384


def _sel_decode_kernel(q_ref, kc_ref, vc_ref, last_ref, bias_ref, oc_ref, sel_ref, *, t_pos):
    b = pl.program_id(0)
    n_past = kc_ref.shape[1]
    n_blocks = n_past + 1
    n_lane = lax.broadcasted_iota(jnp.int32, (1, SEL_LANES), 1)
    n_lane_f = n_lane.astype(F32)
    head_row = lax.broadcasted_iota(jnp.int32, (8, 1), 0) < NSA_R
    tb = t_pos // BLK
    for g in range(NSA_G):
        q = q_ref[g] * (DH ** -0.5)
        bias = bias_ref[g]
        s_p = _nt(q, kc_ref[g], HI) + bias[:, :n_past]
        row = b * NSA_G + g
        k_last = last_ref[0, pl.ds(row, 1), :]
        v_last = last_ref[1, pl.ds(row, 1), :]
        s_l = jnp.sum(q * k_last, axis=-1, keepdims=True) + bias[:, n_past:n_past + 1]
        valid_p = (t_pos - (n_lane[:, :n_past] * BLK + (BLK - 1))) >= 0
        valid_l = (t_pos - (n_past * BLK + (BLK - 1))) >= 0
        s_p = jnp.where(valid_p, s_p, -jnp.inf)
        s_l = jnp.where(valid_l, s_l, -jnp.inf)
        m = jnp.maximum(jnp.max(s_p, axis=-1, keepdims=True), s_l)
        m = jnp.where(m > -jnp.inf, m, 0.0)
        e_p = jnp.exp(s_p - m)
        e_l = jnp.exp(s_l - m)
        den = jnp.maximum(jnp.sum(e_p, axis=-1, keepdims=True) + e_l, 1e-30)
        p_p = e_p / den
        p_l = e_l / den
        oc_ref[g] = _dot(p_p, vc_ref[g], HI) + p_l * v_last
        imp_p = jnp.sum(jnp.where(head_row, p_p, 0.0), axis=0, keepdims=True)
        imp_l = jnp.sum(jnp.where(head_row, p_l, 0.0), axis=0, keepdims=True)
        imp = jnp.concatenate([imp_p, jnp.broadcast_to(imp_l, (1, SEL_LANES - n_past))], axis=1)
        forced = (n_lane == 0) | (n_lane == tb) | (n_lane == tb - 1)
        future = n_lane * BLK > t_pos
        score = jnp.where(forced, jnp.inf, jnp.where(future, -jnp.inf, imp))
        cand = n_lane < n_blocks
        sel = jnp.zeros((1, LANE), jnp.int32)
        k_lane = lax.broadcasted_iota(jnp.int32, (1, LANE), 1)
        for k in range(min(TOPN, n_blocks)):
            best = jnp.max(jnp.where(cand, score, -jnp.inf), axis=-1, keepdims=True)
            idx_f = jnp.min(jnp.where(cand & (score == best), n_lane_f, float(SEL_LANES)), axis=-1, keepdims=True)
            idx = idx_f.astype(jnp.int32)
            sel = jnp.where(k_lane == k, idx, sel)
            cand = cand & (n_lane != idx)
        sel_ref[g] = jnp.broadcast_to(sel, (8, LANE))


def _sel_decode(q8, cmp_kv, cmp_last, bias_cd, t_pos):
    nd = q8.shape[0]
    n_past = cmp_kv.shape[3]
    return pl.pallas_call(
        functools.partial(_sel_decode_kernel, t_pos=t_pos),
        out_shape=(jax.ShapeDtypeStruct((nd, NSA_G, 8, DH), F32),
                   jax.ShapeDtypeStruct((nd, NSA_G, 8, LANE), jnp.int32)),
        grid=(nd,),
        in_specs=[
            pl.BlockSpec((None, NSA_G, 8, DH), lambda b: (b, 0, 0, 0)),
            pl.BlockSpec((None, None, NSA_G, n_past, DH), lambda b: (0, b, 0, 0, 0)),
            pl.BlockSpec((None, None, NSA_G, n_past, DH), lambda b: (1, b, 0, 0, 0)),
            pl.BlockSpec(cmp_last.shape, lambda b: (0, 0, 0)),
            pl.BlockSpec(bias_cd.shape, lambda b: (0, 0, 0)),
        ],
        out_specs=(pl.BlockSpec((None, NSA_G, 8, DH), lambda b: (b, 0, 0, 0)),
                   pl.BlockSpec((None, NSA_G, 8, LANE), lambda b: (b, 0, 0, 0))),
        compiler_params=_cparams(("parallel",)),
        name="sel_decode",
    )(q8, cmp_kv, cmp_kv, cmp_last, bias_cd)


HALF_ROWS = BLK * ROW_W


def _slc_decode_kernel(sel_ref, pt_ref, q_ref, oc_ref, gl_ref, gb_ref, slc_ref, bias_ref, nk_ref, nv_ref,
                       win_ref, nwk_ref, nwv_ref, bw_ref, o_ref, gbuf, sem, *, t_pos, n_past):
    b = pl.program_id(0)
    n_win = win_ref.shape[0] // ROW_W

    def block_copy(g, k):
        n = jnp.minimum(sel_ref[(b * NSA_G + g) * TOPN + k], n_past - 1)
        half = pt_ref[b, n // 2] * 2 + n % 2
        return pltpu.make_async_copy(slc_ref.at[pl.ds(pl.multiple_of(half * HALF_ROWS, HALF_ROWS), HALF_ROWS), :],
                                     gbuf.at[g * TOPN + k], sem)

    for g in range(NSA_G):
        for k in range(TOPN):
            block_copy(g, k).start()

    c = lax.broadcasted_iota(jnp.int32, (1, n_win), 1)
    dist = n_win - c
    valid = (dist >= 0) & (dist <= WINDOW) & (t_pos - dist >= 0)
    qs, o_w = [], []
    for g in range(NSA_G):
        q = q_ref[g] * (DH ** -0.5)
        q_bf = q.astype(BF16)
        qs.append((q, q_bf))
        kw = win_ref[pl.ds(2 * g, n_win, stride=ROW_W), :].astype(BF16)
        vw = win_ref[pl.ds(2 * g + 1, n_win, stride=ROW_W), :].astype(BF16)
        bw = bw_ref[g]
        s_w = jnp.where(valid, _nt(q_bf, kw) + bw[:, :n_win], -jnp.inf)
        s_n = jnp.sum(q * nwk_ref[g, 0:1, :], axis=-1, keepdims=True) + bw[:, n_win:n_win + 1]
        m_w = jnp.maximum(jnp.max(s_w, axis=-1, keepdims=True), s_n)
        e_w = jnp.exp(s_w - m_w)
        e_n = jnp.exp(s_n - m_w)
        den = jnp.maximum(jnp.sum(e_w, axis=-1, keepdims=True) + e_n, 1e-30)
        o_w.append((_dot(e_w.astype(BF16), vw) + e_n * nwv_ref[g, 0:1, :]) / den)

    for g in range(NSA_G):
        for k in range(TOPN):
            block_copy(g, k).wait()

    row0 = lax.broadcasted_iota(jnp.int32, (BLK, 1), 0) == 0
    j_row = lax.broadcasted_iota(jnp.int32, (1, BLK), 1)
    for g in range(NSA_G):
        q, q_bf = qs[g]
        scores, values = [], []
        for k in range(TOPN):
            n = sel_ref[(b * NSA_G + g) * TOPN + k]
            is_new = (n == n_past) & row0
            kb = jnp.where(is_new, nk_ref[g, 0:1, :], gbuf[g * TOPN + k, pl.ds(2 * g, BLK, stride=ROW_W), :])
            vb = jnp.where(is_new, nv_ref[g, 0:1, :], gbuf[g * TOPN + k, pl.ds(2 * g + 1, BLK, stride=ROW_W), :])
            s = _nt(q_bf, kb.astype(BF16)) + bias_ref[g, n]
            scores.append(jnp.where(t_pos - (n * BLK + j_row) >= 0, s, -jnp.inf))
            values.append(vb.astype(BF16))
        m = functools.reduce(jnp.maximum, [jnp.max(s, axis=-1, keepdims=True) for s in scores])
        m = jnp.where(m > -jnp.inf, m, 0.0)
        probs = [jnp.exp(s - m) for s in scores]
        l = functools.reduce(jnp.add, [jnp.sum(p, axis=-1, keepdims=True) for p in probs])
        acc = functools.reduce(jnp.add, [_dot(p.astype(BF16), v) for p, v in zip(probs, values)])
        o_s = acc / jnp.maximum(l, 1e-30)
        gates = _sigmoid(gl_ref[g] + gb_ref[g])
        o_ref[g] = gates[:, 0:1] * oc_ref[g] + gates[:, 1:2] * o_s + gates[:, 2:3] * o_w[g]


def _slc_decode(sel_flat, page_table, q8, o_c, gate_l, gate_b, slc_cache, bias_sd, new_k, new_v,
                win_cache, new_wk, new_wv, bias_wd, t_pos):
    nd = q8.shape[0]
    n_past = page_table.shape[1] * (PAGE // BLK)
    win_rows = win_cache.shape[0] // nd

    def per_b(b, sel, pt):
        return (b, 0, 0, 0)

    def whole(shape):
        return pl.BlockSpec(shape, lambda b, sel, pt: (0,) * len(shape), pipeline_mode=pl.Buffered(1))

    b_spec = pl.BlockSpec((None, NSA_G, 8, DH), per_b)
    grid_spec = pltpu.PrefetchScalarGridSpec(
        num_scalar_prefetch=2,
        grid=(nd,),
        in_specs=[
            b_spec, b_spec, b_spec, whole(gate_b.shape),
            pl.BlockSpec(memory_space=pl.ANY),
            whole(bias_sd.shape),
            b_spec, b_spec,
            pl.BlockSpec((win_rows, DH), lambda b, sel, pt: (b, 0)),
            b_spec, b_spec,
            whole(bias_wd.shape),
        ],
        out_specs=b_spec,
        scratch_shapes=[pltpu.VMEM((NSA_G * TOPN, HALF_ROWS, DH), F32), pltpu.SemaphoreType.DMA(())],
    )
    return pl.pallas_call(
        functools.partial(_slc_decode_kernel, t_pos=t_pos, n_past=n_past),
        out_shape=jax.ShapeDtypeStruct((nd, NSA_G, 8, DH), F32),
        grid_spec=grid_spec,
        compiler_params=_cparams(("arbitrary",)),
        name="slc_decode",
    )(sel_flat, page_table, q8, o_c, gate_l, gate_b, slc_cache, bias_sd, new_k, new_v,
      win_cache, new_wk, new_wv, bias_wd)


def _t5_bucket(dist):
    n = jnp.maximum(dist, 0)
    nf = jnp.maximum(n, REL_MAX_EXACT).astype(F32)
    large = REL_MAX_EXACT + (jnp.log(nf / REL_MAX_EXACT) / math.log(REL_MAX_DIST / REL_MAX_EXACT)
                             * (REL_BUCKETS - REL_MAX_EXACT)).astype(jnp.int32)
    return jnp.where(n < REL_MAX_EXACT, n, jnp.minimum(large, REL_BUCKETS - 1))


def _bias_table(rel_bias, dist):
    onehot = (_t5_bucket(dist)[..., None] == jnp.arange(REL_BUCKETS)).astype(F32)
    b = jnp.einsum("...k,kh->...h", onehot, rel_bias, precision=HI)
    b = jnp.moveaxis(b, -1, 0)
    return b.reshape((NSA_G, NSA_R) + dist.shape)


def _prompt_bias_tables(rel_bias):
    i = jnp.arange(QBLK)[:, None]
    rel = jnp.arange(LANE)[None, :]
    bct = _bias_table(rel_bias, BLK * (rel - 1) + i % BLK + 1)
    d_near = SLC_PAD + i - jnp.arange(SLC_NEAR)[None, :]
    far = rel_bias[REL_BUCKETS - 1].reshape(NSA_G, NSA_R, 1, 1)
    bsn = _bias_table(rel_bias, d_near) - far + jnp.where(d_near >= 0, 0.0, NEG)
    d_win = WIN_PAD + i - jnp.arange(WIN_KEYS)[None, :]
    bw = _bias_table(rel_bias, d_win) + jnp.where((d_win >= 0) & (d_win <= WINDOW), 0.0, NEG)
    return bct, bsn, bw


def _pad_rows(a, rows, axis):
    pad = [(0, 0)] * a.ndim
    pad[axis] = (0, rows - a.shape[axis])
    return jnp.pad(a, pad)


def _decode_bias_tables(rel_bias, t_pos, n_past, n_win):
    n = jnp.arange(SEL_LANES)
    bias_cd = _pad_rows(_bias_table(rel_bias, t_pos - (n * BLK + BLK - 1)), 8, 1)
    tok = jnp.arange(n_past + 1)[:, None] * BLK + jnp.arange(BLK)[None, :]
    bias_sd = _pad_rows(jnp.swapaxes(_bias_table(rel_bias, t_pos - tok), 1, 2), 8, 2)
    c = jnp.arange(n_win + LANE)
    bias_wd = _pad_rows(_bias_table(rel_bias, n_win - c), 8, 1)
    return bias_cd, bias_sd, bias_wd


def kernel(x_prompt, x_sample, mem_prompt, cache_cmp_kv, cache_slc_kv, state_win_kv, state_gla, cache_mem_kv,
           page_table, norm_g, w_ffn_gate, w_ffn_up, w_ffn_down, w_in_gla, w_in_nsa, w_out, mem_norm_g, w_mem_kv,
           w_gla_a2, b_gla_a, gla_onorm_g, nsa_gate_b, cmp_pe, cmp_w1, cmp_b1, cmp_w2, rel_bias):
    nb, seq, _ = x_prompt.shape
    nd = x_sample.shape[0]
    depth = norm_g.shape[0]
    n_pages = page_table.shape[1]
    past_len = n_pages * PAGE
    n_past = past_len // BLK
    n_win = state_win_kv.shape[2]
    sr = SAMPLE_ROWS

    xp = x_prompt.reshape(nb * seq, D_MODEL)
    xs = _pad_rows(x_sample.reshape(nd, D_MODEL), sr, 0)
    mem_x = mem_prompt.reshape(nb * MEM_LEN, D_MODEL)

    w_o = w_out.astype(BF16)
    w_mkv = w_mem_kv.astype(BF16)
    qkvr = 2 * GLA_HEADS * GLA_DK + 2 * TOK_W
    wt = jnp.swapaxes(w_in_gla, 1, 2).astype(BF16)
    w_gla = jnp.concatenate(
        [wt[:, :qkvr], wt[:, qkvr + GLA_RANK:], wt[:, qkvr:qkvr + GLA_RANK],
         jnp.zeros((wt.shape[0], GLA_N - wt.shape[1], D_MODEL), BF16)], axis=1)
    n_gate = 3 * NSA_HEADS
    wt = jnp.swapaxes(w_in_nsa, 1, 2).astype(BF16)
    w_nsa = jnp.concatenate(
        [wt[:, :TOK_W], wt[:, TOK_W + 6 * KV_W + n_gate:], wt[:, TOK_W:TOK_W + 6 * KV_W + n_gate],
         jnp.zeros((wt.shape[0], NSA_N - wt.shape[1], D_MODEL), BF16)], axis=1)
    wa_pad = _pad_rows(w_gla_a2, LANE, 1)
    w1 = cmp_w1.astype(BF16)
    w2 = cmp_w2.astype(BF16)

    def ffn_both(x_p, x_s, i, j):
        g1, g2 = norm_g[i, 4 * j][None], norm_g[i, 4 * j + 1][None]
        x_s, wg, wu, wd = _ffn_half(x_s, g1, w_ffn_gate, w_ffn_up, w_ffn_down, g2, TM_SAMPLE, f32_weights_at=(i, j))
        return _ffn_half(x_p, g1, wg, wu, wd, g2, TM_FFN), x_s

    def every(a, step):
        return a.reshape(nd, step, a.shape[-1])[:, 0]

    def per_seq(a):
        return _pad_rows(a[:, None, :], sr, 1).reshape(nd * sr, a.shape[-1])

    outs = dict(gla_p=[], gla_s=[], cmp_p=[], cmp_s=[], slc_p=[], slc_s=[], win_p=[], win_s=[], mem_p=[])
    for i in range(depth):
        li = i // 2
        mem_kv_p = _norm_matmul(mem_x, mem_norm_g[i][None], w_mkv, i, TM_MEMKV, MEM_W)
        outs["mem_p"].append(mem_kv_p.reshape(nb, MEM_LEN, 2, N_MEM_HEADS, MEM_HEAD_DIM))
        mem_kv_p = mem_kv_p.reshape(nb, MEM_LEN, 2 * MEM_W)
        mem_kv_s = cache_mem_kv[i].reshape(nd, MEM_LEN, 2 * MEM_W)
        xp, xs = ffn_both(xp, xs, i, 0)
        g_mix = norm_g[i, 2][None]
        if i % 2 == 0:
            proj_p = _norm_matmul(xp, g_mix, w_gla, li, TM_PROJ, TN_PROJ, w_transposed=True)
            proj_s = _norm_matmul(xs, g_mix, w_gla, li, TM_SAMPLE, TN_PROJ, w_transposed=True)
            mem_col = (qkvr) // MEM_W
            b_a = b_gla_a[li][None]
            gn = gla_onorm_g[li][None]
            s0 = jnp.zeros((nb, GLA_HEADS, GLA_DK, GLA_DV), F32)
            tok_p, sp = _gla(proj_p, s0, wa_pad[li], b_a, gn, seq, TL_GLA, GLA_CHUNK, seq)
            tok_s, ss = _gla(per_seq(proj_s[:nd]), state_gla[li], wa_pad[li], b_a, gn, sr, sr, sr, 1)
            tok_s = _pad_rows(every(tok_s, sr), sr, 0)
            outs["gla_p"].append(sp)
            outs["gla_s"].append(ss)
        else:
            proj_p = _norm_matmul(xp, g_mix, w_nsa, li, TM_PROJ, TN_PROJ, w_transposed=True)
            proj_s = _norm_matmul(xs, g_mix, w_nsa, li, TM_SAMPLE, TN_PROJ, w_transposed=True)
            mem_col = NSA_MEM_COL // MEM_W
            b1 = cmp_b1[li][:, None, :]
            gate_b = nsa_gate_b[li]
            bct, bsn, bw = _prompt_bias_tables(rel_bias)
            proj3 = proj_p.reshape(nb, seq, NSA_N)
            cmp_kv = _cmp_prompt(proj3, cmp_pe[li], w1[li], b1, w2[li])
            tok_p = _nsa_prompt(proj_p, cmp_kv, _pad_rows(gate_b[None], LANE, 1), bct, bsn, bw, nb, seq)
            kv_p = proj3[:, :, NSA_KV_COL:NSA_GATE_COL].reshape(nb, seq, 3, 2, NSA_G, DH)
            outs["cmp_p"].append(kv_p[:, :, 0].reshape(nb, seq // PAGE, PAGE, 2, NSA_G, DH))
            outs["slc_p"].append(kv_p[:, :, 1].reshape(nb, seq // PAGE, PAGE, 2, NSA_G, DH))
            outs["win_p"].append(kv_p[:, seq - n_win:, 2])
            t_pos = past_len
            bias_cd, bias_sd, bias_wd = _decode_bias_tables(rel_bias, t_pos, n_past, n_win)
            kv_s = proj_s[:nd, NSA_KV_COL:NSA_GATE_COL].reshape(nd, 3, 2, NSA_G, DH)
            outs["cmp_s"].append(kv_s[:, None, 0])
            outs["slc_s"].append(kv_s[:, None, 1])
            outs["win_s"].append(jnp.concatenate([state_win_kv[li][:, 1:], kv_s[:, None, 2]], axis=1))
            q8 = _pad_rows(proj_s[:nd, :TOK_W].reshape(nd, NSA_G, NSA_R, DH), 8, 2)
            new_cmp = _pad_rows(jnp.moveaxis(kv_s[:, 0], 1, 0).reshape(2, nd * NSA_G, DH), 32, 1)
            cmp_kv_s, cmp_last = _cmp_decode(page_table, _linear_cache(cache_cmp_kv[li]), new_cmp,
                                             cmp_pe[li], w1[li], b1, w2[li])
            o_c, sel = _sel_decode(q8, cmp_kv_s, cmp_last, bias_cd, t_pos)
            sel_flat = sel[:, :, 0, :TOPN].reshape(-1)

            def row8(a):
                return _pad_rows(a[:, :, None, :], 8, 2)

            gate_l = _pad_rows(_pad_rows(proj_s[:nd, NSA_GATE_COL:NSA_GATE_COL + n_gate].reshape(nd, NSA_G, NSA_R, 3),
                                         8, 2), LANE, 3)
            gate_b8 = _pad_rows(_pad_rows(gate_b.reshape(NSA_G, NSA_R, 3), 8, 1), LANE, 2)
            tok_s = _slc_decode(sel_flat, page_table, q8, o_c, gate_l, gate_b8,
                                _linear_cache(cache_slc_kv[li]), bias_sd,
                                row8(kv_s[:, 1, 0]), row8(kv_s[:, 1, 1]),
                                _linear_cache(state_win_kv[li]),
                                row8(kv_s[:, 2, 0]), row8(kv_s[:, 2, 1]), bias_wd, t_pos)
            tok_s = _pad_rows(tok_s[:, :, :NSA_R].reshape(nd, TOK_W), sr, 0).astype(BF16)
        mem_o_p = _mem_attn(proj_p, mem_col, mem_kv_p, seq, TM_MEM_ATTN)
        q_s = jnp.broadcast_to(proj_s[:nd, None, mem_col * MEM_W:(mem_col + 1) * MEM_W],
                               (nd, sr, MEM_W)).reshape(nd * sr, MEM_W)
        mem_o_s = _pad_rows(every(_mem_attn(q_s, 0, mem_kv_s, sr, sr), sr), sr, 0)
        xp = _out_proj(xp, tok_p, mem_o_p, w_o, i, norm_g[i, 3][None], TM_OUT)
        xs = _out_proj(xs, tok_s, mem_o_s, w_o, i, norm_g[i, 3][None], sr)
        xp, xs = ffn_both(xp, xs, i, 1)

    y_prompt = xp.reshape(nb, seq, D_MODEL)
    y_sample = xs[:nd].reshape(nd, 1, D_MODEL)
    st = lambda k: jnp.stack(outs[k])
    return (y_prompt, y_sample, st("gla_p"), st("cmp_p"), st("slc_p"), st("win_p"), st("mem_p"),
            st("gla_s"), st("cmp_s"), st("slc_s"), st("win_s"))
```

```python
import functools
import math

import jax
import jax.numpy as jnp
from jax import lax
from jax.experimental import pallas as pl
from jax.experimental.pallas import tpu as pltpu

F32 = jnp.float32
BF16 = jnp.bfloat16
HI = lax.Precision.HIGHEST

D_MODEL = 2048
D_FF = 5632
EPS = 1e-6
MEM_LEN = 256
N_MEM_HEADS = 4
MEM_HEAD_DIM = 128
MEM_W = N_MEM_HEADS * MEM_HEAD_DIM
TOK_W = D_MODEL - MEM_W
GLA_HEADS = 4
GLA_DV = TOK_W // GLA_HEADS
GLA_DK = GLA_DV // 2
GLA_RANK = 16
GLA_TAU = 16.0
GLA_CHUNK = 64
GLA_PAIR_W = 2 * GLA_DK
DH = 128
NSA_HEADS = TOK_W // DH
NSA_G = 3
NSA_R = NSA_HEADS // NSA_G
BLK = 64
TOPN = 16
WINDOW = 512
CMP_HID = 256
QBLK = 256
KV_W = NSA_G * DH
REL_BUCKETS = 32
REL_MAX_EXACT = 16
REL_MAX_DIST = 128
PAGE = 128
LANE = 128
HALF_LANE = LANE // 2
BLK_SHIFT = BLK.bit_length() - 1
VMEM_LIMIT = 56 * 1024 * 1024

TN_PROJ = 768
GLA_MAIN = 2 * GLA_HEADS * GLA_DK + 2 * TOK_W
GLA_A_COL = GLA_MAIN + MEM_W
GLA_N = GLA_MAIN + TN_PROJ
NSA_KV_COL = TOK_W
NSA_GATE_COL = NSA_KV_COL + 6 * KV_W
NSA_MEM_COL = NSA_GATE_COL + TN_PROJ - MEM_W
NSA_N = NSA_GATE_COL + TN_PROJ
SAMPLE_ROWS = 16
TM_SAMPLE = SAMPLE_ROWS
TM_FFN = 512
TM_PROJ = 1024
TM_MEMKV = 512
TM_OUT = 256
TM_MEM_ATTN = 512
TL_GLA = 512
SLC_PAD = 256
WIN_PAD = WINDOW
WIN_KEYS = WIN_PAD + QBLK
SLC_NEAR = SLC_PAD + QBLK
FAR_CHUNK = 1024
NEG = -1e30
M_FLOOR = -1e29


def _cparams(sem, vmem=VMEM_LIMIT):
    return pltpu.CompilerParams(dimension_semantics=sem, vmem_limit_bytes=vmem)


def _sigmoid(x):
    return 1.0 / (1.0 + jnp.exp(-x))


def _rms(x, g):
    ms = jnp.mean(x * x, axis=-1, keepdims=True)
    return x * lax.rsqrt(ms + EPS) * g


def _nt(a, b, precision=None):
    return lax.dot_general(a, b, (((1,), (1,)), ((), ())), precision=precision,
                           preferred_element_type=F32)


def _tn(a, b, precision=None):
    return lax.dot_general(a, b, (((0,), (0,)), ((), ())), precision=precision,
                           preferred_element_type=F32)


def _dot(a, b, precision=None):
    return jnp.dot(a, b, precision=precision, preferred_element_type=F32)


def _split3(x):
    hi = x.astype(BF16)
    r1 = x - hi.astype(F32)
    mid = r1.astype(BF16)
    lo = (r1 - mid.astype(F32)).astype(BF16)
    return hi, mid, lo


def _ffn_kernel(x_ref, g1_ref, wg_ref, wu_ref, wd_ref, g2_ref, o_ref, *rest):
    bf_out, (xn_ref, acc_ref) = rest[:-2], rest[-2:]
    j = pl.program_id(1)

    @pl.when(j == 0)
    def _():
        xn_ref[...] = _rms(x_ref[...], g1_ref[...]).astype(BF16)
        acc_ref[...] = jnp.zeros_like(acc_ref)

    wg, wu, wd = (w_ref[...].astype(BF16) for w_ref in (wg_ref, wu_ref, wd_ref))
    for out_ref, w in zip(bf_out, (wg, wu, wd)):
        out_ref[...] = w
    xn = xn_ref[...]
    gate = _dot(xn, wg)
    up = _dot(xn, wu)
    h = (gate * _sigmoid(gate) * up).astype(BF16)
    acc_ref[...] += _dot(h, wd)

    @pl.when(j == pl.num_programs(1) - 1)
    def _():
        o_ref[...] = x_ref[...] + 0.5 * _rms(acc_ref[...], g2_ref[...])


def _ffn_half(x, g1, wg, wu, wd, g2, tm, tf=512, f32_weights_at=None):
    m = x.shape[0]
    y_shape = jax.ShapeDtypeStruct((m, D_MODEL), F32)
    y_spec = pl.BlockSpec((tm, D_MODEL), lambda i, j: (i, 0))
    col_spec = pl.BlockSpec((D_MODEL, tf), lambda i, j: (0, j))
    row_spec = pl.BlockSpec((tf, D_MODEL), lambda i, j: (j, 0))
    if f32_weights_at is None:
        w_specs = [col_spec, col_spec, row_spec]
        out_shape, out_specs = y_shape, y_spec
    else:
        assert m == tm, "the bf16 copies are written once, by a single row tile"
        layer, half = f32_weights_at
        w_specs = [pl.BlockSpec((None, None, D_MODEL, tf), lambda i, j: (layer, half, 0, j)),
                   pl.BlockSpec((None, None, D_MODEL, tf), lambda i, j: (layer, half, 0, j)),
                   pl.BlockSpec((None, None, tf, D_MODEL), lambda i, j: (layer, half, j, 0))]
        out_shape = (y_shape, jax.ShapeDtypeStruct((D_MODEL, D_FF), BF16),
                     jax.ShapeDtypeStruct((D_MODEL, D_FF), BF16), jax.ShapeDtypeStruct((D_FF, D_MODEL), BF16))
        out_specs = (y_spec, col_spec, col_spec, row_spec)
    return pl.pallas_call(
        _ffn_kernel,
        out_shape=out_shape,
        grid=(m // tm, D_FF // tf),
        in_specs=[y_spec, pl.BlockSpec((1, D_MODEL), lambda i, j: (0, 0))] + w_specs
                 + [pl.BlockSpec((1, D_MODEL), lambda i, j: (0, 0))],
        out_specs=out_specs,
        scratch_shapes=[pltpu.VMEM((tm, D_MODEL), BF16), pltpu.VMEM((tm, D_MODEL), F32)],
        compiler_params=_cparams(("parallel", "arbitrary")),
        name="ffn_half",
    )(x, g1, wg, wu, wd, g2)


def _norm_matmul_kernel(x_ref, g_ref, w_ref, o_ref, xn_ref):
    @pl.when(pl.program_id(1) == 0)
    def _():
        xn_ref[...] = _rms(x_ref[...], g_ref[...]).astype(BF16)

    o_ref[...] = _dot(xn_ref[...], w_ref[...])


def _norm_matmul(x, g, w, layer, tm, tn):
    m, n = x.shape[0], w.shape[2]
    return pl.pallas_call(
        _norm_matmul_kernel,
        out_shape=jax.ShapeDtypeStruct((m, n), F32),
        grid=(m // tm, n // tn),
        in_specs=[
            pl.BlockSpec((tm, D_MODEL), lambda i, j: (i, 0)),
            pl.BlockSpec((1, D_MODEL), lambda i, j: (0, 0)),
            pl.BlockSpec((None, D_MODEL, tn), lambda i, j: (layer, 0, j)),
        ],
        out_specs=pl.BlockSpec((tm, tn), lambda i, j: (i, j)),
        scratch_shapes=[pltpu.VMEM((tm, D_MODEL), BF16)],
        compiler_params=_cparams(("parallel", "arbitrary")),
        name="norm_matmul",
    )(x, g, w)


def _in_proj_kernel(x_ref, g_ref, wm_ref, wt_ref, o_ref, xn_ref, *, n_main):
    j = pl.program_id(1)

    @pl.when(j == 0)
    def _():
        xn_ref[...] = _rms(x_ref[...], g_ref[...]).astype(BF16)

    @pl.when(j < n_main)
    def _():
        o_ref[...] = _nt(xn_ref[...], wm_ref[...])

    @pl.when(j == n_main)
    def _():
        o_ref[...] = _nt(xn_ref[...], wt_ref[...])


def _in_proj(x, g, w_main, main_cols, w_tail, layer, tm):
    m = x.shape[0]
    n_main = main_cols // TN_PROJ
    return pl.pallas_call(
        functools.partial(_in_proj_kernel, n_main=n_main),
        out_shape=jax.ShapeDtypeStruct((m, main_cols + TN_PROJ), F32),
        grid=(m // tm, n_main + 1),
        in_specs=[
            pl.BlockSpec((tm, D_MODEL), lambda i, j: (i, 0)),
            pl.BlockSpec((1, D_MODEL), lambda i, j: (0, 0)),
            pl.BlockSpec((None, TN_PROJ, D_MODEL), lambda i, j: (layer, jnp.minimum(j, n_main - 1), 0)),
            pl.BlockSpec((None, TN_PROJ, D_MODEL), lambda i, j: (layer, 0, 0)),
        ],
        out_specs=pl.BlockSpec((tm, TN_PROJ), lambda i, j: (i, j)),
        scratch_shapes=[pltpu.VMEM((tm, D_MODEL), BF16)],
        compiler_params=_cparams(("parallel", "arbitrary")),
        name="in_proj",
    )(x, g, w_main, w_tail)


def _out_proj_kernel(x_ref, tok_ref, mem_ref, wt_ref, wm_ref, g_ref, o_ref):
    y = _dot(tok_ref[...], wt_ref[...]) + _dot(mem_ref[...], wm_ref[...])
    o_ref[...] = x_ref[...] + _rms(y, g_ref[...])


def _out_proj(x, tok, mem_o, w_o, layer, g, tm):
    m = x.shape[0]
    return pl.pallas_call(
        _out_proj_kernel,
        out_shape=jax.ShapeDtypeStruct((m, D_MODEL), F32),
        grid=(m // tm,),
        in_specs=[
            pl.BlockSpec((tm, D_MODEL), lambda i: (i, 0)),
            pl.BlockSpec((tm, TOK_W), lambda i: (i, 0)),
            pl.BlockSpec((tm, MEM_W), lambda i: (i, 0)),
            pl.BlockSpec((None, TOK_W, D_MODEL), lambda i: (layer, 0, 0), pipeline_mode=pl.Buffered(1)),
            pl.BlockSpec((None, MEM_W, D_MODEL), lambda i: (layer, TOK_W // MEM_W, 0), pipeline_mode=pl.Buffered(1)),
            pl.BlockSpec((1, D_MODEL), lambda i: (0, 0)),
        ],
        out_specs=pl.BlockSpec((tm, D_MODEL), lambda i: (i, 0)),
        compiler_params=_cparams(("parallel",)),
        name="out_proj",
    )(x, tok, mem_o, w_o, w_o, g)


def _mem_attn_kernel(q_ref, kv_ref, o_ref):
    for h in range(N_MEM_HEADS):
        q = (q_ref[:, h * DH:(h + 1) * DH] * (MEM_HEAD_DIM ** -0.5)).astype(BF16)
        k = kv_ref[:, h * DH:(h + 1) * DH].astype(BF16)
        v = kv_ref[:, MEM_W + h * DH:MEM_W + (h + 1) * DH].astype(BF16)
        s = _nt(q, k)
        e = jnp.exp(s - jnp.max(s, axis=-1, keepdims=True))
        p = e / jnp.sum(e, axis=-1, keepdims=True)
        o_ref[:, h * DH:(h + 1) * DH] = _dot(p.astype(BF16), v).astype(o_ref.dtype)


def _mem_attn(q_arr, q_col_block, mem_kv, rows_per_batch, tm):
    nb = mem_kv.shape[0]
    per = rows_per_batch // tm
    return pl.pallas_call(
        _mem_attn_kernel,
        out_shape=jax.ShapeDtypeStruct((nb * rows_per_batch, MEM_W), BF16),
        grid=(nb, per),
        in_specs=[
            pl.BlockSpec((tm, MEM_W), lambda b, i: (b * per + i, q_col_block)),
            pl.BlockSpec((None, MEM_LEN, 2 * MEM_W), lambda b, i: (b, 0, 0)),
        ],
        out_specs=pl.BlockSpec((tm, MEM_W), lambda b, i: (b * per + i, 0)),
        compiler_params=_cparams(("parallel", "parallel")),
        name="mem_attn",
    )(q_arr, mem_kv)


GLA_PAIRS = GLA_HEADS // 2


def _gla_kernel(q_ref, k_ref, v_ref, r_ref, a_ref, s0_ref, wa_ref, ba_ref, gn_ref,
                tok_ref, s_out_ref, s_ref, la_ref, *, chunk, n_valid):
    l = pl.program_id(1)
    tl = q_ref.shape[0]

    @pl.when(l == 0)
    def _():
        for p in range(GLA_PAIRS):
            s_ref[p] = s0_ref[2 * p:2 * p + 2].reshape(GLA_PAIR_W, GLA_DV).T

    lane = lax.broadcasted_iota(jnp.int32, (1, GLA_PAIR_W), 1)
    head_mask = [(lane < GLA_DK).astype(F32), (lane >= GLA_DK).astype(F32)]
    ti = lax.broadcasted_iota(jnp.int32, (chunk, chunk), 0)
    si = lax.broadcasted_iota(jnp.int32, (chunk, chunk), 1)
    causal = si <= ti
    tri = jnp.where(causal, 1.0, 0.0).astype(BF16)

    a_hi, a_lo, _ = _split3(a_ref[...])
    w_hi, w_lo, _ = _split3(wa_ref[...])
    z = _dot(a_hi, w_hi) + _dot(a_lo, w_hi) + _dot(a_hi, w_lo) + ba_ref[...]
    la_all = -(jnp.maximum(-z, 0.0) + jnp.log1p(jnp.exp(-jnp.abs(z)))) / GLA_TAU
    pos = l * tl + lax.broadcasted_iota(jnp.int32, (tl, 1), 0)
    la_ref[...] = jnp.where(pos < n_valid, la_all, 0.0)

    def step(ci, carry):
        r0 = pl.multiple_of(ci * chunk, chunk)
        rows = pl.ds(r0, chunk)
        for p in range(GLA_PAIRS):
            pair = slice(p * GLA_PAIR_W, (p + 1) * GLA_PAIR_W)
            b = functools.reduce(jnp.add, [_dot(tri, piece) for piece in _split3(la_ref[rows, pair])])
            bl = b[chunk - 1:chunk, :]
            q = q_ref[rows, pair] * (GLA_DK ** -0.5)
            k = k_ref[rows, pair]
            qe = q * jnp.exp(b)
            ke = (k * jnp.exp(-b)).astype(BF16)
            kd = k * jnp.exp(bl - b)
            st_old = s_ref[p]
            st_bf = st_old.astype(BF16)
            upd = None
            for h in range(2):
                head = slice((2 * p + h) * GLA_DV, (2 * p + h + 1) * GLA_DV)
                v = v_ref[rows, head].astype(BF16)
                qm = (qe * head_mask[h]).astype(BF16)
                att = jnp.where(causal, _nt(qm, ke), 0.0)
                o = _nt(qm, st_bf) + _dot(att.astype(BF16), v)
                o = _rms(o, gn_ref[...])
                r = r_ref[rows, head]
                tok_ref[rows, head] = (o * (r * _sigmoid(r))).astype(tok_ref.dtype)
                u = _tn(v, (kd * head_mask[h]).astype(BF16))
                upd = u if upd is None else upd + u
            s_ref[p] = jnp.exp(bl) * st_old + upd
        return carry

    lax.fori_loop(0, tl // chunk, step, 0)

    @pl.when(l == pl.num_programs(1) - 1)
    def _():
        for p in range(GLA_PAIRS):
            s_out_ref[2 * p:2 * p + 2] = s_ref[p].T.reshape(2, GLA_DK, GLA_DV)


def _gla(proj, s0, wa_pad, b_a, gn, seq, tl, chunk, n_valid):
    nb = s0.shape[0]
    per = seq // tl
    qk_w = GLA_HEADS * GLA_DK
    row = lambda b, l: b * per + l

    return pl.pallas_call(
        functools.partial(_gla_kernel, chunk=chunk, n_valid=n_valid),
        out_shape=(jax.ShapeDtypeStruct((nb * seq, TOK_W), BF16),
                   jax.ShapeDtypeStruct((nb, GLA_HEADS, GLA_DK, GLA_DV), F32)),
        grid=(nb, per),
        in_specs=[
            pl.BlockSpec((tl, qk_w), lambda b, l: (row(b, l), 0)),
            pl.BlockSpec((tl, qk_w), lambda b, l: (row(b, l), 1)),
            pl.BlockSpec((tl, TOK_W), lambda b, l: (row(b, l), 2 * qk_w // TOK_W)),
            pl.BlockSpec((tl, TOK_W), lambda b, l: (row(b, l), 2 * qk_w // TOK_W + 1)),
            pl.BlockSpec((tl, LANE), lambda b, l: (row(b, l), GLA_A_COL // LANE)),
            pl.BlockSpec((None, GLA_HEADS, GLA_DK, GLA_DV), lambda b, l: (b, 0, 0, 0)),
            pl.BlockSpec((LANE, qk_w), lambda b, l: (0, 0)),
            pl.BlockSpec((1, qk_w), lambda b, l: (0, 0)),
            pl.BlockSpec((1, GLA_DV), lambda b, l: (0, 0)),
        ],
        out_specs=(pl.BlockSpec((tl, TOK_W), lambda b, l: (row(b, l), 0)),
                   pl.BlockSpec((None, GLA_HEADS, GLA_DK, GLA_DV), lambda b, l: (b, 0, 0, 0))),
        scratch_shapes=[pltpu.VMEM((GLA_PAIRS, GLA_DV, GLA_PAIR_W), F32), pltpu.VMEM((tl, qk_w), F32)],
        compiler_params=_cparams(("parallel", "arbitrary")),
        name="gla",
    )(proj, proj, proj, proj, proj, s0, wa_pad, b_a, gn)


def _masked_softmax(s, valid):
    s = jnp.where(valid, s, -jnp.inf)
    m = jnp.max(s, axis=-1, keepdims=True)
    m = jnp.where(m > -jnp.inf, m, 0.0)
    e = jnp.exp(s - m)
    return e / jnp.maximum(jnp.sum(e, axis=-1, keepdims=True), 1e-30)


def _online_update(state, s, v):
    m, l, acc = state
    r, nq, w = s.shape
    m_new = jnp.maximum(m, jnp.max(s, axis=-1, keepdims=True))
    alpha = jnp.exp(m - m_new)
    p = jnp.exp(s - m_new)
    l = alpha * l + jnp.sum(p, axis=-1, keepdims=True)
    pv = _dot(p.reshape(r * nq, w).astype(BF16), v).reshape(r, nq, DH)
    return m_new, l, alpha * acc + pv


def _compress_tail(xflat, w1, b1, w2):
    h = _dot(xflat, w1) + b1
    h = h * _sigmoid(h)
    return _dot(h.astype(BF16), w2)


def _cmp_prompt_kernel(x0_ref, x1_ref, x2_ref, pe_ref, w1_ref, b1_ref, w2_ref, o_ref, xflat_ref):
    x_refs = (x0_ref, x1_ref, x2_ref)
    nb, seq = x0_ref.shape[0], x0_ref.shape[1]
    nblk = seq // BLK
    for j in range(BLK):
        pe_j = pe_ref[j:j + 1, :]
        for b in range(nb):
            for g in range(NSA_G):
                xj = x_refs[g][b, pl.ds(j, nblk, stride=BLK), :]
                row = (b * NSA_G + g) * nblk
                xflat_ref[row:row + nblk, j * DH:(j + 1) * DH] = (xj + pe_j).astype(BF16)
    out = _compress_tail(xflat_ref[...], w1_ref[...], b1_ref[...], w2_ref[...])
    o_ref[...] = out.reshape(nb, NSA_G, nblk, DH)


def _cmp_prompt(proj3, pe, w1, b1, w2):
    nb, seq, _ = proj3.shape
    nblk = seq // BLK
    return pl.pallas_call(
        _cmp_prompt_kernel,
        out_shape=jax.ShapeDtypeStruct((2, nb, NSA_G, nblk, DH), F32),
        grid=(2,),
        in_specs=[
            pl.BlockSpec((nb, seq, DH), lambda kv: (0, 0, NSA_KV_COL // DH + NSA_G * kv)),
            pl.BlockSpec((nb, seq, DH), lambda kv: (0, 0, NSA_KV_COL // DH + NSA_G * kv + 1)),
            pl.BlockSpec((nb, seq, DH), lambda kv: (0, 0, NSA_KV_COL // DH + NSA_G * kv + 2)),
            pl.BlockSpec((None, BLK, DH), lambda kv: (kv, 0, 0)),
            pl.BlockSpec((None, BLK * DH, CMP_HID), lambda kv: (kv, 0, 0)),
            pl.BlockSpec((None, 1, CMP_HID), lambda kv: (kv, 0, 0)),
            pl.BlockSpec((None, CMP_HID, DH), lambda kv: (kv, 0, 0)),
        ],
        out_specs=pl.BlockSpec((None, nb, NSA_G, nblk, DH), lambda kv: (kv, 0, 0, 0, 0)),
        scratch_shapes=[pltpu.VMEM((nb * NSA_G * nblk, BLK * DH), BF16)],
        compiler_params=_cparams(("arbitrary",)),
        name="cmp_prompt",
    )(proj3, proj3, proj3, pe, w1, b1, w2)


def _nsa_prompt_kernel(q_ref, gl_ref, gb_ref, kc_ref, vc_ref, ks_ref, vs_ref, kw_ref, vw_ref,
                       bct_ref, bsn_ref, bw_ref, o_ref, ksb, vsb, kwb, vwb):
    g = pl.program_id(1)
    qi = pl.program_id(2)
    seq = ks_ref.shape[0]
    nblk = seq // BLK
    R = NSA_R

    @pl.when(qi == 0)
    def _():
        for src, dst, pad in ((vs_ref, vsb, SLC_PAD), (vw_ref, vwb, WIN_PAD)):
            dst[0:pad, :] = jnp.zeros((pad, DH), BF16)
            dst[pad:pad + seq, :] = src[...].astype(BF16)
        for src, dst, pad, per_block in ((ks_ref, ksb, SLC_PAD, True), (kw_ref, kwb, WIN_PAD, False)):
            dst[0:pad, 0:DH] = jnp.zeros((pad, DH), BF16)
            dst[pad:pad + seq, 0:DH] = src[...].astype(BF16)
            pos = lax.broadcasted_iota(jnp.int32, (pad + seq, 1), 0) - pad
            feat = lax.broadcasted_iota(jnp.int32, (1, DH), 1)
            masked = (feat == HALF_LANE) & (pos < 0)
            if per_block:
                masked = masked | ((pos >= 0) & (lax.shift_right_arithmetic(pos, BLK_SHIFT) == feat))
            dst[:, DH:2 * DH] = jnp.where(masked, NEG, 0.0).astype(BF16)

    q = q_ref[...] * (DH ** -0.5)
    q_all = jnp.concatenate([q[:, r * DH:(r + 1) * DH] for r in range(R)], axis=0)
    q_bf = q_all.astype(BF16)
    i_col = lax.broadcasted_iota(jnp.int32, (QBLK, 1), 0)
    t_col = qi * QBLK + i_col
    n_row = lax.broadcasted_iota(jnp.int32, (1, nblk), 1)
    row0 = pl.multiple_of(qi * QBLK, QBLK)
    lane2 = lax.broadcasted_iota(jnp.int32, (1, LANE), 1)

    s_c = _nt(q_all, kc_ref[...], HI).reshape(R, QBLK, nblk)
    tb_col = lax.shift_right_arithmetic(t_col, BLK_SHIFT)
    rel = tb_col - n_row
    bias_c = []
    for r in range(R):
        tab = bct_ref[r]
        bias_c.append(jnp.where(rel == 0, tab[:, 0:1],
                      jnp.where(rel == 1, tab[:, 1:2],
                      jnp.where(rel == 2, tab[:, 2:3], tab[:, 3:4]))))
    s_c = s_c + jnp.stack(bias_c, axis=0)
    valid_c = (t_col - (n_row * BLK + (BLK - 1))) >= 0
    p_c = _masked_softmax(s_c, valid_c[None])
    o_c = _dot(p_c.reshape(R * QBLK, nblk).astype(BF16), vc_ref[...].astype(BF16))

    imp = jnp.sum(p_c, axis=0)
    forced = (n_row == 0) | (n_row == tb_col) | (n_row == tb_col - 1)
    future = n_row * BLK > t_col
    score = jnp.where(forced, jnp.inf, jnp.where(future, -jnp.inf, imp))
    if nblk < HALF_LANE:
        score = jnp.concatenate([score, jnp.full((QBLK, HALF_LANE - nblk), -jnp.inf, F32)], axis=1)
    half = HALF_LANE // 2
    score2 = jnp.concatenate([score, score], axis=1)
    left = lane2 < HALF_LANE
    n2 = jnp.where(left, lane2, lane2 - HALF_LANE)
    rank2 = jnp.zeros((QBLK, LANE), F32)
    for i in range(half):
        col = jnp.where(left, score[:, i:i + 1], score[:, i + half:i + half + 1])
        wins_tie = jnp.where(n2 > jnp.where(left, i, i + half), 1.0, 0.0)
        rank2 = rank2 + jnp.where(col > score2, 1.0, jnp.where(col == score2, wins_tie, 0.0))
    rank = rank2 + jnp.concatenate([rank2[:, HALF_LANE:], rank2[:, :HALF_LANE]], axis=1)
    n_far = (qi * QBLK - SLC_PAD) // BLK
    dropped = rank >= float(min(TOPN, nblk))
    flag_all = jnp.where(left, jnp.where(dropped, 1.0, 0.0), jnp.where(lane2 == HALF_LANE, 1.0, 0.0))
    flag_far = jnp.where(left & (lane2 >= n_far), 1.0, flag_all)
    q_far = jnp.concatenate([q_bf, jnp.concatenate([flag_far.astype(BF16)] * R, axis=0)], axis=1)
    q_near = jnp.concatenate([q_bf, jnp.concatenate([flag_all.astype(BF16)] * R, axis=0)], axis=1)

    def far_body(kc_i, state):
        start = pl.multiple_of(SLC_PAD + kc_i * FAR_CHUNK, BLK)
        s = _nt(q_far, ksb[pl.ds(start, FAR_CHUNK), :]).reshape(R, QBLK, FAR_CHUNK)
        return _online_update(state, s, vsb[pl.ds(start, FAR_CHUNK), :])

    blk_per_chunk = FAR_CHUNK // BLK
    n_chunks = (jnp.maximum(n_far, 0) + (blk_per_chunk - 1)) // blk_per_chunk
    state = (jnp.full((R, QBLK, 1), M_FLOOR, F32), jnp.zeros((R, QBLK, 1), F32), jnp.zeros((R, QBLK, DH), F32))
    state = lax.fori_loop(0, n_chunks, far_body, state)

    s = _nt(q_near, ksb[pl.ds(row0, SLC_NEAR), :]).reshape(R, QBLK, SLC_NEAR) + bsn_ref[...]
    _, l_s, acc_s = _online_update(state, s, vsb[pl.ds(row0, SLC_NEAR), :])
    o_s = (acc_s / jnp.maximum(l_s, 1e-30)).reshape(R * QBLK, DH)

    s = _nt(q_near, kwb[pl.ds(row0, WIN_KEYS), :]).reshape(R, QBLK, WIN_KEYS) + bw_ref[...]
    e_w = jnp.exp(s - jnp.max(s, axis=-1, keepdims=True))
    p_w = e_w / jnp.maximum(jnp.sum(e_w, axis=-1, keepdims=True), 1e-30)
    o_w = _dot(p_w.reshape(R * QBLK, WIN_KEYS).astype(BF16), vwb[pl.ds(row0, WIN_KEYS), :])

    gates = _sigmoid(gl_ref[...] + gb_ref[...])
    src = lax.broadcasted_iota(jnp.int32, (LANE, LANE), 0)
    dst = lax.broadcasted_iota(jnp.int32, (LANE, LANE), 1)
    pick = jnp.where((src == g * (3 * R) + dst) & (dst < 3 * R), 1.0, 0.0)
    gsel = _dot(gates, pick, HI)
    for r in range(R):
        rows = slice(r * QBLK, (r + 1) * QBLK)
        o = (gsel[:, 3 * r:3 * r + 1] * o_c[rows] + gsel[:, 3 * r + 1:3 * r + 2] * o_s[rows]
             + gsel[:, 3 * r + 2:3 * r + 3] * o_w[rows])
        o_ref[:, r * DH:(r + 1) * DH] = o.astype(o_ref.dtype)


def _nsa_prompt(proj2, cmp_kv, gate_b_pad, bct, bsn, bw, nb, seq):
    nq = seq // QBLK
    nblk = seq // BLK
    assert nblk <= HALF_LANE, "mask features of the selected branch hold at most 64 key blocks"
    assert seq % FAR_CHUNK == 0
    proj3 = proj2.reshape(nb, seq, NSA_N)

    def kv_spec(col):
        return pl.BlockSpec((None, seq, DH), lambda b, g, qi: (b, 0, col // DH + g))

    return pl.pallas_call(
        _nsa_prompt_kernel,
        out_shape=jax.ShapeDtypeStruct((nb * seq, TOK_W), BF16),
        grid=(nb, NSA_G, nq),
        in_specs=[
            pl.BlockSpec((QBLK, NSA_R * DH), lambda b, g, qi: (b * nq + qi, g)),
            pl.BlockSpec((QBLK, LANE), lambda b, g, qi: (b * nq + qi, NSA_GATE_COL // LANE)),
            pl.BlockSpec((1, LANE), lambda b, g, qi: (0, 0)),
            pl.BlockSpec((None, None, None, nblk, DH), lambda b, g, qi: (0, b, g, 0, 0)),
            pl.BlockSpec((None, None, None, nblk, DH), lambda b, g, qi: (1, b, g, 0, 0)),
            kv_spec(NSA_KV_COL + 2 * KV_W), kv_spec(NSA_KV_COL + 3 * KV_W),
            kv_spec(NSA_KV_COL + 4 * KV_W), kv_spec(NSA_KV_COL + 5 * KV_W),
            pl.BlockSpec((None, NSA_R, QBLK, LANE), lambda b, g, qi: (g, 0, 0, 0)),
            pl.BlockSpec((None, NSA_R, QBLK, SLC_NEAR), lambda b, g, qi: (g, 0, 0, 0)),
            pl.BlockSpec((None, NSA_R, QBLK, WIN_KEYS), lambda b, g, qi: (g, 0, 0, 0)),
        ],
        out_specs=pl.BlockSpec((QBLK, NSA_R * DH), lambda b, g, qi: (b * nq + qi, g)),
        scratch_shapes=[pltpu.VMEM((SLC_PAD + seq, 2 * DH), BF16), pltpu.VMEM((SLC_PAD + seq, DH), BF16),
                        pltpu.VMEM((WIN_PAD + seq, 2 * DH), BF16), pltpu.VMEM((WIN_PAD + seq, DH), BF16)],
        compiler_params=_cparams(("parallel", "parallel", "arbitrary")),
        name="nsa_prompt",
    )(proj2, proj2, gate_b_pad, cmp_kv, cmp_kv, proj3, proj3, proj3, proj3, bct, bsn, bw)


SUB_PAGES = 8
RING = 4
ROW_W = 2 * NSA_G
PAGE_ROWS = PAGE * ROW_W


def _linear_cache(cache):
    return jnp.transpose(cache, (0, 1, 3, 2, 4)).reshape(-1, DH)


def _cmp_decode_kernel(pt_ref, cache_ref, new_ref, pe_ref, w1_ref, b1_ref, w2_ref, o_ref, last_ref,
                       buf, sem, xflat_ref, xlast_ref, *, n_sub):
    b = pl.program_id(0)
    total = pl.num_programs(0) * n_sub
    sub_blk = SUB_PAGES * PAGE // BLK
    seq_blk = n_sub * sub_blk

    def page_copy(s, p):
        page = pt_ref[s // n_sub, (s % n_sub) * SUB_PAGES + p]
        return pltpu.make_async_copy(cache_ref.at[pl.ds(pl.multiple_of(page * PAGE_ROWS, PAGE_ROWS), PAGE_ROWS), :],
                                     buf.at[s % RING, pl.ds(p * PAGE_ROWS, PAGE_ROWS), :], sem.at[s % RING])

    def start_sub(s):
        for p in range(SUB_PAGES):
            page_copy(s, p).start()

    @pl.when(b == 0)
    def _():
        for s in range(RING):
            start_sub(s)

    def body(i, c):
        s = b * n_sub + i
        for p in range(SUB_PAGES):
            page_copy(s, p).wait()
        slot = s % RING
        row0 = pl.multiple_of(i * sub_blk, sub_blk)
        for kv in range(2):
            for j in range(BLK):
                pe_j = pe_ref[kv, j:j + 1, :]
                for g in range(NSA_G):
                    xj = buf[slot, pl.ds(j * ROW_W + g * 2 + kv, sub_blk, stride=BLK * ROW_W), :]
                    xflat_ref[kv, pl.ds(g * seq_blk + row0, sub_blk), j * DH:(j + 1) * DH] = (xj + pe_j).astype(BF16)

        @pl.when(s + RING < total)
        def _():
            start_sub(s + RING)

        return c

    lax.fori_loop(0, n_sub, body, 0)
    for kv in range(2):
        out = _compress_tail(xflat_ref[kv], w1_ref[kv], b1_ref[kv], w2_ref[kv])
        o_ref[kv] = out.reshape(NSA_G, seq_blk, DH)

    @pl.when(b == 0)
    def _():
        rows = new_ref.shape[1]
        for kv in range(2):
            for j in range(BLK):
                pe_j = jnp.broadcast_to(pe_ref[kv, j:j + 1, :], (rows, DH))
                xj = new_ref[kv] + pe_j if j == 0 else pe_j
                xlast_ref[:, j * DH:(j + 1) * DH] = xj.astype(BF16)
            last_ref[kv] = _compress_tail(xlast_ref[...], w1_ref[kv], b1_ref[kv], w2_ref[kv])


def _cmp_decode(page_table, cache, new_rows, pe, w1, b1, w2):
    nd, n_pages = page_table.shape
    n_sub = n_pages // SUB_PAGES
    seq_blk = n_pages * PAGE // BLK
    rows = new_rows.shape[1]

    def whole(shape):
        return pl.BlockSpec(shape, lambda b, pt: (0,) * len(shape), pipeline_mode=pl.Buffered(1))

    grid_spec = pltpu.PrefetchScalarGridSpec(
        num_scalar_prefetch=1,
        grid=(nd,),
        in_specs=[
            pl.BlockSpec(memory_space=pl.ANY),
            whole(new_rows.shape), whole(pe.shape), whole(w1.shape), whole(b1.shape), whole(w2.shape),
        ],
        out_specs=(pl.BlockSpec((2, None, NSA_G, seq_blk, DH), lambda b, pt: (0, b, 0, 0, 0)),
                   pl.BlockSpec((2, rows, DH), lambda b, pt: (0, 0, 0))),
        scratch_shapes=[pltpu.VMEM((RING, SUB_PAGES * PAGE_ROWS, DH), F32),
                        pltpu.SemaphoreType.DMA((RING,)),
                        pltpu.VMEM((2, NSA_G * seq_blk, BLK * DH), BF16),
                        pltpu.VMEM((rows, BLK * DH), BF16)],
    )
    return pl.pallas_call(
        functools.partial(_cmp_decode_kernel, n_sub=n_sub),
        out_shape=(jax.ShapeDtypeStruct((2, nd, NSA_G, seq_blk, DH), F32),
                   jax.ShapeDtypeStruct((2, rows, DH), F32)),
        grid_spec=grid_spec,
        compiler_params=_cparams(("arbitrary",)),
        name="cmp_decode",
    )(page_table, cache, new_rows, pe, w1, b1, w2)


SEL_LANES = 384


def _sel_decode_kernel(q_ref, kc_ref, vc_ref, last_ref, bias_ref, oc_ref, sel_ref, *, t_pos):
    b = pl.program_id(0)
    n_past = kc_ref.shape[1]
    n_blocks = n_past + 1
    n_lane = lax.broadcasted_iota(jnp.int32, (1, SEL_LANES), 1)
    n_lane_f = n_lane.astype(F32)
    head_row = lax.broadcasted_iota(jnp.int32, (8, 1), 0) < NSA_R
    tb = t_pos // BLK
    for g in range(NSA_G):
        q = q_ref[g] * (DH ** -0.5)
        bias = bias_ref[g]
        s_p = _nt(q, kc_ref[g], HI) + bias[:, :n_past]
        row = b * NSA_G + g
        k_last = last_ref[0, pl.ds(row, 1), :]
        v_last = last_ref[1, pl.ds(row, 1), :]
        s_l = jnp.sum(q * k_last, axis=-1, keepdims=True) + bias[:, n_past:n_past + 1]
        valid_p = (t_pos - (n_lane[:, :n_past] * BLK + (BLK - 1))) >= 0
        valid_l = (t_pos - (n_past * BLK + (BLK - 1))) >= 0
        s_p = jnp.where(valid_p, s_p, -jnp.inf)
        s_l = jnp.where(valid_l, s_l, -jnp.inf)
        m = jnp.maximum(jnp.max(s_p, axis=-1, keepdims=True), s_l)
        m = jnp.where(m > -jnp.inf, m, 0.0)
        e_p = jnp.exp(s_p - m)
        e_l = jnp.exp(s_l - m)
        den = jnp.maximum(jnp.sum(e_p, axis=-1, keepdims=True) + e_l, 1e-30)
        p_p = e_p / den
        p_l = e_l / den
        oc_ref[g] = _dot(p_p, vc_ref[g], HI) + p_l * v_last
        imp_p = jnp.sum(jnp.where(head_row, p_p, 0.0), axis=0, keepdims=True)
        imp_l = jnp.sum(jnp.where(head_row, p_l, 0.0), axis=0, keepdims=True)
        imp = jnp.concatenate([imp_p, jnp.broadcast_to(imp_l, (1, SEL_LANES - n_past))], axis=1)
        forced = (n_lane == 0) | (n_lane == tb) | (n_lane == tb - 1)
        future = n_lane * BLK > t_pos
        score = jnp.where(forced, jnp.inf, jnp.where(future, -jnp.inf, imp))
        cand = n_lane < n_blocks
        sel = jnp.zeros((1, LANE), jnp.int32)
        k_lane = lax.broadcasted_iota(jnp.int32, (1, LANE), 1)
        for k in range(min(TOPN, n_blocks)):
            best = jnp.max(jnp.where(cand, score, -jnp.inf), axis=-1, keepdims=True)
            idx_f = jnp.min(jnp.where(cand & (score == best), n_lane_f, float(SEL_LANES)), axis=-1, keepdims=True)
            idx = idx_f.astype(jnp.int32)
            sel = jnp.where(k_lane == k, idx, sel)
            cand = cand & (n_lane != idx)
        sel_ref[g] = jnp.broadcast_to(sel, (8, LANE))


def _sel_decode(q8, cmp_kv, cmp_last, bias_cd, t_pos):
    nd = q8.shape[0]
    n_past = cmp_kv.shape[3]
    return pl.pallas_call(
        functools.partial(_sel_decode_kernel, t_pos=t_pos),
        out_shape=(jax.ShapeDtypeStruct((nd, NSA_G, 8, DH), F32),
                   jax.ShapeDtypeStruct((nd, NSA_G, 8, LANE), jnp.int32)),
        grid=(nd,),
        in_specs=[
            pl.BlockSpec((None, NSA_G, 8, DH), lambda b: (b, 0, 0, 0)),
            pl.BlockSpec((None, None, NSA_G, n_past, DH), lambda b: (0, b, 0, 0, 0)),
            pl.BlockSpec((None, None, NSA_G, n_past, DH), lambda b: (1, b, 0, 0, 0)),
            pl.BlockSpec(cmp_last.shape, lambda b: (0, 0, 0)),
            pl.BlockSpec(bias_cd.shape, lambda b: (0, 0, 0)),
        ],
        out_specs=(pl.BlockSpec((None, NSA_G, 8, DH), lambda b: (b, 0, 0, 0)),
                   pl.BlockSpec((None, NSA_G, 8, LANE), lambda b: (b, 0, 0, 0))),
        compiler_params=_cparams(("parallel",)),
        name="sel_decode",
    )(q8, cmp_kv, cmp_kv, cmp_last, bias_cd)


HALF_ROWS = BLK * ROW_W


def _slc_decode_kernel(sel_ref, pt_ref, q_ref, oc_ref, gl_ref, gb_ref, slc_ref, bias_ref, nk_ref, nv_ref,
                       win_ref, nwk_ref, nwv_ref, bw_ref, o_ref, gbuf, sem, *, t_pos, n_past):
    b = pl.program_id(0)
    n_win = win_ref.shape[0] // ROW_W

    def block_copy(g, k):
        n = jnp.minimum(sel_ref[(b * NSA_G + g) * TOPN + k], n_past - 1)
        half = pt_ref[b, n // 2] * 2 + n % 2
        return pltpu.make_async_copy(slc_ref.at[pl.ds(pl.multiple_of(half * HALF_ROWS, HALF_ROWS), HALF_ROWS), :],
                                     gbuf.at[g * TOPN + k], sem)

    for g in range(NSA_G):
        for k in range(TOPN):
            block_copy(g, k).start()

    c = lax.broadcasted_iota(jnp.int32, (1, n_win), 1)
    dist = n_win - c
    valid = (dist >= 0) & (dist <= WINDOW) & (t_pos - dist >= 0)
    qs, o_w = [], []
    for g in range(NSA_G):
        q = q_ref[g] * (DH ** -0.5)
        q_bf = q.astype(BF16)
        qs.append((q, q_bf))
        kw = win_ref[pl.ds(2 * g, n_win, stride=ROW_W), :].astype(BF16)
        vw = win_ref[pl.ds(2 * g + 1, n_win, stride=ROW_W), :].astype(BF16)
        bw = bw_ref[g]
        s_w = jnp.where(valid, _nt(q_bf, kw) + bw[:, :n_win], -jnp.inf)
        s_n = jnp.sum(q * nwk_ref[g, 0:1, :], axis=-1, keepdims=True) + bw[:, n_win:n_win + 1]
        m_w = jnp.maximum(jnp.max(s_w, axis=-1, keepdims=True), s_n)
        e_w = jnp.exp(s_w - m_w)
        e_n = jnp.exp(s_n - m_w)
        den = jnp.maximum(jnp.sum(e_w, axis=-1, keepdims=True) + e_n, 1e-30)
        o_w.append((_dot(e_w.astype(BF16), vw) + e_n * nwv_ref[g, 0:1, :]) / den)

    for g in range(NSA_G):
        for k in range(TOPN):
            block_copy(g, k).wait()

    row0 = lax.broadcasted_iota(jnp.int32, (BLK, 1), 0) == 0
    j_row = lax.broadcasted_iota(jnp.int32, (1, BLK), 1)
    for g in range(NSA_G):
        q, q_bf = qs[g]
        scores, values = [], []
        for k in range(TOPN):
            n = sel_ref[(b * NSA_G + g) * TOPN + k]
            is_new = (n == n_past) & row0
            kb = jnp.where(is_new, nk_ref[g, 0:1, :], gbuf[g * TOPN + k, pl.ds(2 * g, BLK, stride=ROW_W), :])
            vb = jnp.where(is_new, nv_ref[g, 0:1, :], gbuf[g * TOPN + k, pl.ds(2 * g + 1, BLK, stride=ROW_W), :])
            s = _nt(q_bf, kb.astype(BF16)) + bias_ref[g, n]
            scores.append(jnp.where(t_pos - (n * BLK + j_row) >= 0, s, -jnp.inf))
            values.append(vb.astype(BF16))
        m = functools.reduce(jnp.maximum, [jnp.max(s, axis=-1, keepdims=True) for s in scores])
        m = jnp.where(m > -jnp.inf, m, 0.0)
        probs = [jnp.exp(s - m) for s in scores]
        l = functools.reduce(jnp.add, [jnp.sum(p, axis=-1, keepdims=True) for p in probs])
        acc = functools.reduce(jnp.add, [_dot(p.astype(BF16), v) for p, v in zip(probs, values)])
        o_s = acc / jnp.maximum(l, 1e-30)
        gates = _sigmoid(gl_ref[g] + gb_ref[g])
        o_ref[g] = gates[:, 0:1] * oc_ref[g] + gates[:, 1:2] * o_s + gates[:, 2:3] * o_w[g]


def _slc_decode(sel_flat, page_table, q8, o_c, gate_l, gate_b, slc_cache, bias_sd, new_k, new_v,
                win_cache, new_wk, new_wv, bias_wd, t_pos):
    nd = q8.shape[0]
    n_past = page_table.shape[1] * (PAGE // BLK)
    win_rows = win_cache.shape[0] // nd

    def per_b(b, sel, pt):
        return (b, 0, 0, 0)

    def whole(shape):
        return pl.BlockSpec(shape, lambda b, sel, pt: (0,) * len(shape), pipeline_mode=pl.Buffered(1))

    b_spec = pl.BlockSpec((None, NSA_G, 8, DH), per_b)
    grid_spec = pltpu.PrefetchScalarGridSpec(
        num_scalar_prefetch=2,
        grid=(nd,),
        in_specs=[
            b_spec, b_spec, b_spec, whole(gate_b.shape),
            pl.BlockSpec(memory_space=pl.ANY),
            whole(bias_sd.shape),
            b_spec, b_spec,
            pl.BlockSpec((win_rows, DH), lambda b, sel, pt: (b, 0)),
            b_spec, b_spec,
            whole(bias_wd.shape),
        ],
        out_specs=b_spec,
        scratch_shapes=[pltpu.VMEM((NSA_G * TOPN, HALF_ROWS, DH), F32), pltpu.SemaphoreType.DMA(())],
    )
    return pl.pallas_call(
        functools.partial(_slc_decode_kernel, t_pos=t_pos, n_past=n_past),
        out_shape=jax.ShapeDtypeStruct((nd, NSA_G, 8, DH), F32),
        grid_spec=grid_spec,
        compiler_params=_cparams(("arbitrary",)),
        name="slc_decode",
    )(sel_flat, page_table, q8, o_c, gate_l, gate_b, slc_cache, bias_sd, new_k, new_v,
      win_cache, new_wk, new_wv, bias_wd)


def _t5_bucket(dist):
    n = jnp.maximum(dist, 0)
    nf = jnp.maximum(n, REL_MAX_EXACT).astype(F32)
    large = REL_MAX_EXACT + (jnp.log(nf / REL_MAX_EXACT) / math.log(REL_MAX_DIST / REL_MAX_EXACT)
                             * (REL_BUCKETS - REL_MAX_EXACT)).astype(jnp.int32)
    return jnp.where(n < REL_MAX_EXACT, n, jnp.minimum(large, REL_BUCKETS - 1))


def _bias_table(rel_bias, dist):
    onehot = (_t5_bucket(dist)[..., None] == jnp.arange(REL_BUCKETS)).astype(F32)
    b = jnp.einsum("...k,kh->...h", onehot, rel_bias, precision=HI)
    b = jnp.moveaxis(b, -1, 0)
    return b.reshape((NSA_G, NSA_R) + dist.shape)


def _prompt_bias_tables(rel_bias):
    i = jnp.arange(QBLK)[:, None]
    rel = jnp.arange(LANE)[None, :]
    bct = _bias_table(rel_bias, BLK * (rel - 1) + i % BLK + 1)
    d_near = SLC_PAD + i - jnp.arange(SLC_NEAR)[None, :]
    far = rel_bias[REL_BUCKETS - 1].reshape(NSA_G, NSA_R, 1, 1)
    bsn = _bias_table(rel_bias, d_near) - far + jnp.where(d_near >= 0, 0.0, NEG)
    d_win = WIN_PAD + i - jnp.arange(WIN_KEYS)[None, :]
    bw = _bias_table(rel_bias, d_win) + jnp.where((d_win >= 0) & (d_win <= WINDOW), 0.0, NEG)
    return bct, bsn, bw


def _pad_rows(a, rows, axis):
    pad = [(0, 0)] * a.ndim
    pad[axis] = (0, rows - a.shape[axis])
    return jnp.pad(a, pad)


def _decode_bias_tables(rel_bias, t_pos, n_past, n_win):
    n = jnp.arange(SEL_LANES)
    bias_cd = _pad_rows(_bias_table(rel_bias, t_pos - (n * BLK + BLK - 1)), 8, 1)
    tok = jnp.arange(n_past + 1)[:, None] * BLK + jnp.arange(BLK)[None, :]
    bias_sd = _pad_rows(jnp.swapaxes(_bias_table(rel_bias, t_pos - tok), 1, 2), 8, 2)
    c = jnp.arange(n_win + LANE)
    bias_wd = _pad_rows(_bias_table(rel_bias, n_win - c), 8, 1)
    return bias_cd, bias_sd, bias_wd


def kernel(x_prompt, x_sample, mem_prompt, cache_cmp_kv, cache_slc_kv, state_win_kv, state_gla, cache_mem_kv,
           page_table, norm_g, w_ffn_gate, w_ffn_up, w_ffn_down, w_in_gla, w_in_nsa, w_out, mem_norm_g, w_mem_kv,
           w_gla_a2, b_gla_a, gla_onorm_g, nsa_gate_b, cmp_pe, cmp_w1, cmp_b1, cmp_w2, rel_bias):
    nb, seq, _ = x_prompt.shape
    nd = x_sample.shape[0]
    depth = norm_g.shape[0]
    n_pages = page_table.shape[1]
    past_len = n_pages * PAGE
    n_past = past_len // BLK
    n_win = state_win_kv.shape[2]
    sr = SAMPLE_ROWS

    xp = x_prompt.reshape(nb * seq, D_MODEL)
    xs = _pad_rows(x_sample.reshape(nd, D_MODEL), sr, 0)
    mem_x = mem_prompt.reshape(nb * MEM_LEN, D_MODEL)

    w_o = w_out.astype(BF16)
    w_mkv = w_mem_kv.astype(BF16)
    def zero_rows(like, rows):
        return jnp.zeros((like.shape[0], rows, D_MODEL), BF16)

    wt_gla = jnp.swapaxes(w_in_gla, 1, 2).astype(BF16)
    tail_gla = jnp.concatenate(
        [wt_gla[:, GLA_MAIN + GLA_RANK:], wt_gla[:, GLA_MAIN:GLA_MAIN + GLA_RANK],
         zero_rows(wt_gla, TN_PROJ - MEM_W - GLA_RANK)], axis=1)
    n_gate = 3 * NSA_HEADS
    wt_nsa = jnp.swapaxes(w_in_nsa, 1, 2).astype(BF16)
    tail_nsa = jnp.concatenate(
        [wt_nsa[:, NSA_GATE_COL:NSA_GATE_COL + n_gate], zero_rows(wt_nsa, TN_PROJ - MEM_W - n_gate),
         wt_nsa[:, NSA_GATE_COL + n_gate:]], axis=1)
    wa_pad = _pad_rows(w_gla_a2, LANE, 1)
    w1 = cmp_w1.astype(BF16)
    w2 = cmp_w2.astype(BF16)

    def ffn_both(x_p, x_s, i, j):
        g1, g2 = norm_g[i, 4 * j][None], norm_g[i, 4 * j + 1][None]
        x_s, wg, wu, wd = _ffn_half(x_s, g1, w_ffn_gate, w_ffn_up, w_ffn_down, g2, TM_SAMPLE, f32_weights_at=(i, j))
        return _ffn_half(x_p, g1, wg, wu, wd, g2, TM_FFN), x_s

    def every(a, step):
        return a.reshape(nd, step, a.shape[-1])[:, 0]

    def per_seq(a):
        return _pad_rows(a[:, None, :], sr, 1).reshape(nd * sr, a.shape[-1])

    outs = dict(gla_p=[], gla_s=[], cmp_p=[], cmp_s=[], slc_p=[], slc_s=[], win_p=[], win_s=[], mem_p=[])
    for i in range(depth):
        li = i // 2
        mem_kv_p = _norm_matmul(mem_x, mem_norm_g[i][None], w_mkv, i, TM_MEMKV, MEM_W)
        outs["mem_p"].append(mem_kv_p.reshape(nb, MEM_LEN, 2, N_MEM_HEADS, MEM_HEAD_DIM))
        mem_kv_p = mem_kv_p.reshape(nb, MEM_LEN, 2 * MEM_W)
        mem_kv_s = cache_mem_kv[i].reshape(nd, MEM_LEN, 2 * MEM_W)
        xp, xs = ffn_both(xp, xs, i, 0)
        g_mix = norm_g[i, 2][None]
        if i % 2 == 0:
            proj_p = _in_proj(xp, g_mix, wt_gla, GLA_MAIN, tail_gla, li, TM_PROJ)
            proj_s = _in_proj(xs, g_mix, wt_gla, GLA_MAIN, tail_gla, li, TM_SAMPLE)
            mem_col = GLA_MAIN // MEM_W
            b_a = b_gla_a[li][None]
            gn = gla_onorm_g[li][None]
            s0 = jnp.zeros((nb, GLA_HEADS, GLA_DK, GLA_DV), F32)
            tok_p, sp = _gla(proj_p, s0, wa_pad[li], b_a, gn, seq, TL_GLA, GLA_CHUNK, seq)
            tok_s, ss = _gla(per_seq(proj_s[:nd]), state_gla[li], wa_pad[li], b_a, gn, sr, sr, sr, 1)
            tok_s = _pad_rows(every(tok_s, sr), sr, 0)
            outs["gla_p"].append(sp)
            outs["gla_s"].append(ss)
        else:
            proj_p = _in_proj(xp, g_mix, wt_nsa, NSA_GATE_COL, tail_nsa, li, TM_PROJ)
            proj_s = _in_proj(xs, g_mix, wt_nsa, NSA_GATE_COL, tail_nsa, li, TM_SAMPLE)
            mem_col = NSA_MEM_COL // MEM_W
            b1 = cmp_b1[li][:, None, :]
            gate_b = nsa_gate_b[li]
            bct, bsn, bw = _prompt_bias_tables(rel_bias)
            proj3 = proj_p.reshape(nb, seq, NSA_N)
            cmp_kv = _cmp_prompt(proj3, cmp_pe[li], w1[li], b1, w2[li])
            tok_p = _nsa_prompt(proj_p, cmp_kv, _pad_rows(gate_b[None], LANE, 1), bct, bsn, bw, nb, seq)
            kv_p = proj3[:, :, NSA_KV_COL:NSA_GATE_COL].reshape(nb, seq, 3, 2, NSA_G, DH)
            outs["cmp_p"].append(kv_p[:, :, 0].reshape(nb, seq // PAGE, PAGE, 2, NSA_G, DH))
            outs["slc_p"].append(kv_p[:, :, 1].reshape(nb, seq // PAGE, PAGE, 2, NSA_G, DH))
            outs["win_p"].append(kv_p[:, seq - n_win:, 2])
            t_pos = past_len
            bias_cd, bias_sd, bias_wd = _decode_bias_tables(rel_bias, t_pos, n_past, n_win)
            kv_s = proj_s[:nd, NSA_KV_COL:NSA_GATE_COL].reshape(nd, 3, 2, NSA_G, DH)
            outs["cmp_s"].append(kv_s[:, None, 0])
            outs["slc_s"].append(kv_s[:, None, 1])
            outs["win_s"].append(jnp.concatenate([state_win_kv[li][:, 1:], kv_s[:, None, 2]], axis=1))
            q8 = _pad_rows(proj_s[:nd, :TOK_W].reshape(nd, NSA_G, NSA_R, DH), 8, 2)
            new_cmp = _pad_rows(jnp.moveaxis(kv_s[:, 0], 1, 0).reshape(2, nd * NSA_G, DH), 32, 1)
            cmp_kv_s, cmp_last = _cmp_decode(page_table, _linear_cache(cache_cmp_kv[li]), new_cmp,
                                             cmp_pe[li], w1[li], b1, w2[li])
            o_c, sel = _sel_decode(q8, cmp_kv_s, cmp_last, bias_cd, t_pos)
            sel_flat = sel[:, :, 0, :TOPN].reshape(-1)

            def row8(a):
                return _pad_rows(a[:, :, None, :], 8, 2)

            gate_l = _pad_rows(_pad_rows(proj_s[:nd, NSA_GATE_COL:NSA_GATE_COL + n_gate].reshape(nd, NSA_G, NSA_R, 3),
                                         8, 2), LANE, 3)
            gate_b8 = _pad_rows(_pad_rows(gate_b.reshape(NSA_G, NSA_R, 3), 8, 1), LANE, 2)
            tok_s = _slc_decode(sel_flat, page_table, q8, o_c, gate_l, gate_b8,
                                _linear_cache(cache_slc_kv[li]), bias_sd,
                                row8(kv_s[:, 1, 0]), row8(kv_s[:, 1, 1]),
                                _linear_cache(state_win_kv[li]),
                                row8(kv_s[:, 2, 0]), row8(kv_s[:, 2, 1]), bias_wd, t_pos)
            tok_s = _pad_rows(tok_s[:, :, :NSA_R].reshape(nd, TOK_W), sr, 0).astype(BF16)
        mem_o_p = _mem_attn(proj_p, mem_col, mem_kv_p, seq, TM_MEM_ATTN)
        q_s = jnp.broadcast_to(proj_s[:nd, None, mem_col * MEM_W:(mem_col + 1) * MEM_W],
                               (nd, sr, MEM_W)).reshape(nd * sr, MEM_W)
        mem_o_s = _pad_rows(every(_mem_attn(q_s, 0, mem_kv_s, sr, sr), sr), sr, 0)
        xp = _out_proj(xp, tok_p, mem_o_p, w_o, i, norm_g[i, 3][None], TM_OUT)
        xs = _out_proj(xs, tok_s, mem_o_s, w_o, i, norm_g[i, 3][None], sr)
        xp, xs = ffn_both(xp, xs, i, 1)

    y_prompt = xp.reshape(nb, seq, D_MODEL)
    y_sample = xs[:nd].reshape(nd, 1, D_MODEL)
    st = lambda k: jnp.stack(outs[k])
    return (y_prompt, y_sample, st("gla_p"), st("cmp_p"), st("slc_p"), st("win_p"), st("mem_p"),
            st("gla_s"), st("cmp_s"), st("slc_s"), st("win_s"))
```

```python
import functools
import math

import jax
import jax.numpy as jnp
from jax import lax
from jax.experimental import pallas as pl
from jax.experimental.pallas import tpu as pltpu

F32 = jnp.float32
BF16 = jnp.bfloat16
HI = lax.Precision.HIGHEST

D_MODEL = 2048
D_FF = 5632
EPS = 1e-6
MEM_LEN = 256
N_MEM_HEADS = 4
MEM_HEAD_DIM = 128
MEM_W = N_MEM_HEADS * MEM_HEAD_DIM
TOK_W = D_MODEL - MEM_W
GLA_HEADS = 4
GLA_DV = TOK_W // GLA_HEADS
GLA_DK = GLA_DV // 2
GLA_RANK = 16
GLA_TAU = 16.0
GLA_CHUNK = 64
GLA_PAIR_W = 2 * GLA_DK
DH = 128
NSA_HEADS = TOK_W // DH
NSA_G = 3
NSA_R = NSA_HEADS // NSA_G
BLK = 64
TOPN = 16
WINDOW = 512
CMP_HID = 256
QBLK = 256
KV_W = NSA_G * DH
REL_BUCKETS = 32
REL_MAX_EXACT = 16
REL_MAX_DIST = 128
PAGE = 128
LANE = 128
HALF_LANE = LANE // 2
BLK_SHIFT = BLK.bit_length() - 1
VMEM_LIMIT = 56 * 1024 * 1024

TN_PROJ = 768
GLA_MAIN = 2 * GLA_HEADS * GLA_DK + 2 * TOK_W
GLA_A_COL = GLA_MAIN + MEM_W
GLA_N = GLA_MAIN + TN_PROJ
NSA_KV_COL = TOK_W
NSA_GATE_COL = NSA_KV_COL + 6 * KV_W
NSA_MEM_COL = NSA_GATE_COL + TN_PROJ - MEM_W
NSA_N = NSA_GATE_COL + TN_PROJ
SAMPLE_ROWS = 16
TM_SAMPLE = SAMPLE_ROWS
TM_FFN = 512
TM_PROJ = 1024
TM_MEMKV = 512
TM_OUT = 256
TM_MEM_ATTN = 512
TL_GLA = 512
SLC_PAD = 256
WIN_PAD = WINDOW
WIN_KEYS = WIN_PAD + QBLK
SLC_NEAR = SLC_PAD + QBLK
FAR_CHUNK = 1024
NEG = -1e30
M_FLOOR = -1e29


def _cparams(sem, vmem=VMEM_LIMIT):
    return pltpu.CompilerParams(dimension_semantics=sem, vmem_limit_bytes=vmem)


def _sigmoid(x):
    return 1.0 / (1.0 + jnp.exp(-x))


def _rms(x, g):
    ms = jnp.mean(x * x, axis=-1, keepdims=True)
    return x * lax.rsqrt(ms + EPS) * g


def _nt(a, b, precision=None):
    return lax.dot_general(a, b, (((1,), (1,)), ((), ())), precision=precision,
                           preferred_element_type=F32)


def _tn(a, b, precision=None):
    return lax.dot_general(a, b, (((0,), (0,)), ((), ())), precision=precision,
                           preferred_element_type=F32)


def _dot(a, b, precision=None):
    return jnp.dot(a, b, precision=precision, preferred_element_type=F32)


def _split3(x):
    hi = x.astype(BF16)
    r1 = x - hi.astype(F32)
    mid = r1.astype(BF16)
    lo = (r1 - mid.astype(F32)).astype(BF16)
    return hi, mid, lo


def _ffn_kernel(x_ref, g1_ref, wg_ref, wu_ref, wd_ref, g2_ref, o_ref, *rest):
    bf_out, (xn_ref, acc_ref) = rest[:-2], rest[-2:]
    j = pl.program_id(1)

    @pl.when(j == 0)
    def _():
        xn_ref[...] = _rms(x_ref[...], g1_ref[...]).astype(BF16)
        acc_ref[...] = jnp.zeros_like(acc_ref)

    wg, wu, wd = (w_ref[...].astype(BF16) for w_ref in (wg_ref, wu_ref, wd_ref))
    for out_ref, w in zip(bf_out, (wg, wu, wd)):
        out_ref[...] = w
    xn = xn_ref[...]
    gate = _dot(xn, wg)
    up = _dot(xn, wu)
    h = (gate * _sigmoid(gate) * up).astype(BF16)
    acc_ref[...] += _dot(h, wd)

    @pl.when(j == pl.num_programs(1) - 1)
    def _():
        o_ref[...] = x_ref[...] + 0.5 * _rms(acc_ref[...], g2_ref[...])


def _ffn_half(x, g1, wg, wu, wd, g2, tm, tf=512, f32_weights_at=None):
    m = x.shape[0]
    y_shape = jax.ShapeDtypeStruct((m, D_MODEL), F32)
    y_spec = pl.BlockSpec((tm, D_MODEL), lambda i, j: (i, 0))
    col_spec = pl.BlockSpec((D_MODEL, tf), lambda i, j: (0, j))
    row_spec = pl.BlockSpec((tf, D_MODEL), lambda i, j: (j, 0))
    if f32_weights_at is None:
        w_specs = [col_spec, col_spec, row_spec]
        out_shape, out_specs = y_shape, y_spec
    else:
        assert m == tm, "the bf16 copies are written once, by a single row tile"
        layer, half = f32_weights_at
        w_specs = [pl.BlockSpec((None, None, D_MODEL, tf), lambda i, j: (layer, half, 0, j)),
                   pl.BlockSpec((None, None, D_MODEL, tf), lambda i, j: (layer, half, 0, j)),
                   pl.BlockSpec((None, None, tf, D_MODEL), lambda i, j: (layer, half, j, 0))]
        out_shape = (y_shape, jax.ShapeDtypeStruct((D_MODEL, D_FF), BF16),
                     jax.ShapeDtypeStruct((D_MODEL, D_FF), BF16), jax.ShapeDtypeStruct((D_FF, D_MODEL), BF16))
        out_specs = (y_spec, col_spec, col_spec, row_spec)
    return pl.pallas_call(
        _ffn_kernel,
        out_shape=out_shape,
        grid=(m // tm, D_FF // tf),
        in_specs=[y_spec, pl.BlockSpec((1, D_MODEL), lambda i, j: (0, 0))] + w_specs
                 + [pl.BlockSpec((1, D_MODEL), lambda i, j: (0, 0))],
        out_specs=out_specs,
        scratch_shapes=[pltpu.VMEM((tm, D_MODEL), BF16), pltpu.VMEM((tm, D_MODEL), F32)],
        compiler_params=_cparams(("parallel", "arbitrary")),
        name="ffn_half",
    )(x, g1, wg, wu, wd, g2)


def _norm_matmul_kernel(x_ref, g_ref, w_ref, o_ref, xn_ref):
    @pl.when(pl.program_id(1) == 0)
    def _():
        xn_ref[...] = _rms(x_ref[...], g_ref[...]).astype(BF16)

    o_ref[...] = _dot(xn_ref[...], w_ref[...])


def _norm_matmul(x, g, w, layer, tm, tn):
    m, n = x.shape[0], w.shape[2]
    return pl.pallas_call(
        _norm_matmul_kernel,
        out_shape=jax.ShapeDtypeStruct((m, n), F32),
        grid=(m // tm, n // tn),
        in_specs=[
            pl.BlockSpec((tm, D_MODEL), lambda i, j: (i, 0)),
            pl.BlockSpec((1, D_MODEL), lambda i, j: (0, 0)),
            pl.BlockSpec((None, D_MODEL, tn), lambda i, j: (layer, 0, j)),
        ],
        out_specs=pl.BlockSpec((tm, tn), lambda i, j: (i, j)),
        scratch_shapes=[pltpu.VMEM((tm, D_MODEL), BF16)],
        compiler_params=_cparams(("parallel", "arbitrary")),
        name="norm_matmul",
    )(x, g, w)


def _in_proj_kernel(x_ref, g_ref, wm_ref, wt_ref, o_ref, xn_ref, *, n_main):
    j = pl.program_id(1)

    @pl.when(j == 0)
    def _():
        xn_ref[...] = _rms(x_ref[...], g_ref[...]).astype(BF16)

    @pl.when(j < n_main)
    def _():
        o_ref[...] = _nt(xn_ref[...], wm_ref[...])

    @pl.when(j == n_main)
    def _():
        o_ref[...] = _nt(xn_ref[...], wt_ref[...])


def _in_proj(x, g, w_main, main_cols, w_tail, layer, tm):
    m = x.shape[0]
    n_main = main_cols // TN_PROJ
    return pl.pallas_call(
        functools.partial(_in_proj_kernel, n_main=n_main),
        out_shape=jax.ShapeDtypeStruct((m, main_cols + TN_PROJ), F32),
        grid=(m // tm, n_main + 1),
        in_specs=[
            pl.BlockSpec((tm, D_MODEL), lambda i, j: (i, 0)),
            pl.BlockSpec((1, D_MODEL), lambda i, j: (0, 0)),
            pl.BlockSpec((None, TN_PROJ, D_MODEL), lambda i, j: (layer, jnp.minimum(j, n_main - 1), 0)),
            pl.BlockSpec((None, TN_PROJ, D_MODEL), lambda i, j: (layer, 0, 0)),
        ],
        out_specs=pl.BlockSpec((tm, TN_PROJ), lambda i, j: (i, j)),
        scratch_shapes=[pltpu.VMEM((tm, D_MODEL), BF16)],
        compiler_params=_cparams(("parallel", "arbitrary")),
        name="in_proj",
    )(x, g, w_main, w_tail)


def _out_proj_kernel(x_ref, tok_ref, mem_ref, wt_ref, wm_ref, g_ref, o_ref):
    y = _dot(tok_ref[...], wt_ref[...]) + _dot(mem_ref[...], wm_ref[...])
    o_ref[...] = x_ref[...] + _rms(y, g_ref[...])


def _out_proj(x, tok, mem_o, w_o, layer, g, tm):
    m = x.shape[0]
    return pl.pallas_call(
        _out_proj_kernel,
        out_shape=jax.ShapeDtypeStruct((m, D_MODEL), F32),
        grid=(m // tm,),
        in_specs=[
            pl.BlockSpec((tm, D_MODEL), lambda i: (i, 0)),
            pl.BlockSpec((tm, TOK_W), lambda i: (i, 0)),
            pl.BlockSpec((tm, MEM_W), lambda i: (i, 0)),
            pl.BlockSpec((None, TOK_W, D_MODEL), lambda i: (layer, 0, 0), pipeline_mode=pl.Buffered(1)),
            pl.BlockSpec((None, MEM_W, D_MODEL), lambda i: (layer, TOK_W // MEM_W, 0), pipeline_mode=pl.Buffered(1)),
            pl.BlockSpec((1, D_MODEL), lambda i: (0, 0)),
        ],
        out_specs=pl.BlockSpec((tm, D_MODEL), lambda i: (i, 0)),
        compiler_params=_cparams(("parallel",)),
        name="out_proj",
    )(x, tok, mem_o, w_o, w_o, g)


def _mem_attn_kernel(q_ref, kv_ref, o_ref):
    for h in range(N_MEM_HEADS):
        q = (q_ref[:, h * DH:(h + 1) * DH] * (MEM_HEAD_DIM ** -0.5)).astype(BF16)
        k = kv_ref[:, h * DH:(h + 1) * DH].astype(BF16)
        v = kv_ref[:, MEM_W + h * DH:MEM_W + (h + 1) * DH].astype(BF16)
        s = _nt(q, k)
        e = jnp.exp(s - jnp.max(s, axis=-1, keepdims=True))
        p = e / jnp.sum(e, axis=-1, keepdims=True)
        o_ref[:, h * DH:(h + 1) * DH] = _dot(p.astype(BF16), v).astype(o_ref.dtype)


def _mem_attn(q_arr, q_col_block, mem_kv, rows_per_batch, tm):
    nb = mem_kv.shape[0]
    per = rows_per_batch // tm
    return pl.pallas_call(
        _mem_attn_kernel,
        out_shape=jax.ShapeDtypeStruct((nb * rows_per_batch, MEM_W), BF16),
        grid=(nb, per),
        in_specs=[
            pl.BlockSpec((tm, MEM_W), lambda b, i: (b * per + i, q_col_block)),
            pl.BlockSpec((None, MEM_LEN, 2 * MEM_W), lambda b, i: (b, 0, 0)),
        ],
        out_specs=pl.BlockSpec((tm, MEM_W), lambda b, i: (b * per + i, 0)),
        compiler_params=_cparams(("parallel", "parallel")),
        name="mem_attn",
    )(q_arr, mem_kv)


GLA_PAIRS = GLA_HEADS // 2


def _gla_kernel(q_ref, k_ref, v_ref, r_ref, a_ref, s0_ref, wa_ref, ba_ref, gn_ref,
                tok_ref, s_out_ref, s_ref, la_ref, *, chunk, n_valid):
    l = pl.program_id(1)
    tl = q_ref.shape[0]

    @pl.when(l == 0)
    def _():
        for p in range(GLA_PAIRS):
            s_ref[p] = s0_ref[2 * p:2 * p + 2].reshape(GLA_PAIR_W, GLA_DV).T

    lane = lax.broadcasted_iota(jnp.int32, (1, GLA_PAIR_W), 1)
    head_mask = [(lane < GLA_DK).astype(F32), (lane >= GLA_DK).astype(F32)]
    ti = lax.broadcasted_iota(jnp.int32, (chunk, chunk), 0)
    si = lax.broadcasted_iota(jnp.int32, (chunk, chunk), 1)
    causal = si <= ti
    tri = jnp.where(causal, 1.0, 0.0).astype(BF16)

    a_hi, a_lo, _ = _split3(a_ref[...])
    w_hi, w_lo, _ = _split3(wa_ref[...])
    z = _dot(a_hi, w_hi) + _dot(a_lo, w_hi) + _dot(a_hi, w_lo) + ba_ref[...]
    la_all = -(jnp.maximum(-z, 0.0) + jnp.log1p(jnp.exp(-jnp.abs(z)))) / GLA_TAU
    pos = l * tl + lax.broadcasted_iota(jnp.int32, (tl, 1), 0)
    la_ref[...] = jnp.where(pos < n_valid, la_all, 0.0)

    def step(ci, carry):
        r0 = pl.multiple_of(ci * chunk, chunk)
        rows = pl.ds(r0, chunk)
        for p in range(GLA_PAIRS):
            pair = slice(p * GLA_PAIR_W, (p + 1) * GLA_PAIR_W)
            b = functools.reduce(jnp.add, [_dot(tri, piece) for piece in _split3(la_ref[rows, pair])])
            bl = b[chunk - 1:chunk, :]
            q = q_ref[rows, pair] * (GLA_DK ** -0.5)
            k = k_ref[rows, pair]
            qe = q * jnp.exp(b)
            ke = (k * jnp.exp(-b)).astype(BF16)
            kd = k * jnp.exp(bl - b)
            st_old = s_ref[p]
            st_bf = st_old.astype(BF16)
            upd = None
            for h in range(2):
                head = slice((2 * p + h) * GLA_DV, (2 * p + h + 1) * GLA_DV)
                v = v_ref[rows, head].astype(BF16)
                qm = (qe * head_mask[h]).astype(BF16)
                att = jnp.where(causal, _nt(qm, ke), 0.0)
                o = _nt(qm, st_bf) + _dot(att.astype(BF16), v)
                o = _rms(o, gn_ref[...])
                r = r_ref[rows, head]
                tok_ref[rows, head] = (o * (r * _sigmoid(r))).astype(tok_ref.dtype)
                u = _tn(v, (kd * head_mask[h]).astype(BF16))
                upd = u if upd is None else upd + u
            s_ref[p] = jnp.exp(bl) * st_old + upd
        return carry

    lax.fori_loop(0, tl // chunk, step, 0)

    @pl.when(l == pl.num_programs(1) - 1)
    def _():
        for p in range(GLA_PAIRS):
            s_out_ref[2 * p:2 * p + 2] = s_ref[p].T.reshape(2, GLA_DK, GLA_DV)


def _gla(proj, s0, wa_pad, b_a, gn, seq, tl, chunk, n_valid):
    nb = s0.shape[0]
    per = seq // tl
    qk_w = GLA_HEADS * GLA_DK
    row = lambda b, l: b * per + l

    return pl.pallas_call(
        functools.partial(_gla_kernel, chunk=chunk, n_valid=n_valid),
        out_shape=(jax.ShapeDtypeStruct((nb * seq, TOK_W), BF16),
                   jax.ShapeDtypeStruct((nb, GLA_HEADS, GLA_DK, GLA_DV), F32)),
        grid=(nb, per),
        in_specs=[
            pl.BlockSpec((tl, qk_w), lambda b, l: (row(b, l), 0)),
            pl.BlockSpec((tl, qk_w), lambda b, l: (row(b, l), 1)),
            pl.BlockSpec((tl, TOK_W), lambda b, l: (row(b, l), 2 * qk_w // TOK_W)),
            pl.BlockSpec((tl, TOK_W), lambda b, l: (row(b, l), 2 * qk_w // TOK_W + 1)),
            pl.BlockSpec((tl, LANE), lambda b, l: (row(b, l), GLA_A_COL // LANE)),
            pl.BlockSpec((None, GLA_HEADS, GLA_DK, GLA_DV), lambda b, l: (b, 0, 0, 0)),
            pl.BlockSpec((LANE, qk_w), lambda b, l: (0, 0)),
            pl.BlockSpec((1, qk_w), lambda b, l: (0, 0)),
            pl.BlockSpec((1, GLA_DV), lambda b, l: (0, 0)),
        ],
        out_specs=(pl.BlockSpec((tl, TOK_W), lambda b, l: (row(b, l), 0)),
                   pl.BlockSpec((None, GLA_HEADS, GLA_DK, GLA_DV), lambda b, l: (b, 0, 0, 0))),
        scratch_shapes=[pltpu.VMEM((GLA_PAIRS, GLA_DV, GLA_PAIR_W), F32), pltpu.VMEM((tl, qk_w), F32)],
        compiler_params=_cparams(("parallel", "arbitrary")),
        name="gla",
    )(proj, proj, proj, proj, proj, s0, wa_pad, b_a, gn)


def _masked_softmax(s, valid):
    s = jnp.where(valid, s, -jnp.inf)
    m = jnp.max(s, axis=-1, keepdims=True)
    m = jnp.where(m > -jnp.inf, m, 0.0)
    e = jnp.exp(s - m)
    return e / jnp.maximum(jnp.sum(e, axis=-1, keepdims=True), 1e-30)


def _online_update(state, s, v):
    m, l, acc = state
    r, nq, w = s.shape
    m_new = jnp.maximum(m, jnp.max(s, axis=-1, keepdims=True))
    alpha = jnp.exp(m - m_new)
    p = jnp.exp(s - m_new)
    l = alpha * l + jnp.sum(p, axis=-1, keepdims=True)
    pv = _dot(p.reshape(r * nq, w).astype(BF16), v).reshape(r, nq, DH)
    return m_new, l, alpha * acc + pv


def _compress_tail(xflat, w1, b1, w2):
    h = _dot(xflat, w1) + b1
    h = h * _sigmoid(h)
    return _dot(h.astype(BF16), w2)


def _cmp_prompt_kernel(x0_ref, x1_ref, x2_ref, pe_ref, w1_ref, b1_ref, w2_ref, o_ref, xflat_ref):
    x_refs = (x0_ref, x1_ref, x2_ref)
    nb, seq = x0_ref.shape[0], x0_ref.shape[1]
    nblk = seq // BLK
    for j in range(BLK):
        pe_j = pe_ref[j:j + 1, :]
        for b in range(nb):
            for g in range(NSA_G):
                xj = x_refs[g][b, pl.ds(j, nblk, stride=BLK), :]
                row = (b * NSA_G + g) * nblk
                xflat_ref[row:row + nblk, j * DH:(j + 1) * DH] = (xj + pe_j).astype(BF16)
    out = _compress_tail(xflat_ref[...], w1_ref[...], b1_ref[...], w2_ref[...])
    o_ref[...] = out.reshape(nb, NSA_G, nblk, DH)


def _cmp_prompt(proj3, pe, w1, b1, w2):
    nb, seq, _ = proj3.shape
    nblk = seq // BLK
    return pl.pallas_call(
        _cmp_prompt_kernel,
        out_shape=jax.ShapeDtypeStruct((2, nb, NSA_G, nblk, DH), F32),
        grid=(2,),
        in_specs=[
            pl.BlockSpec((nb, seq, DH), lambda kv: (0, 0, NSA_KV_COL // DH + NSA_G * kv)),
            pl.BlockSpec((nb, seq, DH), lambda kv: (0, 0, NSA_KV_COL // DH + NSA_G * kv + 1)),
            pl.BlockSpec((nb, seq, DH), lambda kv: (0, 0, NSA_KV_COL // DH + NSA_G * kv + 2)),
            pl.BlockSpec((None, BLK, DH), lambda kv: (kv, 0, 0)),
            pl.BlockSpec((None, BLK * DH, CMP_HID), lambda kv: (kv, 0, 0)),
            pl.BlockSpec((None, 1, CMP_HID), lambda kv: (kv, 0, 0)),
            pl.BlockSpec((None, CMP_HID, DH), lambda kv: (kv, 0, 0)),
        ],
        out_specs=pl.BlockSpec((None, nb, NSA_G, nblk, DH), lambda kv: (kv, 0, 0, 0, 0)),
        scratch_shapes=[pltpu.VMEM((nb * NSA_G * nblk, BLK * DH), BF16)],
        compiler_params=_cparams(("arbitrary",)),
        name="cmp_prompt",
    )(proj3, proj3, proj3, pe, w1, b1, w2)


def _nsa_prompt_kernel(q_ref, gl_ref, gb_ref, kc_ref, vc_ref, ks_ref, vs_ref, kw_ref, vw_ref,
                       bct_ref, bsn_ref, bw_ref, o_ref, ksb, vsb, kwb, vwb):
    g = pl.program_id(1)
    qi = pl.program_id(2)
    seq = ks_ref.shape[0]
    nblk = seq // BLK
    R = NSA_R

    @pl.when(qi == 0)
    def _():
        for src, dst, pad in ((vs_ref, vsb, SLC_PAD), (vw_ref, vwb, WIN_PAD)):
            dst[0:pad, :] = jnp.zeros((pad, DH), BF16)
            dst[pad:pad + seq, :] = src[...].astype(BF16)
        for src, dst, pad, per_block in ((ks_ref, ksb, SLC_PAD, True), (kw_ref, kwb, WIN_PAD, False)):
            dst[0:pad, 0:DH] = jnp.zeros((pad, DH), BF16)
            dst[pad:pad + seq, 0:DH] = src[...].astype(BF16)
            pos = lax.broadcasted_iota(jnp.int32, (pad + seq, 1), 0) - pad
            feat = lax.broadcasted_iota(jnp.int32, (1, DH), 1)
            masked = (feat == HALF_LANE) & (pos < 0)
            if per_block:
                masked = masked | ((pos >= 0) & (lax.shift_right_arithmetic(pos, BLK_SHIFT) == feat))
            dst[:, DH:2 * DH] = jnp.where(masked, NEG, 0.0).astype(BF16)

    q = q_ref[...] * (DH ** -0.5)
    q_all = jnp.concatenate([q[:, r * DH:(r + 1) * DH] for r in range(R)], axis=0)
    q_bf = q_all.astype(BF16)
    i_col = lax.broadcasted_iota(jnp.int32, (QBLK, 1), 0)
    t_col = qi * QBLK + i_col
    n_row = lax.broadcasted_iota(jnp.int32, (1, nblk), 1)
    row0 = pl.multiple_of(qi * QBLK, QBLK)
    lane2 = lax.broadcasted_iota(jnp.int32, (1, LANE), 1)

    s_c = _nt(q_all, kc_ref[...], HI).reshape(R, QBLK, nblk)
    tb_col = lax.shift_right_arithmetic(t_col, BLK_SHIFT)
    rel = tb_col - n_row
    bias_c = []
    for r in range(R):
        tab = bct_ref[r]
        bias_c.append(jnp.where(rel == 0, tab[:, 0:1],
                      jnp.where(rel == 1, tab[:, 1:2],
                      jnp.where(rel == 2, tab[:, 2:3], tab[:, 3:4]))))
    s_c = s_c + jnp.stack(bias_c, axis=0)
    valid_c = (t_col - (n_row * BLK + (BLK - 1))) >= 0
    p_c = _masked_softmax(s_c, valid_c[None])
    o_c = _dot(p_c.reshape(R * QBLK, nblk).astype(BF16), vc_ref[...].astype(BF16))

    imp = jnp.sum(p_c, axis=0)
    forced = (n_row == 0) | (n_row == tb_col) | (n_row == tb_col - 1)
    future = n_row * BLK > t_col
    score = jnp.where(forced, jnp.inf, jnp.where(future, -jnp.inf, imp))
    score_t = jnp.concatenate([score, jnp.full((QBLK, LANE - nblk), -jnp.inf, F32)], axis=1).T[:HALF_LANE]
    n_sub = lax.broadcasted_iota(jnp.int32, (HALF_LANE, 1), 0)
    rank_t = jnp.zeros((HALF_LANE, QBLK), F32)
    for i in range(nblk):
        cand = score_t[i:i + 1, :]
        wins_tie = jnp.where(n_sub > i, 1.0, 0.0)
        rank_t = rank_t + jnp.where(cand > score_t, 1.0, jnp.where(cand == score_t, wins_tie, 0.0))
    rank = jnp.concatenate([rank_t, jnp.zeros((LANE - HALF_LANE, QBLK), F32)], axis=0).T
    left = lane2 < HALF_LANE
    n_far = (qi * QBLK - SLC_PAD) // BLK
    dropped = rank >= float(min(TOPN, nblk))
    flag_all = jnp.where(left, jnp.where(dropped, 1.0, 0.0), jnp.where(lane2 == HALF_LANE, 1.0, 0.0))
    flag_far = jnp.where(left & (lane2 >= n_far), 1.0, flag_all)
    q_far = jnp.concatenate([q_bf, jnp.concatenate([flag_far.astype(BF16)] * R, axis=0)], axis=1)
    q_near = jnp.concatenate([q_bf, jnp.concatenate([flag_all.astype(BF16)] * R, axis=0)], axis=1)

    def far_body(kc_i, state):
        start = pl.multiple_of(SLC_PAD + kc_i * FAR_CHUNK, BLK)
        s = _nt(q_far, ksb[pl.ds(start, FAR_CHUNK), :]).reshape(R, QBLK, FAR_CHUNK)
        return _online_update(state, s, vsb[pl.ds(start, FAR_CHUNK), :])

    blk_per_chunk = FAR_CHUNK // BLK
    n_chunks = (jnp.maximum(n_far, 0) + (blk_per_chunk - 1)) // blk_per_chunk
    state = (jnp.full((R, QBLK, 1), M_FLOOR, F32), jnp.zeros((R, QBLK, 1), F32), jnp.zeros((R, QBLK, DH), F32))
    state = lax.fori_loop(0, n_chunks, far_body, state)

    s = _nt(q_near, ksb[pl.ds(row0, SLC_NEAR), :]).reshape(R, QBLK, SLC_NEAR) + bsn_ref[...]
    _, l_s, acc_s = _online_update(state, s, vsb[pl.ds(row0, SLC_NEAR), :])
    o_s = (acc_s / jnp.maximum(l_s, 1e-30)).reshape(R * QBLK, DH)

    s = _nt(q_near, kwb[pl.ds(row0, WIN_KEYS), :]).reshape(R, QBLK, WIN_KEYS) + bw_ref[...]
    e_w = jnp.exp(s - jnp.max(s, axis=-1, keepdims=True))
    p_w = e_w / jnp.maximum(jnp.sum(e_w, axis=-1, keepdims=True), 1e-30)
    o_w = _dot(p_w.reshape(R * QBLK, WIN_KEYS).astype(BF16), vwb[pl.ds(row0, WIN_KEYS), :])

    gates = _sigmoid(gl_ref[...] + gb_ref[...])
    src = lax.broadcasted_iota(jnp.int32, (LANE, LANE), 0)
    dst = lax.broadcasted_iota(jnp.int32, (LANE, LANE), 1)
    pick = jnp.where((src == g * (3 * R) + dst) & (dst < 3 * R), 1.0, 0.0)
    gsel = _dot(gates, pick, HI)
    for r in range(R):
        rows = slice(r * QBLK, (r + 1) * QBLK)
        o = (gsel[:, 3 * r:3 * r + 1] * o_c[rows] + gsel[:, 3 * r + 1:3 * r + 2] * o_s[rows]
             + gsel[:, 3 * r + 2:3 * r + 3] * o_w[rows])
        o_ref[:, r * DH:(r + 1) * DH] = o.astype(o_ref.dtype)


def _nsa_prompt(proj2, cmp_kv, gate_b_pad, bct, bsn, bw, nb, seq):
    nq = seq // QBLK
    nblk = seq // BLK
    assert nblk <= HALF_LANE, "mask features of the selected branch hold at most 64 key blocks"
    assert seq % FAR_CHUNK == 0
    proj3 = proj2.reshape(nb, seq, NSA_N)

    def kv_spec(col):
        return pl.BlockSpec((None, seq, DH), lambda b, g, qi: (b, 0, col // DH + g))

    return pl.pallas_call(
        _nsa_prompt_kernel,
        out_shape=jax.ShapeDtypeStruct((nb * seq, TOK_W), BF16),
        grid=(nb, NSA_G, nq),
        in_specs=[
            pl.BlockSpec((QBLK, NSA_R * DH), lambda b, g, qi: (b * nq + qi, g)),
            pl.BlockSpec((QBLK, LANE), lambda b, g, qi: (b * nq + qi, NSA_GATE_COL // LANE)),
            pl.BlockSpec((1, LANE), lambda b, g, qi: (0, 0)),
            pl.BlockSpec((None, None, None, nblk, DH), lambda b, g, qi: (0, b, g, 0, 0)),
            pl.BlockSpec((None, None, None, nblk, DH), lambda b, g, qi: (1, b, g, 0, 0)),
            kv_spec(NSA_KV_COL + 2 * KV_W), kv_spec(NSA_KV_COL + 3 * KV_W),
            kv_spec(NSA_KV_COL + 4 * KV_W), kv_spec(NSA_KV_COL + 5 * KV_W),
            pl.BlockSpec((None, NSA_R, QBLK, LANE), lambda b, g, qi: (g, 0, 0, 0)),
            pl.BlockSpec((None, NSA_R, QBLK, SLC_NEAR), lambda b, g, qi: (g, 0, 0, 0)),
            pl.BlockSpec((None, NSA_R, QBLK, WIN_KEYS), lambda b, g, qi: (g, 0, 0, 0)),
        ],
        out_specs=pl.BlockSpec((QBLK, NSA_R * DH), lambda b, g, qi: (b * nq + qi, g)),
        scratch_shapes=[pltpu.VMEM((SLC_PAD + seq, 2 * DH), BF16), pltpu.VMEM((SLC_PAD + seq, DH), BF16),
                        pltpu.VMEM((WIN_PAD + seq, 2 * DH), BF16), pltpu.VMEM((WIN_PAD + seq, DH), BF16)],
        compiler_params=_cparams(("parallel", "parallel", "arbitrary")),
        name="nsa_prompt",
    )(proj2, proj2, gate_b_pad, cmp_kv, cmp_kv, proj3, proj3, proj3, proj3, bct, bsn, bw)


SUB_PAGES = 8
RING = 4
ROW_W = 2 * NSA_G
PAGE_ROWS = PAGE * ROW_W


def _linear_cache(cache):
    return jnp.transpose(cache, (0, 1, 3, 2, 4)).reshape(-1, DH)


def _cmp_decode_kernel(pt_ref, cache_ref, new_ref, pe_ref, w1_ref, b1_ref, w2_ref, o_ref, last_ref,
                       buf, sem, xflat_ref, xlast_ref, *, n_sub):
    b = pl.program_id(0)
    total = pl.num_programs(0) * n_sub
    sub_blk = SUB_PAGES * PAGE // BLK
    seq_blk = n_sub * sub_blk

    def page_copy(s, p):
        page = pt_ref[s // n_sub, (s % n_sub) * SUB_PAGES + p]
        return pltpu.make_async_copy(cache_ref.at[pl.ds(pl.multiple_of(page * PAGE_ROWS, PAGE_ROWS), PAGE_ROWS), :],
                                     buf.at[s % RING, pl.ds(p * PAGE_ROWS, PAGE_ROWS), :], sem.at[s % RING])

    def start_sub(s):
        for p in range(SUB_PAGES):
            page_copy(s, p).start()

    @pl.when(b == 0)
    def _():
        for s in range(RING):
            start_sub(s)

    def body(i, c):
        s = b * n_sub + i
        for p in range(SUB_PAGES):
            page_copy(s, p).wait()
        slot = s % RING
        row0 = pl.multiple_of(i * sub_blk, sub_blk)
        by_row = jnp.swapaxes(buf[slot].reshape(sub_blk, BLK * ROW_W, DH), 0, 1)
        for kv in range(2):
            for j in range(BLK):
                pe_j = pe_ref[kv, j:j + 1, :]
                for g in range(NSA_G):
                    xj = by_row[j * ROW_W + g * 2 + kv]
                    xflat_ref[kv, pl.ds(g * seq_blk + row0, sub_blk), j * DH:(j + 1) * DH] = (xj + pe_j).astype(BF16)

        @pl.when(s + RING < total)
        def _():
            start_sub(s + RING)

        return c

    lax.fori_loop(0, n_sub, body, 0)
    for kv in range(2):
        out = _compress_tail(xflat_ref[kv], w1_ref[kv], b1_ref[kv], w2_ref[kv])
        o_ref[kv] = out.reshape(NSA_G, seq_blk, DH)

    @pl.when(b == 0)
    def _():
        rows = new_ref.shape[1]
        for kv in range(2):
            for j in range(BLK):
                pe_j = jnp.broadcast_to(pe_ref[kv, j:j + 1, :], (rows, DH))
                xj = new_ref[kv] + pe_j if j == 0 else pe_j
                xlast_ref[:, j * DH:(j + 1) * DH] = xj.astype(BF16)
            last_ref[kv] = _compress_tail(xlast_ref[...], w1_ref[kv], b1_ref[kv], w2_ref[kv])


def _cmp_decode(page_table, cache, new_rows, pe, w1, b1, w2):
    nd, n_pages = page_table.shape
    n_sub = n_pages // SUB_PAGES
    seq_blk = n_pages * PAGE // BLK
    rows = new_rows.shape[1]

    def whole(shape):
        return pl.BlockSpec(shape, lambda b, pt: (0,) * len(shape), pipeline_mode=pl.Buffered(1))

    grid_spec = pltpu.PrefetchScalarGridSpec(
        num_scalar_prefetch=1,
        grid=(nd,),
        in_specs=[
            pl.BlockSpec(memory_space=pl.ANY),
            whole(new_rows.shape), whole(pe.shape), whole(w1.shape), whole(b1.shape), whole(w2.shape),
        ],
        out_specs=(pl.BlockSpec((2, None, NSA_G, seq_blk, DH), lambda b, pt: (0, b, 0, 0, 0)),
                   pl.BlockSpec((2, rows, DH), lambda b, pt: (0, 0, 0))),
        scratch_shapes=[pltpu.VMEM((RING, SUB_PAGES * PAGE_ROWS, DH), F32),
                        pltpu.SemaphoreType.DMA((RING,)),
                        pltpu.VMEM((2, NSA_G * seq_blk, BLK * DH), BF16),
                        pltpu.VMEM((rows, BLK * DH), BF16)],
    )
    return pl.pallas_call(
        functools.partial(_cmp_decode_kernel, n_sub=n_sub),
        out_shape=(jax.ShapeDtypeStruct((2, nd, NSA_G, seq_blk, DH), F32),
                   jax.ShapeDtypeStruct((2, rows, DH), F32)),
        grid_spec=grid_spec,
        compiler_params=_cparams(("arbitrary",)),
        name="cmp_decode",
    )(page_table, cache, new_rows, pe, w1, b1, w2)


SEL_LANES = 384


def _sel_decode_kernel(q_ref, kc_ref, vc_ref, last_ref, bias_ref, oc_ref, sel_ref, *, t_pos):
    b = pl.program_id(0)
    n_past = kc_ref.shape[1]
    n_blocks = n_past + 1
    n_lane = lax.broadcasted_iota(jnp.int32, (1, SEL_LANES), 1)
    n_lane_f = n_lane.astype(F32)
    head_row = lax.broadcasted_iota(jnp.int32, (8, 1), 0) < NSA_R
    tb = t_pos // BLK
    for g in range(NSA_G):
        q = q_ref[g] * (DH ** -0.5)
        bias = bias_ref[g]
        s_p = _nt(q, kc_ref[g], HI) + bias[:, :n_past]
        row = b * NSA_G + g
        k_last = last_ref[0, pl.ds(row, 1), :]
        v_last = last_ref[1, pl.ds(row, 1), :]
        s_l = jnp.sum(q * k_last, axis=-1, keepdims=True) + bias[:, n_past:n_past + 1]
        valid_p = (t_pos - (n_lane[:, :n_past] * BLK + (BLK - 1))) >= 0
        valid_l = (t_pos - (n_past * BLK + (BLK - 1))) >= 0
        s_p = jnp.where(valid_p, s_p, -jnp.inf)
        s_l = jnp.where(valid_l, s_l, -jnp.inf)
        m = jnp.maximum(jnp.max(s_p, axis=-1, keepdims=True), s_l)
        m = jnp.where(m > -jnp.inf, m, 0.0)
        e_p = jnp.exp(s_p - m)
        e_l = jnp.exp(s_l - m)
        den = jnp.maximum(jnp.sum(e_p, axis=-1, keepdims=True) + e_l, 1e-30)
        p_p = e_p / den
        p_l = e_l / den
        oc_ref[g] = _dot(p_p, vc_ref[g], HI) + p_l * v_last
        imp_p = jnp.sum(jnp.where(head_row, p_p, 0.0), axis=0, keepdims=True)
        imp_l = jnp.sum(jnp.where(head_row, p_l, 0.0), axis=0, keepdims=True)
        imp = jnp.concatenate([imp_p, jnp.broadcast_to(imp_l, (1, SEL_LANES - n_past))], axis=1)
        forced = (n_lane == 0) | (n_lane == tb) | (n_lane == tb - 1)
        future = n_lane * BLK > t_pos
        score = jnp.where(forced, jnp.inf, jnp.where(future, -jnp.inf, imp))
        cand = n_lane < n_blocks
        sel = jnp.zeros((1, LANE), jnp.int32)
        k_lane = lax.broadcasted_iota(jnp.int32, (1, LANE), 1)
        for k in range(min(TOPN, n_blocks)):
            best = jnp.max(jnp.where(cand, score, -jnp.inf), axis=-1, keepdims=True)
            idx_f = jnp.min(jnp.where(cand & (score == best), n_lane_f, float(SEL_LANES)), axis=-1, keepdims=True)
            idx = idx_f.astype(jnp.int32)
            sel = jnp.where(k_lane == k, idx, sel)
            cand = cand & (n_lane != idx)
        sel_ref[g] = jnp.broadcast_to(sel, (8, LANE))


def _sel_decode(q8, cmp_kv, cmp_last, bias_cd, t_pos):
    nd = q8.shape[0]
    n_past = cmp_kv.shape[3]
    return pl.pallas_call(
        functools.partial(_sel_decode_kernel, t_pos=t_pos),
        out_shape=(jax.ShapeDtypeStruct((nd, NSA_G, 8, DH), F32),
                   jax.ShapeDtypeStruct((nd, NSA_G, 8, LANE), jnp.int32)),
        grid=(nd,),
        in_specs=[
            pl.BlockSpec((None, NSA_G, 8, DH), lambda b: (b, 0, 0, 0)),
            pl.BlockSpec((None, None, NSA_G, n_past, DH), lambda b: (0, b, 0, 0, 0)),
            pl.BlockSpec((None, None, NSA_G, n_past, DH), lambda b: (1, b, 0, 0, 0)),
            pl.BlockSpec(cmp_last.shape, lambda b: (0, 0, 0)),
            pl.BlockSpec(bias_cd.shape, lambda b: (0, 0, 0)),
        ],
        out_specs=(pl.BlockSpec((None, NSA_G, 8, DH), lambda b: (b, 0, 0, 0)),
                   pl.BlockSpec((None, NSA_G, 8, LANE), lambda b: (b, 0, 0, 0))),
        compiler_params=_cparams(("parallel",)),
        name="sel_decode",
    )(q8, cmp_kv, cmp_kv, cmp_last, bias_cd)


HALF_ROWS = BLK * ROW_W


def _slc_decode_kernel(sel_ref, pt_ref, q_ref, oc_ref, gl_ref, gb_ref, slc_ref, bias_ref, nk_ref, nv_ref,
                       win_ref, nwk_ref, nwv_ref, bw_ref, o_ref, gbuf, sem, *, t_pos, n_past):
    b = pl.program_id(0)
    n_win = win_ref.shape[0] // ROW_W

    def block_copy(g, k):
        n = jnp.minimum(sel_ref[(b * NSA_G + g) * TOPN + k], n_past - 1)
        half = pt_ref[b, n // 2] * 2 + n % 2
        return pltpu.make_async_copy(slc_ref.at[pl.ds(pl.multiple_of(half * HALF_ROWS, HALF_ROWS), HALF_ROWS), :],
                                     gbuf.at[g * TOPN + k], sem)

    for g in range(NSA_G):
        for k in range(TOPN):
            block_copy(g, k).start()

    c = lax.broadcasted_iota(jnp.int32, (1, n_win), 1)
    dist = n_win - c
    valid = (dist >= 0) & (dist <= WINDOW) & (t_pos - dist >= 0)
    qs, o_w = [], []
    for g in range(NSA_G):
        q = q_ref[g] * (DH ** -0.5)
        q_bf = q.astype(BF16)
        qs.append((q, q_bf))
        kw = win_ref[pl.ds(2 * g, n_win, stride=ROW_W), :].astype(BF16)
        vw = win_ref[pl.ds(2 * g + 1, n_win, stride=ROW_W), :].astype(BF16)
        bw = bw_ref[g]
        s_w = jnp.where(valid, _nt(q_bf, kw) + bw[:, :n_win], -jnp.inf)
        s_n = jnp.sum(q * nwk_ref[g, 0:1, :], axis=-1, keepdims=True) + bw[:, n_win:n_win + 1]
        m_w = jnp.maximum(jnp.max(s_w, axis=-1, keepdims=True), s_n)
        e_w = jnp.exp(s_w - m_w)
        e_n = jnp.exp(s_n - m_w)
        den = jnp.maximum(jnp.sum(e_w, axis=-1, keepdims=True) + e_n, 1e-30)
        o_w.append((_dot(e_w.astype(BF16), vw) + e_n * nwv_ref[g, 0:1, :]) / den)

    for g in range(NSA_G):
        for k in range(TOPN):
            block_copy(g, k).wait()

    row0 = lax.broadcasted_iota(jnp.int32, (BLK, 1), 0) == 0
    j_row = lax.broadcasted_iota(jnp.int32, (1, BLK), 1)
    for g in range(NSA_G):
        q, q_bf = qs[g]
        scores, values = [], []
        for k in range(TOPN):
            n = sel_ref[(b * NSA_G + g) * TOPN + k]
            is_new = (n == n_past) & row0
            kb = jnp.where(is_new, nk_ref[g, 0:1, :], gbuf[g * TOPN + k, pl.ds(2 * g, BLK, stride=ROW_W), :])
            vb = jnp.where(is_new, nv_ref[g, 0:1, :], gbuf[g * TOPN + k, pl.ds(2 * g + 1, BLK, stride=ROW_W), :])
            s = _nt(q_bf, kb.astype(BF16)) + bias_ref[g, n]
            scores.append(jnp.where(t_pos - (n * BLK + j_row) >= 0, s, -jnp.inf))
            values.append(vb.astype(BF16))
        m = functools.reduce(jnp.maximum, [jnp.max(s, axis=-1, keepdims=True) for s in scores])
        m = jnp.where(m > -jnp.inf, m, 0.0)
        probs = [jnp.exp(s - m) for s in scores]
        l = functools.reduce(jnp.add, [jnp.sum(p, axis=-1, keepdims=True) for p in probs])
        acc = functools.reduce(jnp.add, [_dot(p.astype(BF16), v) for p, v in zip(probs, values)])
        o_s = acc / jnp.maximum(l, 1e-30)
        gates = _sigmoid(gl_ref[g] + gb_ref[g])
        o_ref[g] = gates[:, 0:1] * oc_ref[g] + gates[:, 1:2] * o_s + gates[:, 2:3] * o_w[g]


def _slc_decode(sel_flat, page_table, q8, o_c, gate_l, gate_b, slc_cache, bias_sd, new_k, new_v,
                win_cache, new_wk, new_wv, bias_wd, t_pos):
    nd = q8.shape[0]
    n_past = page_table.shape[1] * (PAGE // BLK)
    win_rows = win_cache.shape[0] // nd

    def per_b(b, sel, pt):
        return (b, 0, 0, 0)

    def whole(shape):
        return pl.BlockSpec(shape, lambda b, sel, pt: (0,) * len(shape), pipeline_mode=pl.Buffered(1))

    b_spec = pl.BlockSpec((None, NSA_G, 8, DH), per_b)
    grid_spec = pltpu.PrefetchScalarGridSpec(
        num_scalar_prefetch=2,
        grid=(nd,),
        in_specs=[
            b_spec, b_spec, b_spec, whole(gate_b.shape),
            pl.BlockSpec(memory_space=pl.ANY),
            whole(bias_sd.shape),
            b_spec, b_spec,
            pl.BlockSpec((win_rows, DH), lambda b, sel, pt: (b, 0)),
            b_spec, b_spec,
            whole(bias_wd.shape),
        ],
        out_specs=b_spec,
        scratch_shapes=[pltpu.VMEM((NSA_G * TOPN, HALF_ROWS, DH), F32), pltpu.SemaphoreType.DMA(())],
    )
    return pl.pallas_call(
        functools.partial(_slc_decode_kernel, t_pos=t_pos, n_past=n_past),
        out_shape=jax.ShapeDtypeStruct((nd, NSA_G, 8, DH), F32),
        grid_spec=grid_spec,
        compiler_params=_cparams(("arbitrary",)),
        name="slc_decode",
    )(sel_flat, page_table, q8, o_c, gate_l, gate_b, slc_cache, bias_sd, new_k, new_v,
      win_cache, new_wk, new_wv, bias_wd)


def _t5_bucket(dist):
    n = jnp.maximum(dist, 0)
    nf = jnp.maximum(n, REL_MAX_EXACT).astype(F32)
    large = REL_MAX_EXACT + (jnp.log(nf / REL_MAX_EXACT) / math.log(REL_MAX_DIST / REL_MAX_EXACT)
                             * (REL_BUCKETS - REL_MAX_EXACT)).astype(jnp.int32)
    return jnp.where(n < REL_MAX_EXACT, n, jnp.minimum(large, REL_BUCKETS - 1))


def _bias_table(rel_bias, dist):
    onehot = (_t5_bucket(dist)[..., None] == jnp.arange(REL_BUCKETS)).astype(F32)
    b = jnp.einsum("...k,kh->...h", onehot, rel_bias, precision=HI)
    b = jnp.moveaxis(b, -1, 0)
    return b.reshape((NSA_G, NSA_R) + dist.shape)


def _prompt_bias_tables(rel_bias):
    i = jnp.arange(QBLK)[:, None]
    rel = jnp.arange(LANE)[None, :]
    bct = _bias_table(rel_bias, BLK * (rel - 1) + i % BLK + 1)
    d_near = SLC_PAD + i - jnp.arange(SLC_NEAR)[None, :]
    far = rel_bias[REL_BUCKETS - 1].reshape(NSA_G, NSA_R, 1, 1)
    bsn = _bias_table(rel_bias, d_near) - far + jnp.where(d_near >= 0, 0.0, NEG)
    d_win = WIN_PAD + i - jnp.arange(WIN_KEYS)[None, :]
    bw = _bias_table(rel_bias, d_win) + jnp.where((d_win >= 0) & (d_win <= WINDOW), 0.0, NEG)
    return bct, bsn, bw


def _pad_rows(a, rows, axis):
    pad = [(0, 0)] * a.ndim
    pad[axis] = (0, rows - a.shape[axis])
    return jnp.pad(a, pad)


def _decode_bias_tables(rel_bias, t_pos, n_past, n_win):
    n = jnp.arange(SEL_LANES)
    bias_cd = _pad_rows(_bias_table(rel_bias, t_pos - (n * BLK + BLK - 1)), 8, 1)
    tok = jnp.arange(n_past + 1)[:, None] * BLK + jnp.arange(BLK)[None, :]
    bias_sd = _pad_rows(jnp.swapaxes(_bias_table(rel_bias, t_pos - tok), 1, 2), 8, 2)
    c = jnp.arange(n_win + LANE)
    bias_wd = _pad_rows(_bias_table(rel_bias, n_win - c), 8, 1)
    return bias_cd, bias_sd, bias_wd


def kernel(x_prompt, x_sample, mem_prompt, cache_cmp_kv, cache_slc_kv, state_win_kv, state_gla, cache_mem_kv,
           page_table, norm_g, w_ffn_gate, w_ffn_up, w_ffn_down, w_in_gla, w_in_nsa, w_out, mem_norm_g, w_mem_kv,
           w_gla_a2, b_gla_a, gla_onorm_g, nsa_gate_b, cmp_pe, cmp_w1, cmp_b1, cmp_w2, rel_bias):
    nb, seq, _ = x_prompt.shape
    nd = x_sample.shape[0]
    depth = norm_g.shape[0]
    n_pages = page_table.shape[1]
    past_len = n_pages * PAGE
    n_past = past_len // BLK
    n_win = state_win_kv.shape[2]
    sr = SAMPLE_ROWS

    xp = x_prompt.reshape(nb * seq, D_MODEL)
    xs = _pad_rows(x_sample.reshape(nd, D_MODEL), sr, 0)
    mem_x = mem_prompt.reshape(nb * MEM_LEN, D_MODEL)

    w_o = w_out.astype(BF16)
    w_mkv = w_mem_kv.astype(BF16)
    def zero_rows(like, rows):
        return jnp.zeros((like.shape[0], rows, D_MODEL), BF16)

    wt_gla = jnp.swapaxes(w_in_gla, 1, 2).astype(BF16)
    tail_gla = jnp.concatenate(
        [wt_gla[:, GLA_MAIN + GLA_RANK:], wt_gla[:, GLA_MAIN:GLA_MAIN + GLA_RANK],
         zero_rows(wt_gla, TN_PROJ - MEM_W - GLA_RANK)], axis=1)
    n_gate = 3 * NSA_HEADS
    wt_nsa = jnp.swapaxes(w_in_nsa, 1, 2).astype(BF16)
    tail_nsa = jnp.concatenate(
        [wt_nsa[:, NSA_GATE_COL:NSA_GATE_COL + n_gate], zero_rows(wt_nsa, TN_PROJ - MEM_W - n_gate),
         wt_nsa[:, NSA_GATE_COL + n_gate:]], axis=1)
    wa_pad = _pad_rows(w_gla_a2, LANE, 1)
    w1 = cmp_w1.astype(BF16)
    w2 = cmp_w2.astype(BF16)

    def ffn_both(x_p, x_s, i, j):
        g1, g2 = norm_g[i, 4 * j][None], norm_g[i, 4 * j + 1][None]
        x_s, wg, wu, wd = _ffn_half(x_s, g1, w_ffn_gate, w_ffn_up, w_ffn_down, g2, TM_SAMPLE, f32_weights_at=(i, j))
        return _ffn_half(x_p, g1, wg, wu, wd, g2, TM_FFN), x_s

    def every(a, step):
        return a.reshape(nd, step, a.shape[-1])[:, 0]

    def per_seq(a):
        return _pad_rows(a[:, None, :], sr, 1).reshape(nd * sr, a.shape[-1])

    outs = dict(gla_p=[], gla_s=[], cmp_p=[], cmp_s=[], slc_p=[], slc_s=[], win_p=[], win_s=[], mem_p=[])
    for i in range(depth):
        li = i // 2
        mem_kv_p = _norm_matmul(mem_x, mem_norm_g[i][None], w_mkv, i, TM_MEMKV, MEM_W)
        outs["mem_p"].append(mem_kv_p.reshape(nb, MEM_LEN, 2, N_MEM_HEADS, MEM_HEAD_DIM))
        mem_kv_p = mem_kv_p.reshape(nb, MEM_LEN, 2 * MEM_W)
        mem_kv_s = cache_mem_kv[i].reshape(nd, MEM_LEN, 2 * MEM_W)
        xp, xs = ffn_both(xp, xs, i, 0)
        g_mix = norm_g[i, 2][None]
        if i % 2 == 0:
            proj_p = _in_proj(xp, g_mix, wt_gla, GLA_MAIN, tail_gla, li, TM_PROJ)
            proj_s = _in_proj(xs, g_mix, wt_gla, GLA_MAIN, tail_gla, li, TM_SAMPLE)
            mem_col = GLA_MAIN // MEM_W
            b_a = b_gla_a[li][None]
            gn = gla_onorm_g[li][None]
            s0 = jnp.zeros((nb, GLA_HEADS, GLA_DK, GLA_DV), F32)
            tok_p, sp = _gla(proj_p, s0, wa_pad[li], b_a, gn, seq, TL_GLA, GLA_CHUNK, seq)
            tok_s, ss = _gla(per_seq(proj_s[:nd]), state_gla[li], wa_pad[li], b_a, gn, sr, sr, sr, 1)
            tok_s = _pad_rows(every(tok_s, sr), sr, 0)
            outs["gla_p"].append(sp)
            outs["gla_s"].append(ss)
        else:
            proj_p = _in_proj(xp, g_mix, wt_nsa, NSA_GATE_COL, tail_nsa, li, TM_PROJ)
            proj_s = _in_proj(xs, g_mix, wt_nsa, NSA_GATE_COL, tail_nsa, li, TM_SAMPLE)
            mem_col = NSA_MEM_COL // MEM_W
            b1 = cmp_b1[li][:, None, :]
            gate_b = nsa_gate_b[li]
            bct, bsn, bw = _prompt_bias_tables(rel_bias)
            proj3 = proj_p.reshape(nb, seq, NSA_N)
            cmp_kv = _cmp_prompt(proj3, cmp_pe[li], w1[li], b1, w2[li])
            tok_p = _nsa_prompt(proj_p, cmp_kv, _pad_rows(gate_b[None], LANE, 1), bct, bsn, bw, nb, seq)
            kv_p = proj3[:, :, NSA_KV_COL:NSA_GATE_COL].reshape(nb, seq, 3, 2, NSA_G, DH)
            outs["cmp_p"].append(kv_p[:, :, 0].reshape(nb, seq // PAGE, PAGE, 2, NSA_G, DH))
            outs["slc_p"].append(kv_p[:, :, 1].reshape(nb, seq // PAGE, PAGE, 2, NSA_G, DH))
            outs["win_p"].append(kv_p[:, seq - n_win:, 2])
            t_pos = past_len
            bias_cd, bias_sd, bias_wd = _decode_bias_tables(rel_bias, t_pos, n_past, n_win)
            kv_s = proj_s[:nd, NSA_KV_COL:NSA_GATE_COL].reshape(nd, 3, 2, NSA_G, DH)
            outs["cmp_s"].append(kv_s[:, None, 0])
            outs["slc_s"].append(kv_s[:, None, 1])
            outs["win_s"].append(jnp.concatenate([state_win_kv[li][:, 1:], kv_s[:, None, 2]], axis=1))
            q8 = _pad_rows(proj_s[:nd, :TOK_W].reshape(nd, NSA_G, NSA_R, DH), 8, 2)
            new_cmp = _pad_rows(jnp.moveaxis(kv_s[:, 0], 1, 0).reshape(2, nd * NSA_G, DH), 32, 1)
            cmp_kv_s, cmp_last = _cmp_decode(page_table, _linear_cache(cache_cmp_kv[li]), new_cmp,
                                             cmp_pe[li], w1[li], b1, w2[li])
            o_c, sel = _sel_decode(q8, cmp_kv_s, cmp_last, bias_cd, t_pos)
            sel_flat = sel[:, :, 0, :TOPN].reshape(-1)

            def row8(a):
                return _pad_rows(a[:, :, None, :], 8, 2)

            gate_l = _pad_rows(_pad_rows(proj_s[:nd, NSA_GATE_COL:NSA_GATE_COL + n_gate].reshape(nd, NSA_G, NSA_R, 3),
                                         8, 2), LANE, 3)
            gate_b8 = _pad_rows(_pad_rows(gate_b.reshape(NSA_G, NSA_R, 3), 8, 1), LANE, 2)
            tok_s = _slc_decode(sel_flat, page_table, q8, o_c, gate_l, gate_b8,
                                _linear_cache(cache_slc_kv[li]), bias_sd,
                                row8(kv_s[:, 1, 0]), row8(kv_s[:, 1, 1]),
                                _linear_cache(state_win_kv[li]),
                                row8(kv_s[:, 2, 0]), row8(kv_s[:, 2, 1]), bias_wd, t_pos)
            tok_s = _pad_rows(tok_s[:, :, :NSA_R].reshape(nd, TOK_W), sr, 0).astype(BF16)
        mem_o_p = _mem_attn(proj_p, mem_col, mem_kv_p, seq, TM_MEM_ATTN)
        q_s = jnp.broadcast_to(proj_s[:nd, None, mem_col * MEM_W:(mem_col + 1) * MEM_W],
                               (nd, sr, MEM_W)).reshape(nd * sr, MEM_W)
        mem_o_s = _pad_rows(every(_mem_attn(q_s, 0, mem_kv_s, sr, sr), sr), sr, 0)
        xp = _out_proj(xp, tok_p, mem_o_p, w_o, i, norm_g[i, 3][None], TM_OUT)
        xs = _out_proj(xs, tok_s, mem_o_s, w_o, i, norm_g[i, 3][None], sr)
        xp, xs = ffn_both(xp, xs, i, 1)

    y_prompt = xp.reshape(nb, seq, D_MODEL)
    y_sample = xs[:nd].reshape(nd, 1, D_MODEL)
    st = lambda k: jnp.stack(outs[k])
    return (y_prompt, y_sample, st("gla_p"), st("cmp_p"), st("slc_p"), st("win_p"), st("mem_p"),
            st("gla_s"), st("cmp_s"), st("slc_s"), st("win_s"))
```

```python
import functools
import math

import jax
import jax.numpy as jnp
from jax import lax
from jax.experimental import pallas as pl
from jax.experimental.pallas import tpu as pltpu

F32 = jnp.float32
BF16 = jnp.bfloat16
HI = lax.Precision.HIGHEST

D_MODEL = 2048
D_FF = 5632
EPS = 1e-6
MEM_LEN = 256
N_MEM_HEADS = 4
MEM_HEAD_DIM = 128
MEM_W = N_MEM_HEADS * MEM_HEAD_DIM
TOK_W = D_MODEL - MEM_W
GLA_HEADS = 4
GLA_DV = TOK_W // GLA_HEADS
GLA_DK = GLA_DV // 2
GLA_RANK = 16
GLA_TAU = 16.0
GLA_CHUNK = 64
GLA_PAIR_W = 2 * GLA_DK
DH = 128
NSA_HEADS = TOK_W // DH
NSA_G = 3
NSA_R = NSA_HEADS // NSA_G
BLK = 64
TOPN = 16
WINDOW = 512
CMP_HID = 256
QBLK = 256
KV_W = NSA_G * DH
REL_BUCKETS = 32
REL_MAX_EXACT = 16
REL_MAX_DIST = 128
PAGE = 128
LANE = 128
HALF_LANE = LANE // 2
BLK_SHIFT = BLK.bit_length() - 1
VMEM_LIMIT = 56 * 1024 * 1024

TN_PROJ = 768
GLA_MAIN = 2 * GLA_HEADS * GLA_DK + 2 * TOK_W
GLA_A_COL = GLA_MAIN + MEM_W
GLA_N = GLA_MAIN + TN_PROJ
NSA_KV_COL = TOK_W
NSA_GATE_COL = NSA_KV_COL + 6 * KV_W
NSA_MEM_COL = NSA_GATE_COL + TN_PROJ - MEM_W
NSA_N = NSA_GATE_COL + TN_PROJ
SAMPLE_ROWS = 16
TM_SAMPLE = SAMPLE_ROWS
TM_FFN = 512
TM_PROJ = 1024
TM_MEMKV = 512
TM_OUT = 512
TM_MEM_ATTN = 512
TL_GLA = 512
SLC_PAD = 256
WIN_PAD = WINDOW
WIN_KEYS = WIN_PAD + QBLK
SLC_NEAR = SLC_PAD + QBLK
FAR_CHUNK = 1024
NEG = -1e30
M_FLOOR = -1e29


def _cparams(sem, vmem=VMEM_LIMIT):
    return pltpu.CompilerParams(dimension_semantics=sem, vmem_limit_bytes=vmem)


def _sigmoid(x):
    return 1.0 / (1.0 + jnp.exp(-x))


def _rms(x, g):
    ms = jnp.mean(x * x, axis=-1, keepdims=True)
    return x * lax.rsqrt(ms + EPS) * g


def _nt(a, b, precision=None):
    return lax.dot_general(a, b, (((1,), (1,)), ((), ())), precision=precision,
                           preferred_element_type=F32)


def _tn(a, b, precision=None):
    return lax.dot_general(a, b, (((0,), (0,)), ((), ())), precision=precision,
                           preferred_element_type=F32)


def _dot(a, b, precision=None):
    return jnp.dot(a, b, precision=precision, preferred_element_type=F32)


def _split3(x):
    hi = x.astype(BF16)
    r1 = x - hi.astype(F32)
    mid = r1.astype(BF16)
    lo = (r1 - mid.astype(F32)).astype(BF16)
    return hi, mid, lo


def _ffn_kernel(x_ref, g1_ref, wg_ref, wu_ref, wd_ref, g2_ref, o_ref, *rest):
    bf_out, (xn_ref, acc_ref) = rest[:-2], rest[-2:]
    j = pl.program_id(1)

    @pl.when(j == 0)
    def _():
        xn_ref[...] = _rms(x_ref[...], g1_ref[...]).astype(BF16)
        acc_ref[...] = jnp.zeros_like(acc_ref)

    wg, wu, wd = (w_ref[...].astype(BF16) for w_ref in (wg_ref, wu_ref, wd_ref))
    for out_ref, w in zip(bf_out, (wg, wu, wd)):
        out_ref[...] = w
    xn = xn_ref[...]
    gate = _dot(xn, wg)
    up = _dot(xn, wu)
    h = (gate * _sigmoid(gate) * up).astype(BF16)
    acc_ref[...] += _dot(h, wd)

    @pl.when(j == pl.num_programs(1) - 1)
    def _():
        o_ref[...] = x_ref[...] + 0.5 * _rms(acc_ref[...], g2_ref[...])


def _ffn_half(x, g1, wg, wu, wd, g2, tm, tf=512, f32_weights_at=None):
    m = x.shape[0]
    y_shape = jax.ShapeDtypeStruct((m, D_MODEL), F32)
    y_spec = pl.BlockSpec((tm, D_MODEL), lambda i, j: (i, 0))
    col_spec = pl.BlockSpec((D_MODEL, tf), lambda i, j: (0, j))
    row_spec = pl.BlockSpec((tf, D_MODEL), lambda i, j: (j, 0))
    if f32_weights_at is None:
        w_specs = [col_spec, col_spec, row_spec]
        out_shape, out_specs = y_shape, y_spec
    else:
        assert m == tm, "the bf16 copies are written once, by a single row tile"
        layer, half = f32_weights_at
        w_specs = [pl.BlockSpec((None, None, D_MODEL, tf), lambda i, j: (layer, half, 0, j)),
                   pl.BlockSpec((None, None, D_MODEL, tf), lambda i, j: (layer, half, 0, j)),
                   pl.BlockSpec((None, None, tf, D_MODEL), lambda i, j: (layer, half, j, 0))]
        out_shape = (y_shape, jax.ShapeDtypeStruct((D_MODEL, D_FF), BF16),
                     jax.ShapeDtypeStruct((D_MODEL, D_FF), BF16), jax.ShapeDtypeStruct((D_FF, D_MODEL), BF16))
        out_specs = (y_spec, col_spec, col_spec, row_spec)
    return pl.pallas_call(
        _ffn_kernel,
        out_shape=out_shape,
        grid=(m // tm, D_FF // tf),
        in_specs=[y_spec, pl.BlockSpec((1, D_MODEL), lambda i, j: (0, 0))] + w_specs
                 + [pl.BlockSpec((1, D_MODEL), lambda i, j: (0, 0))],
        out_specs=out_specs,
        scratch_shapes=[pltpu.VMEM((tm, D_MODEL), BF16), pltpu.VMEM((tm, D_MODEL), F32)],
        compiler_params=_cparams(("parallel", "arbitrary")),
        name="ffn_half",
    )(x, g1, wg, wu, wd, g2)


def _norm_matmul_kernel(x_ref, g_ref, w_ref, o_ref, xn_ref):
    @pl.when(pl.program_id(1) == 0)
    def _():
        xn_ref[...] = _rms(x_ref[...], g_ref[...]).astype(BF16)

    o_ref[...] = _dot(xn_ref[...], w_ref[...])


def _norm_matmul(x, g, w, layer, tm, tn):
    m, n = x.shape[0], w.shape[2]
    return pl.pallas_call(
        _norm_matmul_kernel,
        out_shape=jax.ShapeDtypeStruct((m, n), F32),
        grid=(m // tm, n // tn),
        in_specs=[
            pl.BlockSpec((tm, D_MODEL), lambda i, j: (i, 0)),
            pl.BlockSpec((1, D_MODEL), lambda i, j: (0, 0)),
            pl.BlockSpec((None, D_MODEL, tn), lambda i, j: (layer, 0, j)),
        ],
        out_specs=pl.BlockSpec((tm, tn), lambda i, j: (i, j)),
        scratch_shapes=[pltpu.VMEM((tm, D_MODEL), BF16)],
        compiler_params=_cparams(("parallel", "arbitrary")),
        name="norm_matmul",
    )(x, g, w)


def _in_proj_kernel(x_ref, g_ref, wm_ref, wt_ref, o_ref, xn_ref, *, n_main, kv_tiles):
    j = pl.program_id(1)
    kv_lo, kv_hi = kv_tiles
    is_kv = (j >= kv_lo) & (j < kv_hi)

    @pl.when(j == 0)
    def _():
        xn_ref[...] = _rms(x_ref[...], g_ref[...]).astype(BF16)

    @pl.when((j < n_main) & jnp.logical_not(is_kv))
    def _():
        o_ref[...] = _nt(xn_ref[...], wm_ref[...])

    @pl.when(is_kv)
    def _():
        res = _nt(xn_ref[...], wm_ref[...])
        for g in range(NSA_G):
            for kv in range(2):
                src, dst = kv * NSA_G + g, g * 2 + kv
                o_ref[:, dst * DH:(dst + 1) * DH] = res[:, src * DH:(src + 1) * DH]

    @pl.when(j == n_main)
    def _():
        o_ref[...] = _nt(xn_ref[...], wt_ref[...])


def _in_proj(x, g, w_main, main_cols, w_tail, layer, tm, kv_tiles=(0, 0)):
    m = x.shape[0]
    n_main = main_cols // TN_PROJ
    return pl.pallas_call(
        functools.partial(_in_proj_kernel, n_main=n_main, kv_tiles=kv_tiles),
        out_shape=jax.ShapeDtypeStruct((m, main_cols + TN_PROJ), F32),
        grid=(m // tm, n_main + 1),
        in_specs=[
            pl.BlockSpec((tm, D_MODEL), lambda i, j: (i, 0)),
            pl.BlockSpec((1, D_MODEL), lambda i, j: (0, 0)),
            pl.BlockSpec((None, TN_PROJ, D_MODEL), lambda i, j: (layer, jnp.minimum(j, n_main - 1), 0)),
            pl.BlockSpec((None, TN_PROJ, D_MODEL), lambda i, j: (layer, 0, 0)),
        ],
        out_specs=pl.BlockSpec((tm, TN_PROJ), lambda i, j: (i, j)),
        scratch_shapes=[pltpu.VMEM((tm, D_MODEL), BF16)],
        compiler_params=_cparams(("parallel", "arbitrary")),
        name="in_proj",
    )(x, g, w_main, w_tail)


def _out_proj_kernel(x_ref, tok_ref, mem_ref, wt_ref, wm_ref, g_ref, o_ref):
    y = _dot(tok_ref[...], wt_ref[...]) + _dot(mem_ref[...], wm_ref[...])
    o_ref[...] = x_ref[...] + _rms(y, g_ref[...])


def _out_proj(x, tok, mem_o, w_o, layer, g, tm):
    m = x.shape[0]
    return pl.pallas_call(
        _out_proj_kernel,
        out_shape=jax.ShapeDtypeStruct((m, D_MODEL), F32),
        grid=(m // tm,),
        in_specs=[
            pl.BlockSpec((tm, D_MODEL), lambda i: (i, 0)),
            pl.BlockSpec((tm, TOK_W), lambda i: (i, 0)),
            pl.BlockSpec((tm, MEM_W), lambda i: (i, 0)),
            pl.BlockSpec((None, TOK_W, D_MODEL), lambda i: (layer, 0, 0), pipeline_mode=pl.Buffered(1)),
            pl.BlockSpec((None, MEM_W, D_MODEL), lambda i: (layer, TOK_W // MEM_W, 0), pipeline_mode=pl.Buffered(1)),
            pl.BlockSpec((1, D_MODEL), lambda i: (0, 0)),
        ],
        out_specs=pl.BlockSpec((tm, D_MODEL), lambda i: (i, 0)),
        compiler_params=_cparams(("parallel",)),
        name="out_proj",
    )(x, tok, mem_o, w_o, w_o, g)


def _mem_attn_kernel(q_ref, kv_ref, o_ref):
    for h in range(N_MEM_HEADS):
        q = (q_ref[:, h * DH:(h + 1) * DH] * (MEM_HEAD_DIM ** -0.5)).astype(BF16)
        k = kv_ref[:, h * DH:(h + 1) * DH].astype(BF16)
        v = kv_ref[:, MEM_W + h * DH:MEM_W + (h + 1) * DH].astype(BF16)
        s = _nt(q, k)
        e = jnp.exp(s - jnp.max(s, axis=-1, keepdims=True))
        p = e / jnp.sum(e, axis=-1, keepdims=True)
        o_ref[:, h * DH:(h + 1) * DH] = _dot(p.astype(BF16), v).astype(o_ref.dtype)


def _mem_attn(q_arr, q_col_block, mem_kv, rows_per_batch, tm):
    nb = mem_kv.shape[0]
    per = rows_per_batch // tm
    return pl.pallas_call(
        _mem_attn_kernel,
        out_shape=jax.ShapeDtypeStruct((nb * rows_per_batch, MEM_W), BF16),
        grid=(nb, per),
        in_specs=[
            pl.BlockSpec((tm, MEM_W), lambda b, i: (b * per + i, q_col_block)),
            pl.BlockSpec((None, MEM_LEN, 2 * MEM_W), lambda b, i: (b, 0, 0)),
        ],
        out_specs=pl.BlockSpec((tm, MEM_W), lambda b, i: (b * per + i, 0)),
        compiler_params=_cparams(("parallel", "parallel")),
        name="mem_attn",
    )(q_arr, mem_kv)


GLA_PAIRS = GLA_HEADS // 2


def _gla_kernel(q_ref, k_ref, v_ref, r_ref, a_ref, s0_ref, wa_ref, ba_ref, gn_ref,
                tok_ref, s_out_ref, s_ref, la_ref, *, chunk, n_valid):
    l = pl.program_id(1)
    tl = q_ref.shape[0]

    @pl.when(l == 0)
    def _():
        for p in range(GLA_PAIRS):
            s_ref[p] = s0_ref[2 * p:2 * p + 2].reshape(GLA_PAIR_W, GLA_DV).T

    lane = lax.broadcasted_iota(jnp.int32, (1, GLA_PAIR_W), 1)
    head_mask = [(lane < GLA_DK).astype(F32), (lane >= GLA_DK).astype(F32)]
    ti = lax.broadcasted_iota(jnp.int32, (chunk, chunk), 0)
    si = lax.broadcasted_iota(jnp.int32, (chunk, chunk), 1)
    causal = si <= ti
    tri = jnp.where(causal, 1.0, 0.0).astype(BF16)

    a_hi, a_lo, _ = _split3(a_ref[...])
    w_hi, w_lo, _ = _split3(wa_ref[...])
    z = _dot(a_hi, w_hi) + _dot(a_lo, w_hi) + _dot(a_hi, w_lo) + ba_ref[...]
    la_all = -(jnp.maximum(-z, 0.0) + jnp.log1p(jnp.exp(-jnp.abs(z)))) / GLA_TAU
    pos = l * tl + lax.broadcasted_iota(jnp.int32, (tl, 1), 0)
    la_ref[...] = jnp.where(pos < n_valid, la_all, 0.0)

    def step(ci, carry):
        r0 = pl.multiple_of(ci * chunk, chunk)
        rows = pl.ds(r0, chunk)
        for p in range(GLA_PAIRS):
            pair = slice(p * GLA_PAIR_W, (p + 1) * GLA_PAIR_W)
            b = functools.reduce(jnp.add, [_dot(tri, piece) for piece in _split3(la_ref[rows, pair])])
            bl = b[chunk - 1:chunk, :]
            q = q_ref[rows, pair] * (GLA_DK ** -0.5)
            k = k_ref[rows, pair]
            qe = q * jnp.exp(b)
            ke = (k * jnp.exp(-b)).astype(BF16)
            kd = k * jnp.exp(bl - b)
            st_old = s_ref[p]
            st_bf = st_old.astype(BF16)
            upd = None
            for h in range(2):
                head = slice((2 * p + h) * GLA_DV, (2 * p + h + 1) * GLA_DV)
                v = v_ref[rows, head].astype(BF16)
                qm = (qe * head_mask[h]).astype(BF16)
                att = jnp.where(causal, _nt(qm, ke), 0.0)
                o = _nt(qm, st_bf) + _dot(att.astype(BF16), v)
                o = _rms(o, gn_ref[...])
                r = r_ref[rows, head]
                tok_ref[rows, head] = (o * (r * _sigmoid(r))).astype(tok_ref.dtype)
                u = _tn(v, (kd * head_mask[h]).astype(BF16))
                upd = u if upd is None else upd + u
            s_ref[p] = jnp.exp(bl) * st_old + upd
        return carry

    lax.fori_loop(0, tl // chunk, step, 0)

    @pl.when(l == pl.num_programs(1) - 1)
    def _():
        for p in range(GLA_PAIRS):
            s_out_ref[2 * p:2 * p + 2] = s_ref[p].T.reshape(2, GLA_DK, GLA_DV)


def _gla(proj, s0, wa_pad, b_a, gn, seq, tl, chunk, n_valid):
    nb = s0.shape[0]
    per = seq // tl
    qk_w = GLA_HEADS * GLA_DK
    row = lambda b, l: b * per + l

    return pl.pallas_call(
        functools.partial(_gla_kernel, chunk=chunk, n_valid=n_valid),
        out_shape=(jax.ShapeDtypeStruct((nb * seq, TOK_W), BF16),
                   jax.ShapeDtypeStruct((nb, GLA_HEADS, GLA_DK, GLA_DV), F32)),
        grid=(nb, per),
        in_specs=[
            pl.BlockSpec((tl, qk_w), lambda b, l: (row(b, l), 0)),
            pl.BlockSpec((tl, qk_w), lambda b, l: (row(b, l), 1)),
            pl.BlockSpec((tl, TOK_W), lambda b, l: (row(b, l), 2 * qk_w // TOK_W)),
            pl.BlockSpec((tl, TOK_W), lambda b, l: (row(b, l), 2 * qk_w // TOK_W + 1)),
            pl.BlockSpec((tl, LANE), lambda b, l: (row(b, l), GLA_A_COL // LANE)),
            pl.BlockSpec((None, GLA_HEADS, GLA_DK, GLA_DV), lambda b, l: (b, 0, 0, 0)),
            pl.BlockSpec((LANE, qk_w), lambda b, l: (0, 0)),
            pl.BlockSpec((1, qk_w), lambda b, l: (0, 0)),
            pl.BlockSpec((1, GLA_DV), lambda b, l: (0, 0)),
        ],
        out_specs=(pl.BlockSpec((tl, TOK_W), lambda b, l: (row(b, l), 0)),
                   pl.BlockSpec((None, GLA_HEADS, GLA_DK, GLA_DV), lambda b, l: (b, 0, 0, 0))),
        scratch_shapes=[pltpu.VMEM((GLA_PAIRS, GLA_DV, GLA_PAIR_W), F32), pltpu.VMEM((tl, qk_w), F32)],
        compiler_params=_cparams(("parallel", "arbitrary")),
        name="gla",
    )(proj, proj, proj, proj, proj, s0, wa_pad, b_a, gn)


def _masked_softmax(s, valid):
    s = jnp.where(valid, s, -jnp.inf)
    m = jnp.max(s, axis=-1, keepdims=True)
    m = jnp.where(m > -jnp.inf, m, 0.0)
    e = jnp.exp(s - m)
    return e / jnp.maximum(jnp.sum(e, axis=-1, keepdims=True), 1e-30)


def _online_update(state, s, v):
    m, l, acc = state
    r, nq, w = s.shape
    m_new = jnp.maximum(m, jnp.max(s, axis=-1, keepdims=True))
    alpha = jnp.exp(m - m_new)
    p = jnp.exp(s - m_new)
    l = alpha * l + jnp.sum(p, axis=-1, keepdims=True)
    pv = _dot(p.reshape(r * nq, w).astype(BF16), v).reshape(r, nq, DH)
    return m_new, l, alpha * acc + pv


def _compress_tail(xflat, w1, b1, w2):
    h = _dot(xflat, w1) + b1
    h = h * _sigmoid(h)
    return _dot(h.astype(BF16), w2)


def _cmp_prompt_kernel(x0_ref, x1_ref, x2_ref, pe_ref, w1_ref, b1_ref, w2_ref, o_ref, xflat_ref):
    x_refs = (x0_ref, x1_ref, x2_ref)
    nb, seq = x0_ref.shape[0], x0_ref.shape[1]
    nblk = seq // BLK
    for j in range(BLK):
        pe_j = pe_ref[j:j + 1, :]
        for b in range(nb):
            for g in range(NSA_G):
                xj = x_refs[g][b, pl.ds(j, nblk, stride=BLK), :]
                row = (b * NSA_G + g) * nblk
                xflat_ref[row:row + nblk, j * DH:(j + 1) * DH] = (xj + pe_j).astype(BF16)
    out = _compress_tail(xflat_ref[...], w1_ref[...], b1_ref[...], w2_ref[...])
    o_ref[...] = out.reshape(nb, NSA_G, nblk, DH)


def _kv_block(branch, g, kv):
    return NSA_KV_COL // DH + branch * 2 * NSA_G + g * 2 + kv


def _cmp_prompt(proj3, pe, w1, b1, w2):
    nb, seq, _ = proj3.shape
    nblk = seq // BLK
    return pl.pallas_call(
        _cmp_prompt_kernel,
        out_shape=jax.ShapeDtypeStruct((2, nb, NSA_G, nblk, DH), F32),
        grid=(2,),
        in_specs=[
            pl.BlockSpec((nb, seq, DH), lambda kv: (0, 0, _kv_block(0, 0, kv))),
            pl.BlockSpec((nb, seq, DH), lambda kv: (0, 0, _kv_block(0, 1, kv))),
            pl.BlockSpec((nb, seq, DH), lambda kv: (0, 0, _kv_block(0, 2, kv))),
            pl.BlockSpec((None, BLK, DH), lambda kv: (kv, 0, 0)),
            pl.BlockSpec((None, BLK * DH, CMP_HID), lambda kv: (kv, 0, 0)),
            pl.BlockSpec((None, 1, CMP_HID), lambda kv: (kv, 0, 0)),
            pl.BlockSpec((None, CMP_HID, DH), lambda kv: (kv, 0, 0)),
        ],
        out_specs=pl.BlockSpec((None, nb, NSA_G, nblk, DH), lambda kv: (kv, 0, 0, 0, 0)),
        scratch_shapes=[pltpu.VMEM((nb * NSA_G * nblk, BLK * DH), BF16)],
        compiler_params=_cparams(("arbitrary",)),
        name="cmp_prompt",
    )(proj3, proj3, proj3, pe, w1, b1, w2)


def _nsa_prompt_kernel(q_ref, gl_ref, gb_ref, kc_ref, vc_ref, ks_ref, vs_ref, kw_ref, vw_ref,
                       bct_ref, bsn_ref, bw_ref, o_ref, ksb, vsb, kwb, vwb):
    g = pl.program_id(1)
    qi = pl.program_id(2)
    seq = ks_ref.shape[0]
    nblk = seq // BLK
    R = NSA_R

    @pl.when(qi == 0)
    def _():
        for src, dst, pad in ((vs_ref, vsb, SLC_PAD), (vw_ref, vwb, WIN_PAD)):
            dst[0:pad, :] = jnp.zeros((pad, DH), BF16)
            dst[pad:pad + seq, :] = src[...].astype(BF16)
        for src, dst, pad, per_block in ((ks_ref, ksb, SLC_PAD, True), (kw_ref, kwb, WIN_PAD, False)):
            dst[0:pad, 0:DH] = jnp.zeros((pad, DH), BF16)
            dst[pad:pad + seq, 0:DH] = src[...].astype(BF16)
            pos = lax.broadcasted_iota(jnp.int32, (pad + seq, 1), 0) - pad
            feat = lax.broadcasted_iota(jnp.int32, (1, DH), 1)
            masked = (feat == HALF_LANE) & (pos < 0)
            if per_block:
                masked = masked | ((pos >= 0) & (lax.shift_right_arithmetic(pos, BLK_SHIFT) == feat))
            dst[:, DH:2 * DH] = jnp.where(masked, NEG, 0.0).astype(BF16)

    q = q_ref[...] * (DH ** -0.5)
    q_all = jnp.concatenate([q[:, r * DH:(r + 1) * DH] for r in range(R)], axis=0)
    q_bf = q_all.astype(BF16)
    i_col = lax.broadcasted_iota(jnp.int32, (QBLK, 1), 0)
    t_col = qi * QBLK + i_col
    n_row = lax.broadcasted_iota(jnp.int32, (1, nblk), 1)
    row0 = pl.multiple_of(qi * QBLK, QBLK)
    lane2 = lax.broadcasted_iota(jnp.int32, (1, LANE), 1)

    s_c = _nt(q_all, kc_ref[...], HI).reshape(R, QBLK, nblk)
    tb_col = lax.shift_right_arithmetic(t_col, BLK_SHIFT)
    rel = tb_col - n_row
    bias_c = []
    for r in range(R):
        tab = bct_ref[r]
        bias_c.append(jnp.where(rel == 0, tab[:, 0:1],
                      jnp.where(rel == 1, tab[:, 1:2],
                      jnp.where(rel == 2, tab[:, 2:3], tab[:, 3:4]))))
    s_c = s_c + jnp.stack(bias_c, axis=0)
    valid_c = (t_col - (n_row * BLK + (BLK - 1))) >= 0
    p_c = _masked_softmax(s_c, valid_c[None])
    o_c = _dot(p_c.reshape(R * QBLK, nblk).astype(BF16), vc_ref[...].astype(BF16))

    imp = jnp.sum(p_c, axis=0)
    forced = (n_row == 0) | (n_row == tb_col) | (n_row == tb_col - 1)
    future = n_row * BLK > t_col
    score = jnp.where(forced, jnp.inf, jnp.where(future, -jnp.inf, imp))
    score_t = jnp.concatenate([score, jnp.full((QBLK, LANE - nblk), -jnp.inf, F32)], axis=1).T[:HALF_LANE]
    n_sub = lax.broadcasted_iota(jnp.int32, (HALF_LANE, 1), 0)
    rank_t = jnp.zeros((HALF_LANE, QBLK), F32)
    for i in range(nblk):
        cand = score_t[i:i + 1, :]
        wins_tie = jnp.where(n_sub > i, 1.0, 0.0)
        rank_t = rank_t + jnp.where(cand > score_t, 1.0, jnp.where(cand == score_t, wins_tie, 0.0))
    rank = jnp.concatenate([rank_t, jnp.zeros((LANE - HALF_LANE, QBLK), F32)], axis=0).T
    left = lane2 < HALF_LANE
    n_far = (qi * QBLK - SLC_PAD) // BLK
    dropped = rank >= float(min(TOPN, nblk))
    flag_all = jnp.where(left, jnp.where(dropped, 1.0, 0.0), jnp.where(lane2 == HALF_LANE, 1.0, 0.0))
    flag_far = jnp.where(left & (lane2 >= n_far), 1.0, flag_all)
    q_far = jnp.concatenate([q_bf, jnp.concatenate([flag_far.astype(BF16)] * R, axis=0)], axis=1)
    q_near = jnp.concatenate([q_bf, jnp.concatenate([flag_all.astype(BF16)] * R, axis=0)], axis=1)

    def far_body(kc_i, state):
        start = pl.multiple_of(SLC_PAD + kc_i * FAR_CHUNK, BLK)
        s = _nt(q_far, ksb[pl.ds(start, FAR_CHUNK), :]).reshape(R, QBLK, FAR_CHUNK)
        return _online_update(state, s, vsb[pl.ds(start, FAR_CHUNK), :])

    blk_per_chunk = FAR_CHUNK // BLK
    n_chunks = (jnp.maximum(n_far, 0) + (blk_per_chunk - 1)) // blk_per_chunk
    state = (jnp.full((R, QBLK, 1), M_FLOOR, F32), jnp.zeros((R, QBLK, 1), F32), jnp.zeros((R, QBLK, DH), F32))
    state = lax.fori_loop(0, n_chunks, far_body, state)

    s = _nt(q_near, ksb[pl.ds(row0, SLC_NEAR), :]).reshape(R, QBLK, SLC_NEAR) + bsn_ref[...]
    _, l_s, acc_s = _online_update(state, s, vsb[pl.ds(row0, SLC_NEAR), :])
    o_s = (acc_s / jnp.maximum(l_s, 1e-30)).reshape(R * QBLK, DH)

    s = _nt(q_near, kwb[pl.ds(row0, WIN_KEYS), :]).reshape(R, QBLK, WIN_KEYS) + bw_ref[...]
    e_w = jnp.exp(s - jnp.max(s, axis=-1, keepdims=True))
    p_w = e_w / jnp.maximum(jnp.sum(e_w, axis=-1, keepdims=True), 1e-30)
    o_w = _dot(p_w.reshape(R * QBLK, WIN_KEYS).astype(BF16), vwb[pl.ds(row0, WIN_KEYS), :])

    gates = _sigmoid(gl_ref[...] + gb_ref[...])
    src = lax.broadcasted_iota(jnp.int32, (LANE, LANE), 0)
    dst = lax.broadcasted_iota(jnp.int32, (LANE, LANE), 1)
    pick = jnp.where((src == g * (3 * R) + dst) & (dst < 3 * R), 1.0, 0.0)
    gsel = _dot(gates, pick, HI)
    for r in range(R):
        rows = slice(r * QBLK, (r + 1) * QBLK)
        o = (gsel[:, 3 * r:3 * r + 1] * o_c[rows] + gsel[:, 3 * r + 1:3 * r + 2] * o_s[rows]
             + gsel[:, 3 * r + 2:3 * r + 3] * o_w[rows])
        o_ref[:, r * DH:(r + 1) * DH] = o.astype(o_ref.dtype)


def _nsa_prompt(proj2, cmp_kv, gate_b_pad, bct, bsn, bw, nb, seq):
    nq = seq // QBLK
    nblk = seq // BLK
    assert nblk <= HALF_LANE, "mask features of the selected branch hold at most 64 key blocks"
    assert seq % FAR_CHUNK == 0
    proj3 = proj2.reshape(nb, seq, NSA_N)

    def kv_spec(branch, kv):
        return pl.BlockSpec((None, seq, DH), lambda b, g, qi: (b, 0, _kv_block(branch, g, kv)))

    return pl.pallas_call(
        _nsa_prompt_kernel,
        out_shape=jax.ShapeDtypeStruct((nb * seq, TOK_W), BF16),
        grid=(nb, NSA_G, nq),
        in_specs=[
            pl.BlockSpec((QBLK, NSA_R * DH), lambda b, g, qi: (b * nq + qi, g)),
            pl.BlockSpec((QBLK, LANE), lambda b, g, qi: (b * nq + qi, NSA_GATE_COL // LANE)),
            pl.BlockSpec((1, LANE), lambda b, g, qi: (0, 0)),
            pl.BlockSpec((None, None, None, nblk, DH), lambda b, g, qi: (0, b, g, 0, 0)),
            pl.BlockSpec((None, None, None, nblk, DH), lambda b, g, qi: (1, b, g, 0, 0)),
            kv_spec(1, 0), kv_spec(1, 1), kv_spec(2, 0), kv_spec(2, 1),
            pl.BlockSpec((None, NSA_R, QBLK, LANE), lambda b, g, qi: (g, 0, 0, 0)),
            pl.BlockSpec((None, NSA_R, QBLK, SLC_NEAR), lambda b, g, qi: (g, 0, 0, 0)),
            pl.BlockSpec((None, NSA_R, QBLK, WIN_KEYS), lambda b, g, qi: (g, 0, 0, 0)),
        ],
        out_specs=pl.BlockSpec((QBLK, NSA_R * DH), lambda b, g, qi: (b * nq + qi, g)),
        scratch_shapes=[pltpu.VMEM((SLC_PAD + seq, 2 * DH), BF16), pltpu.VMEM((SLC_PAD + seq, DH), BF16),
                        pltpu.VMEM((WIN_PAD + seq, 2 * DH), BF16), pltpu.VMEM((WIN_PAD + seq, DH), BF16)],
        compiler_params=_cparams(("parallel", "parallel", "arbitrary")),
        name="nsa_prompt",
    )(proj2, proj2, gate_b_pad, cmp_kv, cmp_kv, proj3, proj3, proj3, proj3, bct, bsn, bw)


SUB_PAGES = 8
RING = 4
ROW_W = 2 * NSA_G
PAGE_ROWS = PAGE * ROW_W


def _linear_cache(cache):
    return jnp.transpose(cache, (0, 1, 3, 2, 4)).reshape(-1, DH)


def _cmp_decode_kernel(pt_ref, cache_ref, new_ref, pe_ref, w1_ref, b1_ref, w2_ref, o_ref, last_ref,
                       buf, sem, xflat_ref, xlast_ref, *, n_sub):
    b = pl.program_id(0)
    total = pl.num_programs(0) * n_sub
    sub_blk = SUB_PAGES * PAGE // BLK
    seq_blk = n_sub * sub_blk

    def page_copy(s, p):
        page = pt_ref[s // n_sub, (s % n_sub) * SUB_PAGES + p]
        return pltpu.make_async_copy(cache_ref.at[pl.ds(pl.multiple_of(page * PAGE_ROWS, PAGE_ROWS), PAGE_ROWS), :],
                                     buf.at[s % RING, pl.ds(p * PAGE_ROWS, PAGE_ROWS), :], sem.at[s % RING])

    def start_sub(s):
        for p in range(SUB_PAGES):
            page_copy(s, p).start()

    @pl.when(b == 0)
    def _():
        for s in range(RING):
            start_sub(s)

    def body(i, c):
        s = b * n_sub + i
        for p in range(SUB_PAGES):
            page_copy(s, p).wait()
        slot = s % RING
        row0 = pl.multiple_of(i * sub_blk, sub_blk)
        by_row = jnp.swapaxes(buf[slot].reshape(sub_blk, BLK * ROW_W, DH), 0, 1)
        for kv in range(2):
            for j in range(BLK):
                pe_j = pe_ref[kv, j:j + 1, :]
                for g in range(NSA_G):
                    xj = by_row[j * ROW_W + g * 2 + kv]
                    xflat_ref[kv, pl.ds(g * seq_blk + row0, sub_blk), j * DH:(j + 1) * DH] = (xj + pe_j).astype(BF16)

        @pl.when(s + RING < total)
        def _():
            start_sub(s + RING)

        return c

    lax.fori_loop(0, n_sub, body, 0)
    for kv in range(2):
        out = _compress_tail(xflat_ref[kv], w1_ref[kv], b1_ref[kv], w2_ref[kv])
        o_ref[kv] = out.reshape(NSA_G, seq_blk, DH)

    @pl.when(b == 0)
    def _():
        rows = new_ref.shape[1]
        for kv in range(2):
            for j in range(BLK):
                pe_j = jnp.broadcast_to(pe_ref[kv, j:j + 1, :], (rows, DH))
                xj = new_ref[kv] + pe_j if j == 0 else pe_j
                xlast_ref[:, j * DH:(j + 1) * DH] = xj.astype(BF16)
            last_ref[kv] = _compress_tail(xlast_ref[...], w1_ref[kv], b1_ref[kv], w2_ref[kv])


def _cmp_decode(page_table, cache, new_rows, pe, w1, b1, w2):
    nd, n_pages = page_table.shape
    n_sub = n_pages // SUB_PAGES
    seq_blk = n_pages * PAGE // BLK
    rows = new_rows.shape[1]

    def whole(shape):
        return pl.BlockSpec(shape, lambda b, pt: (0,) * len(shape), pipeline_mode=pl.Buffered(1))

    grid_spec = pltpu.PrefetchScalarGridSpec(
        num_scalar_prefetch=1,
        grid=(nd,),
        in_specs=[
            pl.BlockSpec(memory_space=pl.ANY),
            whole(new_rows.shape), whole(pe.shape), whole(w1.shape), whole(b1.shape), whole(w2.shape),
        ],
        out_specs=(pl.BlockSpec((2, None, NSA_G, seq_blk, DH), lambda b, pt: (0, b, 0, 0, 0)),
                   pl.BlockSpec((2, rows, DH), lambda b, pt: (0, 0, 0))),
        scratch_shapes=[pltpu.VMEM((RING, SUB_PAGES * PAGE_ROWS, DH), F32),
                        pltpu.SemaphoreType.DMA((RING,)),
                        pltpu.VMEM((2, NSA_G * seq_blk, BLK * DH), BF16),
                        pltpu.VMEM((rows, BLK * DH), BF16)],
    )
    return pl.pallas_call(
        functools.partial(_cmp_decode_kernel, n_sub=n_sub),
        out_shape=(jax.ShapeDtypeStruct((2, nd, NSA_G, seq_blk, DH), F32),
                   jax.ShapeDtypeStruct((2, rows, DH), F32)),
        grid_spec=grid_spec,
        compiler_params=_cparams(("arbitrary",)),
        name="cmp_decode",
    )(page_table, cache, new_rows, pe, w1, b1, w2)


SEL_LANES = 384


def _sel_decode_kernel(q_ref, kc_ref, vc_ref, last_ref, bias_ref, oc_ref, sel_ref, *, t_pos):
    b = pl.program_id(0)
    n_past = kc_ref.shape[1]
    n_blocks = n_past + 1
    n_lane = lax.broadcasted_iota(jnp.int32, (1, SEL_LANES), 1)
    n_lane_f = n_lane.astype(F32)
    head_row = lax.broadcasted_iota(jnp.int32, (8, 1), 0) < NSA_R
    tb = t_pos // BLK
    for g in range(NSA_G):
        q = q_ref[g] * (DH ** -0.5)
        bias = bias_ref[g]
        s_p = _nt(q, kc_ref[g], HI) + bias[:, :n_past]
        row = b * NSA_G + g
        k_last = last_ref[0, pl.ds(row, 1), :]
        v_last = last_ref[1, pl.ds(row, 1), :]
        s_l = jnp.sum(q * k_last, axis=-1, keepdims=True) + bias[:, n_past:n_past + 1]
        valid_p = (t_pos - (n_lane[:, :n_past] * BLK + (BLK - 1))) >= 0
        valid_l = (t_pos - (n_past * BLK + (BLK - 1))) >= 0
        s_p = jnp.where(valid_p, s_p, -jnp.inf)
        s_l = jnp.where(valid_l, s_l, -jnp.inf)
        m = jnp.maximum(jnp.max(s_p, axis=-1, keepdims=True), s_l)
        m = jnp.where(m > -jnp.inf, m, 0.0)
        e_p = jnp.exp(s_p - m)
        e_l = jnp.exp(s_l - m)
        den = jnp.maximum(jnp.sum(e_p, axis=-1, keepdims=True) + e_l, 1e-30)
        p_p = e_p / den
        p_l = e_l / den
        oc_ref[g] = _dot(p_p, vc_ref[g], HI) + p_l * v_last
        imp_p = jnp.sum(jnp.where(head_row, p_p, 0.0), axis=0, keepdims=True)
        imp_l = jnp.sum(jnp.where(head_row, p_l, 0.0), axis=0, keepdims=True)
        imp = jnp.concatenate([imp_p, jnp.broadcast_to(imp_l, (1, SEL_LANES - n_past))], axis=1)
        forced = (n_lane == 0) | (n_lane == tb) | (n_lane == tb - 1)
        future = n_lane * BLK > t_pos
        score = jnp.where(forced, jnp.inf, jnp.where(future, -jnp.inf, imp))
        cand = n_lane < n_blocks
        sel = jnp.zeros((1, LANE), jnp.int32)
        k_lane = lax.broadcasted_iota(jnp.int32, (1, LANE), 1)
        for k in range(min(TOPN, n_blocks)):
            best = jnp.max(jnp.where(cand, score, -jnp.inf), axis=-1, keepdims=True)
            idx_f = jnp.min(jnp.where(cand & (score == best), n_lane_f, float(SEL_LANES)), axis=-1, keepdims=True)
            idx = idx_f.astype(jnp.int32)
            sel = jnp.where(k_lane == k, idx, sel)
            cand = cand & (n_lane != idx)
        sel_ref[g] = jnp.broadcast_to(sel, (8, LANE))


def _sel_decode(q8, cmp_kv, cmp_last, bias_cd, t_pos):
    nd = q8.shape[0]
    n_past = cmp_kv.shape[3]
    return pl.pallas_call(
        functools.partial(_sel_decode_kernel, t_pos=t_pos),
        out_shape=(jax.ShapeDtypeStruct((nd, NSA_G, 8, DH), F32),
                   jax.ShapeDtypeStruct((nd, NSA_G, 8, LANE), jnp.int32)),
        grid=(nd,),
        in_specs=[
            pl.BlockSpec((None, NSA_G, 8, DH), lambda b: (b, 0, 0, 0)),
            pl.BlockSpec((None, None, NSA_G, n_past, DH), lambda b: (0, b, 0, 0, 0)),
            pl.BlockSpec((None, None, NSA_G, n_past, DH), lambda b: (1, b, 0, 0, 0)),
            pl.BlockSpec(cmp_last.shape, lambda b: (0, 0, 0)),
            pl.BlockSpec(bias_cd.shape, lambda b: (0, 0, 0)),
        ],
        out_specs=(pl.BlockSpec((None, NSA_G, 8, DH), lambda b: (b, 0, 0, 0)),
                   pl.BlockSpec((None, NSA_G, 8, LANE), lambda b: (b, 0, 0, 0))),
        compiler_params=_cparams(("parallel",)),
        name="sel_decode",
    )(q8, cmp_kv, cmp_kv, cmp_last, bias_cd)


HALF_ROWS = BLK * ROW_W


def _slc_decode_kernel(sel_ref, pt_ref, q_ref, oc_ref, gl_ref, gb_ref, slc_ref, bias_ref, nk_ref, nv_ref,
                       win_ref, nwk_ref, nwv_ref, bw_ref, o_ref, gbuf, sem, *, t_pos, n_past):
    b = pl.program_id(0)
    n_win = win_ref.shape[0] // ROW_W

    def block_copy(g, k):
        n = jnp.minimum(sel_ref[(b * NSA_G + g) * TOPN + k], n_past - 1)
        half = pt_ref[b, n // 2] * 2 + n % 2
        return pltpu.make_async_copy(slc_ref.at[pl.ds(pl.multiple_of(half * HALF_ROWS, HALF_ROWS), HALF_ROWS), :],
                                     gbuf.at[g * TOPN + k], sem)

    for g in range(NSA_G):
        for k in range(TOPN):
            block_copy(g, k).start()

    c = lax.broadcasted_iota(jnp.int32, (1, n_win), 1)
    dist = n_win - c
    valid = (dist >= 0) & (dist <= WINDOW) & (t_pos - dist >= 0)
    qs, o_w = [], []
    for g in range(NSA_G):
        q = q_ref[g] * (DH ** -0.5)
        q_bf = q.astype(BF16)
        qs.append((q, q_bf))
        kw = win_ref[pl.ds(2 * g, n_win, stride=ROW_W), :].astype(BF16)
        vw = win_ref[pl.ds(2 * g + 1, n_win, stride=ROW_W), :].astype(BF16)
        bw = bw_ref[g]
        s_w = jnp.where(valid, _nt(q_bf, kw) + bw[:, :n_win], -jnp.inf)
        s_n = jnp.sum(q * nwk_ref[g, 0:1, :], axis=-1, keepdims=True) + bw[:, n_win:n_win + 1]
        m_w = jnp.maximum(jnp.max(s_w, axis=-1, keepdims=True), s_n)
        e_w = jnp.exp(s_w - m_w)
        e_n = jnp.exp(s_n - m_w)
        den = jnp.maximum(jnp.sum(e_w, axis=-1, keepdims=True) + e_n, 1e-30)
        o_w.append((_dot(e_w.astype(BF16), vw) + e_n * nwv_ref[g, 0:1, :]) / den)

    for g in range(NSA_G):
        for k in range(TOPN):
            block_copy(g, k).wait()

    row0 = lax.broadcasted_iota(jnp.int32, (BLK, 1), 0) == 0
    j_row = lax.broadcasted_iota(jnp.int32, (1, BLK), 1)
    for g in range(NSA_G):
        q, q_bf = qs[g]
        scores, values = [], []
        for k in range(TOPN):
            n = sel_ref[(b * NSA_G + g) * TOPN + k]
            is_new = (n == n_past) & row0
            kb = jnp.where(is_new, nk_ref[g, 0:1, :], gbuf[g * TOPN + k, pl.ds(2 * g, BLK, stride=ROW_W), :])
            vb = jnp.where(is_new, nv_ref[g, 0:1, :], gbuf[g * TOPN + k, pl.ds(2 * g + 1, BLK, stride=ROW_W), :])
            s = _nt(q_bf, kb.astype(BF16)) + bias_ref[g, n]
            scores.append(jnp.where(t_pos - (n * BLK + j_row) >= 0, s, -jnp.inf))
            values.append(vb.astype(BF16))
        m = functools.reduce(jnp.maximum, [jnp.max(s, axis=-1, keepdims=True) for s in scores])
        m = jnp.where(m > -jnp.inf, m, 0.0)
        probs = [jnp.exp(s - m) for s in scores]
        l = functools.reduce(jnp.add, [jnp.sum(p, axis=-1, keepdims=True) for p in probs])
        acc = functools.reduce(jnp.add, [_dot(p.astype(BF16), v) for p, v in zip(probs, values)])
        o_s = acc / jnp.maximum(l, 1e-30)
        gates = _sigmoid(gl_ref[g] + gb_ref[g])
        o_ref[g] = gates[:, 0:1] * oc_ref[g] + gates[:, 1:2] * o_s + gates[:, 2:3] * o_w[g]


def _slc_decode(sel_flat, page_table, q8, o_c, gate_l, gate_b, slc_cache, bias_sd, new_k, new_v,
                win_cache, new_wk, new_wv, bias_wd, t_pos):
    nd = q8.shape[0]
    n_past = page_table.shape[1] * (PAGE // BLK)
    win_rows = win_cache.shape[0] // nd

    def per_b(b, sel, pt):
        return (b, 0, 0, 0)

    def whole(shape):
        return pl.BlockSpec(shape, lambda b, sel, pt: (0,) * len(shape), pipeline_mode=pl.Buffered(1))

    b_spec = pl.BlockSpec((None, NSA_G, 8, DH), per_b)
    grid_spec = pltpu.PrefetchScalarGridSpec(
        num_scalar_prefetch=2,
        grid=(nd,),
        in_specs=[
            b_spec, b_spec, b_spec, whole(gate_b.shape),
            pl.BlockSpec(memory_space=pl.ANY),
            whole(bias_sd.shape),
            b_spec, b_spec,
            pl.BlockSpec((win_rows, DH), lambda b, sel, pt: (b, 0)),
            b_spec, b_spec,
            whole(bias_wd.shape),
        ],
        out_specs=b_spec,
        scratch_shapes=[pltpu.VMEM((NSA_G * TOPN, HALF_ROWS, DH), F32), pltpu.SemaphoreType.DMA(())],
    )
    return pl.pallas_call(
        functools.partial(_slc_decode_kernel, t_pos=t_pos, n_past=n_past),
        out_shape=jax.ShapeDtypeStruct((nd, NSA_G, 8, DH), F32),
        grid_spec=grid_spec,
        compiler_params=_cparams(("arbitrary",)),
        name="slc_decode",
    )(sel_flat, page_table, q8, o_c, gate_l, gate_b, slc_cache, bias_sd, new_k, new_v,
      win_cache, new_wk, new_wv, bias_wd)


def _t5_bucket(dist):
    n = jnp.maximum(dist, 0)
    nf = jnp.maximum(n, REL_MAX_EXACT).astype(F32)
    large = REL_MAX_EXACT + (jnp.log(nf / REL_MAX_EXACT) / math.log(REL_MAX_DIST / REL_MAX_EXACT)
                             * (REL_BUCKETS - REL_MAX_EXACT)).astype(jnp.int32)
    return jnp.where(n < REL_MAX_EXACT, n, jnp.minimum(large, REL_BUCKETS - 1))


def _bias_table(rel_bias, dist):
    onehot = (_t5_bucket(dist)[..., None] == jnp.arange(REL_BUCKETS)).astype(F32)
    b = jnp.einsum("...k,kh->...h", onehot, rel_bias, precision=HI)
    b = jnp.moveaxis(b, -1, 0)
    return b.reshape((NSA_G, NSA_R) + dist.shape)


def _prompt_bias_tables(rel_bias):
    i = jnp.arange(QBLK)[:, None]
    rel = jnp.arange(LANE)[None, :]
    bct = _bias_table(rel_bias, BLK * (rel - 1) + i % BLK + 1)
    d_near = SLC_PAD + i - jnp.arange(SLC_NEAR)[None, :]
    far = rel_bias[REL_BUCKETS - 1].reshape(NSA_G, NSA_R, 1, 1)
    bsn = _bias_table(rel_bias, d_near) - far + jnp.where(d_near >= 0, 0.0, NEG)
    d_win = WIN_PAD + i - jnp.arange(WIN_KEYS)[None, :]
    bw = _bias_table(rel_bias, d_win) + jnp.where((d_win >= 0) & (d_win <= WINDOW), 0.0, NEG)
    return bct, bsn, bw


def _pad_rows(a, rows, axis):
    pad = [(0, 0)] * a.ndim
    pad[axis] = (0, rows - a.shape[axis])
    return jnp.pad(a, pad)


def _decode_bias_tables(rel_bias, t_pos, n_past, n_win):
    n = jnp.arange(SEL_LANES)
    bias_cd = _pad_rows(_bias_table(rel_bias, t_pos - (n * BLK + BLK - 1)), 8, 1)
    tok = jnp.arange(n_past + 1)[:, None] * BLK + jnp.arange(BLK)[None, :]
    bias_sd = _pad_rows(jnp.swapaxes(_bias_table(rel_bias, t_pos - tok), 1, 2), 8, 2)
    c = jnp.arange(n_win + LANE)
    bias_wd = _pad_rows(_bias_table(rel_bias, n_win - c), 8, 1)
    return bias_cd, bias_sd, bias_wd


def kernel(x_prompt, x_sample, mem_prompt, cache_cmp_kv, cache_slc_kv, state_win_kv, state_gla, cache_mem_kv,
           page_table, norm_g, w_ffn_gate, w_ffn_up, w_ffn_down, w_in_gla, w_in_nsa, w_out, mem_norm_g, w_mem_kv,
           w_gla_a2, b_gla_a, gla_onorm_g, nsa_gate_b, cmp_pe, cmp_w1, cmp_b1, cmp_w2, rel_bias):
    nb, seq, _ = x_prompt.shape
    nd = x_sample.shape[0]
    depth = norm_g.shape[0]
    n_pages = page_table.shape[1]
    past_len = n_pages * PAGE
    n_past = past_len // BLK
    n_win = state_win_kv.shape[2]
    sr = SAMPLE_ROWS

    xp = x_prompt.reshape(nb * seq, D_MODEL)
    xs = _pad_rows(x_sample.reshape(nd, D_MODEL), sr, 0)
    mem_x = mem_prompt.reshape(nb * MEM_LEN, D_MODEL)

    w_o = w_out.astype(BF16)
    w_mkv = w_mem_kv.astype(BF16)
    def zero_rows(like, rows):
        return jnp.zeros((like.shape[0], rows, D_MODEL), BF16)

    wt_gla = jnp.swapaxes(w_in_gla, 1, 2).astype(BF16)
    tail_gla = jnp.concatenate(
        [wt_gla[:, GLA_MAIN + GLA_RANK:], wt_gla[:, GLA_MAIN:GLA_MAIN + GLA_RANK],
         zero_rows(wt_gla, TN_PROJ - MEM_W - GLA_RANK)], axis=1)
    n_gate = 3 * NSA_HEADS
    wt_nsa = jnp.swapaxes(w_in_nsa, 1, 2).astype(BF16)
    tail_nsa = jnp.concatenate(
        [wt_nsa[:, NSA_GATE_COL:NSA_GATE_COL + n_gate], zero_rows(wt_nsa, TN_PROJ - MEM_W - n_gate),
         wt_nsa[:, NSA_GATE_COL + n_gate:]], axis=1)
    wa_pad = _pad_rows(w_gla_a2, LANE, 1)
    w1 = cmp_w1.astype(BF16)
    w2 = cmp_w2.astype(BF16)

    def ffn_both(x_p, x_s, i, j):
        g1, g2 = norm_g[i, 4 * j][None], norm_g[i, 4 * j + 1][None]
        x_s, wg, wu, wd = _ffn_half(x_s, g1, w_ffn_gate, w_ffn_up, w_ffn_down, g2, TM_SAMPLE, f32_weights_at=(i, j))
        return _ffn_half(x_p, g1, wg, wu, wd, g2, TM_FFN), x_s

    def every(a, step):
        return a.reshape(nd, step, a.shape[-1])[:, 0]

    def per_seq(a):
        return _pad_rows(a[:, None, :], sr, 1).reshape(nd * sr, a.shape[-1])

    outs = dict(gla_p=[], gla_s=[], cmp_p=[], cmp_s=[], slc_p=[], slc_s=[], win_p=[], win_s=[], mem_p=[])
    for i in range(depth):
        li = i // 2
        mem_kv_p = _norm_matmul(mem_x, mem_norm_g[i][None], w_mkv, i, TM_MEMKV, MEM_W)
        outs["mem_p"].append(mem_kv_p.reshape(nb, MEM_LEN, 2, N_MEM_HEADS, MEM_HEAD_DIM))
        mem_kv_p = mem_kv_p.reshape(nb, MEM_LEN, 2 * MEM_W)
        mem_kv_s = cache_mem_kv[i].reshape(nd, MEM_LEN, 2 * MEM_W)
        xp, xs = ffn_both(xp, xs, i, 0)
        g_mix = norm_g[i, 2][None]
        if i % 2 == 0:
            proj_p = _in_proj(xp, g_mix, wt_gla, GLA_MAIN, tail_gla, li, TM_PROJ)
            proj_s = _in_proj(xs, g_mix, wt_gla, GLA_MAIN, tail_gla, li, TM_SAMPLE)
            mem_col = GLA_MAIN // MEM_W
            b_a = b_gla_a[li][None]
            gn = gla_onorm_g[li][None]
            s0 = jnp.zeros((nb, GLA_HEADS, GLA_DK, GLA_DV), F32)
            tok_p, sp = _gla(proj_p, s0, wa_pad[li], b_a, gn, seq, TL_GLA, GLA_CHUNK, seq)
            tok_s, ss = _gla(per_seq(proj_s[:nd]), state_gla[li], wa_pad[li], b_a, gn, sr, sr, sr, 1)
            tok_s = _pad_rows(every(tok_s, sr), sr, 0)
            outs["gla_p"].append(sp)
            outs["gla_s"].append(ss)
        else:
            assert 2 * KV_W == TN_PROJ and NSA_KV_COL % TN_PROJ == 0
            kv_tiles = (NSA_KV_COL // TN_PROJ, NSA_GATE_COL // TN_PROJ)
            proj_p = _in_proj(xp, g_mix, wt_nsa, NSA_GATE_COL, tail_nsa, li, TM_PROJ, kv_tiles)
            proj_s = _in_proj(xs, g_mix, wt_nsa, NSA_GATE_COL, tail_nsa, li, TM_SAMPLE, kv_tiles)
            mem_col = NSA_MEM_COL // MEM_W
            b1 = cmp_b1[li][:, None, :]
            gate_b = nsa_gate_b[li]
            bct, bsn, bw = _prompt_bias_tables(rel_bias)
            proj3 = proj_p.reshape(nb, seq, NSA_N)
            cmp_kv = _cmp_prompt(proj3, cmp_pe[li], w1[li], b1, w2[li])
            tok_p = _nsa_prompt(proj_p, cmp_kv, _pad_rows(gate_b[None], LANE, 1), bct, bsn, bw, nb, seq)
            def kv_rows(branch, first_row):
                cols = proj3[:, first_row:, NSA_KV_COL + branch * TN_PROJ:NSA_KV_COL + (branch + 1) * TN_PROJ]
                return jnp.swapaxes(cols.reshape(nb, seq - first_row, NSA_G, 2, DH), 2, 3)

            outs["cmp_p"].append(kv_rows(0, 0).reshape(nb, seq // PAGE, PAGE, 2, NSA_G, DH))
            outs["slc_p"].append(kv_rows(1, 0).reshape(nb, seq // PAGE, PAGE, 2, NSA_G, DH))
            outs["win_p"].append(kv_rows(2, seq - n_win))
            t_pos = past_len
            bias_cd, bias_sd, bias_wd = _decode_bias_tables(rel_bias, t_pos, n_past, n_win)
            kv_s = jnp.swapaxes(proj_s[:nd, NSA_KV_COL:NSA_GATE_COL].reshape(nd, 3, NSA_G, 2, DH), 2, 3)
            outs["cmp_s"].append(kv_s[:, None, 0])
            outs["slc_s"].append(kv_s[:, None, 1])
            outs["win_s"].append(jnp.concatenate([state_win_kv[li][:, 1:], kv_s[:, None, 2]], axis=1))
            q8 = _pad_rows(proj_s[:nd, :TOK_W].reshape(nd, NSA_G, NSA_R, DH), 8, 2)
            new_cmp = _pad_rows(jnp.moveaxis(kv_s[:, 0], 1, 0).reshape(2, nd * NSA_G, DH), 32, 1)
            cmp_kv_s, cmp_last = _cmp_decode(page_table, _linear_cache(cache_cmp_kv[li]), new_cmp,
                                             cmp_pe[li], w1[li], b1, w2[li])
            o_c, sel = _sel_decode(q8, cmp_kv_s, cmp_last, bias_cd, t_pos)
            sel_flat = sel[:, :, 0, :TOPN].reshape(-1)

            def row8(a):
                return _pad_rows(a[:, :, None, :], 8, 2)

            gate_l = _pad_rows(_pad_rows(proj_s[:nd, NSA_GATE_COL:NSA_GATE_COL + n_gate].reshape(nd, NSA_G, NSA_R, 3),
                                         8, 2), LANE, 3)
            gate_b8 = _pad_rows(_pad_rows(gate_b.reshape(NSA_G, NSA_R, 3), 8, 1), LANE, 2)
            tok_s = _slc_decode(sel_flat, page_table, q8, o_c, gate_l, gate_b8,
                                _linear_cache(cache_slc_kv[li]), bias_sd,
                                row8(kv_s[:, 1, 0]), row8(kv_s[:, 1, 1]),
                                _linear_cache(state_win_kv[li]),
                                row8(kv_s[:, 2, 0]), row8(kv_s[:, 2, 1]), bias_wd, t_pos)
            tok_s = _pad_rows(tok_s[:, :, :NSA_R].reshape(nd, TOK_W), sr, 0).astype(BF16)
        mem_o_p = _mem_attn(proj_p, mem_col, mem_kv_p, seq, TM_MEM_ATTN)
        q_s = jnp.broadcast_to(proj_s[:nd, None, mem_col * MEM_W:(mem_col + 1) * MEM_W],
                               (nd, sr, MEM_W)).reshape(nd * sr, MEM_W)
        mem_o_s = _pad_rows(every(_mem_attn(q_s, 0, mem_kv_s, sr, sr), sr), sr, 0)
        xp = _out_proj(xp, tok_p, mem_o_p, w_o, i, norm_g[i, 3][None], TM_OUT)
        xs = _out_proj(xs, tok_s, mem_o_s, w_o, i, norm_g[i, 3][None], sr)
        xp, xs = ffn_both(xp, xs, i, 1)

    y_prompt = xp.reshape(nb, seq, D_MODEL)
    y_sample = xs[:nd].reshape(nd, 1, D_MODEL)
    st = lambda k: jnp.stack(outs[k])
    return (y_prompt, y_sample, st("gla_p"), st("cmp_p"), st("slc_p"), st("win_p"), st("mem_p"),
            st("gla_s"), st("cmp_s"), st("slc_s"), st("win_s"))
```

```python
import functools
import math

import jax
import jax.numpy as jnp
from jax import lax
from jax.experimental import pallas as pl
from jax.experimental.pallas import tpu as pltpu

F32 = jnp.float32
BF16 = jnp.bfloat16
HI = lax.Precision.HIGHEST

D_MODEL = 2048
D_FF = 5632
EPS = 1e-6
MEM_LEN = 256
N_MEM_HEADS = 4
MEM_HEAD_DIM = 128
MEM_W = N_MEM_HEADS * MEM_HEAD_DIM
TOK_W = D_MODEL - MEM_W
GLA_HEADS = 4
GLA_DV = TOK_W // GLA_HEADS
GLA_DK = GLA_DV // 2
GLA_RANK = 16
GLA_TAU = 16.0
GLA_CHUNK = 64
GLA_PAIR_W = 2 * GLA_DK
DH = 128
NSA_HEADS = TOK_W // DH
NSA_G = 3
NSA_R = NSA_HEADS // NSA_G
BLK = 64
TOPN = 16
WINDOW = 512
CMP_HID = 256
QBLK = 256
KV_W = NSA_G * DH
REL_BUCKETS = 32
REL_MAX_EXACT = 16
REL_MAX_DIST = 128
PAGE = 128
LANE = 128
HALF_LANE = LANE // 2
BLK_SHIFT = BLK.bit_length() - 1
VMEM_LIMIT = 56 * 1024 * 1024

TN_PROJ = 768
GLA_MAIN = 2 * GLA_HEADS * GLA_DK + 2 * TOK_W
GLA_A_COL = GLA_MAIN + MEM_W
GLA_N = GLA_MAIN + TN_PROJ
NSA_KV_COL = TOK_W
NSA_GATE_COL = NSA_KV_COL + 6 * KV_W
NSA_MEM_COL = NSA_GATE_COL + TN_PROJ - MEM_W
NSA_N = NSA_GATE_COL + TN_PROJ
SAMPLE_ROWS = 16
TM_SAMPLE = SAMPLE_ROWS
TM_FFN = 512
TM_PROJ = 1024
TM_MEMKV = 512
TM_OUT = 512
TM_MEM_ATTN = 512
TL_GLA = 512
SLC_PAD = 256
WIN_PAD = WINDOW
WIN_KEYS = WIN_PAD + QBLK
SLC_NEAR = SLC_PAD + QBLK
FAR_CHUNK = 1024
NEG = -1e30
M_FLOOR = -1e29


def _cparams(sem, vmem=VMEM_LIMIT):
    return pltpu.CompilerParams(dimension_semantics=sem, vmem_limit_bytes=vmem)


def _sigmoid(x):
    return 1.0 / (1.0 + jnp.exp(-x))


def _rms(x, g):
    ms = jnp.mean(x * x, axis=-1, keepdims=True)
    return x * lax.rsqrt(ms + EPS) * g


def _nt(a, b, precision=None):
    return lax.dot_general(a, b, (((1,), (1,)), ((), ())), precision=precision,
                           preferred_element_type=F32)


def _tn(a, b, precision=None):
    return lax.dot_general(a, b, (((0,), (0,)), ((), ())), precision=precision,
                           preferred_element_type=F32)


def _dot(a, b, precision=None):
    return jnp.dot(a, b, precision=precision, preferred_element_type=F32)


def _split3(x):
    hi = x.astype(BF16)
    r1 = x - hi.astype(F32)
    mid = r1.astype(BF16)
    lo = (r1 - mid.astype(F32)).astype(BF16)
    return hi, mid, lo


def _ffn_kernel(x_ref, g1_ref, wg_ref, wu_ref, wd_ref, g2_ref, o_ref, *rest):
    bf_out, (xn_ref, acc_ref) = rest[:-2], rest[-2:]
    j = pl.program_id(1)

    @pl.when(j == 0)
    def _():
        xn_ref[...] = _rms(x_ref[...], g1_ref[...]).astype(BF16)
        acc_ref[...] = jnp.zeros_like(acc_ref)

    wg, wu, wd = (w_ref[...].astype(BF16) for w_ref in (wg_ref, wu_ref, wd_ref))
    for out_ref, w in zip(bf_out, (wg, wu, wd)):
        out_ref[...] = w
    xn = xn_ref[...]
    gate = _dot(xn, wg)
    up = _dot(xn, wu)
    h = (gate * _sigmoid(gate) * up).astype(BF16)
    acc_ref[...] += _dot(h, wd)

    @pl.when(j == pl.num_programs(1) - 1)
    def _():
        o_ref[...] = x_ref[...] + 0.5 * _rms(acc_ref[...], g2_ref[...])


def _ffn_half(x, g1, wg, wu, wd, g2, tm, tf=512, f32_weights_at=None):
    m = x.shape[0]
    y_shape = jax.ShapeDtypeStruct((m, D_MODEL), F32)
    y_spec = pl.BlockSpec((tm, D_MODEL), lambda i, j: (i, 0))
    col_spec = pl.BlockSpec((D_MODEL, tf), lambda i, j: (0, j))
    row_spec = pl.BlockSpec((tf, D_MODEL), lambda i, j: (j, 0))
    if f32_weights_at is None:
        w_specs = [col_spec, col_spec, row_spec]
        out_shape, out_specs = y_shape, y_spec
    else:
        assert m == tm, "the bf16 copies are written once, by a single row tile"
        layer, half = f32_weights_at
        w_specs = [pl.BlockSpec((None, None, D_MODEL, tf), lambda i, j: (layer, half, 0, j)),
                   pl.BlockSpec((None, None, D_MODEL, tf), lambda i, j: (layer, half, 0, j)),
                   pl.BlockSpec((None, None, tf, D_MODEL), lambda i, j: (layer, half, j, 0))]
        out_shape = (y_shape, jax.ShapeDtypeStruct((D_MODEL, D_FF), BF16),
                     jax.ShapeDtypeStruct((D_MODEL, D_FF), BF16), jax.ShapeDtypeStruct((D_FF, D_MODEL), BF16))
        out_specs = (y_spec, col_spec, col_spec, row_spec)
    return pl.pallas_call(
        _ffn_kernel,
        out_shape=out_shape,
        grid=(m // tm, D_FF // tf),
        in_specs=[y_spec, pl.BlockSpec((1, D_MODEL), lambda i, j: (0, 0))] + w_specs
                 + [pl.BlockSpec((1, D_MODEL), lambda i, j: (0, 0))],
        out_specs=out_specs,
        scratch_shapes=[pltpu.VMEM((tm, D_MODEL), BF16), pltpu.VMEM((tm, D_MODEL), F32)],
        compiler_params=_cparams(("parallel", "arbitrary")),
        name="ffn_half",
    )(x, g1, wg, wu, wd, g2)


def _norm_matmul_kernel(x_ref, g_ref, w_ref, o_ref, xn_ref):
    @pl.when(pl.program_id(1) == 0)
    def _():
        xn_ref[...] = _rms(x_ref[...], g_ref[...]).astype(BF16)

    o_ref[...] = _dot(xn_ref[...], w_ref[...])


def _norm_matmul(x, g, w, layer, tm, tn):
    m, n = x.shape[0], w.shape[2]
    return pl.pallas_call(
        _norm_matmul_kernel,
        out_shape=jax.ShapeDtypeStruct((m, n), F32),
        grid=(m // tm, n // tn),
        in_specs=[
            pl.BlockSpec((tm, D_MODEL), lambda i, j: (i, 0)),
            pl.BlockSpec((1, D_MODEL), lambda i, j: (0, 0)),
            pl.BlockSpec((None, D_MODEL, tn), lambda i, j: (layer, 0, j)),
        ],
        out_specs=pl.BlockSpec((tm, tn), lambda i, j: (i, j)),
        scratch_shapes=[pltpu.VMEM((tm, D_MODEL), BF16)],
        compiler_params=_cparams(("parallel", "arbitrary")),
        name="norm_matmul",
    )(x, g, w)


def _in_proj_kernel(x_ref, g_ref, wm_ref, wt_ref, o_ref, xn_ref, *, n_main, kv_tiles):
    j = pl.program_id(1)
    kv_lo, kv_hi = kv_tiles
    is_kv = (j >= kv_lo) & (j < kv_hi)

    @pl.when(j == 0)
    def _():
        xn_ref[...] = _rms(x_ref[...], g_ref[...]).astype(BF16)

    @pl.when((j < n_main) & jnp.logical_not(is_kv))
    def _():
        o_ref[...] = _nt(xn_ref[...], wm_ref[...])

    @pl.when(is_kv)
    def _():
        res = _nt(xn_ref[...], wm_ref[...])
        for g in range(NSA_G):
            for kv in range(2):
                src, dst = kv * NSA_G + g, g * 2 + kv
                o_ref[:, dst * DH:(dst + 1) * DH] = res[:, src * DH:(src + 1) * DH]

    @pl.when(j == n_main)
    def _():
        o_ref[...] = _nt(xn_ref[...], wt_ref[...])


def _in_proj(x, g, w_main, main_cols, w_tail, layer, tm, kv_tiles=(0, 0)):
    m = x.shape[0]
    n_main = main_cols // TN_PROJ
    return pl.pallas_call(
        functools.partial(_in_proj_kernel, n_main=n_main, kv_tiles=kv_tiles),
        out_shape=jax.ShapeDtypeStruct((m, main_cols + TN_PROJ), F32),
        grid=(m // tm, n_main + 1),
        in_specs=[
            pl.BlockSpec((tm, D_MODEL), lambda i, j: (i, 0)),
            pl.BlockSpec((1, D_MODEL), lambda i, j: (0, 0)),
            pl.BlockSpec((None, TN_PROJ, D_MODEL), lambda i, j: (layer, jnp.minimum(j, n_main - 1), 0)),
            pl.BlockSpec((None, TN_PROJ, D_MODEL), lambda i, j: (layer, 0, 0)),
        ],
        out_specs=pl.BlockSpec((tm, TN_PROJ), lambda i, j: (i, j)),
        scratch_shapes=[pltpu.VMEM((tm, D_MODEL), BF16)],
        compiler_params=_cparams(("parallel", "arbitrary")),
        name="in_proj",
    )(x, g, w_main, w_tail)


def _out_proj_kernel(x_ref, tok_ref, mem_ref, wt_ref, wm_ref, g_ref, o_ref):
    y = _dot(tok_ref[...], wt_ref[...]) + _dot(mem_ref[...], wm_ref[...])
    o_ref[...] = x_ref[...] + _rms(y, g_ref[...])


def _out_proj(x, tok, mem_o, w_o, layer, g, tm):
    m = x.shape[0]
    return pl.pallas_call(
        _out_proj_kernel,
        out_shape=jax.ShapeDtypeStruct((m, D_MODEL), F32),
        grid=(m // tm,),
        in_specs=[
            pl.BlockSpec((tm, D_MODEL), lambda i: (i, 0)),
            pl.BlockSpec((tm, TOK_W), lambda i: (i, 0)),
            pl.BlockSpec((tm, MEM_W), lambda i: (i, 0)),
            pl.BlockSpec((None, TOK_W, D_MODEL), lambda i: (layer, 0, 0), pipeline_mode=pl.Buffered(1)),
            pl.BlockSpec((None, MEM_W, D_MODEL), lambda i: (layer, TOK_W // MEM_W, 0), pipeline_mode=pl.Buffered(1)),
            pl.BlockSpec((1, D_MODEL), lambda i: (0, 0)),
        ],
        out_specs=pl.BlockSpec((tm, D_MODEL), lambda i: (i, 0)),
        compiler_params=_cparams(("parallel",)),
        name="out_proj",
    )(x, tok, mem_o, w_o, w_o, g)


def _mem_attn_kernel(q_ref, kv_ref, o_ref):
    for h in range(N_MEM_HEADS):
        q = (q_ref[:, h * DH:(h + 1) * DH] * (MEM_HEAD_DIM ** -0.5)).astype(BF16)
        k = kv_ref[:, h * DH:(h + 1) * DH].astype(BF16)
        v = kv_ref[:, MEM_W + h * DH:MEM_W + (h + 1) * DH].astype(BF16)
        s = _nt(q, k)
        e = jnp.exp(s - jnp.max(s, axis=-1, keepdims=True))
        p = e / jnp.sum(e, axis=-1, keepdims=True)
        o_ref[:, h * DH:(h + 1) * DH] = _dot(p.astype(BF16), v).astype(o_ref.dtype)


def _mem_attn(q_arr, q_col_block, mem_kv, rows_per_batch, tm):
    nb = mem_kv.shape[0]
    per = rows_per_batch // tm
    return pl.pallas_call(
        _mem_attn_kernel,
        out_shape=jax.ShapeDtypeStruct((nb * rows_per_batch, MEM_W), BF16),
        grid=(nb, per),
        in_specs=[
            pl.BlockSpec((tm, MEM_W), lambda b, i: (b * per + i, q_col_block)),
            pl.BlockSpec((None, MEM_LEN, 2 * MEM_W), lambda b, i: (b, 0, 0)),
        ],
        out_specs=pl.BlockSpec((tm, MEM_W), lambda b, i: (b * per + i, 0)),
        compiler_params=_cparams(("parallel", "parallel")),
        name="mem_attn",
    )(q_arr, mem_kv)


GLA_PAIRS = GLA_HEADS // 2


def _gla_kernel(q_ref, k_ref, v_ref, r_ref, a_ref, s0_ref, wa_ref, ba_ref, gn_ref,
                tok_ref, s_out_ref, s_ref, cum_ref, *, chunk, n_valid):
    l = pl.program_id(1)
    tl = q_ref.shape[0]

    @pl.when(l == 0)
    def _():
        for p in range(GLA_PAIRS):
            s_ref[p] = s0_ref[2 * p:2 * p + 2].reshape(GLA_PAIR_W, GLA_DV).T

    lane = lax.broadcasted_iota(jnp.int32, (1, GLA_PAIR_W), 1)
    head_mask = [(lane < GLA_DK).astype(F32), (lane >= GLA_DK).astype(F32)]
    ti = lax.broadcasted_iota(jnp.int32, (chunk, chunk), 0)
    si = lax.broadcasted_iota(jnp.int32, (chunk, chunk), 1)
    causal = si <= ti
    tri = jnp.where(causal, 1.0, 0.0).astype(BF16)

    a_hi, a_lo, _ = _split3(a_ref[...])
    w_hi, w_lo, _ = _split3(wa_ref[...])
    z = _dot(a_hi, w_hi) + _dot(a_lo, w_hi) + _dot(a_hi, w_lo) + ba_ref[...]
    la_all = -(jnp.maximum(-z, 0.0) + jnp.log1p(jnp.exp(-jnp.abs(z)))) / GLA_TAU
    pos = l * tl + lax.broadcasted_iota(jnp.int32, (tl, 1), 0)
    la_all = jnp.where(pos < n_valid, la_all, 0.0)
    pieces = _split3(la_all)
    for c0 in range(0, tl, chunk):
        cum_ref[c0:c0 + chunk, :] = functools.reduce(jnp.add, [_dot(tri, pc[c0:c0 + chunk]) for pc in pieces])

    def step(ci, carry):
        r0 = pl.multiple_of(ci * chunk, chunk)
        rows = pl.ds(r0, chunk)
        for p in range(GLA_PAIRS):
            pair = slice(p * GLA_PAIR_W, (p + 1) * GLA_PAIR_W)
            b = cum_ref[rows, pair]
            bl = b[chunk - 1:chunk, :]
            q = q_ref[rows, pair] * (GLA_DK ** -0.5)
            k = k_ref[rows, pair]
            qe = q * jnp.exp(b)
            ke = (k * jnp.exp(-b)).astype(BF16)
            kd = k * jnp.exp(bl - b)
            st_old = s_ref[p]
            st_bf = st_old.astype(BF16)
            upd = None
            for h in range(2):
                head = slice((2 * p + h) * GLA_DV, (2 * p + h + 1) * GLA_DV)
                v = v_ref[rows, head].astype(BF16)
                qm = (qe * head_mask[h]).astype(BF16)
                att = jnp.where(causal, _nt(qm, ke), 0.0)
                o = _nt(qm, st_bf) + _dot(att.astype(BF16), v)
                o = _rms(o, gn_ref[...])
                r = r_ref[rows, head]
                tok_ref[rows, head] = (o * (r * _sigmoid(r))).astype(tok_ref.dtype)
                u = _tn(v, (kd * head_mask[h]).astype(BF16))
                upd = u if upd is None else upd + u
            s_ref[p] = jnp.exp(bl) * st_old + upd
        return carry

    lax.fori_loop(0, tl // chunk, step, 0)

    @pl.when(l == pl.num_programs(1) - 1)
    def _():
        for p in range(GLA_PAIRS):
            s_out_ref[2 * p:2 * p + 2] = s_ref[p].T.reshape(2, GLA_DK, GLA_DV)


def _gla(proj, s0, wa_pad, b_a, gn, seq, tl, chunk, n_valid):
    nb = s0.shape[0]
    per = seq // tl
    qk_w = GLA_HEADS * GLA_DK
    row = lambda b, l: b * per + l

    return pl.pallas_call(
        functools.partial(_gla_kernel, chunk=chunk, n_valid=n_valid),
        out_shape=(jax.ShapeDtypeStruct((nb * seq, TOK_W), BF16),
                   jax.ShapeDtypeStruct((nb, GLA_HEADS, GLA_DK, GLA_DV), F32)),
        grid=(nb, per),
        in_specs=[
            pl.BlockSpec((tl, qk_w), lambda b, l: (row(b, l), 0)),
            pl.BlockSpec((tl, qk_w), lambda b, l: (row(b, l), 1)),
            pl.BlockSpec((tl, TOK_W), lambda b, l: (row(b, l), 2 * qk_w // TOK_W)),
            pl.BlockSpec((tl, TOK_W), lambda b, l: (row(b, l), 2 * qk_w // TOK_W + 1)),
            pl.BlockSpec((tl, LANE), lambda b, l: (row(b, l), GLA_A_COL // LANE)),
            pl.BlockSpec((None, GLA_HEADS, GLA_DK, GLA_DV), lambda b, l: (b, 0, 0, 0)),
            pl.BlockSpec((LANE, qk_w), lambda b, l: (0, 0)),
            pl.BlockSpec((1, qk_w), lambda b, l: (0, 0)),
            pl.BlockSpec((1, GLA_DV), lambda b, l: (0, 0)),
        ],
        out_specs=(pl.BlockSpec((tl, TOK_W), lambda b, l: (row(b, l), 0)),
                   pl.BlockSpec((None, GLA_HEADS, GLA_DK, GLA_DV), lambda b, l: (b, 0, 0, 0))),
        scratch_shapes=[pltpu.VMEM((GLA_PAIRS, GLA_DV, GLA_PAIR_W), F32), pltpu.VMEM((tl, qk_w), F32)],
        compiler_params=_cparams(("parallel", "arbitrary")),
        name="gla",
    )(proj, proj, proj, proj, proj, s0, wa_pad, b_a, gn)


def _masked_softmax(s, valid):
    s = jnp.where(valid, s, -jnp.inf)
    m = jnp.max(s, axis=-1, keepdims=True)
    m = jnp.where(m > -jnp.inf, m, 0.0)
    e = jnp.exp(s - m)
    return e / jnp.maximum(jnp.sum(e, axis=-1, keepdims=True), 1e-30)


def _online_update(state, s, v):
    m, l, acc = state
    r, nq, w = s.shape
    m_new = jnp.maximum(m, jnp.max(s, axis=-1, keepdims=True))
    alpha = jnp.exp(m - m_new)
    p = jnp.exp(s - m_new)
    l = alpha * l + jnp.sum(p, axis=-1, keepdims=True)
    pv = _dot(p.reshape(r * nq, w).astype(BF16), v).reshape(r, nq, DH)
    return m_new, l, alpha * acc + pv


def _compress_tail(xflat, w1, b1, w2):
    h = _dot(xflat, w1) + b1
    h = h * _sigmoid(h)
    return _dot(h.astype(BF16), w2)


def _cmp_prompt_kernel(x0_ref, x1_ref, x2_ref, pe_ref, w1_ref, b1_ref, w2_ref, o_ref, xflat_ref):
    x_refs = (x0_ref, x1_ref, x2_ref)
    nb, seq = x0_ref.shape[0], x0_ref.shape[1]
    nblk = seq // BLK
    for j in range(BLK):
        pe_j = pe_ref[j:j + 1, :]
        for b in range(nb):
            for g in range(NSA_G):
                xj = x_refs[g][b, pl.ds(j, nblk, stride=BLK), :]
                row = (b * NSA_G + g) * nblk
                xflat_ref[row:row + nblk, j * DH:(j + 1) * DH] = (xj + pe_j).astype(BF16)
    out = _compress_tail(xflat_ref[...], w1_ref[...], b1_ref[...], w2_ref[...])
    o_ref[...] = out.reshape(nb, NSA_G, nblk, DH)


def _kv_block(branch, g, kv):
    return NSA_KV_COL // DH + branch * 2 * NSA_G + g * 2 + kv


def _cmp_prompt(proj3, pe, w1, b1, w2):
    nb, seq, _ = proj3.shape
    nblk = seq // BLK
    return pl.pallas_call(
        _cmp_prompt_kernel,
        out_shape=jax.ShapeDtypeStruct((2, nb, NSA_G, nblk, DH), F32),
        grid=(2,),
        in_specs=[
            pl.BlockSpec((nb, seq, DH), lambda kv: (0, 0, _kv_block(0, 0, kv))),
            pl.BlockSpec((nb, seq, DH), lambda kv: (0, 0, _kv_block(0, 1, kv))),
            pl.BlockSpec((nb, seq, DH), lambda kv: (0, 0, _kv_block(0, 2, kv))),
            pl.BlockSpec((None, BLK, DH), lambda kv: (kv, 0, 0)),
            pl.BlockSpec((None, BLK * DH, CMP_HID), lambda kv: (kv, 0, 0)),
            pl.BlockSpec((None, 1, CMP_HID), lambda kv: (kv, 0, 0)),
            pl.BlockSpec((None, CMP_HID, DH), lambda kv: (kv, 0, 0)),
        ],
        out_specs=pl.BlockSpec((None, nb, NSA_G, nblk, DH), lambda kv: (kv, 0, 0, 0, 0)),
        scratch_shapes=[pltpu.VMEM((nb * NSA_G * nblk, BLK * DH), BF16)],
        compiler_params=_cparams(("arbitrary",)),
        name="cmp_prompt",
    )(proj3, proj3, proj3, pe, w1, b1, w2)


def _nsa_prompt_kernel(q_ref, gl_ref, gb_ref, kc_ref, vc_ref, ks_ref, vs_ref, kw_ref, vw_ref,
                       bct_ref, bsn_ref, bw_ref, o_ref, ksb, vsb, kwb, vwb):
    g = pl.program_id(1)
    qi = pl.program_id(2)
    seq = ks_ref.shape[0]
    nblk = seq // BLK
    R = NSA_R

    @pl.when(qi == 0)
    def _():
        vsb[0:SLC_PAD, :] = jnp.zeros((SLC_PAD, DH), BF16)
        vsb[SLC_PAD:SLC_PAD + seq, :] = vs_ref[...].astype(BF16)
        ones_col = jnp.where(lax.broadcasted_iota(jnp.int32, (1, DH), 1) == 0, 1.0, 0.0)
        vwb[0:WIN_PAD, 0:DH] = jnp.zeros((WIN_PAD, DH), BF16)
        vwb[WIN_PAD:WIN_PAD + seq, 0:DH] = vw_ref[...].astype(BF16)
        vwb[:, DH:2 * DH] = jnp.broadcast_to(ones_col, (WIN_PAD + seq, DH)).astype(BF16)
        for src, dst, pad, per_block in ((ks_ref, ksb, SLC_PAD, True), (kw_ref, kwb, WIN_PAD, False)):
            dst[0:pad, 0:DH] = jnp.zeros((pad, DH), BF16)
            dst[pad:pad + seq, 0:DH] = src[...].astype(BF16)
            pos = lax.broadcasted_iota(jnp.int32, (pad + seq, 1), 0) - pad
            feat = lax.broadcasted_iota(jnp.int32, (1, DH), 1)
            masked = (feat == HALF_LANE) & (pos < 0)
            if per_block:
                masked = masked | ((pos >= 0) & (lax.shift_right_arithmetic(pos, BLK_SHIFT) == feat))
            dst[:, DH:2 * DH] = jnp.where(masked, NEG, 0.0).astype(BF16)

    q = q_ref[...] * (DH ** -0.5)
    q_all = jnp.concatenate([q[:, r * DH:(r + 1) * DH] for r in range(R)], axis=0)
    q_bf = q_all.astype(BF16)
    i_col = lax.broadcasted_iota(jnp.int32, (QBLK, 1), 0)
    t_col = qi * QBLK + i_col
    n_row = lax.broadcasted_iota(jnp.int32, (1, nblk), 1)
    row0 = pl.multiple_of(qi * QBLK, QBLK)
    lane2 = lax.broadcasted_iota(jnp.int32, (1, LANE), 1)

    s_c = _nt(q_all, kc_ref[...], HI).reshape(R, QBLK, nblk)
    tb_col = lax.shift_right_arithmetic(t_col, BLK_SHIFT)
    rel = tb_col - n_row
    bias_c = []
    for r in range(R):
        tab = bct_ref[r]
        bias_c.append(jnp.where(rel == 0, tab[:, 0:1],
                      jnp.where(rel == 1, tab[:, 1:2],
                      jnp.where(rel == 2, tab[:, 2:3], tab[:, 3:4]))))
    s_c = s_c + jnp.stack(bias_c, axis=0)
    valid_c = (t_col - (n_row * BLK + (BLK - 1))) >= 0
    p_c = _masked_softmax(s_c, valid_c[None])
    o_c = _dot(p_c.reshape(R * QBLK, nblk).astype(BF16), vc_ref[...].astype(BF16))

    imp = jnp.sum(p_c, axis=0)
    forced = (n_row == 0) | (n_row == tb_col) | (n_row == tb_col - 1)
    future = n_row * BLK > t_col
    score = jnp.where(forced, jnp.inf, jnp.where(future, -jnp.inf, imp))
    score_t = jnp.concatenate([score, jnp.full((QBLK, LANE - nblk), -jnp.inf, F32)], axis=1).T[:HALF_LANE]
    n_sub = lax.broadcasted_iota(jnp.int32, (HALF_LANE, 1), 0)
    rank_t = jnp.zeros((HALF_LANE, QBLK), F32)
    for i in range(nblk):
        cand = score_t[i:i + 1, :]
        wins_tie = jnp.where(n_sub > i, 1.0, 0.0)
        rank_t = rank_t + jnp.where(cand > score_t, 1.0, jnp.where(cand == score_t, wins_tie, 0.0))
    rank = jnp.concatenate([rank_t, jnp.zeros((LANE - HALF_LANE, QBLK), F32)], axis=0).T
    left = lane2 < HALF_LANE
    n_far = (qi * QBLK - SLC_PAD) // BLK
    dropped = rank >= float(min(TOPN, nblk))
    flag_all = jnp.where(left, jnp.where(dropped, 1.0, 0.0), jnp.where(lane2 == HALF_LANE, 1.0, 0.0))
    flag_far = jnp.where(left & (lane2 >= n_far), 1.0, flag_all)
    q_far = jnp.concatenate([q_bf, jnp.concatenate([flag_far.astype(BF16)] * R, axis=0)], axis=1)
    q_near = jnp.concatenate([q_bf, jnp.concatenate([flag_all.astype(BF16)] * R, axis=0)], axis=1)

    def far_body(kc_i, state):
        start = pl.multiple_of(SLC_PAD + kc_i * FAR_CHUNK, BLK)
        s = _nt(q_far, ksb[pl.ds(start, FAR_CHUNK), :]).reshape(R, QBLK, FAR_CHUNK)
        return _online_update(state, s, vsb[pl.ds(start, FAR_CHUNK), :])

    blk_per_chunk = FAR_CHUNK // BLK
    n_chunks = (jnp.maximum(n_far, 0) + (blk_per_chunk - 1)) // blk_per_chunk
    state = (jnp.full((R, QBLK, 1), M_FLOOR, F32), jnp.zeros((R, QBLK, 1), F32), jnp.zeros((R, QBLK, DH), F32))
    state = lax.fori_loop(0, n_chunks, far_body, state)

    s = _nt(q_near, ksb[pl.ds(row0, SLC_NEAR), :]).reshape(R, QBLK, SLC_NEAR) + bsn_ref[...]
    _, l_s, acc_s = _online_update(state, s, vsb[pl.ds(row0, SLC_NEAR), :])
    o_s = (acc_s / jnp.maximum(l_s, 1e-30)).reshape(R * QBLK, DH)

    s = _nt(q_near, kwb[pl.ds(row0, WIN_KEYS), :]).reshape(R, QBLK, WIN_KEYS) + bw_ref[...]
    e_w = jnp.exp(s - jnp.max(s, axis=-1, keepdims=True))
    ow = _dot(e_w.reshape(R * QBLK, WIN_KEYS).astype(BF16), vwb[pl.ds(row0, WIN_KEYS), :])
    o_w = ow[:, :DH] / jnp.maximum(ow[:, DH:DH + 1], 1e-30)

    gates = _sigmoid(gl_ref[...] + gb_ref[...])
    src = lax.broadcasted_iota(jnp.int32, (LANE, LANE), 0)
    dst = lax.broadcasted_iota(jnp.int32, (LANE, LANE), 1)
    pick = jnp.where((src == g * (3 * R) + dst) & (dst < 3 * R), 1.0, 0.0)
    gsel = _dot(gates, pick, HI)
    for r in range(R):
        rows = slice(r * QBLK, (r + 1) * QBLK)
        o = (gsel[:, 3 * r:3 * r + 1] * o_c[rows] + gsel[:, 3 * r + 1:3 * r + 2] * o_s[rows]
             + gsel[:, 3 * r + 2:3 * r + 3] * o_w[rows])
        o_ref[:, r * DH:(r + 1) * DH] = o.astype(o_ref.dtype)


def _nsa_prompt(proj2, cmp_kv, gate_b_pad, bct, bsn, bw, nb, seq):
    nq = seq // QBLK
    nblk = seq // BLK
    assert nblk <= HALF_LANE, "mask features of the selected branch hold at most 64 key blocks"
    assert seq % FAR_CHUNK == 0
    proj3 = proj2.reshape(nb, seq, NSA_N)

    def kv_spec(branch, kv):
        return pl.BlockSpec((None, seq, DH), lambda b, g, qi: (b, 0, _kv_block(branch, g, kv)))

    return pl.pallas_call(
        _nsa_prompt_kernel,
        out_shape=jax.ShapeDtypeStruct((nb * seq, TOK_W), BF16),
        grid=(nb, NSA_G, nq),
        in_specs=[
            pl.BlockSpec((QBLK, NSA_R * DH), lambda b, g, qi: (b * nq + qi, g)),
            pl.BlockSpec((QBLK, LANE), lambda b, g, qi: (b * nq + qi, NSA_GATE_COL // LANE)),
            pl.BlockSpec((1, LANE), lambda b, g, qi: (0, 0)),
            pl.BlockSpec((None, None, None, nblk, DH), lambda b, g, qi: (0, b, g, 0, 0)),
            pl.BlockSpec((None, None, None, nblk, DH), lambda b, g, qi: (1, b, g, 0, 0)),
            kv_spec(1, 0), kv_spec(1, 1), kv_spec(2, 0), kv_spec(2, 1),
            pl.BlockSpec((None, NSA_R, QBLK, LANE), lambda b, g, qi: (g, 0, 0, 0)),
            pl.BlockSpec((None, NSA_R, QBLK, SLC_NEAR), lambda b, g, qi: (g, 0, 0, 0)),
            pl.BlockSpec((None, NSA_R, QBLK, WIN_KEYS), lambda b, g, qi: (g, 0, 0, 0)),
        ],
        out_specs=pl.BlockSpec((QBLK, NSA_R * DH), lambda b, g, qi: (b * nq + qi, g)),
        scratch_shapes=[pltpu.VMEM((SLC_PAD + seq, 2 * DH), BF16), pltpu.VMEM((SLC_PAD + seq, DH), BF16),
                        pltpu.VMEM((WIN_PAD + seq, 2 * DH), BF16), pltpu.VMEM((WIN_PAD + seq, 2 * DH), BF16)],
        compiler_params=_cparams(("parallel", "parallel", "arbitrary")),
        name="nsa_prompt",
    )(proj2, proj2, gate_b_pad, cmp_kv, cmp_kv, proj3, proj3, proj3, proj3, bct, bsn, bw)


SUB_PAGES = 8
RING = 4
ROW_W = 2 * NSA_G
PAGE_ROWS = PAGE * ROW_W


def _linear_cache(cache):
    return jnp.transpose(cache, (0, 1, 3, 2, 4)).reshape(-1, DH)


def _cmp_decode_kernel(pt_ref, cache_ref, new_ref, pe_ref, w1_ref, b1_ref, w2_ref, o_ref, last_ref,
                       buf, sem, xflat_ref, xlast_ref, *, n_sub):
    b = pl.program_id(0)
    total = pl.num_programs(0) * n_sub
    sub_blk = SUB_PAGES * PAGE // BLK
    seq_blk = n_sub * sub_blk

    def page_copy(s, p):
        page = pt_ref[s // n_sub, (s % n_sub) * SUB_PAGES + p]
        return pltpu.make_async_copy(cache_ref.at[pl.ds(pl.multiple_of(page * PAGE_ROWS, PAGE_ROWS), PAGE_ROWS), :],
                                     buf.at[s % RING, pl.ds(p * PAGE_ROWS, PAGE_ROWS), :], sem.at[s % RING])

    def start_sub(s):
        for p in range(SUB_PAGES):
            page_copy(s, p).start()

    @pl.when(b == 0)
    def _():
        for s in range(RING):
            start_sub(s)

    def body(i, c):
        s = b * n_sub + i
        for p in range(SUB_PAGES):
            page_copy(s, p).wait()
        slot = s % RING
        row0 = pl.multiple_of(i * sub_blk, sub_blk)
        by_row = jnp.swapaxes(buf[slot].reshape(sub_blk, BLK * ROW_W, DH), 0, 1)
        for kv in range(2):
            for j in range(BLK):
                pe_j = pe_ref[kv, j:j + 1, :]
                for g in range(NSA_G):
                    xj = by_row[j * ROW_W + g * 2 + kv]
                    xflat_ref[kv, pl.ds(g * seq_blk + row0, sub_blk), j * DH:(j + 1) * DH] = (xj + pe_j).astype(BF16)

        @pl.when(s + RING < total)
        def _():
            start_sub(s + RING)

        return c

    lax.fori_loop(0, n_sub, body, 0)
    for kv in range(2):
        out = _compress_tail(xflat_ref[kv], w1_ref[kv], b1_ref[kv], w2_ref[kv])
        o_ref[kv] = out.reshape(NSA_G, seq_blk, DH)

    @pl.when(b == 0)
    def _():
        rows = new_ref.shape[1]
        for kv in range(2):
            for j in range(BLK):
                pe_j = jnp.broadcast_to(pe_ref[kv, j:j + 1, :], (rows, DH))
                xj = new_ref[kv] + pe_j if j == 0 else pe_j
                xlast_ref[:, j * DH:(j + 1) * DH] = xj.astype(BF16)
            last_ref[kv] = _compress_tail(xlast_ref[...], w1_ref[kv], b1_ref[kv], w2_ref[kv])


def _cmp_decode(page_table, cache, new_rows, pe, w1, b1, w2):
    nd, n_pages = page_table.shape
    n_sub = n_pages // SUB_PAGES
    seq_blk = n_pages * PAGE // BLK
    rows = new_rows.shape[1]

    def whole(shape):
        return pl.BlockSpec(shape, lambda b, pt: (0,) * len(shape), pipeline_mode=pl.Buffered(1))

    grid_spec = pltpu.PrefetchScalarGridSpec(
        num_scalar_prefetch=1,
        grid=(nd,),
        in_specs=[
            pl.BlockSpec(memory_space=pl.ANY),
            whole(new_rows.shape), whole(pe.shape), whole(w1.shape), whole(b1.shape), whole(w2.shape),
        ],
        out_specs=(pl.BlockSpec((2, None, NSA_G, seq_blk, DH), lambda b, pt: (0, b, 0, 0, 0)),
                   pl.BlockSpec((2, rows, DH), lambda b, pt: (0, 0, 0))),
        scratch_shapes=[pltpu.VMEM((RING, SUB_PAGES * PAGE_ROWS, DH), F32),
                        pltpu.SemaphoreType.DMA((RING,)),
                        pltpu.VMEM((2, NSA_G * seq_blk, BLK * DH), BF16),
                        pltpu.VMEM((rows, BLK * DH), BF16)],
    )
    return pl.pallas_call(
        functools.partial(_cmp_decode_kernel, n_sub=n_sub),
        out_shape=(jax.ShapeDtypeStruct((2, nd, NSA_G, seq_blk, DH), F32),
                   jax.ShapeDtypeStruct((2, rows, DH), F32)),
        grid_spec=grid_spec,
        compiler_params=_cparams(("arbitrary",)),
        name="cmp_decode",
    )(page_table, cache, new_rows, pe, w1, b1, w2)


SEL_LANES = 384


def _sel_decode_kernel(q_ref, kc_ref, vc_ref, last_ref, bias_ref, oc_ref, sel_ref, *, t_pos):
    b = pl.program_id(0)
    n_past = kc_ref.shape[1]
    n_blocks = n_past + 1
    n_lane = lax.broadcasted_iota(jnp.int32, (1, SEL_LANES), 1)
    n_lane_f = n_lane.astype(F32)
    head_row = lax.broadcasted_iota(jnp.int32, (8, 1), 0) < NSA_R
    tb = t_pos // BLK
    for g in range(NSA_G):
        q = q_ref[g] * (DH ** -0.5)
        bias = bias_ref[g]
        s_p = _nt(q, kc_ref[g], HI) + bias[:, :n_past]
        row = b * NSA_G + g
        k_last = last_ref[0, pl.ds(row, 1), :]
        v_last = last_ref[1, pl.ds(row, 1), :]
        s_l = jnp.sum(q * k_last, axis=-1, keepdims=True) + bias[:, n_past:n_past + 1]
        valid_p = (t_pos - (n_lane[:, :n_past] * BLK + (BLK - 1))) >= 0
        valid_l = (t_pos - (n_past * BLK + (BLK - 1))) >= 0
        s_p = jnp.where(valid_p, s_p, -jnp.inf)
        s_l = jnp.where(valid_l, s_l, -jnp.inf)
        m = jnp.maximum(jnp.max(s_p, axis=-1, keepdims=True), s_l)
        m = jnp.where(m > -jnp.inf, m, 0.0)
        e_p = jnp.exp(s_p - m)
        e_l = jnp.exp(s_l - m)
        den = jnp.maximum(jnp.sum(e_p, axis=-1, keepdims=True) + e_l, 1e-30)
        p_p = e_p / den
        p_l = e_l / den
        oc_ref[g] = _dot(p_p, vc_ref[g], HI) + p_l * v_last
        imp_p = jnp.sum(jnp.where(head_row, p_p, 0.0), axis=0, keepdims=True)
        imp_l = jnp.sum(jnp.where(head_row, p_l, 0.0), axis=0, keepdims=True)
        imp = jnp.concatenate([imp_p, jnp.broadcast_to(imp_l, (1, SEL_LANES - n_past))], axis=1)
        forced = (n_lane == 0) | (n_lane == tb) | (n_lane == tb - 1)
        future = n_lane * BLK > t_pos
        score = jnp.where(forced, jnp.inf, jnp.where(future, -jnp.inf, imp))
        cand = n_lane < n_blocks
        sel = jnp.zeros((1, LANE), jnp.int32)
        k_lane = lax.broadcasted_iota(jnp.int32, (1, LANE), 1)
        for k in range(min(TOPN, n_blocks)):
            best = jnp.max(jnp.where(cand, score, -jnp.inf), axis=-1, keepdims=True)
            idx_f = jnp.min(jnp.where(cand & (score == best), n_lane_f, float(SEL_LANES)), axis=-1, keepdims=True)
            idx = idx_f.astype(jnp.int32)
            sel = jnp.where(k_lane == k, idx, sel)
            cand = cand & (n_lane != idx)
        sel_ref[g] = jnp.broadcast_to(sel, (8, LANE))


def _sel_decode(q8, cmp_kv, cmp_last, bias_cd, t_pos):
    nd = q8.shape[0]
    n_past = cmp_kv.shape[3]
    return pl.pallas_call(
        functools.partial(_sel_decode_kernel, t_pos=t_pos),
        out_shape=(jax.ShapeDtypeStruct((nd, NSA_G, 8, DH), F32),
                   jax.ShapeDtypeStruct((nd, NSA_G, 8, LANE), jnp.int32)),
        grid=(nd,),
        in_specs=[
            pl.BlockSpec((None, NSA_G, 8, DH), lambda b: (b, 0, 0, 0)),
            pl.BlockSpec((None, None, NSA_G, n_past, DH), lambda b: (0, b, 0, 0, 0)),
            pl.BlockSpec((None, None, NSA_G, n_past, DH), lambda b: (1, b, 0, 0, 0)),
            pl.BlockSpec(cmp_last.shape, lambda b: (0, 0, 0)),
            pl.BlockSpec(bias_cd.shape, lambda b: (0, 0, 0)),
        ],
        out_specs=(pl.BlockSpec((None, NSA_G, 8, DH), lambda b: (b, 0, 0, 0)),
                   pl.BlockSpec((None, NSA_G, 8, LANE), lambda b: (b, 0, 0, 0))),
        compiler_params=_cparams(("parallel",)),
        name="sel_decode",
    )(q8, cmp_kv, cmp_kv, cmp_last, bias_cd)


HALF_ROWS = BLK * ROW_W


def _slc_decode_kernel(sel_ref, pt_ref, q_ref, oc_ref, gl_ref, gb_ref, slc_ref, bias_ref, nk_ref, nv_ref,
                       win_ref, nwk_ref, nwv_ref, bw_ref, o_ref, gbuf, sem, *, t_pos, n_past):
    b = pl.program_id(0)
    n_win = win_ref.shape[0] // ROW_W

    def block_copy(g, k):
        n = jnp.minimum(sel_ref[(b * NSA_G + g) * TOPN + k], n_past - 1)
        half = pt_ref[b, n // 2] * 2 + n % 2
        return pltpu.make_async_copy(slc_ref.at[pl.ds(pl.multiple_of(half * HALF_ROWS, HALF_ROWS), HALF_ROWS), :],
                                     gbuf.at[g * TOPN + k], sem)

    for g in range(NSA_G):
        for k in range(TOPN):
            block_copy(g, k).start()

    c = lax.broadcasted_iota(jnp.int32, (1, n_win), 1)
    dist = n_win - c
    valid = (dist >= 0) & (dist <= WINDOW) & (t_pos - dist >= 0)
    qs, o_w = [], []
    for g in range(NSA_G):
        q = q_ref[g] * (DH ** -0.5)
        q_bf = q.astype(BF16)
        qs.append((q, q_bf))
        kw = win_ref[pl.ds(2 * g, n_win, stride=ROW_W), :].astype(BF16)
        vw = win_ref[pl.ds(2 * g + 1, n_win, stride=ROW_W), :].astype(BF16)
        bw = bw_ref[g]
        s_w = jnp.where(valid, _nt(q_bf, kw) + bw[:, :n_win], -jnp.inf)
        s_n = jnp.sum(q * nwk_ref[g, 0:1, :], axis=-1, keepdims=True) + bw[:, n_win:n_win + 1]
        m_w = jnp.maximum(jnp.max(s_w, axis=-1, keepdims=True), s_n)
        e_w = jnp.exp(s_w - m_w)
        e_n = jnp.exp(s_n - m_w)
        den = jnp.maximum(jnp.sum(e_w, axis=-1, keepdims=True) + e_n, 1e-30)
        o_w.append((_dot(e_w.astype(BF16), vw) + e_n * nwv_ref[g, 0:1, :]) / den)

    for g in range(NSA_G):
        for k in range(TOPN):
            block_copy(g, k).wait()

    row0 = lax.broadcasted_iota(jnp.int32, (BLK, 1), 0) == 0
    j_row = lax.broadcasted_iota(jnp.int32, (1, BLK), 1)
    for g in range(NSA_G):
        q, q_bf = qs[g]
        scores, values = [], []
        for k in range(TOPN):
            n = sel_ref[(b * NSA_G + g) * TOPN + k]
            is_new = (n == n_past) & row0
            kb = jnp.where(is_new, nk_ref[g, 0:1, :], gbuf[g * TOPN + k, pl.ds(2 * g, BLK, stride=ROW_W), :])
            vb = jnp.where(is_new, nv_ref[g, 0:1, :], gbuf[g * TOPN + k, pl.ds(2 * g + 1, BLK, stride=ROW_W), :])
            s = _nt(q_bf, kb.astype(BF16)) + bias_ref[g, n]
            scores.append(jnp.where(t_pos - (n * BLK + j_row) >= 0, s, -jnp.inf))
            values.append(vb.astype(BF16))
        m = functools.reduce(jnp.maximum, [jnp.max(s, axis=-1, keepdims=True) for s in scores])
        m = jnp.where(m > -jnp.inf, m, 0.0)
        probs = [jnp.exp(s - m) for s in scores]
        l = functools.reduce(jnp.add, [jnp.sum(p, axis=-1, keepdims=True) for p in probs])
        acc = functools.reduce(jnp.add, [_dot(p.astype(BF16), v) for p, v in zip(probs, values)])
        o_s = acc / jnp.maximum(l, 1e-30)
        gates = _sigmoid(gl_ref[g] + gb_ref[g])
        o_ref[g] = gates[:, 0:1] * oc_ref[g] + gates[:, 1:2] * o_s + gates[:, 2:3] * o_w[g]


def _slc_decode(sel_flat, page_table, q8, o_c, gate_l, gate_b, slc_cache, bias_sd, new_k, new_v,
                win_cache, new_wk, new_wv, bias_wd, t_pos):
    nd = q8.shape[0]
    n_past = page_table.shape[1] * (PAGE // BLK)
    win_rows = win_cache.shape[0] // nd

    def per_b(b, sel, pt):
        return (b, 0, 0, 0)

    def whole(shape):
        return pl.BlockSpec(shape, lambda b, sel, pt: (0,) * len(shape), pipeline_mode=pl.Buffered(1))

    b_spec = pl.BlockSpec((None, NSA_G, 8, DH), per_b)
    grid_spec = pltpu.PrefetchScalarGridSpec(
        num_scalar_prefetch=2,
        grid=(nd,),
        in_specs=[
            b_spec, b_spec, b_spec, whole(gate_b.shape),
            pl.BlockSpec(memory_space=pl.ANY),
            whole(bias_sd.shape),
            b_spec, b_spec,
            pl.BlockSpec((win_rows, DH), lambda b, sel, pt: (b, 0)),
            b_spec, b_spec,
            whole(bias_wd.shape),
        ],
        out_specs=b_spec,
        scratch_shapes=[pltpu.VMEM((NSA_G * TOPN, HALF_ROWS, DH), F32), pltpu.SemaphoreType.DMA(())],
    )
    return pl.pallas_call(
        functools.partial(_slc_decode_kernel, t_pos=t_pos, n_past=n_past),
        out_shape=jax.ShapeDtypeStruct((nd, NSA_G, 8, DH), F32),
        grid_spec=grid_spec,
        compiler_params=_cparams(("arbitrary",)),
        name="slc_decode",
    )(sel_flat, page_table, q8, o_c, gate_l, gate_b, slc_cache, bias_sd, new_k, new_v,
      win_cache, new_wk, new_wv, bias_wd)


def _t5_bucket(dist):
    n = jnp.maximum(dist, 0)
    nf = jnp.maximum(n, REL_MAX_EXACT).astype(F32)
    large = REL_MAX_EXACT + (jnp.log(nf / REL_MAX_EXACT) / math.log(REL_MAX_DIST / REL_MAX_EXACT)
                             * (REL_BUCKETS - REL_MAX_EXACT)).astype(jnp.int32)
    return jnp.where(n < REL_MAX_EXACT, n, jnp.minimum(large, REL_BUCKETS - 1))


def _bias_table(rel_bias, dist):
    onehot = (_t5_bucket(dist)[..., None] == jnp.arange(REL_BUCKETS)).astype(F32)
    b = jnp.einsum("...k,kh->...h", onehot, rel_bias, precision=HI)
    b = jnp.moveaxis(b, -1, 0)
    return b.reshape((NSA_G, NSA_R) + dist.shape)


def _prompt_bias_tables(rel_bias):
    i = jnp.arange(QBLK)[:, None]
    rel = jnp.arange(LANE)[None, :]
    bct = _bias_table(rel_bias, BLK * (rel - 1) + i % BLK + 1)
    d_near = SLC_PAD + i - jnp.arange(SLC_NEAR)[None, :]
    far = rel_bias[REL_BUCKETS - 1].reshape(NSA_G, NSA_R, 1, 1)
    bsn = _bias_table(rel_bias, d_near) - far + jnp.where(d_near >= 0, 0.0, NEG)
    d_win = WIN_PAD + i - jnp.arange(WIN_KEYS)[None, :]
    bw = _bias_table(rel_bias, d_win) + jnp.where((d_win >= 0) & (d_win <= WINDOW), 0.0, NEG)
    return bct, bsn, bw


def _pad_rows(a, rows, axis):
    pad = [(0, 0)] * a.ndim
    pad[axis] = (0, rows - a.shape[axis])
    return jnp.pad(a, pad)


def _decode_bias_tables(rel_bias, t_pos, n_past, n_win):
    n = jnp.arange(SEL_LANES)
    bias_cd = _pad_rows(_bias_table(rel_bias, t_pos - (n * BLK + BLK - 1)), 8, 1)
    tok = jnp.arange(n_past + 1)[:, None] * BLK + jnp.arange(BLK)[None, :]
    bias_sd = _pad_rows(jnp.swapaxes(_bias_table(rel_bias, t_pos - tok), 1, 2), 8, 2)
    c = jnp.arange(n_win + LANE)
    bias_wd = _pad_rows(_bias_table(rel_bias, n_win - c), 8, 1)
    return bias_cd, bias_sd, bias_wd


def kernel(x_prompt, x_sample, mem_prompt, cache_cmp_kv, cache_slc_kv, state_win_kv, state_gla, cache_mem_kv,
           page_table, norm_g, w_ffn_gate, w_ffn_up, w_ffn_down, w_in_gla, w_in_nsa, w_out, mem_norm_g, w_mem_kv,
           w_gla_a2, b_gla_a, gla_onorm_g, nsa_gate_b, cmp_pe, cmp_w1, cmp_b1, cmp_w2, rel_bias):
    nb, seq, _ = x_prompt.shape
    nd = x_sample.shape[0]
    depth = norm_g.shape[0]
    n_pages = page_table.shape[1]
    past_len = n_pages * PAGE
    n_past = past_len // BLK
    n_win = state_win_kv.shape[2]
    sr = SAMPLE_ROWS

    xp = x_prompt.reshape(nb * seq, D_MODEL)
    xs = _pad_rows(x_sample.reshape(nd, D_MODEL), sr, 0)
    mem_x = mem_prompt.reshape(nb * MEM_LEN, D_MODEL)

    w_o = w_out.astype(BF16)
    w_mkv = w_mem_kv.astype(BF16)
    def zero_rows(like, rows):
        return jnp.zeros((like.shape[0], rows, D_MODEL), BF16)

    wt_gla = jnp.swapaxes(w_in_gla, 1, 2).astype(BF16)
    tail_gla = jnp.concatenate(
        [wt_gla[:, GLA_MAIN + GLA_RANK:], wt_gla[:, GLA_MAIN:GLA_MAIN + GLA_RANK],
         zero_rows(wt_gla, TN_PROJ - MEM_W - GLA_RANK)], axis=1)
    n_gate = 3 * NSA_HEADS
    wt_nsa = jnp.swapaxes(w_in_nsa, 1, 2).astype(BF16)
    tail_nsa = jnp.concatenate(
        [wt_nsa[:, NSA_GATE_COL:NSA_GATE_COL + n_gate], zero_rows(wt_nsa, TN_PROJ - MEM_W - n_gate),
         wt_nsa[:, NSA_GATE_COL + n_gate:]], axis=1)
    wa_pad = _pad_rows(w_gla_a2, LANE, 1)
    w1 = cmp_w1.astype(BF16)
    w2 = cmp_w2.astype(BF16)

    def ffn_both(x_p, x_s, i, j):
        g1, g2 = norm_g[i, 4 * j][None], norm_g[i, 4 * j + 1][None]
        x_s, wg, wu, wd = _ffn_half(x_s, g1, w_ffn_gate, w_ffn_up, w_ffn_down, g2, TM_SAMPLE, f32_weights_at=(i, j))
        return _ffn_half(x_p, g1, wg, wu, wd, g2, TM_FFN), x_s

    def every(a, step):
        return a.reshape(nd, step, a.shape[-1])[:, 0]

    def per_seq(a):
        return _pad_rows(a[:, None, :], sr, 1).reshape(nd * sr, a.shape[-1])

    outs = dict(gla_p=[], gla_s=[], cmp_p=[], cmp_s=[], slc_p=[], slc_s=[], win_p=[], win_s=[], mem_p=[])
    for i in range(depth):
        li = i // 2
        mem_kv_p = _norm_matmul(mem_x, mem_norm_g[i][None], w_mkv, i, TM_MEMKV, MEM_W)
        outs["mem_p"].append(mem_kv_p.reshape(nb, MEM_LEN, 2, N_MEM_HEADS, MEM_HEAD_DIM))
        mem_kv_p = mem_kv_p.reshape(nb, MEM_LEN, 2 * MEM_W)
        mem_kv_s = cache_mem_kv[i].reshape(nd, MEM_LEN, 2 * MEM_W)
        xp, xs = ffn_both(xp, xs, i, 0)
        g_mix = norm_g[i, 2][None]
        if i % 2 == 0:
            proj_p = _in_proj(xp, g_mix, wt_gla, GLA_MAIN, tail_gla, li, TM_PROJ)
            proj_s = _in_proj(xs, g_mix, wt_gla, GLA_MAIN, tail_gla, li, TM_SAMPLE)
            mem_col = GLA_MAIN // MEM_W
            b_a = b_gla_a[li][None]
            gn = gla_onorm_g[li][None]
            s0 = jnp.zeros((nb, GLA_HEADS, GLA_DK, GLA_DV), F32)
            tok_p, sp = _gla(proj_p, s0, wa_pad[li], b_a, gn, seq, TL_GLA, GLA_CHUNK, seq)
            tok_s, ss = _gla(per_seq(proj_s[:nd]), state_gla[li], wa_pad[li], b_a, gn, sr, sr, sr, 1)
            tok_s = _pad_rows(every(tok_s, sr), sr, 0)
            outs["gla_p"].append(sp)
            outs["gla_s"].append(ss)
        else:
            assert 2 * KV_W == TN_PROJ and NSA_KV_COL % TN_PROJ == 0
            kv_tiles = (NSA_KV_COL // TN_PROJ, NSA_GATE_COL // TN_PROJ)
            proj_p = _in_proj(xp, g_mix, wt_nsa, NSA_GATE_COL, tail_nsa, li, TM_PROJ, kv_tiles)
            proj_s = _in_proj(xs, g_mix, wt_nsa, NSA_GATE_COL, tail_nsa, li, TM_SAMPLE, kv_tiles)
            mem_col = NSA_MEM_COL // MEM_W
            b1 = cmp_b1[li][:, None, :]
            gate_b = nsa_gate_b[li]
            bct, bsn, bw = _prompt_bias_tables(rel_bias)
            proj3 = proj_p.reshape(nb, seq, NSA_N)
            cmp_kv = _cmp_prompt(proj3, cmp_pe[li], w1[li], b1, w2[li])
            tok_p = _nsa_prompt(proj_p, cmp_kv, _pad_rows(gate_b[None], LANE, 1), bct, bsn, bw, nb, seq)
            def kv_rows(branch, first_row):
                cols = proj3[:, first_row:, NSA_KV_COL + branch * TN_PROJ:NSA_KV_COL + (branch + 1) * TN_PROJ]
                return jnp.swapaxes(cols.reshape(nb, seq - first_row, NSA_G, 2, DH), 2, 3)

            outs["cmp_p"].append(kv_rows(0, 0).reshape(nb, seq // PAGE, PAGE, 2, NSA_G, DH))
            outs["slc_p"].append(kv_rows(1, 0).reshape(nb, seq // PAGE, PAGE, 2, NSA_G, DH))
            outs["win_p"].append(kv_rows(2, seq - n_win))
            t_pos = past_len
            bias_cd, bias_sd, bias_wd = _decode_bias_tables(rel_bias, t_pos, n_past, n_win)
            kv_s = jnp.swapaxes(proj_s[:nd, NSA_KV_COL:NSA_GATE_COL].reshape(nd, 3, NSA_G, 2, DH), 2, 3)
            outs["cmp_s"].append(kv_s[:, None, 0])
            outs["slc_s"].append(kv_s[:, None, 1])
            outs["win_s"].append(jnp.concatenate([state_win_kv[li][:, 1:], kv_s[:, None, 2]], axis=1))
            q8 = _pad_rows(proj_s[:nd, :TOK_W].reshape(nd, NSA_G, NSA_R, DH), 8, 2)
            new_cmp = _pad_rows(jnp.moveaxis(kv_s[:, 0], 1, 0).reshape(2, nd * NSA_G, DH), 32, 1)
            cmp_kv_s, cmp_last = _cmp_decode(page_table, _linear_cache(cache_cmp_kv[li]), new_cmp,
                                             cmp_pe[li], w1[li], b1, w2[li])
            o_c, sel = _sel_decode(q8, cmp_kv_s, cmp_last, bias_cd, t_pos)
            sel_flat = sel[:, :, 0, :TOPN].reshape(-1)

            def row8(a):
                return _pad_rows(a[:, :, None, :], 8, 2)

            gate_l = _pad_rows(_pad_rows(proj_s[:nd, NSA_GATE_COL:NSA_GATE_COL + n_gate].reshape(nd, NSA_G, NSA_R, 3),
                                         8, 2), LANE, 3)
            gate_b8 = _pad_rows(_pad_rows(gate_b.reshape(NSA_G, NSA_R, 3), 8, 1), LANE, 2)
            tok_s = _slc_decode(sel_flat, page_table, q8, o_c, gate_l, gate_b8,
                                _linear_cache(cache_slc_kv[li]), bias_sd,
                                row8(kv_s[:, 1, 0]), row8(kv_s[:, 1, 1]),
                                _linear_cache(state_win_kv[li]),
                                row8(kv_s[:, 2, 0]), row8(kv_s[:, 2, 1]), bias_wd, t_pos)
            tok_s = _pad_rows(tok_s[:, :, :NSA_R].reshape(nd, TOK_W), sr, 0).astype(BF16)
        mem_o_p = _mem_attn(proj_p, mem_col, mem_kv_p, seq, TM_MEM_ATTN)
        q_s = jnp.broadcast_to(proj_s[:nd, None, mem_col * MEM_W:(mem_col + 1) * MEM_W],
                               (nd, sr, MEM_W)).reshape(nd * sr, MEM_W)
        mem_o_s = _pad_rows(every(_mem_attn(q_s, 0, mem_kv_s, sr, sr), sr), sr, 0)
        xp = _out_proj(xp, tok_p, mem_o_p, w_o, i, norm_g[i, 3][None], TM_OUT)
        xs = _out_proj(xs, tok_s, mem_o_s, w_o, i, norm_g[i, 3][None], sr)
        xp, xs = ffn_both(xp, xs, i, 1)

    y_prompt = xp.reshape(nb, seq, D_MODEL)
    y_sample = xs[:nd].reshape(nd, 1, D_MODEL)
    st = lambda k: jnp.stack(outs[k])
    return (y_prompt, y_sample, st("gla_p"), st("cmp_p"), st("slc_p"), st("win_p"), st("mem_p"),
            st("gla_s"), st("cmp_s"), st("slc_s"), st("win_s"))
```

```python
import functools
import math

import jax
import jax.numpy as jnp
from jax import lax
from jax.experimental import pallas as pl
from jax.experimental.pallas import tpu as pltpu

F32 = jnp.float32
BF16 = jnp.bfloat16
HI = lax.Precision.HIGHEST

D_MODEL = 2048
D_FF = 5632
EPS = 1e-6
MEM_LEN = 256
N_MEM_HEADS = 4
MEM_HEAD_DIM = 128
MEM_W = N_MEM_HEADS * MEM_HEAD_DIM
TOK_W = D_MODEL - MEM_W
GLA_HEADS = 4
GLA_DV = TOK_W // GLA_HEADS
GLA_DK = GLA_DV // 2
GLA_RANK = 16
GLA_TAU = 16.0
GLA_CHUNK = 64
GLA_PAIR_W = 2 * GLA_DK
DH = 128
NSA_HEADS = TOK_W // DH
NSA_G = 3
NSA_R = NSA_HEADS // NSA_G
BLK = 64
TOPN = 16
WINDOW = 512
CMP_HID = 256
QBLK = 256
KV_W = NSA_G * DH
REL_BUCKETS = 32
REL_MAX_EXACT = 16
REL_MAX_DIST = 128
PAGE = 128
LANE = 128
HALF_LANE = LANE // 2
BLK_SHIFT = BLK.bit_length() - 1
VMEM_LIMIT = 56 * 1024 * 1024

TN_PROJ = 768
GLA_MAIN = 2 * GLA_HEADS * GLA_DK + 2 * TOK_W
GLA_A_COL = GLA_MAIN + MEM_W
GLA_N = GLA_MAIN + TN_PROJ
NSA_KV_COL = TOK_W
NSA_GATE_COL = NSA_KV_COL + 6 * KV_W
NSA_MEM_COL = NSA_GATE_COL + TN_PROJ - MEM_W
NSA_N = NSA_GATE_COL + TN_PROJ
SAMPLE_ROWS = 16
TM_SAMPLE = SAMPLE_ROWS
TM_FFN = 512
TM_PROJ = 1024
TM_PROJ_KV = 512
TM_MEMKV = 512
TM_OUT = 512
TM_MEM_ATTN = 512
TL_GLA = 512
SLC_PAD = 256
WIN_PAD = WINDOW
WIN_KEYS = WIN_PAD + QBLK
SLC_NEAR = SLC_PAD + QBLK
FAR_CHUNK = 1024
NEG = -1e30
M_FLOOR = -1e29


def _cparams(sem, vmem=VMEM_LIMIT):
    return pltpu.CompilerParams(dimension_semantics=sem, vmem_limit_bytes=vmem)


def _sigmoid(x):
    return 1.0 / (1.0 + jnp.exp(-x))


def _rms(x, g):
    ms = jnp.mean(x * x, axis=-1, keepdims=True)
    return x * lax.rsqrt(ms + EPS) * g


def _nt(a, b, precision=None):
    return lax.dot_general(a, b, (((1,), (1,)), ((), ())), precision=precision,
                           preferred_element_type=F32)


def _tn(a, b, precision=None):
    return lax.dot_general(a, b, (((0,), (0,)), ((), ())), precision=precision,
                           preferred_element_type=F32)


def _dot(a, b, precision=None):
    return jnp.dot(a, b, precision=precision, preferred_element_type=F32)


def _split3(x):
    hi = x.astype(BF16)
    r1 = x - hi.astype(F32)
    mid = r1.astype(BF16)
    lo = (r1 - mid.astype(F32)).astype(BF16)
    return hi, mid, lo


def _ffn_kernel(x_ref, g1_ref, wg_ref, wu_ref, wd_ref, g2_ref, o_ref, *rest):
    bf_out, (xn_ref, acc_ref) = rest[:-2], rest[-2:]
    j = pl.program_id(1)

    @pl.when(j == 0)
    def _():
        xn_ref[...] = _rms(x_ref[...], g1_ref[...]).astype(BF16)
        acc_ref[...] = jnp.zeros_like(acc_ref)

    wg, wu, wd = (w_ref[...].astype(BF16) for w_ref in (wg_ref, wu_ref, wd_ref))
    for out_ref, w in zip(bf_out, (wg, wu, wd)):
        out_ref[...] = w
    xn = xn_ref[...]
    gate = _dot(xn, wg)
    up = _dot(xn, wu)
    h = (gate * _sigmoid(gate) * up).astype(BF16)
    acc_ref[...] += _dot(h, wd)

    @pl.when(j == pl.num_programs(1) - 1)
    def _():
        o_ref[...] = x_ref[...] + 0.5 * _rms(acc_ref[...], g2_ref[...])


def _ffn_half(x, g1, wg, wu, wd, g2, tm, tf=512, f32_weights_at=None):
    m = x.shape[0]
    y_shape = jax.ShapeDtypeStruct((m, D_MODEL), F32)
    y_spec = pl.BlockSpec((tm, D_MODEL), lambda i, j: (i, 0))
    col_spec = pl.BlockSpec((D_MODEL, tf), lambda i, j: (0, j))
    row_spec = pl.BlockSpec((tf, D_MODEL), lambda i, j: (j, 0))
    if f32_weights_at is None:
        w_specs = [col_spec, col_spec, row_spec]
        out_shape, out_specs = y_shape, y_spec
    else:
        assert m == tm, "the bf16 copies are written once, by a single row tile"
        layer, half = f32_weights_at
        w_specs = [pl.BlockSpec((None, None, D_MODEL, tf), lambda i, j: (layer, half, 0, j)),
                   pl.BlockSpec((None, None, D_MODEL, tf), lambda i, j: (layer, half, 0, j)),
                   pl.BlockSpec((None, None, tf, D_MODEL), lambda i, j: (layer, half, j, 0))]
        out_shape = (y_shape, jax.ShapeDtypeStruct((D_MODEL, D_FF), BF16),
                     jax.ShapeDtypeStruct((D_MODEL, D_FF), BF16), jax.ShapeDtypeStruct((D_FF, D_MODEL), BF16))
        out_specs = (y_spec, col_spec, col_spec, row_spec)
    return pl.pallas_call(
        _ffn_kernel,
        out_shape=out_shape,
        grid=(m // tm, D_FF // tf),
        in_specs=[y_spec, pl.BlockSpec((1, D_MODEL), lambda i, j: (0, 0))] + w_specs
                 + [pl.BlockSpec((1, D_MODEL), lambda i, j: (0, 0))],
        out_specs=out_specs,
        scratch_shapes=[pltpu.VMEM((tm, D_MODEL), BF16), pltpu.VMEM((tm, D_MODEL), F32)],
        compiler_params=_cparams(("parallel", "arbitrary")),
        name="ffn_half",
    )(x, g1, wg, wu, wd, g2)


def _norm_matmul_kernel(x_ref, g_ref, w_ref, o_ref, xn_ref):
    @pl.when(pl.program_id(1) == 0)
    def _():
        xn_ref[...] = _rms(x_ref[...], g_ref[...]).astype(BF16)

    o_ref[...] = _dot(xn_ref[...], w_ref[...])


def _norm_matmul(x, g, w, layer, tm, tn):
    m, n = x.shape[0], w.shape[2]
    return pl.pallas_call(
        _norm_matmul_kernel,
        out_shape=jax.ShapeDtypeStruct((m, n), F32),
        grid=(m // tm, n // tn),
        in_specs=[
            pl.BlockSpec((tm, D_MODEL), lambda i, j: (i, 0)),
            pl.BlockSpec((1, D_MODEL), lambda i, j: (0, 0)),
            pl.BlockSpec((None, D_MODEL, tn), lambda i, j: (layer, 0, j)),
        ],
        out_specs=pl.BlockSpec((tm, tn), lambda i, j: (i, j)),
        scratch_shapes=[pltpu.VMEM((tm, D_MODEL), BF16)],
        compiler_params=_cparams(("parallel", "arbitrary")),
        name="norm_matmul",
    )(x, g, w)


def _in_proj_kernel(x_ref, g_ref, wm_ref, wt_ref, o_ref, *rest, n_main, kv_tiles):
    lin_refs, xn_ref = rest[:-1], rest[-1]
    j = pl.program_id(1)
    kv_lo, kv_hi = kv_tiles
    is_kv = (j >= kv_lo) & (j < kv_hi)
    rows = x_ref.shape[0]

    @pl.when(j == 0)
    def _():
        xn_ref[...] = _rms(x_ref[...], g_ref[...]).astype(BF16)

    @pl.when((j < n_main) & jnp.logical_not(is_kv))
    def _():
        o_ref[...] = _nt(xn_ref[...], wm_ref[...])

    for t in range(kv_lo, kv_hi):
        @pl.when(j == t)
        def _(t=t):
            res = _nt(xn_ref[...], wm_ref[...])
            for g in range(NSA_G):
                for kv in range(2):
                    src, dst = kv * NSA_G + g, g * 2 + kv
                    blk = res[:, src * DH:(src + 1) * DH]
                    o_ref[:, dst * DH:(dst + 1) * DH] = blk
                    if lin_refs:
                        lin_refs[t - kv_lo][pl.ds(dst, rows, stride=ROW_W), :] = blk

    @pl.when(j == n_main)
    def _():
        o_ref[...] = _nt(xn_ref[...], wt_ref[...])


def _in_proj(x, g, w_main, main_cols, w_tail, layer, tm, kv_tiles=(0, 0), linear_kv=False):
    m = x.shape[0]
    n_main = main_cols // TN_PROJ
    out_shape = [jax.ShapeDtypeStruct((m, main_cols + TN_PROJ), F32)]
    out_specs = [pl.BlockSpec((tm, TN_PROJ), lambda i, j: (i, j))]
    if linear_kv:
        n_kv = kv_tiles[1] - kv_tiles[0]
        out_shape += [jax.ShapeDtypeStruct((m * ROW_W, DH), F32)] * n_kv
        out_specs += [pl.BlockSpec((tm * ROW_W, DH), lambda i, j: (i, 0))] * n_kv
    out = pl.pallas_call(
        functools.partial(_in_proj_kernel, n_main=n_main, kv_tiles=kv_tiles),
        out_shape=out_shape,
        grid=(m // tm, n_main + 1),
        in_specs=[
            pl.BlockSpec((tm, D_MODEL), lambda i, j: (i, 0)),
            pl.BlockSpec((1, D_MODEL), lambda i, j: (0, 0)),
            pl.BlockSpec((None, TN_PROJ, D_MODEL), lambda i, j: (layer, jnp.minimum(j, n_main - 1), 0)),
            pl.BlockSpec((None, TN_PROJ, D_MODEL), lambda i, j: (layer, 0, 0)),
        ],
        out_specs=out_specs,
        scratch_shapes=[pltpu.VMEM((tm, D_MODEL), BF16)],
        compiler_params=_cparams(("parallel", "arbitrary")),
        name="in_proj",
    )(x, g, w_main, w_tail)
    return out if linear_kv else out[0]


def _out_proj_kernel(x_ref, tok_ref, mem_ref, wt_ref, wm_ref, g_ref, o_ref):
    y = _dot(tok_ref[...], wt_ref[...]) + _dot(mem_ref[...], wm_ref[...])
    o_ref[...] = x_ref[...] + _rms(y, g_ref[...])


def _out_proj(x, tok, mem_o, w_o, layer, g, tm):
    m = x.shape[0]
    return pl.pallas_call(
        _out_proj_kernel,
        out_shape=jax.ShapeDtypeStruct((m, D_MODEL), F32),
        grid=(m // tm,),
        in_specs=[
            pl.BlockSpec((tm, D_MODEL), lambda i: (i, 0)),
            pl.BlockSpec((tm, TOK_W), lambda i: (i, 0)),
            pl.BlockSpec((tm, MEM_W), lambda i: (i, 0)),
            pl.BlockSpec((None, TOK_W, D_MODEL), lambda i: (layer, 0, 0), pipeline_mode=pl.Buffered(1)),
            pl.BlockSpec((None, MEM_W, D_MODEL), lambda i: (layer, TOK_W // MEM_W, 0), pipeline_mode=pl.Buffered(1)),
            pl.BlockSpec((1, D_MODEL), lambda i: (0, 0)),
        ],
        out_specs=pl.BlockSpec((tm, D_MODEL), lambda i: (i, 0)),
        compiler_params=_cparams(("parallel",)),
        name="out_proj",
    )(x, tok, mem_o, w_o, w_o, g)


def _mem_attn_kernel(q_ref, kv_ref, o_ref):
    for h in range(N_MEM_HEADS):
        q = (q_ref[:, h * DH:(h + 1) * DH] * (MEM_HEAD_DIM ** -0.5)).astype(BF16)
        k = kv_ref[:, h * DH:(h + 1) * DH].astype(BF16)
        v = kv_ref[:, MEM_W + h * DH:MEM_W + (h + 1) * DH].astype(BF16)
        s = _nt(q, k)
        e = jnp.exp(s - jnp.max(s, axis=-1, keepdims=True))
        p = e / jnp.sum(e, axis=-1, keepdims=True)
        o_ref[:, h * DH:(h + 1) * DH] = _dot(p.astype(BF16), v).astype(o_ref.dtype)


def _mem_attn(q_arr, q_col_block, mem_kv, rows_per_batch, tm):
    nb = mem_kv.shape[0]
    per = rows_per_batch // tm
    return pl.pallas_call(
        _mem_attn_kernel,
        out_shape=jax.ShapeDtypeStruct((nb * rows_per_batch, MEM_W), BF16),
        grid=(nb, per),
        in_specs=[
            pl.BlockSpec((tm, MEM_W), lambda b, i: (b * per + i, q_col_block)),
            pl.BlockSpec((None, MEM_LEN, 2 * MEM_W), lambda b, i: (b, 0, 0)),
        ],
        out_specs=pl.BlockSpec((tm, MEM_W), lambda b, i: (b * per + i, 0)),
        compiler_params=_cparams(("parallel", "parallel")),
        name="mem_attn",
    )(q_arr, mem_kv)


GLA_PAIRS = GLA_HEADS // 2


def _gla_kernel(q_ref, k_ref, v_ref, r_ref, a_ref, s0_ref, wa_ref, ba_ref, gn_ref,
                tok_ref, s_out_ref, s_ref, cum_ref, *, chunk, n_valid):
    l = pl.program_id(1)
    tl = q_ref.shape[0]

    @pl.when(l == 0)
    def _():
        for p in range(GLA_PAIRS):
            s_ref[p] = s0_ref[2 * p:2 * p + 2].reshape(GLA_PAIR_W, GLA_DV).T

    lane = lax.broadcasted_iota(jnp.int32, (1, GLA_PAIR_W), 1)
    head_mask = [(lane < GLA_DK).astype(F32), (lane >= GLA_DK).astype(F32)]
    ti = lax.broadcasted_iota(jnp.int32, (chunk, chunk), 0)
    si = lax.broadcasted_iota(jnp.int32, (chunk, chunk), 1)
    causal = si <= ti
    tri = jnp.where(causal, 1.0, 0.0).astype(BF16)

    a_hi, a_lo, _ = _split3(a_ref[...])
    w_hi, w_lo, _ = _split3(wa_ref[...])
    z = _dot(a_hi, w_hi) + _dot(a_lo, w_hi) + _dot(a_hi, w_lo) + ba_ref[...]
    la_all = -(jnp.maximum(-z, 0.0) + jnp.log1p(jnp.exp(-jnp.abs(z)))) / GLA_TAU
    pos = l * tl + lax.broadcasted_iota(jnp.int32, (tl, 1), 0)
    la_all = jnp.where(pos < n_valid, la_all, 0.0)
    pieces = _split3(la_all)
    for c0 in range(0, tl, chunk):
        cum_ref[c0:c0 + chunk, :] = functools.reduce(jnp.add, [_dot(tri, pc[c0:c0 + chunk]) for pc in pieces])

    def step(ci, carry):
        r0 = pl.multiple_of(ci * chunk, chunk)
        rows = pl.ds(r0, chunk)
        for p in range(GLA_PAIRS):
            pair = slice(p * GLA_PAIR_W, (p + 1) * GLA_PAIR_W)
            b = cum_ref[rows, pair]
            bl = b[chunk - 1:chunk, :]
            q = q_ref[rows, pair] * (GLA_DK ** -0.5)
            k = k_ref[rows, pair]
            qe = q * jnp.exp(b)
            ke = (k * jnp.exp(-b)).astype(BF16)
            kd = k * jnp.exp(bl - b)
            st_old = s_ref[p]
            st_bf = st_old.astype(BF16)
            upd = None
            for h in range(2):
                head = slice((2 * p + h) * GLA_DV, (2 * p + h + 1) * GLA_DV)
                v = v_ref[rows, head].astype(BF16)
                qm = (qe * head_mask[h]).astype(BF16)
                att = jnp.where(causal, _nt(qm, ke), 0.0)
                o = _nt(qm, st_bf) + _dot(att.astype(BF16), v)
                o = _rms(o, gn_ref[...])
                r = r_ref[rows, head]
                tok_ref[rows, head] = (o * (r * _sigmoid(r))).astype(tok_ref.dtype)
                u = _tn(v, (kd * head_mask[h]).astype(BF16))
                upd = u if upd is None else upd + u
            s_ref[p] = jnp.exp(bl) * st_old + upd
        return carry

    lax.fori_loop(0, tl // chunk, step, 0)

    @pl.when(l == pl.num_programs(1) - 1)
    def _():
        for p in range(GLA_PAIRS):
            s_out_ref[2 * p:2 * p + 2] = s_ref[p].T.reshape(2, GLA_DK, GLA_DV)


def _gla(proj, s0, wa_pad, b_a, gn, seq, tl, chunk, n_valid):
    nb = s0.shape[0]
    per = seq // tl
    qk_w = GLA_HEADS * GLA_DK
    row = lambda b, l: b * per + l

    return pl.pallas_call(
        functools.partial(_gla_kernel, chunk=chunk, n_valid=n_valid),
        out_shape=(jax.ShapeDtypeStruct((nb * seq, TOK_W), BF16),
                   jax.ShapeDtypeStruct((nb, GLA_HEADS, GLA_DK, GLA_DV), F32)),
        grid=(nb, per),
        in_specs=[
            pl.BlockSpec((tl, qk_w), lambda b, l: (row(b, l), 0)),
            pl.BlockSpec((tl, qk_w), lambda b, l: (row(b, l), 1)),
            pl.BlockSpec((tl, TOK_W), lambda b, l: (row(b, l), 2 * qk_w // TOK_W)),
            pl.BlockSpec((tl, TOK_W), lambda b, l: (row(b, l), 2 * qk_w // TOK_W + 1)),
            pl.BlockSpec((tl, LANE), lambda b, l: (row(b, l), GLA_A_COL // LANE)),
            pl.BlockSpec((None, GLA_HEADS, GLA_DK, GLA_DV), lambda b, l: (b, 0, 0, 0)),
            pl.BlockSpec((LANE, qk_w), lambda b, l: (0, 0)),
            pl.BlockSpec((1, qk_w), lambda b, l: (0, 0)),
            pl.BlockSpec((1, GLA_DV), lambda b, l: (0, 0)),
        ],
        out_specs=(pl.BlockSpec((tl, TOK_W), lambda b, l: (row(b, l), 0)),
                   pl.BlockSpec((None, GLA_HEADS, GLA_DK, GLA_DV), lambda b, l: (b, 0, 0, 0))),
        scratch_shapes=[pltpu.VMEM((GLA_PAIRS, GLA_DV, GLA_PAIR_W), F32), pltpu.VMEM((tl, qk_w), F32)],
        compiler_params=_cparams(("parallel", "arbitrary")),
        name="gla",
    )(proj, proj, proj, proj, proj, s0, wa_pad, b_a, gn)


def _masked_softmax(s, valid):
    s = jnp.where(valid, s, -jnp.inf)
    m = jnp.max(s, axis=-1, keepdims=True)
    m = jnp.where(m > -jnp.inf, m, 0.0)
    e = jnp.exp(s - m)
    return e / jnp.maximum(jnp.sum(e, axis=-1, keepdims=True), 1e-30)


def _online_update(state, s, v):
    m, l, acc = state
    r, nq, w = s.shape
    m_new = jnp.maximum(m, jnp.max(s, axis=-1, keepdims=True))
    alpha = jnp.exp(m - m_new)
    p = jnp.exp(s - m_new)
    l = alpha * l + jnp.sum(p, axis=-1, keepdims=True)
    pv = _dot(p.reshape(r * nq, w).astype(BF16), v).reshape(r, nq, DH)
    return m_new, l, alpha * acc + pv


def _compress_tail(xflat, w1, b1, w2):
    h = _dot(xflat, w1) + b1
    h = h * _sigmoid(h)
    return _dot(h.astype(BF16), w2)


def _cmp_prompt_kernel(x0_ref, x1_ref, x2_ref, pe_ref, w1_ref, b1_ref, w2_ref, o_ref, xflat_ref):
    x_refs = (x0_ref, x1_ref, x2_ref)
    nb, seq = x0_ref.shape[0], x0_ref.shape[1]
    nblk = seq // BLK
    for j in range(BLK):
        pe_j = pe_ref[j:j + 1, :]
        for b in range(nb):
            for g in range(NSA_G):
                xj = x_refs[g][b, pl.ds(j, nblk, stride=BLK), :]
                row = (b * NSA_G + g) * nblk
                xflat_ref[row:row + nblk, j * DH:(j + 1) * DH] = (xj + pe_j).astype(BF16)
    out = _compress_tail(xflat_ref[...], w1_ref[...], b1_ref[...], w2_ref[...])
    o_ref[...] = out.reshape(nb, NSA_G, nblk, DH)


def _kv_block(branch, g, kv):
    return NSA_KV_COL // DH + branch * 2 * NSA_G + g * 2 + kv


def _cmp_prompt(proj3, pe, w1, b1, w2):
    nb, seq, _ = proj3.shape
    nblk = seq // BLK
    return pl.pallas_call(
        _cmp_prompt_kernel,
        out_shape=jax.ShapeDtypeStruct((2, nb, NSA_G, nblk, DH), F32),
        grid=(2,),
        in_specs=[
            pl.BlockSpec((nb, seq, DH), lambda kv: (0, 0, _kv_block(0, 0, kv))),
            pl.BlockSpec((nb, seq, DH), lambda kv: (0, 0, _kv_block(0, 1, kv))),
            pl.BlockSpec((nb, seq, DH), lambda kv: (0, 0, _kv_block(0, 2, kv))),
            pl.BlockSpec((None, BLK, DH), lambda kv: (kv, 0, 0)),
            pl.BlockSpec((None, BLK * DH, CMP_HID), lambda kv: (kv, 0, 0)),
            pl.BlockSpec((None, 1, CMP_HID), lambda kv: (kv, 0, 0)),
            pl.BlockSpec((None, CMP_HID, DH), lambda kv: (kv, 0, 0)),
        ],
        out_specs=pl.BlockSpec((None, nb, NSA_G, nblk, DH), lambda kv: (kv, 0, 0, 0, 0)),
        scratch_shapes=[pltpu.VMEM((nb * NSA_G * nblk, BLK * DH), BF16)],
        compiler_params=_cparams(("arbitrary",)),
        name="cmp_prompt",
    )(proj3, proj3, proj3, pe, w1, b1, w2)


def _nsa_prompt_kernel(q_ref, gl_ref, gb_ref, kc_ref, vc_ref, ks_ref, vs_ref, kw_ref, vw_ref,
                       bct_ref, bsn_ref, bw_ref, o_ref, ksb, vsb, kwb, vwb):
    g = pl.program_id(1)
    qi = pl.program_id(2)
    seq = ks_ref.shape[0]
    nblk = seq // BLK
    R = NSA_R

    @pl.when(qi == 0)
    def _():
        vsb[0:SLC_PAD, :] = jnp.zeros((SLC_PAD, DH), BF16)
        vsb[SLC_PAD:SLC_PAD + seq, :] = vs_ref[...].astype(BF16)
        ones_col = jnp.where(lax.broadcasted_iota(jnp.int32, (1, DH), 1) == 0, 1.0, 0.0)
        vwb[0:WIN_PAD, 0:DH] = jnp.zeros((WIN_PAD, DH), BF16)
        vwb[WIN_PAD:WIN_PAD + seq, 0:DH] = vw_ref[...].astype(BF16)
        vwb[:, DH:2 * DH] = jnp.broadcast_to(ones_col, (WIN_PAD + seq, DH)).astype(BF16)
        for src, dst, pad, per_block in ((ks_ref, ksb, SLC_PAD, True), (kw_ref, kwb, WIN_PAD, False)):
            dst[0:pad, 0:DH] = jnp.zeros((pad, DH), BF16)
            dst[pad:pad + seq, 0:DH] = src[...].astype(BF16)
            pos = lax.broadcasted_iota(jnp.int32, (pad + seq, 1), 0) - pad
            feat = lax.broadcasted_iota(jnp.int32, (1, DH), 1)
            masked = (feat == HALF_LANE) & (pos < 0)
            if per_block:
                masked = masked | ((pos >= 0) & (lax.shift_right_arithmetic(pos, BLK_SHIFT) == feat))
            dst[:, DH:2 * DH] = jnp.where(masked, NEG, 0.0).astype(BF16)

    q = q_ref[...] * (DH ** -0.5)
    q_all = jnp.concatenate([q[:, r * DH:(r + 1) * DH] for r in range(R)], axis=0)
    q_bf = q_all.astype(BF16)
    i_col = lax.broadcasted_iota(jnp.int32, (QBLK, 1), 0)
    t_col = qi * QBLK + i_col
    n_row = lax.broadcasted_iota(jnp.int32, (1, nblk), 1)
    row0 = pl.multiple_of(qi * QBLK, QBLK)
    lane2 = lax.broadcasted_iota(jnp.int32, (1, LANE), 1)

    s_c = _nt(q_all, kc_ref[...], HI).reshape(R, QBLK, nblk)
    tb_col = lax.shift_right_arithmetic(t_col, BLK_SHIFT)
    rel = tb_col - n_row
    bias_c = []
    for r in range(R):
        tab = bct_ref[r]
        bias_c.append(jnp.where(rel == 0, tab[:, 0:1],
                      jnp.where(rel == 1, tab[:, 1:2],
                      jnp.where(rel == 2, tab[:, 2:3], tab[:, 3:4]))))
    s_c = s_c + jnp.stack(bias_c, axis=0)
    valid_c = (t_col - (n_row * BLK + (BLK - 1))) >= 0
    p_c = _masked_softmax(s_c, valid_c[None])
    o_c = _dot(p_c.reshape(R * QBLK, nblk).astype(BF16), vc_ref[...].astype(BF16))

    imp = jnp.sum(p_c, axis=0)
    forced = (n_row == 0) | (n_row == tb_col) | (n_row == tb_col - 1)
    future = n_row * BLK > t_col
    score = jnp.where(forced, jnp.inf, jnp.where(future, -jnp.inf, imp))
    score_t = jnp.concatenate([score, jnp.full((QBLK, LANE - nblk), -jnp.inf, F32)], axis=1).T[:HALF_LANE]
    n_sub = lax.broadcasted_iota(jnp.int32, (HALF_LANE, 1), 0)
    rank_t = jnp.zeros((HALF_LANE, QBLK), F32)
    for i in range(nblk):
        cand = score_t[i:i + 1, :]
        wins_tie = jnp.where(n_sub > i, 1.0, 0.0)
        rank_t = rank_t + jnp.where(cand > score_t, 1.0, jnp.where(cand == score_t, wins_tie, 0.0))
    rank = jnp.concatenate([rank_t, jnp.zeros((LANE - HALF_LANE, QBLK), F32)], axis=0).T
    left = lane2 < HALF_LANE
    n_far = (qi * QBLK - SLC_PAD) // BLK
    dropped = rank >= float(min(TOPN, nblk))
    flag_all = jnp.where(left, jnp.where(dropped, 1.0, 0.0), jnp.where(lane2 == HALF_LANE, 1.0, 0.0))
    flag_far = jnp.where(left & (lane2 >= n_far), 1.0, flag_all)
    q_far = jnp.concatenate([q_bf, jnp.concatenate([flag_far.astype(BF16)] * R, axis=0)], axis=1)
    q_near = jnp.concatenate([q_bf, jnp.concatenate([flag_all.astype(BF16)] * R, axis=0)], axis=1)

    def far_body(kc_i, state):
        start = pl.multiple_of(SLC_PAD + kc_i * FAR_CHUNK, BLK)
        s = _nt(q_far, ksb[pl.ds(start, FAR_CHUNK), :]).reshape(R, QBLK, FAR_CHUNK)
        return _online_update(state, s, vsb[pl.ds(start, FAR_CHUNK), :])

    blk_per_chunk = FAR_CHUNK // BLK
    n_chunks = (jnp.maximum(n_far, 0) + (blk_per_chunk - 1)) // blk_per_chunk
    state = (jnp.full((R, QBLK, 1), M_FLOOR, F32), jnp.zeros((R, QBLK, 1), F32), jnp.zeros((R, QBLK, DH), F32))
    state = lax.fori_loop(0, n_chunks, far_body, state)

    s = _nt(q_near, ksb[pl.ds(row0, SLC_NEAR), :]).reshape(R, QBLK, SLC_NEAR) + bsn_ref[...]
    _, l_s, acc_s = _online_update(state, s, vsb[pl.ds(row0, SLC_NEAR), :])
    o_s = (acc_s / jnp.maximum(l_s, 1e-30)).reshape(R * QBLK, DH)

    s = _nt(q_near, kwb[pl.ds(row0, WIN_KEYS), :]).reshape(R, QBLK, WIN_KEYS) + bw_ref[...]
    e_w = jnp.exp(s - jnp.max(s, axis=-1, keepdims=True))
    ow = _dot(e_w.reshape(R * QBLK, WIN_KEYS).astype(BF16), vwb[pl.ds(row0, WIN_KEYS), :])
    o_w = ow[:, :DH] / jnp.maximum(ow[:, DH:DH + 1], 1e-30)

    gates = _sigmoid(gl_ref[...] + gb_ref[...])
    src = lax.broadcasted_iota(jnp.int32, (LANE, LANE), 0)
    dst = lax.broadcasted_iota(jnp.int32, (LANE, LANE), 1)
    pick = jnp.where((src == g * (3 * R) + dst) & (dst < 3 * R), 1.0, 0.0)
    gsel = _dot(gates, pick, HI)
    for r in range(R):
        rows = slice(r * QBLK, (r + 1) * QBLK)
        o = (gsel[:, 3 * r:3 * r + 1] * o_c[rows] + gsel[:, 3 * r + 1:3 * r + 2] * o_s[rows]
             + gsel[:, 3 * r + 2:3 * r + 3] * o_w[rows])
        o_ref[:, r * DH:(r + 1) * DH] = o.astype(o_ref.dtype)


def _nsa_prompt(proj2, cmp_kv, gate_b_pad, bct, bsn, bw, nb, seq):
    nq = seq // QBLK
    nblk = seq // BLK
    assert nblk <= HALF_LANE, "mask features of the selected branch hold at most 64 key blocks"
    assert seq % FAR_CHUNK == 0
    proj3 = proj2.reshape(nb, seq, NSA_N)

    def kv_spec(branch, kv):
        return pl.BlockSpec((None, seq, DH), lambda b, g, qi: (b, 0, _kv_block(branch, g, kv)))

    return pl.pallas_call(
        _nsa_prompt_kernel,
        out_shape=jax.ShapeDtypeStruct((nb * seq, TOK_W), BF16),
        grid=(nb, NSA_G, nq),
        in_specs=[
            pl.BlockSpec((QBLK, NSA_R * DH), lambda b, g, qi: (b * nq + qi, g)),
            pl.BlockSpec((QBLK, LANE), lambda b, g, qi: (b * nq + qi, NSA_GATE_COL // LANE)),
            pl.BlockSpec((1, LANE), lambda b, g, qi: (0, 0)),
            pl.BlockSpec((None, None, None, nblk, DH), lambda b, g, qi: (0, b, g, 0, 0)),
            pl.BlockSpec((None, None, None, nblk, DH), lambda b, g, qi: (1, b, g, 0, 0)),
            kv_spec(1, 0), kv_spec(1, 1), kv_spec(2, 0), kv_spec(2, 1),
            pl.BlockSpec((None, NSA_R, QBLK, LANE), lambda b, g, qi: (g, 0, 0, 0)),
            pl.BlockSpec((None, NSA_R, QBLK, SLC_NEAR), lambda b, g, qi: (g, 0, 0, 0)),
            pl.BlockSpec((None, NSA_R, QBLK, WIN_KEYS), lambda b, g, qi: (g, 0, 0, 0)),
        ],
        out_specs=pl.BlockSpec((QBLK, NSA_R * DH), lambda b, g, qi: (b * nq + qi, g)),
        scratch_shapes=[pltpu.VMEM((SLC_PAD + seq, 2 * DH), BF16), pltpu.VMEM((SLC_PAD + seq, DH), BF16),
                        pltpu.VMEM((WIN_PAD + seq, 2 * DH), BF16), pltpu.VMEM((WIN_PAD + seq, 2 * DH), BF16)],
        compiler_params=_cparams(("parallel", "parallel", "arbitrary")),
        name="nsa_prompt",
    )(proj2, proj2, gate_b_pad, cmp_kv, cmp_kv, proj3, proj3, proj3, proj3, bct, bsn, bw)


SUB_PAGES = 8
RING = 4
ROW_W = 2 * NSA_G
PAGE_ROWS = PAGE * ROW_W


def _linear_cache(cache):
    return jnp.transpose(cache, (0, 1, 3, 2, 4)).reshape(-1, DH)


def _cmp_decode_kernel(pt_ref, cache_ref, new_ref, pe_ref, w1_ref, b1_ref, w2_ref, o_ref, last_ref,
                       buf, sem, xflat_ref, xlast_ref, *, n_sub):
    b = pl.program_id(0)
    total = pl.num_programs(0) * n_sub
    sub_blk = SUB_PAGES * PAGE // BLK
    seq_blk = n_sub * sub_blk

    def page_copy(s, p):
        page = pt_ref[s // n_sub, (s % n_sub) * SUB_PAGES + p]
        return pltpu.make_async_copy(cache_ref.at[pl.ds(pl.multiple_of(page * PAGE_ROWS, PAGE_ROWS), PAGE_ROWS), :],
                                     buf.at[s % RING, pl.ds(p * PAGE_ROWS, PAGE_ROWS), :], sem.at[s % RING])

    def start_sub(s):
        for p in range(SUB_PAGES):
            page_copy(s, p).start()

    @pl.when(b == 0)
    def _():
        for s in range(RING):
            start_sub(s)

    def body(i, c):
        s = b * n_sub + i
        for p in range(SUB_PAGES):
            page_copy(s, p).wait()
        slot = s % RING
        row0 = pl.multiple_of(i * sub_blk, sub_blk)
        by_row = jnp.swapaxes(buf[slot].reshape(sub_blk, BLK * ROW_W, DH), 0, 1)
        for kv in range(2):
            for j in range(BLK):
                pe_j = pe_ref[kv, j:j + 1, :]
                for g in range(NSA_G):
                    xj = by_row[j * ROW_W + g * 2 + kv]
                    xflat_ref[kv, pl.ds(g * seq_blk + row0, sub_blk), j * DH:(j + 1) * DH] = (xj + pe_j).astype(BF16)

        @pl.when(s + RING < total)
        def _():
            start_sub(s + RING)

        return c

    lax.fori_loop(0, n_sub, body, 0)
    for kv in range(2):
        out = _compress_tail(xflat_ref[kv], w1_ref[kv], b1_ref[kv], w2_ref[kv])
        o_ref[kv] = out.reshape(NSA_G, seq_blk, DH)

    @pl.when(b == 0)
    def _():
        rows = new_ref.shape[1]
        for kv in range(2):
            for j in range(BLK):
                pe_j = jnp.broadcast_to(pe_ref[kv, j:j + 1, :], (rows, DH))
                xj = new_ref[kv] + pe_j if j == 0 else pe_j
                xlast_ref[:, j * DH:(j + 1) * DH] = xj.astype(BF16)
            last_ref[kv] = _compress_tail(xlast_ref[...], w1_ref[kv], b1_ref[kv], w2_ref[kv])


def _cmp_decode(page_table, cache, new_rows, pe, w1, b1, w2):
    nd, n_pages = page_table.shape
    n_sub = n_pages // SUB_PAGES
    seq_blk = n_pages * PAGE // BLK
    rows = new_rows.shape[1]

    def whole(shape):
        return pl.BlockSpec(shape, lambda b, pt: (0,) * len(shape), pipeline_mode=pl.Buffered(1))

    grid_spec = pltpu.PrefetchScalarGridSpec(
        num_scalar_prefetch=1,
        grid=(nd,),
        in_specs=[
            pl.BlockSpec(memory_space=pl.ANY),
            whole(new_rows.shape), whole(pe.shape), whole(w1.shape), whole(b1.shape), whole(w2.shape),
        ],
        out_specs=(pl.BlockSpec((2, None, NSA_G, seq_blk, DH), lambda b, pt: (0, b, 0, 0, 0)),
                   pl.BlockSpec((2, rows, DH), lambda b, pt: (0, 0, 0))),
        scratch_shapes=[pltpu.VMEM((RING, SUB_PAGES * PAGE_ROWS, DH), F32),
                        pltpu.SemaphoreType.DMA((RING,)),
                        pltpu.VMEM((2, NSA_G * seq_blk, BLK * DH), BF16),
                        pltpu.VMEM((rows, BLK * DH), BF16)],
    )
    return pl.pallas_call(
        functools.partial(_cmp_decode_kernel, n_sub=n_sub),
        out_shape=(jax.ShapeDtypeStruct((2, nd, NSA_G, seq_blk, DH), F32),
                   jax.ShapeDtypeStruct((2, rows, DH), F32)),
        grid_spec=grid_spec,
        compiler_params=_cparams(("arbitrary",)),
        name="cmp_decode",
    )(page_table, cache, new_rows, pe, w1, b1, w2)


SEL_LANES = 384


def _sel_decode_kernel(q_ref, kc_ref, vc_ref, last_ref, bias_ref, oc_ref, sel_ref, *, t_pos):
    b = pl.program_id(0)
    n_past = kc_ref.shape[1]
    n_blocks = n_past + 1
    n_lane = lax.broadcasted_iota(jnp.int32, (1, SEL_LANES), 1)
    n_lane_f = n_lane.astype(F32)
    head_row = lax.broadcasted_iota(jnp.int32, (8, 1), 0) < NSA_R
    tb = t_pos // BLK
    for g in range(NSA_G):
        q = q_ref[g] * (DH ** -0.5)
        bias = bias_ref[g]
        s_p = _nt(q, kc_ref[g], HI) + bias[:, :n_past]
        row = b * NSA_G + g
        k_last = last_ref[0, pl.ds(row, 1), :]
        v_last = last_ref[1, pl.ds(row, 1), :]
        s_l = jnp.sum(q * k_last, axis=-1, keepdims=True) + bias[:, n_past:n_past + 1]
        valid_p = (t_pos - (n_lane[:, :n_past] * BLK + (BLK - 1))) >= 0
        valid_l = (t_pos - (n_past * BLK + (BLK - 1))) >= 0
        s_p = jnp.where(valid_p, s_p, -jnp.inf)
        s_l = jnp.where(valid_l, s_l, -jnp.inf)
        m = jnp.maximum(jnp.max(s_p, axis=-1, keepdims=True), s_l)
        m = jnp.where(m > -jnp.inf, m, 0.0)
        e_p = jnp.exp(s_p - m)
        e_l = jnp.exp(s_l - m)
        den = jnp.maximum(jnp.sum(e_p, axis=-1, keepdims=True) + e_l, 1e-30)
        p_p = e_p / den
        p_l = e_l / den
        oc_ref[g] = _dot(p_p, vc_ref[g], HI) + p_l * v_last
        imp_p = jnp.sum(jnp.where(head_row, p_p, 0.0), axis=0, keepdims=True)
        imp_l = jnp.sum(jnp.where(head_row, p_l, 0.0), axis=0, keepdims=True)
        imp = jnp.concatenate([imp_p, jnp.broadcast_to(imp_l, (1, SEL_LANES - n_past))], axis=1)
        forced = (n_lane == 0) | (n_lane == tb) | (n_lane == tb - 1)
        future = n_lane * BLK > t_pos
        score = jnp.where(forced, jnp.inf, jnp.where(future, -jnp.inf, imp))
        cand = n_lane < n_blocks
        sel = jnp.zeros((1, LANE), jnp.int32)
        k_lane = lax.broadcasted_iota(jnp.int32, (1, LANE), 1)
        for k in range(min(TOPN, n_blocks)):
            best = jnp.max(jnp.where(cand, score, -jnp.inf), axis=-1, keepdims=True)
            idx_f = jnp.min(jnp.where(cand & (score == best), n_lane_f, float(SEL_LANES)), axis=-1, keepdims=True)
            idx = idx_f.astype(jnp.int32)
            sel = jnp.where(k_lane == k, idx, sel)
            cand = cand & (n_lane != idx)
        sel_ref[g] = jnp.broadcast_to(sel, (8, LANE))


def _sel_decode(q8, cmp_kv, cmp_last, bias_cd, t_pos):
    nd = q8.shape[0]
    n_past = cmp_kv.shape[3]
    return pl.pallas_call(
        functools.partial(_sel_decode_kernel, t_pos=t_pos),
        out_shape=(jax.ShapeDtypeStruct((nd, NSA_G, 8, DH), F32),
                   jax.ShapeDtypeStruct((nd, NSA_G, 8, LANE), jnp.int32)),
        grid=(nd,),
        in_specs=[
            pl.BlockSpec((None, NSA_G, 8, DH), lambda b: (b, 0, 0, 0)),
            pl.BlockSpec((None, None, NSA_G, n_past, DH), lambda b: (0, b, 0, 0, 0)),
            pl.BlockSpec((None, None, NSA_G, n_past, DH), lambda b: (1, b, 0, 0, 0)),
            pl.BlockSpec(cmp_last.shape, lambda b: (0, 0, 0)),
            pl.BlockSpec(bias_cd.shape, lambda b: (0, 0, 0)),
        ],
        out_specs=(pl.BlockSpec((None, NSA_G, 8, DH), lambda b: (b, 0, 0, 0)),
                   pl.BlockSpec((None, NSA_G, 8, LANE), lambda b: (b, 0, 0, 0))),
        compiler_params=_cparams(("parallel",)),
        name="sel_decode",
    )(q8, cmp_kv, cmp_kv, cmp_last, bias_cd)


HALF_ROWS = BLK * ROW_W


def _slc_decode_kernel(sel_ref, pt_ref, q_ref, oc_ref, gl_ref, gb_ref, slc_ref, bias_ref, nk_ref, nv_ref,
                       win_ref, nwk_ref, nwv_ref, bw_ref, o_ref, gbuf, sem, *, t_pos, n_past):
    b = pl.program_id(0)
    n_win = win_ref.shape[0] // ROW_W

    def block_copy(g, k):
        n = jnp.minimum(sel_ref[(b * NSA_G + g) * TOPN + k], n_past - 1)
        half = pt_ref[b, n // 2] * 2 + n % 2
        return pltpu.make_async_copy(slc_ref.at[pl.ds(pl.multiple_of(half * HALF_ROWS, HALF_ROWS), HALF_ROWS), :],
                                     gbuf.at[g * TOPN + k], sem)

    for g in range(NSA_G):
        for k in range(TOPN):
            block_copy(g, k).start()

    c = lax.broadcasted_iota(jnp.int32, (1, n_win), 1)
    dist = n_win - c
    valid = (dist >= 0) & (dist <= WINDOW) & (t_pos - dist >= 0)
    qs, o_w = [], []
    for g in range(NSA_G):
        q = q_ref[g] * (DH ** -0.5)
        q_bf = q.astype(BF16)
        qs.append((q, q_bf))
        kw = win_ref[pl.ds(2 * g, n_win, stride=ROW_W), :].astype(BF16)
        vw = win_ref[pl.ds(2 * g + 1, n_win, stride=ROW_W), :].astype(BF16)
        bw = bw_ref[g]
        s_w = jnp.where(valid, _nt(q_bf, kw) + bw[:, :n_win], -jnp.inf)
        s_n = jnp.sum(q * nwk_ref[g, 0:1, :], axis=-1, keepdims=True) + bw[:, n_win:n_win + 1]
        m_w = jnp.maximum(jnp.max(s_w, axis=-1, keepdims=True), s_n)
        e_w = jnp.exp(s_w - m_w)
        e_n = jnp.exp(s_n - m_w)
        den = jnp.maximum(jnp.sum(e_w, axis=-1, keepdims=True) + e_n, 1e-30)
        o_w.append((_dot(e_w.astype(BF16), vw) + e_n * nwv_ref[g, 0:1, :]) / den)

    for g in range(NSA_G):
        for k in range(TOPN):
            block_copy(g, k).wait()

    row0 = lax.broadcasted_iota(jnp.int32, (BLK, 1), 0) == 0
    j_row = lax.broadcasted_iota(jnp.int32, (1, BLK), 1)
    for g in range(NSA_G):
        q, q_bf = qs[g]
        scores, values = [], []
        for k in range(TOPN):
            n = sel_ref[(b * NSA_G + g) * TOPN + k]
            is_new = (n == n_past) & row0
            kb = jnp.where(is_new, nk_ref[g, 0:1, :], gbuf[g * TOPN + k, pl.ds(2 * g, BLK, stride=ROW_W), :])
            vb = jnp.where(is_new, nv_ref[g, 0:1, :], gbuf[g * TOPN + k, pl.ds(2 * g + 1, BLK, stride=ROW_W), :])
            s = _nt(q_bf, kb.astype(BF16)) + bias_ref[g, n]
            scores.append(jnp.where(t_pos - (n * BLK + j_row) >= 0, s, -jnp.inf))
            values.append(vb.astype(BF16))
        m = functools.reduce(jnp.maximum, [jnp.max(s, axis=-1, keepdims=True) for s in scores])
        m = jnp.where(m > -jnp.inf, m, 0.0)
        probs = [jnp.exp(s - m) for s in scores]
        l = functools.reduce(jnp.add, [jnp.sum(p, axis=-1, keepdims=True) for p in probs])
        acc = functools.reduce(jnp.add, [_dot(p.astype(BF16), v) for p, v in zip(probs, values)])
        o_s = acc / jnp.maximum(l, 1e-30)
        gates = _sigmoid(gl_ref[g] + gb_ref[g])
        o_ref[g] = gates[:, 0:1] * oc_ref[g] + gates[:, 1:2] * o_s + gates[:, 2:3] * o_w[g]


def _slc_decode(sel_flat, page_table, q8, o_c, gate_l, gate_b, slc_cache, bias_sd, new_k, new_v,
                win_cache, new_wk, new_wv, bias_wd, t_pos):
    nd = q8.shape[0]
    n_past = page_table.shape[1] * (PAGE // BLK)
    win_rows = win_cache.shape[0] // nd

    def per_b(b, sel, pt):
        return (b, 0, 0, 0)

    def whole(shape):
        return pl.BlockSpec(shape, lambda b, sel, pt: (0,) * len(shape), pipeline_mode=pl.Buffered(1))

    b_spec = pl.BlockSpec((None, NSA_G, 8, DH), per_b)
    grid_spec = pltpu.PrefetchScalarGridSpec(
        num_scalar_prefetch=2,
        grid=(nd,),
        in_specs=[
            b_spec, b_spec, b_spec, whole(gate_b.shape),
            pl.BlockSpec(memory_space=pl.ANY),
            whole(bias_sd.shape),
            b_spec, b_spec,
            pl.BlockSpec((win_rows, DH), lambda b, sel, pt: (b, 0)),
            b_spec, b_spec,
            whole(bias_wd.shape),
        ],
        out_specs=b_spec,
        scratch_shapes=[pltpu.VMEM((NSA_G * TOPN, HALF_ROWS, DH), F32), pltpu.SemaphoreType.DMA(())],
    )
    return pl.pallas_call(
        functools.partial(_slc_decode_kernel, t_pos=t_pos, n_past=n_past),
        out_shape=jax.ShapeDtypeStruct((nd, NSA_G, 8, DH), F32),
        grid_spec=grid_spec,
        compiler_params=_cparams(("arbitrary",)),
        name="slc_decode",
    )(sel_flat, page_table, q8, o_c, gate_l, gate_b, slc_cache, bias_sd, new_k, new_v,
      win_cache, new_wk, new_wv, bias_wd)


def _t5_bucket(dist):
    n = jnp.maximum(dist, 0)
    nf = jnp.maximum(n, REL_MAX_EXACT).astype(F32)
    large = REL_MAX_EXACT + (jnp.log(nf / REL_MAX_EXACT) / math.log(REL_MAX_DIST / REL_MAX_EXACT)
                             * (REL_BUCKETS - REL_MAX_EXACT)).astype(jnp.int32)
    return jnp.where(n < REL_MAX_EXACT, n, jnp.minimum(large, REL_BUCKETS - 1))


def _bias_table(rel_bias, dist):
    onehot = (_t5_bucket(dist)[..., None] == jnp.arange(REL_BUCKETS)).astype(F32)
    b = jnp.einsum("...k,kh->...h", onehot, rel_bias, precision=HI)
    b = jnp.moveaxis(b, -1, 0)
    return b.reshape((NSA_G, NSA_R) + dist.shape)


def _prompt_bias_tables(rel_bias):
    i = jnp.arange(QBLK)[:, None]
    rel = jnp.arange(LANE)[None, :]
    bct = _bias_table(rel_bias, BLK * (rel - 1) + i % BLK + 1)
    d_near = SLC_PAD + i - jnp.arange(SLC_NEAR)[None, :]
    far = rel_bias[REL_BUCKETS - 1].reshape(NSA_G, NSA_R, 1, 1)
    bsn = _bias_table(rel_bias, d_near) - far + jnp.where(d_near >= 0, 0.0, NEG)
    d_win = WIN_PAD + i - jnp.arange(WIN_KEYS)[None, :]
    bw = _bias_table(rel_bias, d_win) + jnp.where((d_win >= 0) & (d_win <= WINDOW), 0.0, NEG)
    return bct, bsn, bw


def _pad_rows(a, rows, axis):
    pad = [(0, 0)] * a.ndim
    pad[axis] = (0, rows - a.shape[axis])
    return jnp.pad(a, pad)


def _decode_bias_tables(rel_bias, t_pos, n_past, n_win):
    n = jnp.arange(SEL_LANES)
    bias_cd = _pad_rows(_bias_table(rel_bias, t_pos - (n * BLK + BLK - 1)), 8, 1)
    tok = jnp.arange(n_past + 1)[:, None] * BLK + jnp.arange(BLK)[None, :]
    bias_sd = _pad_rows(jnp.swapaxes(_bias_table(rel_bias, t_pos - tok), 1, 2), 8, 2)
    c = jnp.arange(n_win + LANE)
    bias_wd = _pad_rows(_bias_table(rel_bias, n_win - c), 8, 1)
    return bias_cd, bias_sd, bias_wd


def kernel(x_prompt, x_sample, mem_prompt, cache_cmp_kv, cache_slc_kv, state_win_kv, state_gla, cache_mem_kv,
           page_table, norm_g, w_ffn_gate, w_ffn_up, w_ffn_down, w_in_gla, w_in_nsa, w_out, mem_norm_g, w_mem_kv,
           w_gla_a2, b_gla_a, gla_onorm_g, nsa_gate_b, cmp_pe, cmp_w1, cmp_b1, cmp_w2, rel_bias):
    nb, seq, _ = x_prompt.shape
    nd = x_sample.shape[0]
    depth = norm_g.shape[0]
    n_pages = page_table.shape[1]
    past_len = n_pages * PAGE
    n_past = past_len // BLK
    n_win = state_win_kv.shape[2]
    sr = SAMPLE_ROWS

    xp = x_prompt.reshape(nb * seq, D_MODEL)
    xs = _pad_rows(x_sample.reshape(nd, D_MODEL), sr, 0)
    mem_x = mem_prompt.reshape(nb * MEM_LEN, D_MODEL)

    w_o = w_out.astype(BF16)
    w_mkv = w_mem_kv.astype(BF16)
    def zero_rows(like, rows):
        return jnp.zeros((like.shape[0], rows, D_MODEL), BF16)

    wt_gla = jnp.swapaxes(w_in_gla, 1, 2).astype(BF16)
    tail_gla = jnp.concatenate(
        [wt_gla[:, GLA_MAIN + GLA_RANK:], wt_gla[:, GLA_MAIN:GLA_MAIN + GLA_RANK],
         zero_rows(wt_gla, TN_PROJ - MEM_W - GLA_RANK)], axis=1)
    n_gate = 3 * NSA_HEADS
    wt_nsa = jnp.swapaxes(w_in_nsa, 1, 2).astype(BF16)
    tail_nsa = jnp.concatenate(
        [wt_nsa[:, NSA_GATE_COL:NSA_GATE_COL + n_gate], zero_rows(wt_nsa, TN_PROJ - MEM_W - n_gate),
         wt_nsa[:, NSA_GATE_COL + n_gate:]], axis=1)
    wa_pad = _pad_rows(w_gla_a2, LANE, 1)
    w1 = cmp_w1.astype(BF16)
    w2 = cmp_w2.astype(BF16)

    def ffn_both(x_p, x_s, i, j):
        g1, g2 = norm_g[i, 4 * j][None], norm_g[i, 4 * j + 1][None]
        x_s, wg, wu, wd = _ffn_half(x_s, g1, w_ffn_gate, w_ffn_up, w_ffn_down, g2, TM_SAMPLE, f32_weights_at=(i, j))
        return _ffn_half(x_p, g1, wg, wu, wd, g2, TM_FFN), x_s

    def every(a, step):
        return a.reshape(nd, step, a.shape[-1])[:, 0]

    def per_seq(a):
        return _pad_rows(a[:, None, :], sr, 1).reshape(nd * sr, a.shape[-1])

    outs = dict(gla_p=[], gla_s=[], cmp_p=[], cmp_s=[], slc_p=[], slc_s=[], win_p=[], win_s=[], mem_p=[])
    for i in range(depth):
        li = i // 2
        mem_kv_p = _norm_matmul(mem_x, mem_norm_g[i][None], w_mkv, i, TM_MEMKV, MEM_W)
        outs["mem_p"].append(mem_kv_p.reshape(nb, MEM_LEN, 2, N_MEM_HEADS, MEM_HEAD_DIM))
        mem_kv_p = mem_kv_p.reshape(nb, MEM_LEN, 2 * MEM_W)
        mem_kv_s = cache_mem_kv[i].reshape(nd, MEM_LEN, 2 * MEM_W)
        xp, xs = ffn_both(xp, xs, i, 0)
        g_mix = norm_g[i, 2][None]
        if i % 2 == 0:
            proj_p = _in_proj(xp, g_mix, wt_gla, GLA_MAIN, tail_gla, li, TM_PROJ)
            proj_s = _in_proj(xs, g_mix, wt_gla, GLA_MAIN, tail_gla, li, TM_SAMPLE)
            mem_col = GLA_MAIN // MEM_W
            b_a = b_gla_a[li][None]
            gn = gla_onorm_g[li][None]
            s0 = jnp.zeros((nb, GLA_HEADS, GLA_DK, GLA_DV), F32)
            tok_p, sp = _gla(proj_p, s0, wa_pad[li], b_a, gn, seq, TL_GLA, GLA_CHUNK, seq)
            tok_s, ss = _gla(per_seq(proj_s[:nd]), state_gla[li], wa_pad[li], b_a, gn, sr, sr, sr, 1)
            tok_s = _pad_rows(every(tok_s, sr), sr, 0)
            outs["gla_p"].append(sp)
            outs["gla_s"].append(ss)
        else:
            assert 2 * KV_W == TN_PROJ and NSA_KV_COL % TN_PROJ == 0
            kv_tiles = (NSA_KV_COL // TN_PROJ, NSA_GATE_COL // TN_PROJ)
            proj_p, *kv_lin = _in_proj(xp, g_mix, wt_nsa, NSA_GATE_COL, tail_nsa, li, TM_PROJ_KV, kv_tiles,
                                       linear_kv=True)
            proj_s = _in_proj(xs, g_mix, wt_nsa, NSA_GATE_COL, tail_nsa, li, TM_SAMPLE, kv_tiles)
            mem_col = NSA_MEM_COL // MEM_W
            b1 = cmp_b1[li][:, None, :]
            gate_b = nsa_gate_b[li]
            bct, bsn, bw = _prompt_bias_tables(rel_bias)
            proj3 = proj_p.reshape(nb, seq, NSA_N)
            cmp_kv = _cmp_prompt(proj3, cmp_pe[li], w1[li], b1, w2[li])
            tok_p = _nsa_prompt(proj_p, cmp_kv, _pad_rows(gate_b[None], LANE, 1), bct, bsn, bw, nb, seq)
            def kv_rows(branch):
                return jnp.swapaxes(kv_lin[branch].reshape(nb, seq, NSA_G, 2, DH), 2, 3)

            outs["cmp_p"].append(kv_rows(0).reshape(nb, seq // PAGE, PAGE, 2, NSA_G, DH))
            outs["slc_p"].append(kv_rows(1).reshape(nb, seq // PAGE, PAGE, 2, NSA_G, DH))
            outs["win_p"].append(kv_rows(2)[:, seq - n_win:])
            t_pos = past_len
            bias_cd, bias_sd, bias_wd = _decode_bias_tables(rel_bias, t_pos, n_past, n_win)
            kv_s = jnp.swapaxes(proj_s[:nd, NSA_KV_COL:NSA_GATE_COL].reshape(nd, 3, NSA_G, 2, DH), 2, 3)
            outs["cmp_s"].append(kv_s[:, None, 0])
            outs["slc_s"].append(kv_s[:, None, 1])
            outs["win_s"].append(jnp.concatenate([state_win_kv[li][:, 1:], kv_s[:, None, 2]], axis=1))
            q8 = _pad_rows(proj_s[:nd, :TOK_W].reshape(nd, NSA_G, NSA_R, DH), 8, 2)
            new_cmp = _pad_rows(jnp.moveaxis(kv_s[:, 0], 1, 0).reshape(2, nd * NSA_G, DH), 32, 1)
            cmp_kv_s, cmp_last = _cmp_decode(page_table, _linear_cache(cache_cmp_kv[li]), new_cmp,
                                             cmp_pe[li], w1[li], b1, w2[li])
            o_c, sel = _sel_decode(q8, cmp_kv_s, cmp_last, bias_cd, t_pos)
            sel_flat = sel[:, :, 0, :TOPN].reshape(-1)

            def row8(a):
                return _pad_rows(a[:, :, None, :], 8, 2)

            gate_l = _pad_rows(_pad_rows(proj_s[:nd, NSA_GATE_COL:NSA_GATE_COL + n_gate].reshape(nd, NSA_G, NSA_R, 3),
                                         8, 2), LANE, 3)
            gate_b8 = _pad_rows(_pad_rows(gate_b.reshape(NSA_G, NSA_R, 3), 8, 1), LANE, 2)
            tok_s = _slc_decode(sel_flat, page_table, q8, o_c, gate_l, gate_b8,
                                _linear_cache(cache_slc_kv[li]), bias_sd,
                                row8(kv_s[:, 1, 0]), row8(kv_s[:, 1, 1]),
                                _linear_cache(state_win_kv[li]),
                                row8(kv_s[:, 2, 0]), row8(kv_s[:, 2, 1]), bias_wd, t_pos)
            tok_s = _pad_rows(tok_s[:, :, :NSA_R].reshape(nd, TOK_W), sr, 0).astype(BF16)
        mem_o_p = _mem_attn(proj_p, mem_col, mem_kv_p, seq, TM_MEM_ATTN)
        q_s = jnp.broadcast_to(proj_s[:nd, None, mem_col * MEM_W:(mem_col + 1) * MEM_W],
                               (nd, sr, MEM_W)).reshape(nd * sr, MEM_W)
        mem_o_s = _pad_rows(every(_mem_attn(q_s, 0, mem_kv_s, sr, sr), sr), sr, 0)
        xp = _out_proj(xp, tok_p, mem_o_p, w_o, i, norm_g[i, 3][None], TM_OUT)
        xs = _out_proj(xs, tok_s, mem_o_s, w_o, i, norm_g[i, 3][None], sr)
        xp, xs = ffn_both(xp, xs, i, 1)

    y_prompt = xp.reshape(nb, seq, D_MODEL)
    y_sample = xs[:nd].reshape(nd, 1, D_MODEL)
    st = lambda k: jnp.stack(outs[k])
    return (y_prompt, y_sample, st("gla_p"), st("cmp_p"), st("slc_p"), st("win_p"), st("mem_p"),
            st("gla_s"), st("cmp_s"), st("slc_s"), st("win_s"))
```

```python
import functools
import math

import jax
import jax.numpy as jnp
from jax import lax
from jax.experimental import pallas as pl
from jax.experimental.pallas import tpu as pltpu

F32 = jnp.float32
BF16 = jnp.bfloat16
HI = lax.Precision.HIGHEST

D_MODEL = 2048
D_FF = 5632
EPS = 1e-6
MEM_LEN = 256
N_MEM_HEADS = 4
MEM_HEAD_DIM = 128
MEM_W = N_MEM_HEADS * MEM_HEAD_DIM
TOK_W = D_MODEL - MEM_W
GLA_HEADS = 4
GLA_DV = TOK_W // GLA_HEADS
GLA_DK = GLA_DV // 2
GLA_RANK = 16
GLA_TAU = 16.0
GLA_CHUNK = 64
GLA_PAIR_W = 2 * GLA_DK
DH = 128
NSA_HEADS = TOK_W // DH
NSA_G = 3
NSA_R = NSA_HEADS // NSA_G
BLK = 64
TOPN = 16
WINDOW = 512
CMP_HID = 256
QBLK = 256
KV_W = NSA_G * DH
ROW_W = 2 * NSA_G
REL_BUCKETS = 32
REL_MAX_EXACT = 16
REL_MAX_DIST = 128
PAGE = 128
LANE = 128
HALF_LANE = LANE // 2
BLK_SHIFT = BLK.bit_length() - 1
VMEM_LIMIT = 56 * 1024 * 1024

TN_PROJ = 768
GLA_MAIN = 2 * GLA_HEADS * GLA_DK + 2 * TOK_W
GLA_A_COL = GLA_MAIN + MEM_W
GLA_N = GLA_MAIN + TN_PROJ
NSA_KV_COL = TOK_W
NSA_GATE_COL = NSA_KV_COL + 6 * KV_W
NSA_MEM_COL = NSA_GATE_COL + TN_PROJ - MEM_W
NSA_N = NSA_GATE_COL + TN_PROJ
SAMPLE_ROWS = 16
TM_SAMPLE = SAMPLE_ROWS
TM_FFN = 512
TM_PROJ = 1024
TM_PROJ_KV = 512
TM_MEMKV = 512
TM_OUT = 512
TM_MEM_ATTN = 512
TL_GLA = 512
SLC_PAD = 256
WIN_PAD = WINDOW
WIN_KEYS = WIN_PAD + QBLK
SLC_NEAR = SLC_PAD + QBLK
FAR_CHUNK = 1024
NEG = -1e30
M_FLOOR = -1e29


def _cparams(sem, vmem=VMEM_LIMIT):
    return pltpu.CompilerParams(dimension_semantics=sem, vmem_limit_bytes=vmem)


def _sigmoid(x):
    return 1.0 / (1.0 + jnp.exp(-x))


def _rms(x, g):
    ms = jnp.mean(x * x, axis=-1, keepdims=True)
    return x * lax.rsqrt(ms + EPS) * g


def _nt(a, b, precision=None):
    return lax.dot_general(a, b, (((1,), (1,)), ((), ())), precision=precision,
                           preferred_element_type=F32)


def _tn(a, b, precision=None):
    return lax.dot_general(a, b, (((0,), (0,)), ((), ())), precision=precision,
                           preferred_element_type=F32)


def _dot(a, b, precision=None):
    return jnp.dot(a, b, precision=precision, preferred_element_type=F32)


def _split3(x):
    hi = x.astype(BF16)
    r1 = x - hi.astype(F32)
    mid = r1.astype(BF16)
    lo = (r1 - mid.astype(F32)).astype(BF16)
    return hi, mid, lo


def _ffn_kernel(x_ref, g1_ref, wg_ref, wu_ref, wd_ref, g2_ref, o_ref, *rest):
    bf_out, (xn_ref, acc_ref) = rest[:-2], rest[-2:]
    j = pl.program_id(1)

    @pl.when(j == 0)
    def _():
        xn_ref[...] = _rms(x_ref[...], g1_ref[...]).astype(BF16)
        acc_ref[...] = jnp.zeros_like(acc_ref)

    wg, wu, wd = (w_ref[...].astype(BF16) for w_ref in (wg_ref, wu_ref, wd_ref))
    for out_ref, w in zip(bf_out, (wg, wu, wd)):
        out_ref[...] = w
    xn = xn_ref[...]
    gate = _dot(xn, wg)
    up = _dot(xn, wu)
    h = (gate * _sigmoid(gate) * up).astype(BF16)
    acc_ref[...] += _dot(h, wd)

    @pl.when(j == pl.num_programs(1) - 1)
    def _():
        o_ref[...] = x_ref[...] + 0.5 * _rms(acc_ref[...], g2_ref[...])


def _ffn_half(x, g1, wg, wu, wd, g2, tm, tf=512, f32_weights_at=None):
    m = x.shape[0]
    y_shape = jax.ShapeDtypeStruct((m, D_MODEL), F32)
    y_spec = pl.BlockSpec((tm, D_MODEL), lambda i, j: (i, 0))
    col_spec = pl.BlockSpec((D_MODEL, tf), lambda i, j: (0, j))
    row_spec = pl.BlockSpec((tf, D_MODEL), lambda i, j: (j, 0))
    if f32_weights_at is None:
        w_specs = [col_spec, col_spec, row_spec]
        out_shape, out_specs = y_shape, y_spec
    else:
        assert m == tm, "the bf16 copies are written once, by a single row tile"
        layer, half = f32_weights_at
        w_specs = [pl.BlockSpec((None, None, D_MODEL, tf), lambda i, j: (layer, half, 0, j)),
                   pl.BlockSpec((None, None, D_MODEL, tf), lambda i, j: (layer, half, 0, j)),
                   pl.BlockSpec((None, None, tf, D_MODEL), lambda i, j: (layer, half, j, 0))]
        out_shape = (y_shape, jax.ShapeDtypeStruct((D_MODEL, D_FF), BF16),
                     jax.ShapeDtypeStruct((D_MODEL, D_FF), BF16), jax.ShapeDtypeStruct((D_FF, D_MODEL), BF16))
        out_specs = (y_spec, col_spec, col_spec, row_spec)
    return pl.pallas_call(
        _ffn_kernel,
        out_shape=out_shape,
        grid=(m // tm, D_FF // tf),
        in_specs=[y_spec, pl.BlockSpec((1, D_MODEL), lambda i, j: (0, 0))] + w_specs
                 + [pl.BlockSpec((1, D_MODEL), lambda i, j: (0, 0))],
        out_specs=out_specs,
        scratch_shapes=[pltpu.VMEM((tm, D_MODEL), BF16), pltpu.VMEM((tm, D_MODEL), F32)],
        compiler_params=_cparams(("parallel", "arbitrary")),
        name="ffn_half",
    )(x, g1, wg, wu, wd, g2)


def _norm_matmul_kernel(x_ref, g_ref, w_ref, o_ref, xn_ref):
    @pl.when(pl.program_id(1) == 0)
    def _():
        xn_ref[...] = _rms(x_ref[...], g_ref[...]).astype(BF16)

    o_ref[...] = _dot(xn_ref[...], w_ref[...])


def _norm_matmul(x, g, w, layer, tm, tn):
    m, n = x.shape[0], w.shape[2]
    return pl.pallas_call(
        _norm_matmul_kernel,
        out_shape=jax.ShapeDtypeStruct((m, n), F32),
        grid=(m // tm, n // tn),
        in_specs=[
            pl.BlockSpec((tm, D_MODEL), lambda i, j: (i, 0)),
            pl.BlockSpec((1, D_MODEL), lambda i, j: (0, 0)),
            pl.BlockSpec((None, D_MODEL, tn), lambda i, j: (layer, 0, j)),
        ],
        out_specs=pl.BlockSpec((tm, tn), lambda i, j: (i, j)),
        scratch_shapes=[pltpu.VMEM((tm, D_MODEL), BF16)],
        compiler_params=_cparams(("parallel", "arbitrary")),
        name="norm_matmul",
    )(x, g, w)


def _in_proj_kernel(x_ref, g_ref, wm_ref, wt_ref, o_ref, *rest, n_main, kv_tiles):
    lin_refs, xn_ref = rest[:-1], rest[-1]
    j = pl.program_id(1)
    kv_lo, kv_hi = kv_tiles
    is_kv = (j >= kv_lo) & (j < kv_hi)
    rows = x_ref.shape[0]

    @pl.when(j == 0)
    def _():
        xn_ref[...] = _rms(x_ref[...], g_ref[...]).astype(BF16)

    @pl.when((j < n_main) & jnp.logical_not(is_kv))
    def _():
        o_ref[...] = _nt(xn_ref[...], wm_ref[...])

    for t in range(kv_lo, kv_hi):
        @pl.when(j == t)
        def _(t=t):
            res = _nt(xn_ref[...], wm_ref[...])
            for g in range(NSA_G):
                for kv in range(2):
                    src, dst = kv * NSA_G + g, g * 2 + kv
                    blk = res[:, src * DH:(src + 1) * DH]
                    o_ref[:, dst * DH:(dst + 1) * DH] = blk
                    if lin_refs:
                        lin_refs[t - kv_lo][pl.ds(dst, rows, stride=ROW_W), :] = blk

    @pl.when(j == n_main)
    def _():
        o_ref[...] = _nt(xn_ref[...], wt_ref[...])


def _in_proj(x, g, w_main, main_cols, w_tail, layer, tm, kv_tiles=(0, 0), linear_kv=False):
    m = x.shape[0]
    n_main = main_cols // TN_PROJ
    out_shape = [jax.ShapeDtypeStruct((m, main_cols + TN_PROJ), F32)]
    out_specs = [pl.BlockSpec((tm, TN_PROJ), lambda i, j: (i, j))]
    if linear_kv:
        n_kv = kv_tiles[1] - kv_tiles[0]
        out_shape += [jax.ShapeDtypeStruct((m * ROW_W, DH), F32)] * n_kv
        out_specs += [pl.BlockSpec((tm * ROW_W, DH), lambda i, j: (i, 0))] * n_kv
    out = pl.pallas_call(
        functools.partial(_in_proj_kernel, n_main=n_main, kv_tiles=kv_tiles),
        out_shape=out_shape,
        grid=(m // tm, n_main + 1),
        in_specs=[
            pl.BlockSpec((tm, D_MODEL), lambda i, j: (i, 0)),
            pl.BlockSpec((1, D_MODEL), lambda i, j: (0, 0)),
            pl.BlockSpec((None, TN_PROJ, D_MODEL), lambda i, j: (layer, jnp.minimum(j, n_main - 1), 0)),
            pl.BlockSpec((None, TN_PROJ, D_MODEL), lambda i, j: (layer, 0, 0)),
        ],
        out_specs=out_specs,
        scratch_shapes=[pltpu.VMEM((tm, D_MODEL), BF16)],
        compiler_params=_cparams(("parallel", "arbitrary")),
        name="in_proj",
    )(x, g, w_main, w_tail)
    return out if linear_kv else out[0]


def _out_proj_kernel(x_ref, tok_ref, mem_ref, wt_ref, wm_ref, g_ref, o_ref):
    y = _dot(tok_ref[...], wt_ref[...]) + _dot(mem_ref[...], wm_ref[...])
    o_ref[...] = x_ref[...] + _rms(y, g_ref[...])


def _out_proj(x, tok, mem_o, w_o, layer, g, tm):
    m = x.shape[0]
    return pl.pallas_call(
        _out_proj_kernel,
        out_shape=jax.ShapeDtypeStruct((m, D_MODEL), F32),
        grid=(m // tm,),
        in_specs=[
            pl.BlockSpec((tm, D_MODEL), lambda i: (i, 0)),
            pl.BlockSpec((tm, TOK_W), lambda i: (i, 0)),
            pl.BlockSpec((tm, MEM_W), lambda i: (i, 0)),
            pl.BlockSpec((None, TOK_W, D_MODEL), lambda i: (layer, 0, 0), pipeline_mode=pl.Buffered(1)),
            pl.BlockSpec((None, MEM_W, D_MODEL), lambda i: (layer, TOK_W // MEM_W, 0), pipeline_mode=pl.Buffered(1)),
            pl.BlockSpec((1, D_MODEL), lambda i: (0, 0)),
        ],
        out_specs=pl.BlockSpec((tm, D_MODEL), lambda i: (i, 0)),
        compiler_params=_cparams(("parallel",)),
        name="out_proj",
    )(x, tok, mem_o, w_o, w_o, g)


def _mem_attn_kernel(q_ref, kv_ref, o_ref):
    for h in range(N_MEM_HEADS):
        q = (q_ref[:, h * DH:(h + 1) * DH] * (MEM_HEAD_DIM ** -0.5)).astype(BF16)
        k = kv_ref[:, h * DH:(h + 1) * DH].astype(BF16)
        v = kv_ref[:, MEM_W + h * DH:MEM_W + (h + 1) * DH].astype(BF16)
        s = _nt(q, k)
        e = jnp.exp(s - jnp.max(s, axis=-1, keepdims=True))
        p = e / jnp.sum(e, axis=-1, keepdims=True)
        o_ref[:, h * DH:(h + 1) * DH] = _dot(p.astype(BF16), v).astype(o_ref.dtype)


def _mem_attn(q_arr, q_col_block, mem_kv, rows_per_batch, tm):
    nb = mem_kv.shape[0]
    per = rows_per_batch // tm
    return pl.pallas_call(
        _mem_attn_kernel,
        out_shape=jax.ShapeDtypeStruct((nb * rows_per_batch, MEM_W), BF16),
        grid=(nb, per),
        in_specs=[
            pl.BlockSpec((tm, MEM_W), lambda b, i: (b * per + i, q_col_block)),
            pl.BlockSpec((None, MEM_LEN, 2 * MEM_W), lambda b, i: (b, 0, 0)),
        ],
        out_specs=pl.BlockSpec((tm, MEM_W), lambda b, i: (b * per + i, 0)),
        compiler_params=_cparams(("parallel", "parallel")),
        name="mem_attn",
    )(q_arr, mem_kv)


GLA_PAIRS = GLA_HEADS // 2


def _gla_kernel(q_ref, k_ref, v_ref, r_ref, a_ref, s0_ref, wa_ref, ba_ref, gn_ref,
                tok_ref, s_out_ref, s_ref, cum_ref, *, chunk, n_valid):
    l = pl.program_id(1)
    tl = q_ref.shape[0]

    @pl.when(l == 0)
    def _():
        for p in range(GLA_PAIRS):
            s_ref[p] = s0_ref[2 * p:2 * p + 2].reshape(GLA_PAIR_W, GLA_DV).T

    lane = lax.broadcasted_iota(jnp.int32, (1, GLA_PAIR_W), 1)
    head_mask = [(lane < GLA_DK).astype(F32), (lane >= GLA_DK).astype(F32)]
    ti = lax.broadcasted_iota(jnp.int32, (chunk, chunk), 0)
    si = lax.broadcasted_iota(jnp.int32, (chunk, chunk), 1)
    causal = si <= ti
    tri = jnp.where(causal, 1.0, 0.0).astype(BF16)

    a_hi, a_lo, _ = _split3(a_ref[...])
    w_hi, w_lo, _ = _split3(wa_ref[...])
    z = _dot(a_hi, w_hi) + _dot(a_lo, w_hi) + _dot(a_hi, w_lo) + ba_ref[...]
    la_all = -(jnp.maximum(-z, 0.0) + jnp.log1p(jnp.exp(-jnp.abs(z)))) / GLA_TAU
    pos = l * tl + lax.broadcasted_iota(jnp.int32, (tl, 1), 0)
    la_all = jnp.where(pos < n_valid, la_all, 0.0)
    pieces = _split3(la_all)
    for c0 in range(0, tl, chunk):
        cum_ref[c0:c0 + chunk, :] = functools.reduce(jnp.add, [_dot(tri, pc[c0:c0 + chunk]) for pc in pieces])

    def step(ci, carry):
        r0 = pl.multiple_of(ci * chunk, chunk)
        rows = pl.ds(r0, chunk)
        for p in range(GLA_PAIRS):
            pair = slice(p * GLA_PAIR_W, (p + 1) * GLA_PAIR_W)
            b = cum_ref[rows, pair]
            bl = b[chunk - 1:chunk, :]
            q = q_ref[rows, pair] * (GLA_DK ** -0.5)
            k = k_ref[rows, pair]
            qe = q * jnp.exp(b)
            ke = (k * jnp.exp(-b)).astype(BF16)
            kd = k * jnp.exp(bl - b)
            st_old = s_ref[p]
            st_bf = st_old.astype(BF16)
            upd = None
            for h in range(2):
                head = slice((2 * p + h) * GLA_DV, (2 * p + h + 1) * GLA_DV)
                v = v_ref[rows, head].astype(BF16)
                qm = (qe * head_mask[h]).astype(BF16)
                att = jnp.where(causal, _nt(qm, ke), 0.0)
                o = _nt(qm, st_bf) + _dot(att.astype(BF16), v)
                o = _rms(o, gn_ref[...])
                r = r_ref[rows, head]
                tok_ref[rows, head] = (o * (r * _sigmoid(r))).astype(tok_ref.dtype)
                u = _tn(v, (kd * head_mask[h]).astype(BF16))
                upd = u if upd is None else upd + u
            s_ref[p] = jnp.exp(bl) * st_old + upd
        return carry

    lax.fori_loop(0, tl // chunk, step, 0)

    @pl.when(l == pl.num_programs(1) - 1)
    def _():
        for p in range(GLA_PAIRS):
            s_out_ref[2 * p:2 * p + 2] = s_ref[p].T.reshape(2, GLA_DK, GLA_DV)


def _gla(proj, s0, wa_pad, b_a, gn, seq, tl, chunk, n_valid):
    nb = s0.shape[0]
    per = seq // tl
    qk_w = GLA_HEADS * GLA_DK
    row = lambda b, l: b * per + l

    return pl.pallas_call(
        functools.partial(_gla_kernel, chunk=chunk, n_valid=n_valid),
        out_shape=(jax.ShapeDtypeStruct((nb * seq, TOK_W), BF16),
                   jax.ShapeDtypeStruct((nb, GLA_HEADS, GLA_DK, GLA_DV), F32)),
        grid=(nb, per),
        in_specs=[
            pl.BlockSpec((tl, qk_w), lambda b, l: (row(b, l), 0)),
            pl.BlockSpec((tl, qk_w), lambda b, l: (row(b, l), 1)),
            pl.BlockSpec((tl, TOK_W), lambda b, l: (row(b, l), 2 * qk_w // TOK_W)),
            pl.BlockSpec((tl, TOK_W), lambda b, l: (row(b, l), 2 * qk_w // TOK_W + 1)),
            pl.BlockSpec((tl, LANE), lambda b, l: (row(b, l), GLA_A_COL // LANE)),
            pl.BlockSpec((None, GLA_HEADS, GLA_DK, GLA_DV), lambda b, l: (b, 0, 0, 0)),
            pl.BlockSpec((LANE, qk_w), lambda b, l: (0, 0)),
            pl.BlockSpec((1, qk_w), lambda b, l: (0, 0)),
            pl.BlockSpec((1, GLA_DV), lambda b, l: (0, 0)),
        ],
        out_specs=(pl.BlockSpec((tl, TOK_W), lambda b, l: (row(b, l), 0)),
                   pl.BlockSpec((None, GLA_HEADS, GLA_DK, GLA_DV), lambda b, l: (b, 0, 0, 0))),
        scratch_shapes=[pltpu.VMEM((GLA_PAIRS, GLA_DV, GLA_PAIR_W), F32), pltpu.VMEM((tl, qk_w), F32)],
        compiler_params=_cparams(("parallel", "arbitrary")),
        name="gla",
    )(proj, proj, proj, proj, proj, s0, wa_pad, b_a, gn)


def _masked_softmax(s, valid):
    s = jnp.where(valid, s, -jnp.inf)
    m = jnp.max(s, axis=-1, keepdims=True)
    m = jnp.where(m > -jnp.inf, m, 0.0)
    e = jnp.exp(s - m)
    return e / jnp.maximum(jnp.sum(e, axis=-1, keepdims=True), 1e-30)


def _online_update(state, s, v):
    m, l, acc = state
    r, nq, w = s.shape
    m_new = jnp.maximum(m, jnp.max(s, axis=-1, keepdims=True))
    alpha = jnp.exp(m - m_new)
    p = jnp.exp(s - m_new)
    l = alpha * l + jnp.sum(p, axis=-1, keepdims=True)
    pv = _dot(p.reshape(r * nq, w).astype(BF16), v).reshape(r, nq, DH)
    return m_new, l, alpha * acc + pv


def _compress_tail(xflat, w1, b1, w2):
    h = _dot(xflat, w1) + b1
    h = h * _sigmoid(h)
    return _dot(h.astype(BF16), w2)


def _cmp_prompt_kernel(x0_ref, x1_ref, x2_ref, pe_ref, w1_ref, b1_ref, w2_ref, o_ref, xflat_ref):
    x_refs = (x0_ref, x1_ref, x2_ref)
    nb, seq = x0_ref.shape[0], x0_ref.shape[1]
    nblk = seq // BLK
    for j in range(BLK):
        pe_j = pe_ref[j:j + 1, :]
        for b in range(nb):
            for g in range(NSA_G):
                xj = x_refs[g][b, pl.ds(j, nblk, stride=BLK), :]
                row = (b * NSA_G + g) * nblk
                xflat_ref[row:row + nblk, j * DH:(j + 1) * DH] = (xj + pe_j).astype(BF16)
    out = _compress_tail(xflat_ref[...], w1_ref[...], b1_ref[...], w2_ref[...])
    o_ref[...] = out.reshape(nb, NSA_G, nblk, DH)


def _kv_block(branch, g, kv):
    return NSA_KV_COL // DH + branch * 2 * NSA_G + g * 2 + kv


def _cmp_prompt(proj3, pe, w1, b1, w2):
    nb, seq, _ = proj3.shape
    nblk = seq // BLK
    return pl.pallas_call(
        _cmp_prompt_kernel,
        out_shape=jax.ShapeDtypeStruct((2, nb, NSA_G, nblk, DH), F32),
        grid=(2,),
        in_specs=[
            pl.BlockSpec((nb, seq, DH), lambda kv: (0, 0, _kv_block(0, 0, kv))),
            pl.BlockSpec((nb, seq, DH), lambda kv: (0, 0, _kv_block(0, 1, kv))),
            pl.BlockSpec((nb, seq, DH), lambda kv: (0, 0, _kv_block(0, 2, kv))),
            pl.BlockSpec((None, BLK, DH), lambda kv: (kv, 0, 0)),
            pl.BlockSpec((None, BLK * DH, CMP_HID), lambda kv: (kv, 0, 0)),
            pl.BlockSpec((None, 1, CMP_HID), lambda kv: (kv, 0, 0)),
            pl.BlockSpec((None, CMP_HID, DH), lambda kv: (kv, 0, 0)),
        ],
        out_specs=pl.BlockSpec((None, nb, NSA_G, nblk, DH), lambda kv: (kv, 0, 0, 0, 0)),
        scratch_shapes=[pltpu.VMEM((nb * NSA_G * nblk, BLK * DH), BF16)],
        compiler_params=_cparams(("arbitrary",)),
        name="cmp_prompt",
    )(proj3, proj3, proj3, pe, w1, b1, w2)


def _nsa_prompt_kernel(q_ref, gl_ref, gb_ref, kc_ref, vc_ref, ks_ref, vs_ref, kw_ref, vw_ref,
                       bct_ref, bsn_ref, bw_ref, o_ref, ksb, vsb, kwb, vwb):
    g = pl.program_id(1)
    qi = pl.program_id(2)
    seq = ks_ref.shape[0]
    nblk = seq // BLK
    R = NSA_R

    @pl.when(qi == 0)
    def _():
        vsb[0:SLC_PAD, :] = jnp.zeros((SLC_PAD, DH), BF16)
        vsb[SLC_PAD:SLC_PAD + seq, :] = vs_ref[...].astype(BF16)
        ones_col = jnp.where(lax.broadcasted_iota(jnp.int32, (1, DH), 1) == 0, 1.0, 0.0)
        vwb[0:WIN_PAD, 0:DH] = jnp.zeros((WIN_PAD, DH), BF16)
        vwb[WIN_PAD:WIN_PAD + seq, 0:DH] = vw_ref[...].astype(BF16)
        vwb[:, DH:2 * DH] = jnp.broadcast_to(ones_col, (WIN_PAD + seq, DH)).astype(BF16)
        for src, dst, pad, per_block in ((ks_ref, ksb, SLC_PAD, True), (kw_ref, kwb, WIN_PAD, False)):
            dst[0:pad, 0:DH] = jnp.zeros((pad, DH), BF16)
            dst[pad:pad + seq, 0:DH] = src[...].astype(BF16)
            pos = lax.broadcasted_iota(jnp.int32, (pad + seq, 1), 0) - pad
            feat = lax.broadcasted_iota(jnp.int32, (1, DH), 1)
            masked = (feat == HALF_LANE) & (pos < 0)
            if per_block:
                masked = masked | ((pos >= 0) & (lax.shift_right_arithmetic(pos, BLK_SHIFT) == feat))
            dst[:, DH:2 * DH] = jnp.where(masked, NEG, 0.0).astype(BF16)

    q = q_ref[...] * (DH ** -0.5)
    q_all = jnp.concatenate([q[:, r * DH:(r + 1) * DH] for r in range(R)], axis=0)
    q_bf = q_all.astype(BF16)
    i_col = lax.broadcasted_iota(jnp.int32, (QBLK, 1), 0)
    t_col = qi * QBLK + i_col
    n_row = lax.broadcasted_iota(jnp.int32, (1, nblk), 1)
    row0 = pl.multiple_of(qi * QBLK, QBLK)
    lane2 = lax.broadcasted_iota(jnp.int32, (1, LANE), 1)

    s_c = _nt(q_all, kc_ref[...], HI).reshape(R, QBLK, nblk)
    tb_col = lax.shift_right_arithmetic(t_col, BLK_SHIFT)
    rel = tb_col - n_row
    bias_c = []
    for r in range(R):
        tab = bct_ref[r]
        bias_c.append(jnp.where(rel == 0, tab[:, 0:1],
                      jnp.where(rel == 1, tab[:, 1:2],
                      jnp.where(rel == 2, tab[:, 2:3], tab[:, 3:4]))))
    s_c = s_c + jnp.stack(bias_c, axis=0)
    valid_c = (t_col - (n_row * BLK + (BLK - 1))) >= 0
    p_c = _masked_softmax(s_c, valid_c[None])
    o_c = _dot(p_c.reshape(R * QBLK, nblk).astype(BF16), vc_ref[...].astype(BF16))

    imp = jnp.sum(p_c, axis=0)
    forced = (n_row == 0) | (n_row == tb_col) | (n_row == tb_col - 1)
    future = n_row * BLK > t_col
    score = jnp.where(forced, jnp.inf, jnp.where(future, -jnp.inf, imp))
    score_t = jnp.concatenate([score, jnp.full((QBLK, LANE - nblk), -jnp.inf, F32)], axis=1).T[:HALF_LANE]
    n_sub = lax.broadcasted_iota(jnp.int32, (HALF_LANE, 1), 0)
    rank_t = jnp.zeros((HALF_LANE, QBLK), F32)
    for i in range(nblk):
        cand = score_t[i:i + 1, :]
        wins_tie = jnp.where(n_sub > i, 1.0, 0.0)
        rank_t = rank_t + jnp.where(cand > score_t, 1.0, jnp.where(cand == score_t, wins_tie, 0.0))
    rank = jnp.concatenate([rank_t, jnp.zeros((LANE - HALF_LANE, QBLK), F32)], axis=0).T
    left = lane2 < HALF_LANE
    n_far = (qi * QBLK - SLC_PAD) // BLK
    dropped = rank >= float(min(TOPN, nblk))
    flag_all = jnp.where(left, jnp.where(dropped, 1.0, 0.0), jnp.where(lane2 == HALF_LANE, 1.0, 0.0))
    flag_far = jnp.where(left & (lane2 >= n_far), 1.0, flag_all)
    q_far = jnp.concatenate([q_bf, jnp.concatenate([flag_far.astype(BF16)] * R, axis=0)], axis=1)
    q_near = jnp.concatenate([q_bf, jnp.concatenate([flag_all.astype(BF16)] * R, axis=0)], axis=1)

    def far_body(kc_i, state):
        start = pl.multiple_of(SLC_PAD + kc_i * FAR_CHUNK, BLK)
        s = _nt(q_far, ksb[pl.ds(start, FAR_CHUNK), :]).reshape(R, QBLK, FAR_CHUNK)
        return _online_update(state, s, vsb[pl.ds(start, FAR_CHUNK), :])

    blk_per_chunk = FAR_CHUNK // BLK
    n_chunks = (jnp.maximum(n_far, 0) + (blk_per_chunk - 1)) // blk_per_chunk
    state = (jnp.full((R, QBLK, 1), M_FLOOR, F32), jnp.zeros((R, QBLK, 1), F32), jnp.zeros((R, QBLK, DH), F32))
    state = lax.fori_loop(0, n_chunks, far_body, state)

    s = _nt(q_near, ksb[pl.ds(row0, SLC_NEAR), :]).reshape(R, QBLK, SLC_NEAR) + bsn_ref[...]
    _, l_s, acc_s = _online_update(state, s, vsb[pl.ds(row0, SLC_NEAR), :])
    o_s = (acc_s / jnp.maximum(l_s, 1e-30)).reshape(R * QBLK, DH)

    s = _nt(q_near, kwb[pl.ds(row0, WIN_KEYS), :]).reshape(R, QBLK, WIN_KEYS) + bw_ref[...]
    e_w = jnp.exp(s - jnp.max(s, axis=-1, keepdims=True))
    ow = _dot(e_w.reshape(R * QBLK, WIN_KEYS).astype(BF16), vwb[pl.ds(row0, WIN_KEYS), :])
    o_w = ow[:, :DH] / jnp.maximum(ow[:, DH:DH + 1], 1e-30)

    gates = _sigmoid(gl_ref[...] + gb_ref[...])
    src = lax.broadcasted_iota(jnp.int32, (LANE, LANE), 0)
    dst = lax.broadcasted_iota(jnp.int32, (LANE, LANE), 1)
    pick = jnp.where((src == g * (3 * R) + dst) & (dst < 3 * R), 1.0, 0.0)
    gsel = _dot(gates, pick, HI)
    for r in range(R):
        rows = slice(r * QBLK, (r + 1) * QBLK)
        o = (gsel[:, 3 * r:3 * r + 1] * o_c[rows] + gsel[:, 3 * r + 1:3 * r + 2] * o_s[rows]
             + gsel[:, 3 * r + 2:3 * r + 3] * o_w[rows])
        o_ref[:, r * DH:(r + 1) * DH] = o.astype(o_ref.dtype)


def _nsa_prompt(proj2, cmp_kv, gate_b_pad, bct, bsn, bw, nb, seq):
    nq = seq // QBLK
    nblk = seq // BLK
    assert nblk <= HALF_LANE, "mask features of the selected branch hold at most 64 key blocks"
    assert seq % FAR_CHUNK == 0
    proj3 = proj2.reshape(nb, seq, NSA_N)

    def kv_spec(branch, kv):
        return pl.BlockSpec((None, seq, DH), lambda b, g, qi: (b, 0, _kv_block(branch, g, kv)))

    return pl.pallas_call(
        _nsa_prompt_kernel,
        out_shape=jax.ShapeDtypeStruct((nb * seq, TOK_W), BF16),
        grid=(nb, NSA_G, nq),
        in_specs=[
            pl.BlockSpec((QBLK, NSA_R * DH), lambda b, g, qi: (b * nq + qi, g)),
            pl.BlockSpec((QBLK, LANE), lambda b, g, qi: (b * nq + qi, NSA_GATE_COL // LANE)),
            pl.BlockSpec((1, LANE), lambda b, g, qi: (0, 0)),
            pl.BlockSpec((None, None, None, nblk, DH), lambda b, g, qi: (0, b, g, 0, 0)),
            pl.BlockSpec((None, None, None, nblk, DH), lambda b, g, qi: (1, b, g, 0, 0)),
            kv_spec(1, 0), kv_spec(1, 1), kv_spec(2, 0), kv_spec(2, 1),
            pl.BlockSpec((None, NSA_R, QBLK, LANE), lambda b, g, qi: (g, 0, 0, 0)),
            pl.BlockSpec((None, NSA_R, QBLK, SLC_NEAR), lambda b, g, qi: (g, 0, 0, 0)),
            pl.BlockSpec((None, NSA_R, QBLK, WIN_KEYS), lambda b, g, qi: (g, 0, 0, 0)),
        ],
        out_specs=pl.BlockSpec((QBLK, NSA_R * DH), lambda b, g, qi: (b * nq + qi, g)),
        scratch_shapes=[pltpu.VMEM((SLC_PAD + seq, 2 * DH), BF16), pltpu.VMEM((SLC_PAD + seq, DH), BF16),
                        pltpu.VMEM((WIN_PAD + seq, 2 * DH), BF16), pltpu.VMEM((WIN_PAD + seq, 2 * DH), BF16)],
        compiler_params=_cparams(("parallel", "parallel", "arbitrary")),
        name="nsa_prompt",
    )(proj2, proj2, gate_b_pad, cmp_kv, cmp_kv, proj3, proj3, proj3, proj3, bct, bsn, bw)


SUB_PAGES = 8
RING = 4
PAGE_ROWS = PAGE * ROW_W


def _linear_cache(cache):
    return jnp.transpose(cache, (0, 1, 3, 2, 4)).reshape(-1, DH)


def _cmp_decode_kernel(pt_ref, cache_ref, new_ref, pe_ref, w1_ref, b1_ref, w2_ref, o_ref, last_ref,
                       buf, sem, xflat_ref, xlast_ref, *, n_sub):
    b = pl.program_id(0)
    total = pl.num_programs(0) * n_sub
    sub_blk = SUB_PAGES * PAGE // BLK
    seq_blk = n_sub * sub_blk

    def page_copy(s, p):
        page = pt_ref[s // n_sub, (s % n_sub) * SUB_PAGES + p]
        return pltpu.make_async_copy(cache_ref.at[pl.ds(pl.multiple_of(page * PAGE_ROWS, PAGE_ROWS), PAGE_ROWS), :],
                                     buf.at[s % RING, pl.ds(p * PAGE_ROWS, PAGE_ROWS), :], sem.at[s % RING])

    def start_sub(s):
        for p in range(SUB_PAGES):
            page_copy(s, p).start()

    @pl.when(b == 0)
    def _():
        for s in range(RING):
            start_sub(s)

    def body(i, c):
        s = b * n_sub + i
        for p in range(SUB_PAGES):
            page_copy(s, p).wait()
        slot = s % RING
        row0 = pl.multiple_of(i * sub_blk, sub_blk)
        by_row = jnp.swapaxes(buf[slot].reshape(sub_blk, BLK * ROW_W, DH), 0, 1)
        for kv in range(2):
            for j in range(BLK):
                pe_j = pe_ref[kv, j:j + 1, :]
                for g in range(NSA_G):
                    xj = by_row[j * ROW_W + g * 2 + kv]
                    xflat_ref[kv, pl.ds(g * seq_blk + row0, sub_blk), j * DH:(j + 1) * DH] = (xj + pe_j).astype(BF16)

        @pl.when(s + RING < total)
        def _():
            start_sub(s + RING)

        return c

    lax.fori_loop(0, n_sub, body, 0)
    for kv in range(2):
        out = _compress_tail(xflat_ref[kv], w1_ref[kv], b1_ref[kv], w2_ref[kv])
        o_ref[kv] = out.reshape(NSA_G, seq_blk, DH)

    @pl.when(b == 0)
    def _():
        rows = new_ref.shape[1]
        for kv in range(2):
            for j in range(BLK):
                pe_j = jnp.broadcast_to(pe_ref[kv, j:j + 1, :], (rows, DH))
                xj = new_ref[kv] + pe_j if j == 0 else pe_j
                xlast_ref[:, j * DH:(j + 1) * DH] = xj.astype(BF16)
            last_ref[kv] = _compress_tail(xlast_ref[...], w1_ref[kv], b1_ref[kv], w2_ref[kv])


def _cmp_decode(page_table, cache, new_rows, pe, w1, b1, w2):
    nd, n_pages = page_table.shape
    n_sub = n_pages // SUB_PAGES
    seq_blk = n_pages * PAGE // BLK
    rows = new_rows.shape[1]

    def whole(shape):
        return pl.BlockSpec(shape, lambda b, pt: (0,) * len(shape), pipeline_mode=pl.Buffered(1))

    grid_spec = pltpu.PrefetchScalarGridSpec(
        num_scalar_prefetch=1,
        grid=(nd,),
        in_specs=[
            pl.BlockSpec(memory_space=pl.ANY),
            whole(new_rows.shape), whole(pe.shape), whole(w1.shape), whole(b1.shape), whole(w2.shape),
        ],
        out_specs=(pl.BlockSpec((2, None, NSA_G, seq_blk, DH), lambda b, pt: (0, b, 0, 0, 0)),
                   pl.BlockSpec((2, rows, DH), lambda b, pt: (0, 0, 0))),
        scratch_shapes=[pltpu.VMEM((RING, SUB_PAGES * PAGE_ROWS, DH), F32),
                        pltpu.SemaphoreType.DMA((RING,)),
                        pltpu.VMEM((2, NSA_G * seq_blk, BLK * DH), BF16),
                        pltpu.VMEM((rows, BLK * DH), BF16)],
    )
    return pl.pallas_call(
        functools.partial(_cmp_decode_kernel, n_sub=n_sub),
        out_shape=(jax.ShapeDtypeStruct((2, nd, NSA_G, seq_blk, DH), F32),
                   jax.ShapeDtypeStruct((2, rows, DH), F32)),
        grid_spec=grid_spec,
        compiler_params=_cparams(("arbitrary",)),
        name="cmp_decode",
    )(page_table, cache, new_rows, pe, w1, b1, w2)


SEL_LANES = 384


def _sel_decode_kernel(q_ref, kc_ref, vc_ref, last_ref, bias_ref, oc_ref, sel_ref, *, t_pos):
    b = pl.program_id(0)
    n_past = kc_ref.shape[1]
    n_blocks = n_past + 1
    n_lane = lax.broadcasted_iota(jnp.int32, (1, SEL_LANES), 1)
    n_lane_f = n_lane.astype(F32)
    head_row = lax.broadcasted_iota(jnp.int32, (8, 1), 0) < NSA_R
    tb = t_pos // BLK
    for g in range(NSA_G):
        q = q_ref[g] * (DH ** -0.5)
        bias = bias_ref[g]
        s_p = _nt(q, kc_ref[g], HI) + bias[:, :n_past]
        row = b * NSA_G + g
        k_last = last_ref[0, pl.ds(row, 1), :]
        v_last = last_ref[1, pl.ds(row, 1), :]
        s_l = jnp.sum(q * k_last, axis=-1, keepdims=True) + bias[:, n_past:n_past + 1]
        valid_p = (t_pos - (n_lane[:, :n_past] * BLK + (BLK - 1))) >= 0
        valid_l = (t_pos - (n_past * BLK + (BLK - 1))) >= 0
        s_p = jnp.where(valid_p, s_p, -jnp.inf)
        s_l = jnp.where(valid_l, s_l, -jnp.inf)
        m = jnp.maximum(jnp.max(s_p, axis=-1, keepdims=True), s_l)
        m = jnp.where(m > -jnp.inf, m, 0.0)
        e_p = jnp.exp(s_p - m)
        e_l = jnp.exp(s_l - m)
        den = jnp.maximum(jnp.sum(e_p, axis=-1, keepdims=True) + e_l, 1e-30)
        p_p = e_p / den
        p_l = e_l / den
        oc_ref[g] = _dot(p_p, vc_ref[g], HI) + p_l * v_last
        imp_p = jnp.sum(jnp.where(head_row, p_p, 0.0), axis=0, keepdims=True)
        imp_l = jnp.sum(jnp.where(head_row, p_l, 0.0), axis=0, keepdims=True)
        imp = jnp.concatenate([imp_p, jnp.broadcast_to(imp_l, (1, SEL_LANES - n_past))], axis=1)
        forced = (n_lane == 0) | (n_lane == tb) | (n_lane == tb - 1)
        future = n_lane * BLK > t_pos
        score = jnp.where(forced, jnp.inf, jnp.where(future, -jnp.inf, imp))
        cand = n_lane < n_blocks
        sel = jnp.zeros((1, LANE), jnp.int32)
        k_lane = lax.broadcasted_iota(jnp.int32, (1, LANE), 1)
        for k in range(min(TOPN, n_blocks)):
            best = jnp.max(jnp.where(cand, score, -jnp.inf), axis=-1, keepdims=True)
            idx_f = jnp.min(jnp.where(cand & (score == best), n_lane_f, float(SEL_LANES)), axis=-1, keepdims=True)
            idx = idx_f.astype(jnp.int32)
            sel = jnp.where(k_lane == k, idx, sel)
            cand = cand & (n_lane != idx)
        sel_ref[g] = jnp.broadcast_to(sel, (8, LANE))


def _sel_decode(q8, cmp_kv, cmp_last, bias_cd, t_pos):
    nd = q8.shape[0]
    n_past = cmp_kv.shape[3]
    return pl.pallas_call(
        functools.partial(_sel_decode_kernel, t_pos=t_pos),
        out_shape=(jax.ShapeDtypeStruct((nd, NSA_G, 8, DH), F32),
                   jax.ShapeDtypeStruct((nd, NSA_G, 8, LANE), jnp.int32)),
        grid=(nd,),
        in_specs=[
            pl.BlockSpec((None, NSA_G, 8, DH), lambda b: (b, 0, 0, 0)),
            pl.BlockSpec((None, None, NSA_G, n_past, DH), lambda b: (0, b, 0, 0, 0)),
            pl.BlockSpec((None, None, NSA_G, n_past, DH), lambda b: (1, b, 0, 0, 0)),
            pl.BlockSpec(cmp_last.shape, lambda b: (0, 0, 0)),
            pl.BlockSpec(bias_cd.shape, lambda b: (0, 0, 0)),
        ],
        out_specs=(pl.BlockSpec((None, NSA_G, 8, DH), lambda b: (b, 0, 0, 0)),
                   pl.BlockSpec((None, NSA_G, 8, LANE), lambda b: (b, 0, 0, 0))),
        compiler_params=_cparams(("parallel",)),
        name="sel_decode",
    )(q8, cmp_kv, cmp_kv, cmp_last, bias_cd)


HALF_ROWS = BLK * ROW_W


def _slc_decode_kernel(sel_ref, pt_ref, q_ref, oc_ref, gl_ref, gb_ref, slc_ref, bias_ref, nk_ref, nv_ref,
                       win_ref, nwk_ref, nwv_ref, bw_ref, o_ref, gbuf, sem, *, t_pos, n_past):
    b = pl.program_id(0)
    n_win = win_ref.shape[0] // ROW_W

    def block_copy(g, k):
        n = jnp.minimum(sel_ref[(b * NSA_G + g) * TOPN + k], n_past - 1)
        half = pt_ref[b, n // 2] * 2 + n % 2
        return pltpu.make_async_copy(slc_ref.at[pl.ds(pl.multiple_of(half * HALF_ROWS, HALF_ROWS), HALF_ROWS), :],
                                     gbuf.at[g * TOPN + k], sem)

    for g in range(NSA_G):
        for k in range(TOPN):
            block_copy(g, k).start()

    c = lax.broadcasted_iota(jnp.int32, (1, n_win), 1)
    dist = n_win - c
    valid = (dist >= 0) & (dist <= WINDOW) & (t_pos - dist >= 0)
    qs, o_w = [], []
    for g in range(NSA_G):
        q = q_ref[g] * (DH ** -0.5)
        q_bf = q.astype(BF16)
        qs.append((q, q_bf))
        kw = win_ref[pl.ds(2 * g, n_win, stride=ROW_W), :].astype(BF16)
        vw = win_ref[pl.ds(2 * g + 1, n_win, stride=ROW_W), :].astype(BF16)
        bw = bw_ref[g]
        s_w = jnp.where(valid, _nt(q_bf, kw) + bw[:, :n_win], -jnp.inf)
        s_n = jnp.sum(q * nwk_ref[g, 0:1, :], axis=-1, keepdims=True) + bw[:, n_win:n_win + 1]
        m_w = jnp.maximum(jnp.max(s_w, axis=-1, keepdims=True), s_n)
        e_w = jnp.exp(s_w - m_w)
        e_n = jnp.exp(s_n - m_w)
        den = jnp.maximum(jnp.sum(e_w, axis=-1, keepdims=True) + e_n, 1e-30)
        o_w.append((_dot(e_w.astype(BF16), vw) + e_n * nwv_ref[g, 0:1, :]) / den)

    for g in range(NSA_G):
        for k in range(TOPN):
            block_copy(g, k).wait()

    row0 = lax.broadcasted_iota(jnp.int32, (BLK, 1), 0) == 0
    j_row = lax.broadcasted_iota(jnp.int32, (1, BLK), 1)
    for g in range(NSA_G):
        q, q_bf = qs[g]
        scores, values = [], []
        for k in range(TOPN):
            n = sel_ref[(b * NSA_G + g) * TOPN + k]
            is_new = (n == n_past) & row0
            kb = jnp.where(is_new, nk_ref[g, 0:1, :], gbuf[g * TOPN + k, pl.ds(2 * g, BLK, stride=ROW_W), :])
            vb = jnp.where(is_new, nv_ref[g, 0:1, :], gbuf[g * TOPN + k, pl.ds(2 * g + 1, BLK, stride=ROW_W), :])
            s = _nt(q_bf, kb.astype(BF16)) + bias_ref[g, n]
            scores.append(jnp.where(t_pos - (n * BLK + j_row) >= 0, s, -jnp.inf))
            values.append(vb.astype(BF16))
        m = functools.reduce(jnp.maximum, [jnp.max(s, axis=-1, keepdims=True) for s in scores])
        m = jnp.where(m > -jnp.inf, m, 0.0)
        probs = [jnp.exp(s - m) for s in scores]
        l = functools.reduce(jnp.add, [jnp.sum(p, axis=-1, keepdims=True) for p in probs])
        acc = functools.reduce(jnp.add, [_dot(p.astype(BF16), v) for p, v in zip(probs, values)])
        o_s = acc / jnp.maximum(l, 1e-30)
        gates = _sigmoid(gl_ref[g] + gb_ref[g])
        o_ref[g] = gates[:, 0:1] * oc_ref[g] + gates[:, 1:2] * o_s + gates[:, 2:3] * o_w[g]


def _slc_decode(sel_flat, page_table, q8, o_c, gate_l, gate_b, slc_cache, bias_sd, new_k, new_v,
                win_cache, new_wk, new_wv, bias_wd, t_pos):
    nd = q8.shape[0]
    n_past = page_table.shape[1] * (PAGE // BLK)
    win_rows = win_cache.shape[0] // nd

    def per_b(b, sel, pt):
        return (b, 0, 0, 0)

    def whole(shape):
        return pl.BlockSpec(shape, lambda b, sel, pt: (0,) * len(shape), pipeline_mode=pl.Buffered(1))

    b_spec = pl.BlockSpec((None, NSA_G, 8, DH), per_b)
    grid_spec = pltpu.PrefetchScalarGridSpec(
        num_scalar_prefetch=2,
        grid=(nd,),
        in_specs=[
            b_spec, b_spec, b_spec, whole(gate_b.shape),
            pl.BlockSpec(memory_space=pl.ANY),
            whole(bias_sd.shape),
            b_spec, b_spec,
            pl.BlockSpec((win_rows, DH), lambda b, sel, pt: (b, 0)),
            b_spec, b_spec,
            whole(bias_wd.shape),
        ],
        out_specs=b_spec,
        scratch_shapes=[pltpu.VMEM((NSA_G * TOPN, HALF_ROWS, DH), F32), pltpu.SemaphoreType.DMA(())],
    )
    return pl.pallas_call(
        functools.partial(_slc_decode_kernel, t_pos=t_pos, n_past=n_past),
        out_shape=jax.ShapeDtypeStruct((nd, NSA_G, 8, DH), F32),
        grid_spec=grid_spec,
        compiler_params=_cparams(("arbitrary",)),
        name="slc_decode",
    )(sel_flat, page_table, q8, o_c, gate_l, gate_b, slc_cache, bias_sd, new_k, new_v,
      win_cache, new_wk, new_wv, bias_wd)


def _t5_bucket(dist):
    n = jnp.maximum(dist, 0)
    nf = jnp.maximum(n, REL_MAX_EXACT).astype(F32)
    large = REL_MAX_EXACT + (jnp.log(nf / REL_MAX_EXACT) / math.log(REL_MAX_DIST / REL_MAX_EXACT)
                             * (REL_BUCKETS - REL_MAX_EXACT)).astype(jnp.int32)
    return jnp.where(n < REL_MAX_EXACT, n, jnp.minimum(large, REL_BUCKETS - 1))


def _bias_table(rel_bias, dist):
    onehot = (_t5_bucket(dist)[..., None] == jnp.arange(REL_BUCKETS)).astype(F32)
    b = jnp.einsum("...k,kh->...h", onehot, rel_bias, precision=HI)
    b = jnp.moveaxis(b, -1, 0)
    return b.reshape((NSA_G, NSA_R) + dist.shape)


def _prompt_bias_tables(rel_bias):
    i = jnp.arange(QBLK)[:, None]
    rel = jnp.arange(LANE)[None, :]
    bct = _bias_table(rel_bias, BLK * (rel - 1) + i % BLK + 1)
    d_near = SLC_PAD + i - jnp.arange(SLC_NEAR)[None, :]
    far = rel_bias[REL_BUCKETS - 1].reshape(NSA_G, NSA_R, 1, 1)
    bsn = _bias_table(rel_bias, d_near) - far + jnp.where(d_near >= 0, 0.0, NEG)
    d_win = WIN_PAD + i - jnp.arange(WIN_KEYS)[None, :]
    bw = _bias_table(rel_bias, d_win) + jnp.where((d_win >= 0) & (d_win <= WINDOW), 0.0, NEG)
    return bct, bsn, bw


def _pad_rows(a, rows, axis):
    pad = [(0, 0)] * a.ndim
    pad[axis] = (0, rows - a.shape[axis])
    return jnp.pad(a, pad)


def _decode_bias_tables(rel_bias, t_pos, n_past, n_win):
    n = jnp.arange(SEL_LANES)
    bias_cd = _pad_rows(_bias_table(rel_bias, t_pos - (n * BLK + BLK - 1)), 8, 1)
    tok = jnp.arange(n_past + 1)[:, None] * BLK + jnp.arange(BLK)[None, :]
    bias_sd = _pad_rows(jnp.swapaxes(_bias_table(rel_bias, t_pos - tok), 1, 2), 8, 2)
    c = jnp.arange(n_win + LANE)
    bias_wd = _pad_rows(_bias_table(rel_bias, n_win - c), 8, 1)
    return bias_cd, bias_sd, bias_wd


def kernel(x_prompt, x_sample, mem_prompt, cache_cmp_kv, cache_slc_kv, state_win_kv, state_gla, cache_mem_kv,
           page_table, norm_g, w_ffn_gate, w_ffn_up, w_ffn_down, w_in_gla, w_in_nsa, w_out, mem_norm_g, w_mem_kv,
           w_gla_a2, b_gla_a, gla_onorm_g, nsa_gate_b, cmp_pe, cmp_w1, cmp_b1, cmp_w2, rel_bias):
    nb, seq, _ = x_prompt.shape
    nd = x_sample.shape[0]
    depth = norm_g.shape[0]
    n_pages = page_table.shape[1]
    past_len = n_pages * PAGE
    n_past = past_len // BLK
    n_win = state_win_kv.shape[2]
    sr = SAMPLE_ROWS

    xp = x_prompt.reshape(nb * seq, D_MODEL)
    xs = _pad_rows(x_sample.reshape(nd, D_MODEL), sr, 0)
    mem_x = mem_prompt.reshape(nb * MEM_LEN, D_MODEL)

    w_o = w_out.astype(BF16)
    w_mkv = w_mem_kv.astype(BF16)
    def zero_rows(like, rows):
        return jnp.zeros((like.shape[0], rows, D_MODEL), BF16)

    wt_gla = jnp.swapaxes(w_in_gla, 1, 2).astype(BF16)
    tail_gla = jnp.concatenate(
        [wt_gla[:, GLA_MAIN + GLA_RANK:], wt_gla[:, GLA_MAIN:GLA_MAIN + GLA_RANK],
         zero_rows(wt_gla, TN_PROJ - MEM_W - GLA_RANK)], axis=1)
    n_gate = 3 * NSA_HEADS
    wt_nsa = jnp.swapaxes(w_in_nsa, 1, 2).astype(BF16)
    tail_nsa = jnp.concatenate(
        [wt_nsa[:, NSA_GATE_COL:NSA_GATE_COL + n_gate], zero_rows(wt_nsa, TN_PROJ - MEM_W - n_gate),
         wt_nsa[:, NSA_GATE_COL + n_gate:]], axis=1)
    wa_pad = _pad_rows(w_gla_a2, LANE, 1)
    w1 = cmp_w1.astype(BF16)
    w2 = cmp_w2.astype(BF16)

    def ffn_both(x_p, x_s, i, j):
        g1, g2 = norm_g[i, 4 * j][None], norm_g[i, 4 * j + 1][None]
        x_s, wg, wu, wd = _ffn_half(x_s, g1, w_ffn_gate, w_ffn_up, w_ffn_down, g2, TM_SAMPLE, f32_weights_at=(i, j))
        return _ffn_half(x_p, g1, wg, wu, wd, g2, TM_FFN), x_s

    def every(a, step):
        return a.reshape(nd, step, a.shape[-1])[:, 0]

    def per_seq(a):
        return _pad_rows(a[:, None, :], sr, 1).reshape(nd * sr, a.shape[-1])

    outs = dict(gla_p=[], gla_s=[], cmp_p=[], cmp_s=[], slc_p=[], slc_s=[], win_p=[], win_s=[], mem_p=[])
    for i in range(depth):
        li = i // 2
        mem_kv_p = _norm_matmul(mem_x, mem_norm_g[i][None], w_mkv, i, TM_MEMKV, MEM_W)
        outs["mem_p"].append(mem_kv_p.reshape(nb, MEM_LEN, 2, N_MEM_HEADS, MEM_HEAD_DIM))
        mem_kv_p = mem_kv_p.reshape(nb, MEM_LEN, 2 * MEM_W)
        mem_kv_s = cache_mem_kv[i].reshape(nd, MEM_LEN, 2 * MEM_W)
        xp, xs = ffn_both(xp, xs, i, 0)
        g_mix = norm_g[i, 2][None]
        if i % 2 == 0:
            proj_p = _in_proj(xp, g_mix, wt_gla, GLA_MAIN, tail_gla, li, TM_PROJ)
            proj_s = _in_proj(xs, g_mix, wt_gla, GLA_MAIN, tail_gla, li, TM_SAMPLE)
            mem_col = GLA_MAIN // MEM_W
            b_a = b_gla_a[li][None]
            gn = gla_onorm_g[li][None]
            s0 = jnp.zeros((nb, GLA_HEADS, GLA_DK, GLA_DV), F32)
            tok_p, sp = _gla(proj_p, s0, wa_pad[li], b_a, gn, seq, TL_GLA, GLA_CHUNK, seq)
            tok_s, ss = _gla(per_seq(proj_s[:nd]), state_gla[li], wa_pad[li], b_a, gn, sr, sr, sr, 1)
            tok_s = _pad_rows(every(tok_s, sr), sr, 0)
            outs["gla_p"].append(sp)
            outs["gla_s"].append(ss)
        else:
            assert 2 * KV_W == TN_PROJ and NSA_KV_COL % TN_PROJ == 0
            kv_tiles = (NSA_KV_COL // TN_PROJ, NSA_GATE_COL // TN_PROJ)
            proj_p, *kv_lin = _in_proj(xp, g_mix, wt_nsa, NSA_GATE_COL, tail_nsa, li, TM_PROJ_KV, kv_tiles,
                                       linear_kv=True)
            proj_s = _in_proj(xs, g_mix, wt_nsa, NSA_GATE_COL, tail_nsa, li, TM_SAMPLE, kv_tiles)
            mem_col = NSA_MEM_COL // MEM_W
            b1 = cmp_b1[li][:, None, :]
            gate_b = nsa_gate_b[li]
            bct, bsn, bw = _prompt_bias_tables(rel_bias)
            proj3 = proj_p.reshape(nb, seq, NSA_N)
            cmp_kv = _cmp_prompt(proj3, cmp_pe[li], w1[li], b1, w2[li])
            tok_p = _nsa_prompt(proj_p, cmp_kv, _pad_rows(gate_b[None], LANE, 1), bct, bsn, bw, nb, seq)
            def kv_rows(branch):
                return jnp.swapaxes(kv_lin[branch].reshape(nb, seq, NSA_G, 2, DH), 2, 3)

            outs["cmp_p"].append(kv_rows(0).reshape(nb, seq // PAGE, PAGE, 2, NSA_G, DH))
            outs["slc_p"].append(kv_rows(1).reshape(nb, seq // PAGE, PAGE, 2, NSA_G, DH))
            outs["win_p"].append(kv_rows(2)[:, seq - n_win:])
            t_pos = past_len
            bias_cd, bias_sd, bias_wd = _decode_bias_tables(rel_bias, t_pos, n_past, n_win)
            kv_s = jnp.swapaxes(proj_s[:nd, NSA_KV_COL:NSA_GATE_COL].reshape(nd, 3, NSA_G, 2, DH), 2, 3)
            outs["cmp_s"].append(kv_s[:, None, 0])
            outs["slc_s"].append(kv_s[:, None, 1])
            outs["win_s"].append(jnp.concatenate([state_win_kv[li][:, 1:], kv_s[:, None, 2]], axis=1))
            q8 = _pad_rows(proj_s[:nd, :TOK_W].reshape(nd, NSA_G, NSA_R, DH), 8, 2)
            new_cmp = _pad_rows(jnp.moveaxis(kv_s[:, 0], 1, 0).reshape(2, nd * NSA_G, DH), 32, 1)
            cmp_kv_s, cmp_last = _cmp_decode(page_table, _linear_cache(cache_cmp_kv[li]), new_cmp,
                                             cmp_pe[li], w1[li], b1, w2[li])
            o_c, sel = _sel_decode(q8, cmp_kv_s, cmp_last, bias_cd, t_pos)
            sel_flat = sel[:, :, 0, :TOPN].reshape(-1)

            def row8(a):
                return _pad_rows(a[:, :, None, :], 8, 2)

            gate_l = _pad_rows(_pad_rows(proj_s[:nd, NSA_GATE_COL:NSA_GATE_COL + n_gate].reshape(nd, NSA_G, NSA_R, 3),
                                         8, 2), LANE, 3)
            gate_b8 = _pad_rows(_pad_rows(gate_b.reshape(NSA_G, NSA_R, 3), 8, 1), LANE, 2)
            tok_s = _slc_decode(sel_flat, page_table, q8, o_c, gate_l, gate_b8,
                                _linear_cache(cache_slc_kv[li]), bias_sd,
                                row8(kv_s[:, 1, 0]), row8(kv_s[:, 1, 1]),
                                _linear_cache(state_win_kv[li]),
                                row8(kv_s[:, 2, 0]), row8(kv_s[:, 2, 1]), bias_wd, t_pos)
            tok_s = _pad_rows(tok_s[:, :, :NSA_R].reshape(nd, TOK_W), sr, 0).astype(BF16)
        mem_o_p = _mem_attn(proj_p, mem_col, mem_kv_p, seq, TM_MEM_ATTN)
        q_s = jnp.broadcast_to(proj_s[:nd, None, mem_col * MEM_W:(mem_col + 1) * MEM_W],
                               (nd, sr, MEM_W)).reshape(nd * sr, MEM_W)
        mem_o_s = _pad_rows(every(_mem_attn(q_s, 0, mem_kv_s, sr, sr), sr), sr, 0)
        xp = _out_proj(xp, tok_p, mem_o_p, w_o, i, norm_g[i, 3][None], TM_OUT)
        xs = _out_proj(xs, tok_s, mem_o_s, w_o, i, norm_g[i, 3][None], sr)
        xp, xs = ffn_both(xp, xs, i, 1)

    y_prompt = xp.reshape(nb, seq, D_MODEL)
    y_sample = xs[:nd].reshape(nd, 1, D_MODEL)
    st = lambda k: jnp.stack(outs[k])
    return (y_prompt, y_sample, st("gla_p"), st("cmp_p"), st("slc_p"), st("win_p"), st("mem_p"),
            st("gla_s"), st("cmp_s"), st("slc_s"), st("win_s"))
```

```python
import functools
import math

import jax
import jax.numpy as jnp
from jax import lax
from jax.experimental import pallas as pl
from jax.experimental.pallas import tpu as pltpu

F32 = jnp.float32
BF16 = jnp.bfloat16
HI = lax.Precision.HIGHEST

D_MODEL = 2048
D_FF = 5632
EPS = 1e-6
MEM_LEN = 256
N_MEM_HEADS = 4
MEM_HEAD_DIM = 128
MEM_W = N_MEM_HEADS * MEM_HEAD_DIM
TOK_W = D_MODEL - MEM_W
GLA_HEADS = 4
GLA_DV = TOK_W // GLA_HEADS
GLA_DK = GLA_DV // 2
GLA_RANK = 16
GLA_TAU = 16.0
GLA_CHUNK = 64
GLA_PAIR_W = 2 * GLA_DK
DH = 128
NSA_HEADS = TOK_W // DH
NSA_G = 3
NSA_R = NSA_HEADS // NSA_G
BLK = 64
TOPN = 16
WINDOW = 512
CMP_HID = 256
QBLK = 256
KV_W = NSA_G * DH
ROW_W = 2 * NSA_G
REL_BUCKETS = 32
REL_MAX_EXACT = 16
REL_MAX_DIST = 128
PAGE = 128
LANE = 128
HALF_LANE = LANE // 2
BLK_SHIFT = BLK.bit_length() - 1
VMEM_LIMIT = 56 * 1024 * 1024

TN_PROJ = 768
GLA_MAIN = 2 * GLA_HEADS * GLA_DK + 2 * TOK_W
GLA_A_COL = GLA_MAIN + MEM_W
GLA_N = GLA_MAIN + TN_PROJ
NSA_KV_COL = TOK_W
NSA_GATE_COL = NSA_KV_COL + 6 * KV_W
NSA_MEM_COL = NSA_GATE_COL + TN_PROJ - MEM_W
NSA_N = NSA_GATE_COL + TN_PROJ
SAMPLE_ROWS = 16
TM_SAMPLE = SAMPLE_ROWS
TM_FFN = 512
TM_PROJ = 1024
TM_PROJ_KV = 512
TM_MEMKV = 512
TM_OUT = 512
TL_GLA = 512
SLC_PAD = 256
WIN_PAD = WINDOW
WIN_KEYS = WIN_PAD + QBLK
SLC_NEAR = SLC_PAD + QBLK
FAR_CHUNK = 1024
NEG = -1e30
M_FLOOR = -1e29


def _cparams(sem, vmem=VMEM_LIMIT):
    return pltpu.CompilerParams(dimension_semantics=sem, vmem_limit_bytes=vmem)


def _sigmoid(x):
    return 1.0 / (1.0 + jnp.exp(-x))


def _rms(x, g):
    ms = jnp.mean(x * x, axis=-1, keepdims=True)
    return x * lax.rsqrt(ms + EPS) * g


def _nt(a, b, precision=None):
    return lax.dot_general(a, b, (((1,), (1,)), ((), ())), precision=precision,
                           preferred_element_type=F32)


def _tn(a, b, precision=None):
    return lax.dot_general(a, b, (((0,), (0,)), ((), ())), precision=precision,
                           preferred_element_type=F32)


def _dot(a, b, precision=None):
    return jnp.dot(a, b, precision=precision, preferred_element_type=F32)


def _split3(x):
    hi = x.astype(BF16)
    r1 = x - hi.astype(F32)
    mid = r1.astype(BF16)
    lo = (r1 - mid.astype(F32)).astype(BF16)
    return hi, mid, lo


def _ffn_kernel(x_ref, g1_ref, wg_ref, wu_ref, wd_ref, g2_ref, o_ref, *rest):
    bf_out, (xn_ref, acc_ref) = rest[:-2], rest[-2:]
    j = pl.program_id(1)

    @pl.when(j == 0)
    def _():
        xn_ref[...] = _rms(x_ref[...], g1_ref[...]).astype(BF16)
        acc_ref[...] = jnp.zeros_like(acc_ref)

    wg, wu, wd = (w_ref[...].astype(BF16) for w_ref in (wg_ref, wu_ref, wd_ref))
    for out_ref, w in zip(bf_out, (wg, wu, wd)):
        out_ref[...] = w
    xn = xn_ref[...]
    gate = _dot(xn, wg)
    up = _dot(xn, wu)
    h = (gate * _sigmoid(gate) * up).astype(BF16)
    acc_ref[...] += _dot(h, wd)

    @pl.when(j == pl.num_programs(1) - 1)
    def _():
        o_ref[...] = x_ref[...] + 0.5 * _rms(acc_ref[...], g2_ref[...])


def _ffn_half(x, g1, wg, wu, wd, g2, tm, tf=512, f32_weights_at=None):
    m = x.shape[0]
    y_shape = jax.ShapeDtypeStruct((m, D_MODEL), F32)
    y_spec = pl.BlockSpec((tm, D_MODEL), lambda i, j: (i, 0))
    col_spec = pl.BlockSpec((D_MODEL, tf), lambda i, j: (0, j))
    row_spec = pl.BlockSpec((tf, D_MODEL), lambda i, j: (j, 0))
    if f32_weights_at is None:
        w_specs = [col_spec, col_spec, row_spec]
        out_shape, out_specs = y_shape, y_spec
    else:
        assert m == tm, "the bf16 copies are written once, by a single row tile"
        layer, half = f32_weights_at
        w_specs = [pl.BlockSpec((None, None, D_MODEL, tf), lambda i, j: (layer, half, 0, j)),
                   pl.BlockSpec((None, None, D_MODEL, tf), lambda i, j: (layer, half, 0, j)),
                   pl.BlockSpec((None, None, tf, D_MODEL), lambda i, j: (layer, half, j, 0))]
        out_shape = (y_shape, jax.ShapeDtypeStruct((D_MODEL, D_FF), BF16),
                     jax.ShapeDtypeStruct((D_MODEL, D_FF), BF16), jax.ShapeDtypeStruct((D_FF, D_MODEL), BF16))
        out_specs = (y_spec, col_spec, col_spec, row_spec)
    return pl.pallas_call(
        _ffn_kernel,
        out_shape=out_shape,
        grid=(m // tm, D_FF // tf),
        in_specs=[y_spec, pl.BlockSpec((1, D_MODEL), lambda i, j: (0, 0))] + w_specs
                 + [pl.BlockSpec((1, D_MODEL), lambda i, j: (0, 0))],
        out_specs=out_specs,
        scratch_shapes=[pltpu.VMEM((tm, D_MODEL), BF16), pltpu.VMEM((tm, D_MODEL), F32)],
        compiler_params=_cparams(("parallel", "arbitrary")),
        name="ffn_half",
    )(x, g1, wg, wu, wd, g2)


def _norm_matmul_kernel(x_ref, g_ref, w_ref, o_ref, xn_ref):
    @pl.when(pl.program_id(1) == 0)
    def _():
        xn_ref[...] = _rms(x_ref[...], g_ref[...]).astype(BF16)

    o_ref[...] = _dot(xn_ref[...], w_ref[...])


def _norm_matmul(x, g, w, layer, tm, tn):
    m, n = x.shape[0], w.shape[2]
    return pl.pallas_call(
        _norm_matmul_kernel,
        out_shape=jax.ShapeDtypeStruct((m, n), F32),
        grid=(m // tm, n // tn),
        in_specs=[
            pl.BlockSpec((tm, D_MODEL), lambda i, j: (i, 0)),
            pl.BlockSpec((1, D_MODEL), lambda i, j: (0, 0)),
            pl.BlockSpec((None, D_MODEL, tn), lambda i, j: (layer, 0, j)),
        ],
        out_specs=pl.BlockSpec((tm, tn), lambda i, j: (i, j)),
        scratch_shapes=[pltpu.VMEM((tm, D_MODEL), BF16)],
        compiler_params=_cparams(("parallel", "arbitrary")),
        name="norm_matmul",
    )(x, g, w)


def _in_proj_kernel(x_ref, g_ref, wm_ref, wt_ref, o_ref, *rest, n_main, kv_tiles):
    lin_refs, xn_ref = rest[:-1], rest[-1]
    j = pl.program_id(1)
    kv_lo, kv_hi = kv_tiles
    is_kv = (j >= kv_lo) & (j < kv_hi)
    rows = x_ref.shape[0]

    @pl.when(j == 0)
    def _():
        xn_ref[...] = _rms(x_ref[...], g_ref[...]).astype(BF16)

    @pl.when((j < n_main) & jnp.logical_not(is_kv))
    def _():
        o_ref[...] = _nt(xn_ref[...], wm_ref[...])

    for t in range(kv_lo, kv_hi):
        @pl.when(j == t)
        def _(t=t):
            res = _nt(xn_ref[...], wm_ref[...])
            for g in range(NSA_G):
                for kv in range(2):
                    src, dst = kv * NSA_G + g, g * 2 + kv
                    blk = res[:, src * DH:(src + 1) * DH]
                    o_ref[:, dst * DH:(dst + 1) * DH] = blk
                    if lin_refs:
                        lin_refs[t - kv_lo][pl.ds(dst, rows, stride=ROW_W), :] = blk

    @pl.when(j == n_main)
    def _():
        o_ref[...] = _nt(xn_ref[...], wt_ref[...])


def _in_proj(x, g, w_main, main_cols, w_tail, layer, tm, kv_tiles=(0, 0), linear_kv=False):
    m = x.shape[0]
    n_main = main_cols // TN_PROJ
    out_shape = [jax.ShapeDtypeStruct((m, main_cols + TN_PROJ), F32)]
    out_specs = [pl.BlockSpec((tm, TN_PROJ), lambda i, j: (i, j))]
    if linear_kv:
        n_kv = kv_tiles[1] - kv_tiles[0]
        out_shape += [jax.ShapeDtypeStruct((m * ROW_W, DH), F32)] * n_kv
        out_specs += [pl.BlockSpec((tm * ROW_W, DH), lambda i, j: (i, 0))] * n_kv
    out = pl.pallas_call(
        functools.partial(_in_proj_kernel, n_main=n_main, kv_tiles=kv_tiles),
        out_shape=out_shape,
        grid=(m // tm, n_main + 1),
        in_specs=[
            pl.BlockSpec((tm, D_MODEL), lambda i, j: (i, 0)),
            pl.BlockSpec((1, D_MODEL), lambda i, j: (0, 0)),
            pl.BlockSpec((None, TN_PROJ, D_MODEL), lambda i, j: (layer, jnp.minimum(j, n_main - 1), 0)),
            pl.BlockSpec((None, TN_PROJ, D_MODEL), lambda i, j: (layer, 0, 0)),
        ],
        out_specs=out_specs,
        scratch_shapes=[pltpu.VMEM((tm, D_MODEL), BF16)],
        compiler_params=_cparams(("parallel", "arbitrary")),
        name="in_proj",
    )(x, g, w_main, w_tail)
    return out if linear_kv else out[0]


def _out_proj_kernel(x_ref, tok_ref, mem_ref, wt_ref, wm_ref, g_ref, o_ref):
    y = _dot(tok_ref[...], wt_ref[...]) + _dot(mem_ref[...], wm_ref[...])
    o_ref[...] = x_ref[...] + _rms(y, g_ref[...])


def _out_proj(x, tok, mem_o, w_o, layer, g, tm):
    m = x.shape[0]
    return pl.pallas_call(
        _out_proj_kernel,
        out_shape=jax.ShapeDtypeStruct((m, D_MODEL), F32),
        grid=(m // tm,),
        in_specs=[
            pl.BlockSpec((tm, D_MODEL), lambda i: (i, 0)),
            pl.BlockSpec((tm, TOK_W), lambda i: (i, 0)),
            pl.BlockSpec((tm, MEM_W), lambda i: (i, 0)),
            pl.BlockSpec((None, TOK_W, D_MODEL), lambda i: (layer, 0, 0), pipeline_mode=pl.Buffered(1)),
            pl.BlockSpec((None, MEM_W, D_MODEL), lambda i: (layer, TOK_W // MEM_W, 0), pipeline_mode=pl.Buffered(1)),
            pl.BlockSpec((1, D_MODEL), lambda i: (0, 0)),
        ],
        out_specs=pl.BlockSpec((tm, D_MODEL), lambda i: (i, 0)),
        compiler_params=_cparams(("parallel",)),
        name="out_proj",
    )(x, tok, mem_o, w_o, w_o, g)


def _mem_head(q_ref, kv_ref, h):
    q = (q_ref[:, h * DH:(h + 1) * DH] * (MEM_HEAD_DIM ** -0.5)).astype(BF16)
    k = kv_ref[:, h * DH:(h + 1) * DH].astype(BF16)
    v = kv_ref[:, MEM_W + h * DH:MEM_W + (h + 1) * DH].astype(BF16)
    s = _nt(q, k)
    e = jnp.exp(s - jnp.max(s, axis=-1, keepdims=True))
    p = e / jnp.sum(e, axis=-1, keepdims=True)
    return _dot(p.astype(BF16), v).astype(BF16)


def _mem_attn_kernel(q_ref, kv_ref, o_ref):
    for h in range(N_MEM_HEADS):
        o_ref[:, h * DH:(h + 1) * DH] = _mem_head(q_ref, kv_ref, h)


def _out_proj_mem_kernel(x_ref, tok_ref, q_ref, kv_ref, wt_ref, wm_ref, g_ref, o_ref):
    y = _dot(tok_ref[...], wt_ref[...])
    for h in range(N_MEM_HEADS):
        y = y + _dot(_mem_head(q_ref, kv_ref, h), wm_ref[h * DH:(h + 1) * DH, :])
    o_ref[...] = x_ref[...] + _rms(y, g_ref[...])


def _out_proj_mem(x, tok, q_arr, q_col_block, mem_kv, rows_per_batch, w_o, layer, g, tm):
    m = x.shape[0]
    per = rows_per_batch // tm
    return pl.pallas_call(
        _out_proj_mem_kernel,
        out_shape=jax.ShapeDtypeStruct((m, D_MODEL), F32),
        grid=(m // tm,),
        in_specs=[
            pl.BlockSpec((tm, D_MODEL), lambda i: (i, 0)),
            pl.BlockSpec((tm, TOK_W), lambda i: (i, 0)),
            pl.BlockSpec((tm, MEM_W), lambda i: (i, q_col_block)),
            pl.BlockSpec((None, MEM_LEN, 2 * MEM_W), lambda i: (i // per, 0, 0)),
            pl.BlockSpec((None, TOK_W, D_MODEL), lambda i: (layer, 0, 0), pipeline_mode=pl.Buffered(1)),
            pl.BlockSpec((None, MEM_W, D_MODEL), lambda i: (layer, TOK_W // MEM_W, 0), pipeline_mode=pl.Buffered(1)),
            pl.BlockSpec((1, D_MODEL), lambda i: (0, 0)),
        ],
        out_specs=pl.BlockSpec((tm, D_MODEL), lambda i: (i, 0)),
        compiler_params=_cparams(("parallel",)),
        name="out_proj_mem",
    )(x, tok, q_arr, mem_kv, w_o, w_o, g)


def _mem_attn(q_arr, q_col_block, mem_kv, rows_per_batch, tm):
    nb = mem_kv.shape[0]
    per = rows_per_batch // tm
    return pl.pallas_call(
        _mem_attn_kernel,
        out_shape=jax.ShapeDtypeStruct((nb * rows_per_batch, MEM_W), BF16),
        grid=(nb, per),
        in_specs=[
            pl.BlockSpec((tm, MEM_W), lambda b, i: (b * per + i, q_col_block)),
            pl.BlockSpec((None, MEM_LEN, 2 * MEM_W), lambda b, i: (b, 0, 0)),
        ],
        out_specs=pl.BlockSpec((tm, MEM_W), lambda b, i: (b * per + i, 0)),
        compiler_params=_cparams(("parallel", "parallel")),
        name="mem_attn",
    )(q_arr, mem_kv)


GLA_PAIRS = GLA_HEADS // 2


def _gla_kernel(q_ref, k_ref, v_ref, r_ref, a_ref, s0_ref, wa_ref, ba_ref, gn_ref,
                tok_ref, s_out_ref, s_ref, cum_ref, *, chunk, n_valid):
    l = pl.program_id(1)
    tl = q_ref.shape[0]

    @pl.when(l == 0)
    def _():
        for p in range(GLA_PAIRS):
            s_ref[p] = s0_ref[2 * p:2 * p + 2].reshape(GLA_PAIR_W, GLA_DV).T

    lane = lax.broadcasted_iota(jnp.int32, (1, GLA_PAIR_W), 1)
    head_mask = [(lane < GLA_DK).astype(F32), (lane >= GLA_DK).astype(F32)]
    ti = lax.broadcasted_iota(jnp.int32, (chunk, chunk), 0)
    si = lax.broadcasted_iota(jnp.int32, (chunk, chunk), 1)
    causal = si <= ti
    tri = jnp.where(causal, 1.0, 0.0).astype(BF16)

    a_hi, a_lo, _ = _split3(a_ref[...])
    w_hi, w_lo, _ = _split3(wa_ref[...])
    z = _dot(a_hi, w_hi) + _dot(a_lo, w_hi) + _dot(a_hi, w_lo) + ba_ref[...]
    la_all = -(jnp.maximum(-z, 0.0) + jnp.log1p(jnp.exp(-jnp.abs(z)))) / GLA_TAU
    pos = l * tl + lax.broadcasted_iota(jnp.int32, (tl, 1), 0)
    la_all = jnp.where(pos < n_valid, la_all, 0.0)
    pieces = _split3(la_all)
    for c0 in range(0, tl, chunk):
        cum_ref[c0:c0 + chunk, :] = functools.reduce(jnp.add, [_dot(tri, pc[c0:c0 + chunk]) for pc in pieces])

    def step(ci, carry):
        r0 = pl.multiple_of(ci * chunk, chunk)
        rows = pl.ds(r0, chunk)
        for p in range(GLA_PAIRS):
            pair = slice(p * GLA_PAIR_W, (p + 1) * GLA_PAIR_W)
            b = cum_ref[rows, pair]
            bl = b[chunk - 1:chunk, :]
            q = q_ref[rows, pair] * (GLA_DK ** -0.5)
            k = k_ref[rows, pair]
            qe = q * jnp.exp(b)
            ke = (k * jnp.exp(-b)).astype(BF16)
            kd = k * jnp.exp(bl - b)
            st_old = s_ref[p]
            st_bf = st_old.astype(BF16)
            upd = None
            for h in range(2):
                head = slice((2 * p + h) * GLA_DV, (2 * p + h + 1) * GLA_DV)
                v = v_ref[rows, head].astype(BF16)
                qm = (qe * head_mask[h]).astype(BF16)
                att = jnp.where(causal, _nt(qm, ke), 0.0)
                o = _nt(qm, st_bf) + _dot(att.astype(BF16), v)
                o = _rms(o, gn_ref[...])
                r = r_ref[rows, head]
                tok_ref[rows, head] = (o * (r * _sigmoid(r))).astype(tok_ref.dtype)
                u = _tn(v, (kd * head_mask[h]).astype(BF16))
                upd = u if upd is None else upd + u
            s_ref[p] = jnp.exp(bl) * st_old + upd
        return carry

    lax.fori_loop(0, tl // chunk, step, 0)

    @pl.when(l == pl.num_programs(1) - 1)
    def _():
        for p in range(GLA_PAIRS):
            s_out_ref[2 * p:2 * p + 2] = s_ref[p].T.reshape(2, GLA_DK, GLA_DV)


def _gla(proj, s0, wa_pad, b_a, gn, seq, tl, chunk, n_valid):
    nb = s0.shape[0]
    per = seq // tl
    qk_w = GLA_HEADS * GLA_DK
    row = lambda b, l: b * per + l

    return pl.pallas_call(
        functools.partial(_gla_kernel, chunk=chunk, n_valid=n_valid),
        out_shape=(jax.ShapeDtypeStruct((nb * seq, TOK_W), BF16),
                   jax.ShapeDtypeStruct((nb, GLA_HEADS, GLA_DK, GLA_DV), F32)),
        grid=(nb, per),
        in_specs=[
            pl.BlockSpec((tl, qk_w), lambda b, l: (row(b, l), 0)),
            pl.BlockSpec((tl, qk_w), lambda b, l: (row(b, l), 1)),
            pl.BlockSpec((tl, TOK_W), lambda b, l: (row(b, l), 2 * qk_w // TOK_W)),
            pl.BlockSpec((tl, TOK_W), lambda b, l: (row(b, l), 2 * qk_w // TOK_W + 1)),
            pl.BlockSpec((tl, LANE), lambda b, l: (row(b, l), GLA_A_COL // LANE)),
            pl.BlockSpec((None, GLA_HEADS, GLA_DK, GLA_DV), lambda b, l: (b, 0, 0, 0)),
            pl.BlockSpec((LANE, qk_w), lambda b, l: (0, 0)),
            pl.BlockSpec((1, qk_w), lambda b, l: (0, 0)),
            pl.BlockSpec((1, GLA_DV), lambda b, l: (0, 0)),
        ],
        out_specs=(pl.BlockSpec((tl, TOK_W), lambda b, l: (row(b, l), 0)),
                   pl.BlockSpec((None, GLA_HEADS, GLA_DK, GLA_DV), lambda b, l: (b, 0, 0, 0))),
        scratch_shapes=[pltpu.VMEM((GLA_PAIRS, GLA_DV, GLA_PAIR_W), F32), pltpu.VMEM((tl, qk_w), F32)],
        compiler_params=_cparams(("parallel", "arbitrary")),
        name="gla",
    )(proj, proj, proj, proj, proj, s0, wa_pad, b_a, gn)


def _masked_softmax(s, valid):
    s = jnp.where(valid, s, -jnp.inf)
    m = jnp.max(s, axis=-1, keepdims=True)
    m = jnp.where(m > -jnp.inf, m, 0.0)
    e = jnp.exp(s - m)
    return e / jnp.maximum(jnp.sum(e, axis=-1, keepdims=True), 1e-30)


def _online_update(state, s, v):
    m, l, acc = state
    r, nq, w = s.shape
    m_new = jnp.maximum(m, jnp.max(s, axis=-1, keepdims=True))
    alpha = jnp.exp(m - m_new)
    p = jnp.exp(s - m_new)
    l = alpha * l + jnp.sum(p, axis=-1, keepdims=True)
    pv = _dot(p.reshape(r * nq, w).astype(BF16), v).reshape(r, nq, DH)
    return m_new, l, alpha * acc + pv


def _compress_tail(xflat, w1, b1, w2):
    h = _dot(xflat, w1) + b1
    h = h * _sigmoid(h)
    return _dot(h.astype(BF16), w2)


def _cmp_prompt_kernel(x0_ref, x1_ref, x2_ref, pe_ref, w1_ref, b1_ref, w2_ref, o_ref, xflat_ref):
    x_refs = (x0_ref, x1_ref, x2_ref)
    nb, seq = x0_ref.shape[0], x0_ref.shape[1]
    nblk = seq // BLK
    for j in range(BLK):
        pe_j = pe_ref[j:j + 1, :]
        for b in range(nb):
            for g in range(NSA_G):
                xj = x_refs[g][b, pl.ds(j, nblk, stride=BLK), :]
                row = (b * NSA_G + g) * nblk
                xflat_ref[row:row + nblk, j * DH:(j + 1) * DH] = (xj + pe_j).astype(BF16)
    out = _compress_tail(xflat_ref[...], w1_ref[...], b1_ref[...], w2_ref[...])
    o_ref[...] = out.reshape(nb, NSA_G, nblk, DH)


def _kv_block(branch, g, kv):
    return NSA_KV_COL // DH + branch * 2 * NSA_G + g * 2 + kv


def _cmp_prompt(proj3, pe, w1, b1, w2):
    nb, seq, _ = proj3.shape
    nblk = seq // BLK
    return pl.pallas_call(
        _cmp_prompt_kernel,
        out_shape=jax.ShapeDtypeStruct((2, nb, NSA_G, nblk, DH), F32),
        grid=(2,),
        in_specs=[
            pl.BlockSpec((nb, seq, DH), lambda kv: (0, 0, _kv_block(0, 0, kv))),
            pl.BlockSpec((nb, seq, DH), lambda kv: (0, 0, _kv_block(0, 1, kv))),
            pl.BlockSpec((nb, seq, DH), lambda kv: (0, 0, _kv_block(0, 2, kv))),
            pl.BlockSpec((None, BLK, DH), lambda kv: (kv, 0, 0)),
            pl.BlockSpec((None, BLK * DH, CMP_HID), lambda kv: (kv, 0, 0)),
            pl.BlockSpec((None, 1, CMP_HID), lambda kv: (kv, 0, 0)),
            pl.BlockSpec((None, CMP_HID, DH), lambda kv: (kv, 0, 0)),
        ],
        out_specs=pl.BlockSpec((None, nb, NSA_G, nblk, DH), lambda kv: (kv, 0, 0, 0, 0)),
        scratch_shapes=[pltpu.VMEM((nb * NSA_G * nblk, BLK * DH), BF16)],
        compiler_params=_cparams(("arbitrary",)),
        name="cmp_prompt",
    )(proj3, proj3, proj3, pe, w1, b1, w2)


def _nsa_prompt_kernel(q_ref, gl_ref, gb_ref, kc_ref, vc_ref, ks_ref, vs_ref, kw_ref, vw_ref,
                       bct_ref, bsn_ref, bw_ref, o_ref, ksb, vsb, kwb, vwb):
    g = pl.program_id(1)
    qi = pl.program_id(2)
    seq = ks_ref.shape[0]
    nblk = seq // BLK
    R = NSA_R

    @pl.when(qi == 0)
    def _():
        vsb[0:SLC_PAD, :] = jnp.zeros((SLC_PAD, DH), BF16)
        vsb[SLC_PAD:SLC_PAD + seq, :] = vs_ref[...].astype(BF16)
        ones_col = jnp.where(lax.broadcasted_iota(jnp.int32, (1, DH), 1) == 0, 1.0, 0.0)
        vwb[0:WIN_PAD, 0:DH] = jnp.zeros((WIN_PAD, DH), BF16)
        vwb[WIN_PAD:WIN_PAD + seq, 0:DH] = vw_ref[...].astype(BF16)
        vwb[:, DH:2 * DH] = jnp.broadcast_to(ones_col, (WIN_PAD + seq, DH)).astype(BF16)
        for src, dst, pad, per_block in ((ks_ref, ksb, SLC_PAD, True), (kw_ref, kwb, WIN_PAD, False)):
            dst[0:pad, 0:DH] = jnp.zeros((pad, DH), BF16)
            dst[pad:pad + seq, 0:DH] = src[...].astype(BF16)
            pos = lax.broadcasted_iota(jnp.int32, (pad + seq, 1), 0) - pad
            feat = lax.broadcasted_iota(jnp.int32, (1, DH), 1)
            masked = (feat == HALF_LANE) & (pos < 0)
            if per_block:
                masked = masked | ((pos >= 0) & (lax.shift_right_arithmetic(pos, BLK_SHIFT) == feat))
            dst[:, DH:2 * DH] = jnp.where(masked, NEG, 0.0).astype(BF16)

    q = q_ref[...] * (DH ** -0.5)
    q_all = jnp.concatenate([q[:, r * DH:(r + 1) * DH] for r in range(R)], axis=0)
    q_bf = q_all.astype(BF16)
    i_col = lax.broadcasted_iota(jnp.int32, (QBLK, 1), 0)
    t_col = qi * QBLK + i_col
    n_row = lax.broadcasted_iota(jnp.int32, (1, nblk), 1)
    row0 = pl.multiple_of(qi * QBLK, QBLK)
    lane2 = lax.broadcasted_iota(jnp.int32, (1, LANE), 1)

    s_c = _nt(q_all, kc_ref[...], HI).reshape(R, QBLK, nblk)
    tb_col = lax.shift_right_arithmetic(t_col, BLK_SHIFT)
    rel = tb_col - n_row
    bias_c = []
    for r in range(R):
        tab = bct_ref[r]
        bias_c.append(jnp.where(rel == 0, tab[:, 0:1],
                      jnp.where(rel == 1, tab[:, 1:2],
                      jnp.where(rel == 2, tab[:, 2:3], tab[:, 3:4]))))
    s_c = s_c + jnp.stack(bias_c, axis=0)
    valid_c = (t_col - (n_row * BLK + (BLK - 1))) >= 0
    p_c = _masked_softmax(s_c, valid_c[None])
    o_c = _dot(p_c.reshape(R * QBLK, nblk).astype(BF16), vc_ref[...].astype(BF16))

    imp = jnp.sum(p_c, axis=0)
    forced = (n_row == 0) | (n_row == tb_col) | (n_row == tb_col - 1)
    future = n_row * BLK > t_col
    score = jnp.where(forced, jnp.inf, jnp.where(future, -jnp.inf, imp))
    score_t = jnp.concatenate([score, jnp.full((QBLK, LANE - nblk), -jnp.inf, F32)], axis=1).T[:HALF_LANE]
    n_sub = lax.broadcasted_iota(jnp.int32, (HALF_LANE, 1), 0)
    rank_t = jnp.zeros((HALF_LANE, QBLK), F32)
    for i in range(nblk):
        cand = score_t[i:i + 1, :]
        wins_tie = jnp.where(n_sub > i, 1.0, 0.0)
        rank_t = rank_t + jnp.where(cand > score_t, 1.0, jnp.where(cand == score_t, wins_tie, 0.0))
    rank = jnp.concatenate([rank_t, jnp.zeros((LANE - HALF_LANE, QBLK), F32)], axis=0).T
    left = lane2 < HALF_LANE
    n_far = (qi * QBLK - SLC_PAD) // BLK
    dropped = rank >= float(min(TOPN, nblk))
    flag_all = jnp.where(left, jnp.where(dropped, 1.0, 0.0), jnp.where(lane2 == HALF_LANE, 1.0, 0.0))
    flag_far = jnp.where(left & (lane2 >= n_far), 1.0, flag_all)
    q_far = jnp.concatenate([q_bf, jnp.concatenate([flag_far.astype(BF16)] * R, axis=0)], axis=1)
    q_near = jnp.concatenate([q_bf, jnp.concatenate([flag_all.astype(BF16)] * R, axis=0)], axis=1)

    def far_body(kc_i, state):
        start = pl.multiple_of(SLC_PAD + kc_i * FAR_CHUNK, BLK)
        s = _nt(q_far, ksb[pl.ds(start, FAR_CHUNK), :]).reshape(R, QBLK, FAR_CHUNK)
        return _online_update(state, s, vsb[pl.ds(start, FAR_CHUNK), :])

    blk_per_chunk = FAR_CHUNK // BLK
    n_chunks = (jnp.maximum(n_far, 0) + (blk_per_chunk - 1)) // blk_per_chunk
    state = (jnp.full((R, QBLK, 1), M_FLOOR, F32), jnp.zeros((R, QBLK, 1), F32), jnp.zeros((R, QBLK, DH), F32))
    state = lax.fori_loop(0, n_chunks, far_body, state)

    s = _nt(q_near, ksb[pl.ds(row0, SLC_NEAR), :]).reshape(R, QBLK, SLC_NEAR) + bsn_ref[...]
    _, l_s, acc_s = _online_update(state, s, vsb[pl.ds(row0, SLC_NEAR), :])
    o_s = (acc_s / jnp.maximum(l_s, 1e-30)).reshape(R * QBLK, DH)

    s = _nt(q_near, kwb[pl.ds(row0, WIN_KEYS), :]).reshape(R, QBLK, WIN_KEYS) + bw_ref[...]
    e_w = jnp.exp(s - jnp.max(s, axis=-1, keepdims=True))
    ow = _dot(e_w.reshape(R * QBLK, WIN_KEYS).astype(BF16), vwb[pl.ds(row0, WIN_KEYS), :])
    o_w = ow[:, :DH] / jnp.maximum(ow[:, DH:DH + 1], 1e-30)

    gates = _sigmoid(gl_ref[...] + gb_ref[...])
    src = lax.broadcasted_iota(jnp.int32, (LANE, LANE), 0)
    dst = lax.broadcasted_iota(jnp.int32, (LANE, LANE), 1)
    pick = jnp.where((src == g * (3 * R) + dst) & (dst < 3 * R), 1.0, 0.0)
    gsel = _dot(gates, pick, HI)
    for r in range(R):
        rows = slice(r * QBLK, (r + 1) * QBLK)
        o = (gsel[:, 3 * r:3 * r + 1] * o_c[rows] + gsel[:, 3 * r + 1:3 * r + 2] * o_s[rows]
             + gsel[:, 3 * r + 2:3 * r + 3] * o_w[rows])
        o_ref[:, r * DH:(r + 1) * DH] = o.astype(o_ref.dtype)


def _nsa_prompt(proj2, cmp_kv, gate_b_pad, bct, bsn, bw, nb, seq):
    nq = seq // QBLK
    nblk = seq // BLK
    assert nblk <= HALF_LANE, "mask features of the selected branch hold at most 64 key blocks"
    assert seq % FAR_CHUNK == 0
    proj3 = proj2.reshape(nb, seq, NSA_N)

    def kv_spec(branch, kv):
        return pl.BlockSpec((None, seq, DH), lambda b, g, qi: (b, 0, _kv_block(branch, g, kv)))

    return pl.pallas_call(
        _nsa_prompt_kernel,
        out_shape=jax.ShapeDtypeStruct((nb * seq, TOK_W), BF16),
        grid=(nb, NSA_G, nq),
        in_specs=[
            pl.BlockSpec((QBLK, NSA_R * DH), lambda b, g, qi: (b * nq + qi, g)),
            pl.BlockSpec((QBLK, LANE), lambda b, g, qi: (b * nq + qi, NSA_GATE_COL // LANE)),
            pl.BlockSpec((1, LANE), lambda b, g, qi: (0, 0)),
            pl.BlockSpec((None, None, None, nblk, DH), lambda b, g, qi: (0, b, g, 0, 0)),
            pl.BlockSpec((None, None, None, nblk, DH), lambda b, g, qi: (1, b, g, 0, 0)),
            kv_spec(1, 0), kv_spec(1, 1), kv_spec(2, 0), kv_spec(2, 1),
            pl.BlockSpec((None, NSA_R, QBLK, LANE), lambda b, g, qi: (g, 0, 0, 0)),
            pl.BlockSpec((None, NSA_R, QBLK, SLC_NEAR), lambda b, g, qi: (g, 0, 0, 0)),
            pl.BlockSpec((None, NSA_R, QBLK, WIN_KEYS), lambda b, g, qi: (g, 0, 0, 0)),
        ],
        out_specs=pl.BlockSpec((QBLK, NSA_R * DH), lambda b, g, qi: (b * nq + qi, g)),
        scratch_shapes=[pltpu.VMEM((SLC_PAD + seq, 2 * DH), BF16), pltpu.VMEM((SLC_PAD + seq, DH), BF16),
                        pltpu.VMEM((WIN_PAD + seq, 2 * DH), BF16), pltpu.VMEM((WIN_PAD + seq, 2 * DH), BF16)],
        compiler_params=_cparams(("parallel", "parallel", "arbitrary")),
        name="nsa_prompt",
    )(proj2, proj2, gate_b_pad, cmp_kv, cmp_kv, proj3, proj3, proj3, proj3, bct, bsn, bw)


SUB_PAGES = 8
RING = 4
PAGE_ROWS = PAGE * ROW_W


def _linear_cache(cache):
    return jnp.transpose(cache, (0, 1, 3, 2, 4)).reshape(-1, DH)


def _cmp_decode_kernel(pt_ref, cache_ref, new_ref, pe_ref, w1_ref, b1_ref, w2_ref, o_ref, last_ref,
                       buf, sem, xflat_ref, xlast_ref, *, n_sub):
    b = pl.program_id(0)
    total = pl.num_programs(0) * n_sub
    sub_blk = SUB_PAGES * PAGE // BLK
    seq_blk = n_sub * sub_blk

    def page_copy(s, p):
        page = pt_ref[s // n_sub, (s % n_sub) * SUB_PAGES + p]
        return pltpu.make_async_copy(cache_ref.at[pl.ds(pl.multiple_of(page * PAGE_ROWS, PAGE_ROWS), PAGE_ROWS), :],
                                     buf.at[s % RING, pl.ds(p * PAGE_ROWS, PAGE_ROWS), :], sem.at[s % RING])

    def start_sub(s):
        for p in range(SUB_PAGES):
            page_copy(s, p).start()

    @pl.when(b == 0)
    def _():
        for s in range(RING):
            start_sub(s)

    def body(i, c):
        s = b * n_sub + i
        for p in range(SUB_PAGES):
            page_copy(s, p).wait()
        slot = s % RING
        row0 = pl.multiple_of(i * sub_blk, sub_blk)
        by_row = jnp.swapaxes(buf[slot].reshape(sub_blk, BLK * ROW_W, DH), 0, 1)
        for kv in range(2):
            for j in range(BLK):
                pe_j = pe_ref[kv, j:j + 1, :]
                for g in range(NSA_G):
                    xj = by_row[j * ROW_W + g * 2 + kv]
                    xflat_ref[kv, pl.ds(g * seq_blk + row0, sub_blk), j * DH:(j + 1) * DH] = (xj + pe_j).astype(BF16)

        @pl.when(s + RING < total)
        def _():
            start_sub(s + RING)

        return c

    lax.fori_loop(0, n_sub, body, 0)
    for kv in range(2):
        out = _compress_tail(xflat_ref[kv], w1_ref[kv], b1_ref[kv], w2_ref[kv])
        o_ref[kv] = out.reshape(NSA_G, seq_blk, DH)

    @pl.when(b == 0)
    def _():
        rows = new_ref.shape[1]
        for kv in range(2):
            for j in range(BLK):
                pe_j = jnp.broadcast_to(pe_ref[kv, j:j + 1, :], (rows, DH))
                xj = new_ref[kv] + pe_j if j == 0 else pe_j
                xlast_ref[:, j * DH:(j + 1) * DH] = xj.astype(BF16)
            last_ref[kv] = _compress_tail(xlast_ref[...], w1_ref[kv], b1_ref[kv], w2_ref[kv])


def _cmp_decode(page_table, cache, new_rows, pe, w1, b1, w2):
    nd, n_pages = page_table.shape
    n_sub = n_pages // SUB_PAGES
    seq_blk = n_pages * PAGE // BLK
    rows = new_rows.shape[1]

    def whole(shape):
        return pl.BlockSpec(shape, lambda b, pt: (0,) * len(shape), pipeline_mode=pl.Buffered(1))

    grid_spec = pltpu.PrefetchScalarGridSpec(
        num_scalar_prefetch=1,
        grid=(nd,),
        in_specs=[
            pl.BlockSpec(memory_space=pl.ANY),
            whole(new_rows.shape), whole(pe.shape), whole(w1.shape), whole(b1.shape), whole(w2.shape),
        ],
        out_specs=(pl.BlockSpec((2, None, NSA_G, seq_blk, DH), lambda b, pt: (0, b, 0, 0, 0)),
                   pl.BlockSpec((2, rows, DH), lambda b, pt: (0, 0, 0))),
        scratch_shapes=[pltpu.VMEM((RING, SUB_PAGES * PAGE_ROWS, DH), F32),
                        pltpu.SemaphoreType.DMA((RING,)),
                        pltpu.VMEM((2, NSA_G * seq_blk, BLK * DH), BF16),
                        pltpu.VMEM((rows, BLK * DH), BF16)],
    )
    return pl.pallas_call(
        functools.partial(_cmp_decode_kernel, n_sub=n_sub),
        out_shape=(jax.ShapeDtypeStruct((2, nd, NSA_G, seq_blk, DH), F32),
                   jax.ShapeDtypeStruct((2, rows, DH), F32)),
        grid_spec=grid_spec,
        compiler_params=_cparams(("arbitrary",)),
        name="cmp_decode",
    )(page_table, cache, new_rows, pe, w1, b1, w2)


SEL_LANES = 384


def _sel_decode_kernel(q_ref, kc_ref, vc_ref, last_ref, bias_ref, oc_ref, sel_ref, *, t_pos):
    b = pl.program_id(0)
    n_past = kc_ref.shape[1]
    n_blocks = n_past + 1
    n_lane = lax.broadcasted_iota(jnp.int32, (1, SEL_LANES), 1)
    n_lane_f = n_lane.astype(F32)
    head_row = lax.broadcasted_iota(jnp.int32, (8, 1), 0) < NSA_R
    tb = t_pos // BLK
    for g in range(NSA_G):
        q = q_ref[g] * (DH ** -0.5)
        bias = bias_ref[g]
        s_p = _nt(q, kc_ref[g], HI) + bias[:, :n_past]
        row = b * NSA_G + g
        k_last = last_ref[0, pl.ds(row, 1), :]
        v_last = last_ref[1, pl.ds(row, 1), :]
        s_l = jnp.sum(q * k_last, axis=-1, keepdims=True) + bias[:, n_past:n_past + 1]
        valid_p = (t_pos - (n_lane[:, :n_past] * BLK + (BLK - 1))) >= 0
        valid_l = (t_pos - (n_past * BLK + (BLK - 1))) >= 0
        s_p = jnp.where(valid_p, s_p, -jnp.inf)
        s_l = jnp.where(valid_l, s_l, -jnp.inf)
        m = jnp.maximum(jnp.max(s_p, axis=-1, keepdims=True), s_l)
        m = jnp.where(m > -jnp.inf, m, 0.0)
        e_p = jnp.exp(s_p - m)
        e_l = jnp.exp(s_l - m)
        den = jnp.maximum(jnp.sum(e_p, axis=-1, keepdims=True) + e_l, 1e-30)
        p_p = e_p / den
        p_l = e_l / den
        oc_ref[g] = _dot(p_p, vc_ref[g], HI) + p_l * v_last
        imp_p = jnp.sum(jnp.where(head_row, p_p, 0.0), axis=0, keepdims=True)
        imp_l = jnp.sum(jnp.where(head_row, p_l, 0.0), axis=0, keepdims=True)
        imp = jnp.concatenate([imp_p, jnp.broadcast_to(imp_l, (1, SEL_LANES - n_past))], axis=1)
        forced = (n_lane == 0) | (n_lane == tb) | (n_lane == tb - 1)
        future = n_lane * BLK > t_pos
        score = jnp.where(forced, jnp.inf, jnp.where(future, -jnp.inf, imp))
        cand = n_lane < n_blocks
        sel = jnp.zeros((1, LANE), jnp.int32)
        k_lane = lax.broadcasted_iota(jnp.int32, (1, LANE), 1)
        for k in range(min(TOPN, n_blocks)):
            best = jnp.max(jnp.where(cand, score, -jnp.inf), axis=-1, keepdims=True)
            idx_f = jnp.min(jnp.where(cand & (score == best), n_lane_f, float(SEL_LANES)), axis=-1, keepdims=True)
            idx = idx_f.astype(jnp.int32)
            sel = jnp.where(k_lane == k, idx, sel)
            cand = cand & (n_lane != idx)
        sel_ref[g] = jnp.broadcast_to(sel, (8, LANE))


def _sel_decode(q8, cmp_kv, cmp_last, bias_cd, t_pos):
    nd = q8.shape[0]
    n_past = cmp_kv.shape[3]
    return pl.pallas_call(
        functools.partial(_sel_decode_kernel, t_pos=t_pos),
        out_shape=(jax.ShapeDtypeStruct((nd, NSA_G, 8, DH), F32),
                   jax.ShapeDtypeStruct((nd, NSA_G, 8, LANE), jnp.int32)),
        grid=(nd,),
        in_specs=[
            pl.BlockSpec((None, NSA_G, 8, DH), lambda b: (b, 0, 0, 0)),
            pl.BlockSpec((None, None, NSA_G, n_past, DH), lambda b: (0, b, 0, 0, 0)),
            pl.BlockSpec((None, None, NSA_G, n_past, DH), lambda b: (1, b, 0, 0, 0)),
            pl.BlockSpec(cmp_last.shape, lambda b: (0, 0, 0)),
            pl.BlockSpec(bias_cd.shape, lambda b: (0, 0, 0)),
        ],
        out_specs=(pl.BlockSpec((None, NSA_G, 8, DH), lambda b: (b, 0, 0, 0)),
                   pl.BlockSpec((None, NSA_G, 8, LANE), lambda b: (b, 0, 0, 0))),
        compiler_params=_cparams(("parallel",)),
        name="sel_decode",
    )(q8, cmp_kv, cmp_kv, cmp_last, bias_cd)


HALF_ROWS = BLK * ROW_W


def _slc_decode_kernel(sel_ref, pt_ref, q_ref, oc_ref, gl_ref, gb_ref, slc_ref, bias_ref, nk_ref, nv_ref,
                       win_ref, nwk_ref, nwv_ref, bw_ref, o_ref, gbuf, sem, *, t_pos, n_past):
    b = pl.program_id(0)
    n_win = win_ref.shape[0] // ROW_W

    def block_copy(g, k):
        n = jnp.minimum(sel_ref[(b * NSA_G + g) * TOPN + k], n_past - 1)
        half = pt_ref[b, n // 2] * 2 + n % 2
        return pltpu.make_async_copy(slc_ref.at[pl.ds(pl.multiple_of(half * HALF_ROWS, HALF_ROWS), HALF_ROWS), :],
                                     gbuf.at[g * TOPN + k], sem)

    for g in range(NSA_G):
        for k in range(TOPN):
            block_copy(g, k).start()

    c = lax.broadcasted_iota(jnp.int32, (1, n_win), 1)
    dist = n_win - c
    valid = (dist >= 0) & (dist <= WINDOW) & (t_pos - dist >= 0)
    qs, o_w = [], []
    for g in range(NSA_G):
        q = q_ref[g] * (DH ** -0.5)
        q_bf = q.astype(BF16)
        qs.append((q, q_bf))
        kw = win_ref[pl.ds(2 * g, n_win, stride=ROW_W), :].astype(BF16)
        vw = win_ref[pl.ds(2 * g + 1, n_win, stride=ROW_W), :].astype(BF16)
        bw = bw_ref[g]
        s_w = jnp.where(valid, _nt(q_bf, kw) + bw[:, :n_win], -jnp.inf)
        s_n = jnp.sum(q * nwk_ref[g, 0:1, :], axis=-1, keepdims=True) + bw[:, n_win:n_win + 1]
        m_w = jnp.maximum(jnp.max(s_w, axis=-1, keepdims=True), s_n)
        e_w = jnp.exp(s_w - m_w)
        e_n = jnp.exp(s_n - m_w)
        den = jnp.maximum(jnp.sum(e_w, axis=-1, keepdims=True) + e_n, 1e-30)
        o_w.append((_dot(e_w.astype(BF16), vw) + e_n * nwv_ref[g, 0:1, :]) / den)

    for g in range(NSA_G):
        for k in range(TOPN):
            block_copy(g, k).wait()

    row0 = lax.broadcasted_iota(jnp.int32, (BLK, 1), 0) == 0
    j_row = lax.broadcasted_iota(jnp.int32, (1, BLK), 1)
    for g in range(NSA_G):
        q, q_bf = qs[g]
        scores, values = [], []
        for k in range(TOPN):
            n = sel_ref[(b * NSA_G + g) * TOPN + k]
            is_new = (n == n_past) & row0
            kb = jnp.where(is_new, nk_ref[g, 0:1, :], gbuf[g * TOPN + k, pl.ds(2 * g, BLK, stride=ROW_W), :])
            vb = jnp.where(is_new, nv_ref[g, 0:1, :], gbuf[g * TOPN + k, pl.ds(2 * g + 1, BLK, stride=ROW_W), :])
            s = _nt(q_bf, kb.astype(BF16)) + bias_ref[g, n]
            scores.append(jnp.where(t_pos - (n * BLK + j_row) >= 0, s, -jnp.inf))
            values.append(vb.astype(BF16))
        m = functools.reduce(jnp.maximum, [jnp.max(s, axis=-1, keepdims=True) for s in scores])
        m = jnp.where(m > -jnp.inf, m, 0.0)
        probs = [jnp.exp(s - m) for s in scores]
        l = functools.reduce(jnp.add, [jnp.sum(p, axis=-1, keepdims=True) for p in probs])
        acc = functools.reduce(jnp.add, [_dot(p.astype(BF16), v) for p, v in zip(probs, values)])
        o_s = acc / jnp.maximum(l, 1e-30)
        gates = _sigmoid(gl_ref[g] + gb_ref[g])
        o_ref[g] = gates[:, 0:1] * oc_ref[g] + gates[:, 1:2] * o_s + gates[:, 2:3] * o_w[g]


def _slc_decode(sel_flat, page_table, q8, o_c, gate_l, gate_b, slc_cache, bias_sd, new_k, new_v,
                win_cache, new_wk, new_wv, bias_wd, t_pos):
    nd = q8.shape[0]
    n_past = page_table.shape[1] * (PAGE // BLK)
    win_rows = win_cache.shape[0] // nd

    def per_b(b, sel, pt):
        return (b, 0, 0, 0)

    def whole(shape):
        return pl.BlockSpec(shape, lambda b, sel, pt: (0,) * len(shape), pipeline_mode=pl.Buffered(1))

    b_spec = pl.BlockSpec((None, NSA_G, 8, DH), per_b)
    grid_spec = pltpu.PrefetchScalarGridSpec(
        num_scalar_prefetch=2,
        grid=(nd,),
        in_specs=[
            b_spec, b_spec, b_spec, whole(gate_b.shape),
            pl.BlockSpec(memory_space=pl.ANY),
            whole(bias_sd.shape),
            b_spec, b_spec,
            pl.BlockSpec((win_rows, DH), lambda b, sel, pt: (b, 0)),
            b_spec, b_spec,
            whole(bias_wd.shape),
        ],
        out_specs=b_spec,
        scratch_shapes=[pltpu.VMEM((NSA_G * TOPN, HALF_ROWS, DH), F32), pltpu.SemaphoreType.DMA(())],
    )
    return pl.pallas_call(
        functools.partial(_slc_decode_kernel, t_pos=t_pos, n_past=n_past),
        out_shape=jax.ShapeDtypeStruct((nd, NSA_G, 8, DH), F32),
        grid_spec=grid_spec,
        compiler_params=_cparams(("arbitrary",)),
        name="slc_decode",
    )(sel_flat, page_table, q8, o_c, gate_l, gate_b, slc_cache, bias_sd, new_k, new_v,
      win_cache, new_wk, new_wv, bias_wd)


def _t5_bucket(dist):
    n = jnp.maximum(dist, 0)
    nf = jnp.maximum(n, REL_MAX_EXACT).astype(F32)
    large = REL_MAX_EXACT + (jnp.log(nf / REL_MAX_EXACT) / math.log(REL_MAX_DIST / REL_MAX_EXACT)
                             * (REL_BUCKETS - REL_MAX_EXACT)).astype(jnp.int32)
    return jnp.where(n < REL_MAX_EXACT, n, jnp.minimum(large, REL_BUCKETS - 1))


def _bias_table(rel_bias, dist):
    onehot = (_t5_bucket(dist)[..., None] == jnp.arange(REL_BUCKETS)).astype(F32)
    b = jnp.einsum("...k,kh->...h", onehot, rel_bias, precision=HI)
    b = jnp.moveaxis(b, -1, 0)
    return b.reshape((NSA_G, NSA_R) + dist.shape)


def _prompt_bias_tables(rel_bias):
    i = jnp.arange(QBLK)[:, None]
    rel = jnp.arange(LANE)[None, :]
    bct = _bias_table(rel_bias, BLK * (rel - 1) + i % BLK + 1)
    d_near = SLC_PAD + i - jnp.arange(SLC_NEAR)[None, :]
    far = rel_bias[REL_BUCKETS - 1].reshape(NSA_G, NSA_R, 1, 1)
    bsn = _bias_table(rel_bias, d_near) - far + jnp.where(d_near >= 0, 0.0, NEG)
    d_win = WIN_PAD + i - jnp.arange(WIN_KEYS)[None, :]
    bw = _bias_table(rel_bias, d_win) + jnp.where((d_win >= 0) & (d_win <= WINDOW), 0.0, NEG)
    return bct, bsn, bw


def _pad_rows(a, rows, axis):
    pad = [(0, 0)] * a.ndim
    pad[axis] = (0, rows - a.shape[axis])
    return jnp.pad(a, pad)


def _decode_bias_tables(rel_bias, t_pos, n_past, n_win):
    n = jnp.arange(SEL_LANES)
    bias_cd = _pad_rows(_bias_table(rel_bias, t_pos - (n * BLK + BLK - 1)), 8, 1)
    tok = jnp.arange(n_past + 1)[:, None] * BLK + jnp.arange(BLK)[None, :]
    bias_sd = _pad_rows(jnp.swapaxes(_bias_table(rel_bias, t_pos - tok), 1, 2), 8, 2)
    c = jnp.arange(n_win + LANE)
    bias_wd = _pad_rows(_bias_table(rel_bias, n_win - c), 8, 1)
    return bias_cd, bias_sd, bias_wd


def kernel(x_prompt, x_sample, mem_prompt, cache_cmp_kv, cache_slc_kv, state_win_kv, state_gla, cache_mem_kv,
           page_table, norm_g, w_ffn_gate, w_ffn_up, w_ffn_down, w_in_gla, w_in_nsa, w_out, mem_norm_g, w_mem_kv,
           w_gla_a2, b_gla_a, gla_onorm_g, nsa_gate_b, cmp_pe, cmp_w1, cmp_b1, cmp_w2, rel_bias):
    nb, seq, _ = x_prompt.shape
    nd = x_sample.shape[0]
    depth = norm_g.shape[0]
    n_pages = page_table.shape[1]
    past_len = n_pages * PAGE
    n_past = past_len // BLK
    n_win = state_win_kv.shape[2]
    sr = SAMPLE_ROWS

    xp = x_prompt.reshape(nb * seq, D_MODEL)
    xs = _pad_rows(x_sample.reshape(nd, D_MODEL), sr, 0)
    mem_x = mem_prompt.reshape(nb * MEM_LEN, D_MODEL)

    w_o = w_out.astype(BF16)
    w_mkv = w_mem_kv.astype(BF16)
    def zero_rows(like, rows):
        return jnp.zeros((like.shape[0], rows, D_MODEL), BF16)

    wt_gla = jnp.swapaxes(w_in_gla, 1, 2).astype(BF16)
    tail_gla = jnp.concatenate(
        [wt_gla[:, GLA_MAIN + GLA_RANK:], wt_gla[:, GLA_MAIN:GLA_MAIN + GLA_RANK],
         zero_rows(wt_gla, TN_PROJ - MEM_W - GLA_RANK)], axis=1)
    n_gate = 3 * NSA_HEADS
    wt_nsa = jnp.swapaxes(w_in_nsa, 1, 2).astype(BF16)
    tail_nsa = jnp.concatenate(
        [wt_nsa[:, NSA_GATE_COL:NSA_GATE_COL + n_gate], zero_rows(wt_nsa, TN_PROJ - MEM_W - n_gate),
         wt_nsa[:, NSA_GATE_COL + n_gate:]], axis=1)
    wa_pad = _pad_rows(w_gla_a2, LANE, 1)
    w1 = cmp_w1.astype(BF16)
    w2 = cmp_w2.astype(BF16)

    def ffn_both(x_p, x_s, i, j):
        g1, g2 = norm_g[i, 4 * j][None], norm_g[i, 4 * j + 1][None]
        x_s, wg, wu, wd = _ffn_half(x_s, g1, w_ffn_gate, w_ffn_up, w_ffn_down, g2, TM_SAMPLE, f32_weights_at=(i, j))
        return _ffn_half(x_p, g1, wg, wu, wd, g2, TM_FFN), x_s

    def every(a, step):
        return a.reshape(nd, step, a.shape[-1])[:, 0]

    def per_seq(a):
        return _pad_rows(a[:, None, :], sr, 1).reshape(nd * sr, a.shape[-1])

    outs = dict(gla_p=[], gla_s=[], cmp_p=[], cmp_s=[], slc_p=[], slc_s=[], win_p=[], win_s=[], mem_p=[])
    for i in range(depth):
        li = i // 2
        mem_kv_p = _norm_matmul(mem_x, mem_norm_g[i][None], w_mkv, i, TM_MEMKV, MEM_W)
        outs["mem_p"].append(mem_kv_p.reshape(nb, MEM_LEN, 2, N_MEM_HEADS, MEM_HEAD_DIM))
        mem_kv_p = mem_kv_p.reshape(nb, MEM_LEN, 2 * MEM_W)
        mem_kv_s = cache_mem_kv[i].reshape(nd, MEM_LEN, 2 * MEM_W)
        xp, xs = ffn_both(xp, xs, i, 0)
        g_mix = norm_g[i, 2][None]
        if i % 2 == 0:
            proj_p = _in_proj(xp, g_mix, wt_gla, GLA_MAIN, tail_gla, li, TM_PROJ)
            proj_s = _in_proj(xs, g_mix, wt_gla, GLA_MAIN, tail_gla, li, TM_SAMPLE)
            mem_col = GLA_MAIN // MEM_W
            b_a = b_gla_a[li][None]
            gn = gla_onorm_g[li][None]
            s0 = jnp.zeros((nb, GLA_HEADS, GLA_DK, GLA_DV), F32)
            tok_p, sp = _gla(proj_p, s0, wa_pad[li], b_a, gn, seq, TL_GLA, GLA_CHUNK, seq)
            tok_s, ss = _gla(per_seq(proj_s[:nd]), state_gla[li], wa_pad[li], b_a, gn, sr, sr, sr, 1)
            tok_s = _pad_rows(every(tok_s, sr), sr, 0)
            outs["gla_p"].append(sp)
            outs["gla_s"].append(ss)
        else:
            assert 2 * KV_W == TN_PROJ and NSA_KV_COL % TN_PROJ == 0
            kv_tiles = (NSA_KV_COL // TN_PROJ, NSA_GATE_COL // TN_PROJ)
            proj_p, *kv_lin = _in_proj(xp, g_mix, wt_nsa, NSA_GATE_COL, tail_nsa, li, TM_PROJ_KV, kv_tiles,
                                       linear_kv=True)
            proj_s = _in_proj(xs, g_mix, wt_nsa, NSA_GATE_COL, tail_nsa, li, TM_SAMPLE, kv_tiles)
            mem_col = NSA_MEM_COL // MEM_W
            b1 = cmp_b1[li][:, None, :]
            gate_b = nsa_gate_b[li]
            bct, bsn, bw = _prompt_bias_tables(rel_bias)
            proj3 = proj_p.reshape(nb, seq, NSA_N)
            cmp_kv = _cmp_prompt(proj3, cmp_pe[li], w1[li], b1, w2[li])
            tok_p = _nsa_prompt(proj_p, cmp_kv, _pad_rows(gate_b[None], LANE, 1), bct, bsn, bw, nb, seq)
            def kv_rows(branch):
                return jnp.swapaxes(kv_lin[branch].reshape(nb, seq, NSA_G, 2, DH), 2, 3)

            outs["cmp_p"].append(kv_rows(0).reshape(nb, seq // PAGE, PAGE, 2, NSA_G, DH))
            outs["slc_p"].append(kv_rows(1).reshape(nb, seq // PAGE, PAGE, 2, NSA_G, DH))
            outs["win_p"].append(kv_rows(2)[:, seq - n_win:])
            t_pos = past_len
            bias_cd, bias_sd, bias_wd = _decode_bias_tables(rel_bias, t_pos, n_past, n_win)
            kv_s = jnp.swapaxes(proj_s[:nd, NSA_KV_COL:NSA_GATE_COL].reshape(nd, 3, NSA_G, 2, DH), 2, 3)
            outs["cmp_s"].append(kv_s[:, None, 0])
            outs["slc_s"].append(kv_s[:, None, 1])
            outs["win_s"].append(jnp.concatenate([state_win_kv[li][:, 1:], kv_s[:, None, 2]], axis=1))
            q8 = _pad_rows(proj_s[:nd, :TOK_W].reshape(nd, NSA_G, NSA_R, DH), 8, 2)
            new_cmp = _pad_rows(jnp.moveaxis(kv_s[:, 0], 1, 0).reshape(2, nd * NSA_G, DH), 32, 1)
            cmp_kv_s, cmp_last = _cmp_decode(page_table, _linear_cache(cache_cmp_kv[li]), new_cmp,
                                             cmp_pe[li], w1[li], b1, w2[li])
            o_c, sel = _sel_decode(q8, cmp_kv_s, cmp_last, bias_cd, t_pos)
            sel_flat = sel[:, :, 0, :TOPN].reshape(-1)

            def row8(a):
                return _pad_rows(a[:, :, None, :], 8, 2)

            gate_l = _pad_rows(_pad_rows(proj_s[:nd, NSA_GATE_COL:NSA_GATE_COL + n_gate].reshape(nd, NSA_G, NSA_R, 3),
                                         8, 2), LANE, 3)
            gate_b8 = _pad_rows(_pad_rows(gate_b.reshape(NSA_G, NSA_R, 3), 8, 1), LANE, 2)
            tok_s = _slc_decode(sel_flat, page_table, q8, o_c, gate_l, gate_b8,
                                _linear_cache(cache_slc_kv[li]), bias_sd,
                                row8(kv_s[:, 1, 0]), row8(kv_s[:, 1, 1]),
                                _linear_cache(state_win_kv[li]),
                                row8(kv_s[:, 2, 0]), row8(kv_s[:, 2, 1]), bias_wd, t_pos)
            tok_s = _pad_rows(tok_s[:, :, :NSA_R].reshape(nd, TOK_W), sr, 0).astype(BF16)
        q_s = jnp.broadcast_to(proj_s[:nd, None, mem_col * MEM_W:(mem_col + 1) * MEM_W],
                               (nd, sr, MEM_W)).reshape(nd * sr, MEM_W)
        mem_o_s = _pad_rows(every(_mem_attn(q_s, 0, mem_kv_s, sr, sr), sr), sr, 0)
        xp = _out_proj_mem(xp, tok_p, proj_p, mem_col, mem_kv_p, seq, w_o, i, norm_g[i, 3][None], TM_OUT)
        xs = _out_proj(xs, tok_s, mem_o_s, w_o, i, norm_g[i, 3][None], sr)
        xp, xs = ffn_both(xp, xs, i, 1)

    y_prompt = xp.reshape(nb, seq, D_MODEL)
    y_sample = xs[:nd].reshape(nd, 1, D_MODEL)
    st = lambda k: jnp.stack(outs[k])
    return (y_prompt, y_sample, st("gla_p"), st("cmp_p"), st("slc_p"), st("win_p"), st("mem_p"),
            st("gla_s"), st("cmp_s"), st("slc_s"), st("win_s"))
```

```python
import functools
import math

import jax
import jax.numpy as jnp
from jax import lax
from jax.experimental import pallas as pl
from jax.experimental.pallas import tpu as pltpu

F32 = jnp.float32
BF16 = jnp.bfloat16
HI = lax.Precision.HIGHEST

D_MODEL = 2048
D_FF = 5632
EPS = 1e-6
MEM_LEN = 256
N_MEM_HEADS = 4
MEM_HEAD_DIM = 128
MEM_W = N_MEM_HEADS * MEM_HEAD_DIM
TOK_W = D_MODEL - MEM_W
GLA_HEADS = 4
GLA_DV = TOK_W // GLA_HEADS
GLA_DK = GLA_DV // 2
GLA_RANK = 16
GLA_TAU = 16.0
GLA_CHUNK = 64
GLA_PAIR_W = 2 * GLA_DK
DH = 128
NSA_HEADS = TOK_W // DH
NSA_G = 3
NSA_R = NSA_HEADS // NSA_G
BLK = 64
TOPN = 16
WINDOW = 512
CMP_HID = 256
QBLK = 256
KV_W = NSA_G * DH
ROW_W = 2 * NSA_G
REL_BUCKETS = 32
REL_MAX_EXACT = 16
REL_MAX_DIST = 128
PAGE = 128
LANE = 128
HALF_LANE = LANE // 2
BLK_SHIFT = BLK.bit_length() - 1
VMEM_LIMIT = 56 * 1024 * 1024

TN_PROJ = 768
GLA_MAIN = 2 * GLA_HEADS * GLA_DK + 2 * TOK_W
GLA_A_COL = GLA_MAIN + MEM_W
GLA_N = GLA_MAIN + TN_PROJ
NSA_KV_COL = TOK_W
NSA_GATE_COL = NSA_KV_COL + 6 * KV_W
NSA_MEM_COL = NSA_GATE_COL + TN_PROJ - MEM_W
NSA_N = NSA_GATE_COL + TN_PROJ
SAMPLE_ROWS = 16
TM_SAMPLE = SAMPLE_ROWS
TM_FFN = 512
TM_PROJ = 1024
TM_PROJ_KV = 512
TM_MEMKV = 512
TM_OUT = 512
TM_MEM_ATTN = 512
TL_GLA = 512
SLC_PAD = REL_MAX_DIST
WIN_PAD = WINDOW
WIN_KEYS = WIN_PAD + QBLK
SLC_NEAR = SLC_PAD + QBLK
FAR_CHUNK = 1024
NEG = -1e30
M_FLOOR = -1e29


def _cparams(sem, vmem=VMEM_LIMIT):
    return pltpu.CompilerParams(dimension_semantics=sem, vmem_limit_bytes=vmem)


def _sigmoid(x):
    return 1.0 / (1.0 + jnp.exp(-x))


def _rms(x, g):
    ms = jnp.mean(x * x, axis=-1, keepdims=True)
    return x * lax.rsqrt(ms + EPS) * g


def _nt(a, b, precision=None):
    return lax.dot_general(a, b, (((1,), (1,)), ((), ())), precision=precision,
                           preferred_element_type=F32)


def _tn(a, b, precision=None):
    return lax.dot_general(a, b, (((0,), (0,)), ((), ())), precision=precision,
                           preferred_element_type=F32)


def _dot(a, b, precision=None):
    return jnp.dot(a, b, precision=precision, preferred_element_type=F32)


def _split3(x):
    hi = x.astype(BF16)
    r1 = x - hi.astype(F32)
    mid = r1.astype(BF16)
    lo = (r1 - mid.astype(F32)).astype(BF16)
    return hi, mid, lo


def _ffn_kernel(x_ref, g1_ref, wg_ref, wu_ref, wd_ref, g2_ref, o_ref, *rest):
    bf_out, (xn_ref, acc_ref) = rest[:-2], rest[-2:]
    j = pl.program_id(1)

    @pl.when(j == 0)
    def _():
        xn_ref[...] = _rms(x_ref[...], g1_ref[...]).astype(BF16)
        acc_ref[...] = jnp.zeros_like(acc_ref)

    wg, wu, wd = (w_ref[...].astype(BF16) for w_ref in (wg_ref, wu_ref, wd_ref))
    for out_ref, w in zip(bf_out, (wg, wu, wd)):
        out_ref[...] = w
    xn = xn_ref[...]
    gate = _dot(xn, wg)
    up = _dot(xn, wu)
    h = (gate * _sigmoid(gate) * up).astype(BF16)
    acc_ref[...] += _dot(h, wd)

    @pl.when(j == pl.num_programs(1) - 1)
    def _():
        o_ref[...] = x_ref[...] + 0.5 * _rms(acc_ref[...], g2_ref[...])


def _ffn_half(x, g1, wg, wu, wd, g2, tm, tf=512, f32_weights_at=None):
    m = x.shape[0]
    y_shape = jax.ShapeDtypeStruct((m, D_MODEL), F32)
    y_spec = pl.BlockSpec((tm, D_MODEL), lambda i, j: (i, 0))
    col_spec = pl.BlockSpec((D_MODEL, tf), lambda i, j: (0, j))
    row_spec = pl.BlockSpec((tf, D_MODEL), lambda i, j: (j, 0))
    if f32_weights_at is None:
        w_specs = [col_spec, col_spec, row_spec]
        out_shape, out_specs = y_shape, y_spec
    else:
        assert m == tm, "the bf16 copies are written once, by a single row tile"
        layer, half = f32_weights_at
        w_specs = [pl.BlockSpec((None, None, D_MODEL, tf), lambda i, j: (layer, half, 0, j)),
                   pl.BlockSpec((None, None, D_MODEL, tf), lambda i, j: (layer, half, 0, j)),
                   pl.BlockSpec((None, None, tf, D_MODEL), lambda i, j: (layer, half, j, 0))]
        out_shape = (y_shape, jax.ShapeDtypeStruct((D_MODEL, D_FF), BF16),
                     jax.ShapeDtypeStruct((D_MODEL, D_FF), BF16), jax.ShapeDtypeStruct((D_FF, D_MODEL), BF16))
        out_specs = (y_spec, col_spec, col_spec, row_spec)
    return pl.pallas_call(
        _ffn_kernel,
        out_shape=out_shape,
        grid=(m // tm, D_FF // tf),
        in_specs=[y_spec, pl.BlockSpec((1, D_MODEL), lambda i, j: (0, 0))] + w_specs
                 + [pl.BlockSpec((1, D_MODEL), lambda i, j: (0, 0))],
        out_specs=out_specs,
        scratch_shapes=[pltpu.VMEM((tm, D_MODEL), BF16), pltpu.VMEM((tm, D_MODEL), F32)],
        compiler_params=_cparams(("parallel", "arbitrary")),
        name="ffn_half",
    )(x, g1, wg, wu, wd, g2)


def _norm_matmul_kernel(x_ref, g_ref, w_ref, o_ref, xn_ref):
    @pl.when(pl.program_id(1) == 0)
    def _():
        xn_ref[...] = _rms(x_ref[...], g_ref[...]).astype(BF16)

    o_ref[...] = _dot(xn_ref[...], w_ref[...])


def _norm_matmul(x, g, w, layer, tm, tn):
    m, n = x.shape[0], w.shape[2]
    return pl.pallas_call(
        _norm_matmul_kernel,
        out_shape=jax.ShapeDtypeStruct((m, n), F32),
        grid=(m // tm, n // tn),
        in_specs=[
            pl.BlockSpec((tm, D_MODEL), lambda i, j: (i, 0)),
            pl.BlockSpec((1, D_MODEL), lambda i, j: (0, 0)),
            pl.BlockSpec((None, D_MODEL, tn), lambda i, j: (layer, 0, j)),
        ],
        out_specs=pl.BlockSpec((tm, tn), lambda i, j: (i, j)),
        scratch_shapes=[pltpu.VMEM((tm, D_MODEL), BF16)],
        compiler_params=_cparams(("parallel", "arbitrary")),
        name="norm_matmul",
    )(x, g, w)


def _in_proj_kernel(x_ref, g_ref, wm_ref, wt_ref, o_ref, *rest, n_main, kv_tiles):
    lin_refs, xn_ref = rest[:-1], rest[-1]
    j = pl.program_id(1)
    kv_lo, kv_hi = kv_tiles
    is_kv = (j >= kv_lo) & (j < kv_hi)
    rows = x_ref.shape[0]

    @pl.when(j == 0)
    def _():
        xn_ref[...] = _rms(x_ref[...], g_ref[...]).astype(BF16)

    @pl.when((j < n_main) & jnp.logical_not(is_kv))
    def _():
        o_ref[...] = _nt(xn_ref[...], wm_ref[...])

    for t in range(kv_lo, kv_hi):
        @pl.when(j == t)
        def _(t=t):
            res = _nt(xn_ref[...], wm_ref[...])
            for g in range(NSA_G):
                for kv in range(2):
                    src, dst = kv * NSA_G + g, g * 2 + kv
                    blk = res[:, src * DH:(src + 1) * DH]
                    o_ref[:, dst * DH:(dst + 1) * DH] = blk
                    if lin_refs:
                        lin_refs[t - kv_lo][pl.ds(dst, rows, stride=ROW_W), :] = blk

    @pl.when(j == n_main)
    def _():
        o_ref[...] = _nt(xn_ref[...], wt_ref[...])


def _in_proj(x, g, w_main, main_cols, w_tail, layer, tm, kv_tiles=(0, 0), linear_kv=False):
    m = x.shape[0]
    n_main = main_cols // TN_PROJ
    out_shape = [jax.ShapeDtypeStruct((m, main_cols + TN_PROJ), F32)]
    out_specs = [pl.BlockSpec((tm, TN_PROJ), lambda i, j: (i, j))]
    if linear_kv:
        n_kv = kv_tiles[1] - kv_tiles[0]
        out_shape += [jax.ShapeDtypeStruct((m * ROW_W, DH), F32)] * n_kv
        out_specs += [pl.BlockSpec((tm * ROW_W, DH), lambda i, j: (i, 0))] * n_kv
    out = pl.pallas_call(
        functools.partial(_in_proj_kernel, n_main=n_main, kv_tiles=kv_tiles),
        out_shape=out_shape,
        grid=(m // tm, n_main + 1),
        in_specs=[
            pl.BlockSpec((tm, D_MODEL), lambda i, j: (i, 0)),
            pl.BlockSpec((1, D_MODEL), lambda i, j: (0, 0)),
            pl.BlockSpec((None, TN_PROJ, D_MODEL), lambda i, j: (layer, jnp.minimum(j, n_main - 1), 0)),
            pl.BlockSpec((None, TN_PROJ, D_MODEL), lambda i, j: (layer, 0, 0)),
        ],
        out_specs=out_specs,
        scratch_shapes=[pltpu.VMEM((tm, D_MODEL), BF16)],
        compiler_params=_cparams(("parallel", "arbitrary")),
        name="in_proj",
    )(x, g, w_main, w_tail)
    return out if linear_kv else out[0]


def _out_proj_kernel(x_ref, tok_ref, mem_ref, wt_ref, wm_ref, g_ref, o_ref):
    y = _dot(tok_ref[...], wt_ref[...]) + _dot(mem_ref[...], wm_ref[...])
    o_ref[...] = x_ref[...] + _rms(y, g_ref[...])


def _out_proj(x, tok, mem_o, w_o, layer, g, tm):
    m = x.shape[0]
    return pl.pallas_call(
        _out_proj_kernel,
        out_shape=jax.ShapeDtypeStruct((m, D_MODEL), F32),
        grid=(m // tm,),
        in_specs=[
            pl.BlockSpec((tm, D_MODEL), lambda i: (i, 0)),
            pl.BlockSpec((tm, TOK_W), lambda i: (i, 0)),
            pl.BlockSpec((tm, MEM_W), lambda i: (i, 0)),
            pl.BlockSpec((None, TOK_W, D_MODEL), lambda i: (layer, 0, 0), pipeline_mode=pl.Buffered(1)),
            pl.BlockSpec((None, MEM_W, D_MODEL), lambda i: (layer, TOK_W // MEM_W, 0), pipeline_mode=pl.Buffered(1)),
            pl.BlockSpec((1, D_MODEL), lambda i: (0, 0)),
        ],
        out_specs=pl.BlockSpec((tm, D_MODEL), lambda i: (i, 0)),
        compiler_params=_cparams(("parallel",)),
        name="out_proj",
    )(x, tok, mem_o, w_o, w_o, g)


def _mem_attn_kernel(q_ref, kv_ref, o_ref):
    for h in range(N_MEM_HEADS):
        q = (q_ref[:, h * DH:(h + 1) * DH] * (MEM_HEAD_DIM ** -0.5)).astype(BF16)
        k = kv_ref[:, h * DH:(h + 1) * DH].astype(BF16)
        v = kv_ref[:, MEM_W + h * DH:MEM_W + (h + 1) * DH].astype(BF16)
        s = _nt(q, k)
        e = jnp.exp(s - jnp.max(s, axis=-1, keepdims=True))
        p = e / jnp.sum(e, axis=-1, keepdims=True)
        o_ref[:, h * DH:(h + 1) * DH] = _dot(p.astype(BF16), v).astype(o_ref.dtype)


def _mem_attn(q_arr, q_col_block, mem_kv, rows_per_batch, tm):
    nb = mem_kv.shape[0]
    per = rows_per_batch // tm
    return pl.pallas_call(
        _mem_attn_kernel,
        out_shape=jax.ShapeDtypeStruct((nb * rows_per_batch, MEM_W), BF16),
        grid=(nb, per),
        in_specs=[
            pl.BlockSpec((tm, MEM_W), lambda b, i: (b * per + i, q_col_block)),
            pl.BlockSpec((None, MEM_LEN, 2 * MEM_W), lambda b, i: (b, 0, 0)),
        ],
        out_specs=pl.BlockSpec((tm, MEM_W), lambda b, i: (b * per + i, 0)),
        compiler_params=_cparams(("parallel", "parallel")),
        name="mem_attn",
    )(q_arr, mem_kv)


GLA_PAIRS = GLA_HEADS // 2


def _gla_kernel(q_ref, k_ref, v_ref, r_ref, a_ref, s0_ref, wa_ref, ba_ref, gn_ref,
                tok_ref, s_out_ref, s_ref, cum_ref, *, chunk, n_valid):
    l = pl.program_id(1)
    tl = q_ref.shape[0]

    @pl.when(l == 0)
    def _():
        for p in range(GLA_PAIRS):
            s_ref[p] = s0_ref[2 * p:2 * p + 2].reshape(GLA_PAIR_W, GLA_DV).T

    lane = lax.broadcasted_iota(jnp.int32, (1, GLA_PAIR_W), 1)
    head_mask = [(lane < GLA_DK).astype(F32), (lane >= GLA_DK).astype(F32)]
    ti = lax.broadcasted_iota(jnp.int32, (chunk, chunk), 0)
    si = lax.broadcasted_iota(jnp.int32, (chunk, chunk), 1)
    causal = si <= ti
    tri = jnp.where(causal, 1.0, 0.0).astype(BF16)

    a_hi, a_lo, _ = _split3(a_ref[...])
    w_hi, w_lo, _ = _split3(wa_ref[...])
    z = _dot(a_hi, w_hi) + _dot(a_lo, w_hi) + _dot(a_hi, w_lo) + ba_ref[...]
    la_all = -(jnp.maximum(-z, 0.0) + jnp.log1p(jnp.exp(-jnp.abs(z)))) / GLA_TAU
    pos = l * tl + lax.broadcasted_iota(jnp.int32, (tl, 1), 0)
    la_all = jnp.where(pos < n_valid, la_all, 0.0)
    pieces = _split3(la_all)
    for c0 in range(0, tl, chunk):
        cum_ref[c0:c0 + chunk, :] = functools.reduce(jnp.add, [_dot(tri, pc[c0:c0 + chunk]) for pc in pieces])

    def step(ci, carry):
        r0 = pl.multiple_of(ci * chunk, chunk)
        rows = pl.ds(r0, chunk)
        for p in range(GLA_PAIRS):
            pair = slice(p * GLA_PAIR_W, (p + 1) * GLA_PAIR_W)
            b = cum_ref[rows, pair]
            bl = b[chunk - 1:chunk, :]
            q = q_ref[rows, pair] * (GLA_DK ** -0.5)
            k = k_ref[rows, pair]
            qe = q * jnp.exp(b)
            ke = (k * jnp.exp(-b)).astype(BF16)
            kd = k * jnp.exp(bl - b)
            st_old = s_ref[p]
            st_bf = st_old.astype(BF16)
            upd = None
            for h in range(2):
                head = slice((2 * p + h) * GLA_DV, (2 * p + h + 1) * GLA_DV)
                v = v_ref[rows, head].astype(BF16)
                qm = (qe * head_mask[h]).astype(BF16)
                att = jnp.where(causal, _nt(qm, ke), 0.0)
                o = _nt(qm, st_bf) + _dot(att.astype(BF16), v)
                o = _rms(o, gn_ref[...])
                r = r_ref[rows, head]
                tok_ref[rows, head] = (o * (r * _sigmoid(r))).astype(tok_ref.dtype)
                u = _tn(v, (kd * head_mask[h]).astype(BF16))
                upd = u if upd is None else upd + u
            s_ref[p] = jnp.exp(bl) * st_old + upd
        return carry

    lax.fori_loop(0, tl // chunk, step, 0)

    @pl.when(l == pl.num_programs(1) - 1)
    def _():
        for p in range(GLA_PAIRS):
            s_out_ref[2 * p:2 * p + 2] = s_ref[p].T.reshape(2, GLA_DK, GLA_DV)


def _gla(proj, s0, wa_pad, b_a, gn, seq, tl, chunk, n_valid):
    nb = s0.shape[0]
    per = seq // tl
    qk_w = GLA_HEADS * GLA_DK
    row = lambda b, l: b * per + l

    return pl.pallas_call(
        functools.partial(_gla_kernel, chunk=chunk, n_valid=n_valid),
        out_shape=(jax.ShapeDtypeStruct((nb * seq, TOK_W), BF16),
                   jax.ShapeDtypeStruct((nb, GLA_HEADS, GLA_DK, GLA_DV), F32)),
        grid=(nb, per),
        in_specs=[
            pl.BlockSpec((tl, qk_w), lambda b, l: (row(b, l), 0)),
            pl.BlockSpec((tl, qk_w), lambda b, l: (row(b, l), 1)),
            pl.BlockSpec((tl, TOK_W), lambda b, l: (row(b, l), 2 * qk_w // TOK_W)),
            pl.BlockSpec((tl, TOK_W), lambda b, l: (row(b, l), 2 * qk_w // TOK_W + 1)),
            pl.BlockSpec((tl, LANE), lambda b, l: (row(b, l), GLA_A_COL // LANE)),
            pl.BlockSpec((None, GLA_HEADS, GLA_DK, GLA_DV), lambda b, l: (b, 0, 0, 0)),
            pl.BlockSpec((LANE, qk_w), lambda b, l: (0, 0)),
            pl.BlockSpec((1, qk_w), lambda b, l: (0, 0)),
            pl.BlockSpec((1, GLA_DV), lambda b, l: (0, 0)),
        ],
        out_specs=(pl.BlockSpec((tl, TOK_W), lambda b, l: (row(b, l), 0)),
                   pl.BlockSpec((None, GLA_HEADS, GLA_DK, GLA_DV), lambda b, l: (b, 0, 0, 0))),
        scratch_shapes=[pltpu.VMEM((GLA_PAIRS, GLA_DV, GLA_PAIR_W), F32), pltpu.VMEM((tl, qk_w), F32)],
        compiler_params=_cparams(("parallel", "arbitrary")),
        name="gla",
    )(proj, proj, proj, proj, proj, s0, wa_pad, b_a, gn)


def _masked_softmax(s, valid):
    s = jnp.where(valid, s, -jnp.inf)
    m = jnp.max(s, axis=-1, keepdims=True)
    m = jnp.where(m > -jnp.inf, m, 0.0)
    e = jnp.exp(s - m)
    return e / jnp.maximum(jnp.sum(e, axis=-1, keepdims=True), 1e-30)


def _online_update(state, s, v):
    m, l, acc = state
    r, nq, w = s.shape
    m_new = jnp.maximum(m, jnp.max(s, axis=-1, keepdims=True))
    alpha = jnp.exp(m - m_new)
    p = jnp.exp(s - m_new)
    l = alpha * l + jnp.sum(p, axis=-1, keepdims=True)
    pv = _dot(p.reshape(r * nq, w).astype(BF16), v).reshape(r, nq, DH)
    return m_new, l, alpha * acc + pv


def _compress_tail(xflat, w1, b1, w2):
    h = _dot(xflat, w1) + b1
    h = h * _sigmoid(h)
    return _dot(h.astype(BF16), w2)


def _cmp_prompt_kernel(x0_ref, x1_ref, x2_ref, pe_ref, w1_ref, b1_ref, w2_ref, o_ref, xflat_ref):
    x_refs = (x0_ref, x1_ref, x2_ref)
    nb, seq = x0_ref.shape[0], x0_ref.shape[1]
    nblk = seq // BLK
    for j in range(BLK):
        pe_j = pe_ref[j:j + 1, :]
        for b in range(nb):
            for g in range(NSA_G):
                xj = x_refs[g][b, pl.ds(j, nblk, stride=BLK), :]
                row = (b * NSA_G + g) * nblk
                xflat_ref[row:row + nblk, j * DH:(j + 1) * DH] = (xj + pe_j).astype(BF16)
    out = _compress_tail(xflat_ref[...], w1_ref[...], b1_ref[...], w2_ref[...])
    o_ref[...] = out.reshape(nb, NSA_G, nblk, DH)


def _kv_block(branch, g, kv):
    return NSA_KV_COL // DH + branch * 2 * NSA_G + g * 2 + kv


def _cmp_prompt(proj3, pe, w1, b1, w2):
    nb, seq, _ = proj3.shape
    nblk = seq // BLK
    return pl.pallas_call(
        _cmp_prompt_kernel,
        out_shape=jax.ShapeDtypeStruct((2, nb, NSA_G, nblk, DH), F32),
        grid=(2,),
        in_specs=[
            pl.BlockSpec((nb, seq, DH), lambda kv: (0, 0, _kv_block(0, 0, kv))),
            pl.BlockSpec((nb, seq, DH), lambda kv: (0, 0, _kv_block(0, 1, kv))),
            pl.BlockSpec((nb, seq, DH), lambda kv: (0, 0, _kv_block(0, 2, kv))),
            pl.BlockSpec((None, BLK, DH), lambda kv: (kv, 0, 0)),
            pl.BlockSpec((None, BLK * DH, CMP_HID), lambda kv: (kv, 0, 0)),
            pl.BlockSpec((None, 1, CMP_HID), lambda kv: (kv, 0, 0)),
            pl.BlockSpec((None, CMP_HID, DH), lambda kv: (kv, 0, 0)),
        ],
        out_specs=pl.BlockSpec((None, nb, NSA_G, nblk, DH), lambda kv: (kv, 0, 0, 0, 0)),
        scratch_shapes=[pltpu.VMEM((nb * NSA_G * nblk, BLK * DH), BF16)],
        compiler_params=_cparams(("arbitrary",)),
        name="cmp_prompt",
    )(proj3, proj3, proj3, pe, w1, b1, w2)


def _nsa_prompt_kernel(q_ref, gl_ref, gb_ref, kc_ref, vc_ref, ks_ref, vs_ref, kw_ref, vw_ref,
                       bct_ref, bsn_ref, bw_ref, o_ref, ksb, vsb, kwb, vwb):
    g = pl.program_id(1)
    qi = pl.program_id(2)
    seq = ks_ref.shape[0]
    nblk = seq // BLK
    R = NSA_R

    @pl.when(qi == 0)
    def _():
        vsb[0:SLC_PAD, :] = jnp.zeros((SLC_PAD, DH), BF16)
        vsb[SLC_PAD:SLC_PAD + seq, :] = vs_ref[...].astype(BF16)
        ones_col = jnp.where(lax.broadcasted_iota(jnp.int32, (1, DH), 1) == 0, 1.0, 0.0)
        vwb[0:WIN_PAD, 0:DH] = jnp.zeros((WIN_PAD, DH), BF16)
        vwb[WIN_PAD:WIN_PAD + seq, 0:DH] = vw_ref[...].astype(BF16)
        vwb[:, DH:2 * DH] = jnp.broadcast_to(ones_col, (WIN_PAD + seq, DH)).astype(BF16)
        for src, dst, pad, per_block in ((ks_ref, ksb, SLC_PAD, True), (kw_ref, kwb, WIN_PAD, False)):
            dst[0:pad, 0:DH] = jnp.zeros((pad, DH), BF16)
            dst[pad:pad + seq, 0:DH] = src[...].astype(BF16)
            pos = lax.broadcasted_iota(jnp.int32, (pad + seq, 1), 0) - pad
            feat = lax.broadcasted_iota(jnp.int32, (1, DH), 1)
            masked = (feat == HALF_LANE) & (pos < 0)
            if per_block:
                masked = masked | ((pos >= 0) & (lax.shift_right_arithmetic(pos, BLK_SHIFT) == feat))
            dst[:, DH:2 * DH] = jnp.where(masked, NEG, 0.0).astype(BF16)

    q = q_ref[...] * (DH ** -0.5)
    q_all = jnp.concatenate([q[:, r * DH:(r + 1) * DH] for r in range(R)], axis=0)
    q_bf = q_all.astype(BF16)
    i_col = lax.broadcasted_iota(jnp.int32, (QBLK, 1), 0)
    t_col = qi * QBLK + i_col
    n_row = lax.broadcasted_iota(jnp.int32, (1, nblk), 1)
    row0 = pl.multiple_of(qi * QBLK, QBLK)
    lane2 = lax.broadcasted_iota(jnp.int32, (1, LANE), 1)

    s_c = _nt(q_all, kc_ref[...], HI).reshape(R, QBLK, nblk)
    tb_col = lax.shift_right_arithmetic(t_col, BLK_SHIFT)
    rel = tb_col - n_row
    bias_c = []
    for r in range(R):
        tab = bct_ref[r]
        bias_c.append(jnp.where(rel == 0, tab[:, 0:1],
                      jnp.where(rel == 1, tab[:, 1:2],
                      jnp.where(rel == 2, tab[:, 2:3], tab[:, 3:4]))))
    s_c = s_c + jnp.stack(bias_c, axis=0)
    valid_c = (t_col - (n_row * BLK + (BLK - 1))) >= 0
    p_c = _masked_softmax(s_c, valid_c[None])
    o_c = _dot(p_c.reshape(R * QBLK, nblk).astype(BF16), vc_ref[...].astype(BF16))

    imp = jnp.sum(p_c, axis=0)
    forced = (n_row == 0) | (n_row == tb_col) | (n_row == tb_col - 1)
    future = n_row * BLK > t_col
    score = jnp.where(forced, jnp.inf, jnp.where(future, -jnp.inf, imp))
    score_t = jnp.concatenate([score, jnp.full((QBLK, LANE - nblk), -jnp.inf, F32)], axis=1).T[:HALF_LANE]
    n_sub = lax.broadcasted_iota(jnp.int32, (HALF_LANE, 1), 0)
    rank_t = jnp.zeros((HALF_LANE, QBLK), F32)
    for i in range(nblk):
        cand = score_t[i:i + 1, :]
        wins_tie = jnp.where(n_sub > i, 1.0, 0.0)
        rank_t = rank_t + jnp.where(cand > score_t, 1.0, jnp.where(cand == score_t, wins_tie, 0.0))
    rank = jnp.concatenate([rank_t, jnp.zeros((LANE - HALF_LANE, QBLK), F32)], axis=0).T
    left = lane2 < HALF_LANE
    n_far = (qi * QBLK - SLC_PAD) // BLK
    dropped = rank >= float(min(TOPN, nblk))
    flag_all = jnp.where(left, jnp.where(dropped, 1.0, 0.0), jnp.where(lane2 == HALF_LANE, 1.0, 0.0))
    flag_far = jnp.where(left & (lane2 >= n_far), 1.0, flag_all)
    q_far = jnp.concatenate([q_bf, jnp.concatenate([flag_far.astype(BF16)] * R, axis=0)], axis=1)
    q_near = jnp.concatenate([q_bf, jnp.concatenate([flag_all.astype(BF16)] * R, axis=0)], axis=1)

    def far_body(kc_i, state):
        start = pl.multiple_of(SLC_PAD + kc_i * FAR_CHUNK, BLK)
        s = _nt(q_far, ksb[pl.ds(start, FAR_CHUNK), :]).reshape(R, QBLK, FAR_CHUNK)
        return _online_update(state, s, vsb[pl.ds(start, FAR_CHUNK), :])

    blk_per_chunk = FAR_CHUNK // BLK
    n_chunks = (jnp.maximum(n_far, 0) + (blk_per_chunk - 1)) // blk_per_chunk
    state = (jnp.full((R, QBLK, 1), M_FLOOR, F32), jnp.zeros((R, QBLK, 1), F32), jnp.zeros((R, QBLK, DH), F32))
    state = lax.fori_loop(0, n_chunks, far_body, state)

    s = _nt(q_near, ksb[pl.ds(row0, SLC_NEAR), :]).reshape(R, QBLK, SLC_NEAR) + bsn_ref[...]
    _, l_s, acc_s = _online_update(state, s, vsb[pl.ds(row0, SLC_NEAR), :])
    o_s = (acc_s / jnp.maximum(l_s, 1e-30)).reshape(R * QBLK, DH)

    s = _nt(q_near, kwb[pl.ds(row0, WIN_KEYS), :]).reshape(R, QBLK, WIN_KEYS) + bw_ref[...]
    e_w = jnp.exp(s - jnp.max(s, axis=-1, keepdims=True))
    ow = _dot(e_w.reshape(R * QBLK, WIN_KEYS).astype(BF16), vwb[pl.ds(row0, WIN_KEYS), :])
    o_w = ow[:, :DH] / jnp.maximum(ow[:, DH:DH + 1], 1e-30)

    gates = _sigmoid(gl_ref[...] + gb_ref[...])
    src = lax.broadcasted_iota(jnp.int32, (LANE, LANE), 0)
    dst = lax.broadcasted_iota(jnp.int32, (LANE, LANE), 1)
    pick = jnp.where((src == g * (3 * R) + dst) & (dst < 3 * R), 1.0, 0.0)
    gsel = _dot(gates, pick, HI)
    for r in range(R):
        rows = slice(r * QBLK, (r + 1) * QBLK)
        o = (gsel[:, 3 * r:3 * r + 1] * o_c[rows] + gsel[:, 3 * r + 1:3 * r + 2] * o_s[rows]
             + gsel[:, 3 * r + 2:3 * r + 3] * o_w[rows])
        o_ref[:, r * DH:(r + 1) * DH] = o.astype(o_ref.dtype)


def _nsa_prompt(proj2, cmp_kv, gate_b_pad, bct, bsn, bw, nb, seq):
    nq = seq // QBLK
    nblk = seq // BLK
    assert nblk <= HALF_LANE, "mask features of the selected branch hold at most 64 key blocks"
    assert seq % FAR_CHUNK == 0
    proj3 = proj2.reshape(nb, seq, NSA_N)

    def kv_spec(branch, kv):
        return pl.BlockSpec((None, seq, DH), lambda b, g, qi: (b, 0, _kv_block(branch, g, kv)))

    return pl.pallas_call(
        _nsa_prompt_kernel,
        out_shape=jax.ShapeDtypeStruct((nb * seq, TOK_W), BF16),
        grid=(nb, NSA_G, nq),
        in_specs=[
            pl.BlockSpec((QBLK, NSA_R * DH), lambda b, g, qi: (b * nq + qi, g)),
            pl.BlockSpec((QBLK, LANE), lambda b, g, qi: (b * nq + qi, NSA_GATE_COL // LANE)),
            pl.BlockSpec((1, LANE), lambda b, g, qi: (0, 0)),
            pl.BlockSpec((None, None, None, nblk, DH), lambda b, g, qi: (0, b, g, 0, 0)),
            pl.BlockSpec((None, None, None, nblk, DH), lambda b, g, qi: (1, b, g, 0, 0)),
            kv_spec(1, 0), kv_spec(1, 1), kv_spec(2, 0), kv_spec(2, 1),
            pl.BlockSpec((None, NSA_R, QBLK, LANE), lambda b, g, qi: (g, 0, 0, 0)),
            pl.BlockSpec((None, NSA_R, QBLK, SLC_NEAR), lambda b, g, qi: (g, 0, 0, 0)),
            pl.BlockSpec((None, NSA_R, QBLK, WIN_KEYS), lambda b, g, qi: (g, 0, 0, 0)),
        ],
        out_specs=pl.BlockSpec((QBLK, NSA_R * DH), lambda b, g, qi: (b * nq + qi, g)),
        scratch_shapes=[pltpu.VMEM((SLC_PAD + seq, 2 * DH), BF16), pltpu.VMEM((SLC_PAD + seq, DH), BF16),
                        pltpu.VMEM((WIN_PAD + seq, 2 * DH), BF16), pltpu.VMEM((WIN_PAD + seq, 2 * DH), BF16)],
        compiler_params=_cparams(("parallel", "parallel", "arbitrary")),
        name="nsa_prompt",
    )(proj2, proj2, gate_b_pad, cmp_kv, cmp_kv, proj3, proj3, proj3, proj3, bct, bsn, bw)


SUB_PAGES = 8
RING = 4
PAGE_ROWS = PAGE * ROW_W


def _linear_cache(cache):
    return jnp.transpose(cache, (0, 1, 3, 2, 4)).reshape(-1, DH)


def _cmp_decode_kernel(pt_ref, cache_ref, new_ref, pe_ref, w1_ref, b1_ref, w2_ref, o_ref, last_ref,
                       buf, sem, xflat_ref, xlast_ref, *, n_sub):
    b = pl.program_id(0)
    total = pl.num_programs(0) * n_sub
    sub_blk = SUB_PAGES * PAGE // BLK
    seq_blk = n_sub * sub_blk

    def page_copy(s, p):
        page = pt_ref[s // n_sub, (s % n_sub) * SUB_PAGES + p]
        return pltpu.make_async_copy(cache_ref.at[pl.ds(pl.multiple_of(page * PAGE_ROWS, PAGE_ROWS), PAGE_ROWS), :],
                                     buf.at[s % RING, pl.ds(p * PAGE_ROWS, PAGE_ROWS), :], sem.at[s % RING])

    def start_sub(s):
        for p in range(SUB_PAGES):
            page_copy(s, p).start()

    @pl.when(b == 0)
    def _():
        for s in range(RING):
            start_sub(s)

    def body(i, c):
        s = b * n_sub + i
        for p in range(SUB_PAGES):
            page_copy(s, p).wait()
        slot = s % RING
        row0 = pl.multiple_of(i * sub_blk, sub_blk)
        by_row = jnp.swapaxes(buf[slot].reshape(sub_blk, BLK * ROW_W, DH), 0, 1)
        for kv in range(2):
            for j in range(BLK):
                pe_j = pe_ref[kv, j:j + 1, :]
                for g in range(NSA_G):
                    xj = by_row[j * ROW_W + g * 2 + kv]
                    xflat_ref[kv, pl.ds(g * seq_blk + row0, sub_blk), j * DH:(j + 1) * DH] = (xj + pe_j).astype(BF16)

        @pl.when(s + RING < total)
        def _():
            start_sub(s + RING)

        return c

    lax.fori_loop(0, n_sub, body, 0)
    for kv in range(2):
        out = _compress_tail(xflat_ref[kv], w1_ref[kv], b1_ref[kv], w2_ref[kv])
        o_ref[kv] = out.reshape(NSA_G, seq_blk, DH)

    @pl.when(b == 0)
    def _():
        rows = new_ref.shape[1]
        for kv in range(2):
            for j in range(BLK):
                pe_j = jnp.broadcast_to(pe_ref[kv, j:j + 1, :], (rows, DH))
                xj = new_ref[kv] + pe_j if j == 0 else pe_j
                xlast_ref[:, j * DH:(j + 1) * DH] = xj.astype(BF16)
            last_ref[kv] = _compress_tail(xlast_ref[...], w1_ref[kv], b1_ref[kv], w2_ref[kv])


def _cmp_decode(page_table, cache, new_rows, pe, w1, b1, w2):
    nd, n_pages = page_table.shape
    n_sub = n_pages // SUB_PAGES
    seq_blk = n_pages * PAGE // BLK
    rows = new_rows.shape[1]

    def whole(shape):
        return pl.BlockSpec(shape, lambda b, pt: (0,) * len(shape), pipeline_mode=pl.Buffered(1))

    grid_spec = pltpu.PrefetchScalarGridSpec(
        num_scalar_prefetch=1,
        grid=(nd,),
        in_specs=[
            pl.BlockSpec(memory_space=pl.ANY),
            whole(new_rows.shape), whole(pe.shape), whole(w1.shape), whole(b1.shape), whole(w2.shape),
        ],
        out_specs=(pl.BlockSpec((2, None, NSA_G, seq_blk, DH), lambda b, pt: (0, b, 0, 0, 0)),
                   pl.BlockSpec((2, rows, DH), lambda b, pt: (0, 0, 0))),
        scratch_shapes=[pltpu.VMEM((RING, SUB_PAGES * PAGE_ROWS, DH), F32),
                        pltpu.SemaphoreType.DMA((RING,)),
                        pltpu.VMEM((2, NSA_G * seq_blk, BLK * DH), BF16),
                        pltpu.VMEM((rows, BLK * DH), BF16)],
    )
    return pl.pallas_call(
        functools.partial(_cmp_decode_kernel, n_sub=n_sub),
        out_shape=(jax.ShapeDtypeStruct((2, nd, NSA_G, seq_blk, DH), F32),
                   jax.ShapeDtypeStruct((2, rows, DH), F32)),
        grid_spec=grid_spec,
        compiler_params=_cparams(("arbitrary",)),
        name="cmp_decode",
    )(page_table, cache, new_rows, pe, w1, b1, w2)


SEL_LANES = 384


def _sel_decode_kernel(q_ref, kc_ref, vc_ref, last_ref, bias_ref, oc_ref, sel_ref, *, t_pos):
    b = pl.program_id(0)
    n_past = kc_ref.shape[1]
    n_blocks = n_past + 1
    n_lane = lax.broadcasted_iota(jnp.int32, (1, SEL_LANES), 1)
    n_lane_f = n_lane.astype(F32)
    head_row = lax.broadcasted_iota(jnp.int32, (8, 1), 0) < NSA_R
    tb = t_pos // BLK
    for g in range(NSA_G):
        q = q_ref[g] * (DH ** -0.5)
        bias = bias_ref[g]
        s_p = _nt(q, kc_ref[g], HI) + bias[:, :n_past]
        row = b * NSA_G + g
        k_last = last_ref[0, pl.ds(row, 1), :]
        v_last = last_ref[1, pl.ds(row, 1), :]
        s_l = jnp.sum(q * k_last, axis=-1, keepdims=True) + bias[:, n_past:n_past + 1]
        valid_p = (t_pos - (n_lane[:, :n_past] * BLK + (BLK - 1))) >= 0
        valid_l = (t_pos - (n_past * BLK + (BLK - 1))) >= 0
        s_p = jnp.where(valid_p, s_p, -jnp.inf)
        s_l = jnp.where(valid_l, s_l, -jnp.inf)
        m = jnp.maximum(jnp.max(s_p, axis=-1, keepdims=True), s_l)
        m = jnp.where(m > -jnp.inf, m, 0.0)
        e_p = jnp.exp(s_p - m)
        e_l = jnp.exp(s_l - m)
        den = jnp.maximum(jnp.sum(e_p, axis=-1, keepdims=True) + e_l, 1e-30)
        p_p = e_p / den
        p_l = e_l / den
        oc_ref[g] = _dot(p_p, vc_ref[g], HI) + p_l * v_last
        imp_p = jnp.sum(jnp.where(head_row, p_p, 0.0), axis=0, keepdims=True)
        imp_l = jnp.sum(jnp.where(head_row, p_l, 0.0), axis=0, keepdims=True)
        imp = jnp.concatenate([imp_p, jnp.broadcast_to(imp_l, (1, SEL_LANES - n_past))], axis=1)
        forced = (n_lane == 0) | (n_lane == tb) | (n_lane == tb - 1)
        future = n_lane * BLK > t_pos
        score = jnp.where(forced, jnp.inf, jnp.where(future, -jnp.inf, imp))
        cand = n_lane < n_blocks
        sel = jnp.zeros((1, LANE), jnp.int32)
        k_lane = lax.broadcasted_iota(jnp.int32, (1, LANE), 1)
        for k in range(min(TOPN, n_blocks)):
            best = jnp.max(jnp.where(cand, score, -jnp.inf), axis=-1, keepdims=True)
            idx_f = jnp.min(jnp.where(cand & (score == best), n_lane_f, float(SEL_LANES)), axis=-1, keepdims=True)
            idx = idx_f.astype(jnp.int32)
            sel = jnp.where(k_lane == k, idx, sel)
            cand = cand & (n_lane != idx)
        sel_ref[g] = jnp.broadcast_to(sel, (8, LANE))


def _sel_decode(q8, cmp_kv, cmp_last, bias_cd, t_pos):
    nd = q8.shape[0]
    n_past = cmp_kv.shape[3]
    return pl.pallas_call(
        functools.partial(_sel_decode_kernel, t_pos=t_pos),
        out_shape=(jax.ShapeDtypeStruct((nd, NSA_G, 8, DH), F32),
                   jax.ShapeDtypeStruct((nd, NSA_G, 8, LANE), jnp.int32)),
        grid=(nd,),
        in_specs=[
            pl.BlockSpec((None, NSA_G, 8, DH), lambda b: (b, 0, 0, 0)),
            pl.BlockSpec((None, None, NSA_G, n_past, DH), lambda b: (0, b, 0, 0, 0)),
            pl.BlockSpec((None, None, NSA_G, n_past, DH), lambda b: (1, b, 0, 0, 0)),
            pl.BlockSpec(cmp_last.shape, lambda b: (0, 0, 0)),
            pl.BlockSpec(bias_cd.shape, lambda b: (0, 0, 0)),
        ],
        out_specs=(pl.BlockSpec((None, NSA_G, 8, DH), lambda b: (b, 0, 0, 0)),
                   pl.BlockSpec((None, NSA_G, 8, LANE), lambda b: (b, 0, 0, 0))),
        compiler_params=_cparams(("parallel",)),
        name="sel_decode",
    )(q8, cmp_kv, cmp_kv, cmp_last, bias_cd)


HALF_ROWS = BLK * ROW_W


def _slc_decode_kernel(sel_ref, pt_ref, q_ref, oc_ref, gl_ref, gb_ref, slc_ref, bias_ref, nk_ref, nv_ref,
                       win_ref, nwk_ref, nwv_ref, bw_ref, o_ref, gbuf, sem, *, t_pos, n_past):
    b = pl.program_id(0)
    n_win = win_ref.shape[0] // ROW_W

    def block_copy(g, k):
        n = jnp.minimum(sel_ref[(b * NSA_G + g) * TOPN + k], n_past - 1)
        half = pt_ref[b, n // 2] * 2 + n % 2
        return pltpu.make_async_copy(slc_ref.at[pl.ds(pl.multiple_of(half * HALF_ROWS, HALF_ROWS), HALF_ROWS), :],
                                     gbuf.at[g * TOPN + k], sem)

    for g in range(NSA_G):
        for k in range(TOPN):
            block_copy(g, k).start()

    c = lax.broadcasted_iota(jnp.int32, (1, n_win), 1)
    dist = n_win - c
    valid = (dist >= 0) & (dist <= WINDOW) & (t_pos - dist >= 0)
    qs, o_w = [], []
    for g in range(NSA_G):
        q = q_ref[g] * (DH ** -0.5)
        q_bf = q.astype(BF16)
        qs.append((q, q_bf))
        kw = win_ref[pl.ds(2 * g, n_win, stride=ROW_W), :].astype(BF16)
        vw = win_ref[pl.ds(2 * g + 1, n_win, stride=ROW_W), :].astype(BF16)
        bw = bw_ref[g]
        s_w = jnp.where(valid, _nt(q_bf, kw) + bw[:, :n_win], -jnp.inf)
        s_n = jnp.sum(q * nwk_ref[g, 0:1, :], axis=-1, keepdims=True) + bw[:, n_win:n_win + 1]
        m_w = jnp.maximum(jnp.max(s_w, axis=-1, keepdims=True), s_n)
        e_w = jnp.exp(s_w - m_w)
        e_n = jnp.exp(s_n - m_w)
        den = jnp.maximum(jnp.sum(e_w, axis=-1, keepdims=True) + e_n, 1e-30)
        o_w.append((_dot(e_w.astype(BF16), vw) + e_n * nwv_ref[g, 0:1, :]) / den)

    for g in range(NSA_G):
        for k in range(TOPN):
            block_copy(g, k).wait()

    row0 = lax.broadcasted_iota(jnp.int32, (BLK, 1), 0) == 0
    j_row = lax.broadcasted_iota(jnp.int32, (1, BLK), 1)
    for g in range(NSA_G):
        q, q_bf = qs[g]
        scores, values = [], []
        for k in range(TOPN):
            n = sel_ref[(b * NSA_G + g) * TOPN + k]
            is_new = (n == n_past) & row0
            kb = jnp.where(is_new, nk_ref[g, 0:1, :], gbuf[g * TOPN + k, pl.ds(2 * g, BLK, stride=ROW_W), :])
            vb = jnp.where(is_new, nv_ref[g, 0:1, :], gbuf[g * TOPN + k, pl.ds(2 * g + 1, BLK, stride=ROW_W), :])
            s = _nt(q_bf, kb.astype(BF16)) + bias_ref[g, n]
            scores.append(jnp.where(t_pos - (n * BLK + j_row) >= 0, s, -jnp.inf))
            values.append(vb.astype(BF16))
        m = functools.reduce(jnp.maximum, [jnp.max(s, axis=-1, keepdims=True) for s in scores])
        m = jnp.where(m > -jnp.inf, m, 0.0)
        probs = [jnp.exp(s - m) for s in scores]
        l = functools.reduce(jnp.add, [jnp.sum(p, axis=-1, keepdims=True) for p in probs])
        acc = functools.reduce(jnp.add, [_dot(p.astype(BF16), v) for p, v in zip(probs, values)])
        o_s = acc / jnp.maximum(l, 1e-30)
        gates = _sigmoid(gl_ref[g] + gb_ref[g])
        o_ref[g] = gates[:, 0:1] * oc_ref[g] + gates[:, 1:2] * o_s + gates[:, 2:3] * o_w[g]


def _slc_decode(sel_flat, page_table, q8, o_c, gate_l, gate_b, slc_cache, bias_sd, new_k, new_v,
                win_cache, new_wk, new_wv, bias_wd, t_pos):
    nd = q8.shape[0]
    n_past = page_table.shape[1] * (PAGE // BLK)
    win_rows = win_cache.shape[0] // nd

    def per_b(b, sel, pt):
        return (b, 0, 0, 0)

    def whole(shape):
        return pl.BlockSpec(shape, lambda b, sel, pt: (0,) * len(shape), pipeline_mode=pl.Buffered(1))

    b_spec = pl.BlockSpec((None, NSA_G, 8, DH), per_b)
    grid_spec = pltpu.PrefetchScalarGridSpec(
        num_scalar_prefetch=2,
        grid=(nd,),
        in_specs=[
            b_spec, b_spec, b_spec, whole(gate_b.shape),
            pl.BlockSpec(memory_space=pl.ANY),
            whole(bias_sd.shape),
            b_spec, b_spec,
            pl.BlockSpec((win_rows, DH), lambda b, sel, pt: (b, 0)),
            b_spec, b_spec,
            whole(bias_wd.shape),
        ],
        out_specs=b_spec,
        scratch_shapes=[pltpu.VMEM((NSA_G * TOPN, HALF_ROWS, DH), F32), pltpu.SemaphoreType.DMA(())],
    )
    return pl.pallas_call(
        functools.partial(_slc_decode_kernel, t_pos=t_pos, n_past=n_past),
        out_shape=jax.ShapeDtypeStruct((nd, NSA_G, 8, DH), F32),
        grid_spec=grid_spec,
        compiler_params=_cparams(("arbitrary",)),
        name="slc_decode",
    )(sel_flat, page_table, q8, o_c, gate_l, gate_b, slc_cache, bias_sd, new_k, new_v,
      win_cache, new_wk, new_wv, bias_wd)


def _t5_bucket(dist):
    n = jnp.maximum(dist, 0)
    nf = jnp.maximum(n, REL_MAX_EXACT).astype(F32)
    large = REL_MAX_EXACT + (jnp.log(nf / REL_MAX_EXACT) / math.log(REL_MAX_DIST / REL_MAX_EXACT)
                             * (REL_BUCKETS - REL_MAX_EXACT)).astype(jnp.int32)
    return jnp.where(n < REL_MAX_EXACT, n, jnp.minimum(large, REL_BUCKETS - 1))


def _bias_table(rel_bias, dist):
    onehot = (_t5_bucket(dist)[..., None] == jnp.arange(REL_BUCKETS)).astype(F32)
    b = jnp.einsum("...k,kh->...h", onehot, rel_bias, precision=HI)
    b = jnp.moveaxis(b, -1, 0)
    return b.reshape((NSA_G, NSA_R) + dist.shape)


def _prompt_bias_tables(rel_bias):
    i = jnp.arange(QBLK)[:, None]
    rel = jnp.arange(LANE)[None, :]
    bct = _bias_table(rel_bias, BLK * (rel - 1) + i % BLK + 1)
    d_near = SLC_PAD + i - jnp.arange(SLC_NEAR)[None, :]
    far = rel_bias[REL_BUCKETS - 1].reshape(NSA_G, NSA_R, 1, 1)
    bsn = _bias_table(rel_bias, d_near) - far + jnp.where(d_near >= 0, 0.0, NEG)
    d_win = WIN_PAD + i - jnp.arange(WIN_KEYS)[None, :]
    bw = _bias_table(rel_bias, d_win) + jnp.where((d_win >= 0) & (d_win <= WINDOW), 0.0, NEG)
    return bct, bsn, bw


def _pad_rows(a, rows, axis):
    pad = [(0, 0)] * a.ndim
    pad[axis] = (0, rows - a.shape[axis])
    return jnp.pad(a, pad)


def _decode_bias_tables(rel_bias, t_pos, n_past, n_win):
    n = jnp.arange(SEL_LANES)
    bias_cd = _pad_rows(_bias_table(rel_bias, t_pos - (n * BLK + BLK - 1)), 8, 1)
    tok = jnp.arange(n_past + 1)[:, None] * BLK + jnp.arange(BLK)[None, :]
    bias_sd = _pad_rows(jnp.swapaxes(_bias_table(rel_bias, t_pos - tok), 1, 2), 8, 2)
    c = jnp.arange(n_win + LANE)
    bias_wd = _pad_rows(_bias_table(rel_bias, n_win - c), 8, 1)
    return bias_cd, bias_sd, bias_wd


def kernel(x_prompt, x_sample, mem_prompt, cache_cmp_kv, cache_slc_kv, state_win_kv, state_gla, cache_mem_kv,
           page_table, norm_g, w_ffn_gate, w_ffn_up, w_ffn_down, w_in_gla, w_in_nsa, w_out, mem_norm_g, w_mem_kv,
           w_gla_a2, b_gla_a, gla_onorm_g, nsa_gate_b, cmp_pe, cmp_w1, cmp_b1, cmp_w2, rel_bias):
    nb, seq, _ = x_prompt.shape
    nd = x_sample.shape[0]
    depth = norm_g.shape[0]
    n_pages = page_table.shape[1]
    past_len = n_pages * PAGE
    n_past = past_len // BLK
    n_win = state_win_kv.shape[2]
    sr = SAMPLE_ROWS

    xp = x_prompt.reshape(nb * seq, D_MODEL)
    xs = _pad_rows(x_sample.reshape(nd, D_MODEL), sr, 0)
    mem_x = mem_prompt.reshape(nb * MEM_LEN, D_MODEL)

    w_o = w_out.astype(BF16)
    w_mkv = w_mem_kv.astype(BF16)
    def zero_rows(like, rows):
        return jnp.zeros((like.shape[0], rows, D_MODEL), BF16)

    wt_gla = jnp.swapaxes(w_in_gla, 1, 2).astype(BF16)
    tail_gla = jnp.concatenate(
        [wt_gla[:, GLA_MAIN + GLA_RANK:], wt_gla[:, GLA_MAIN:GLA_MAIN + GLA_RANK],
         zero_rows(wt_gla, TN_PROJ - MEM_W - GLA_RANK)], axis=1)
    n_gate = 3 * NSA_HEADS
    wt_nsa = jnp.swapaxes(w_in_nsa, 1, 2).astype(BF16)
    tail_nsa = jnp.concatenate(
        [wt_nsa[:, NSA_GATE_COL:NSA_GATE_COL + n_gate], zero_rows(wt_nsa, TN_PROJ - MEM_W - n_gate),
         wt_nsa[:, NSA_GATE_COL + n_gate:]], axis=1)
    wa_pad = _pad_rows(w_gla_a2, LANE, 1)
    w1 = cmp_w1.astype(BF16)
    w2 = cmp_w2.astype(BF16)

    def ffn_both(x_p, x_s, i, j):
        g1, g2 = norm_g[i, 4 * j][None], norm_g[i, 4 * j + 1][None]
        x_s, wg, wu, wd = _ffn_half(x_s, g1, w_ffn_gate, w_ffn_up, w_ffn_down, g2, TM_SAMPLE, f32_weights_at=(i, j))
        return _ffn_half(x_p, g1, wg, wu, wd, g2, TM_FFN), x_s

    def every(a, step):
        return a.reshape(nd, step, a.shape[-1])[:, 0]

    def per_seq(a):
        return _pad_rows(a[:, None, :], sr, 1).reshape(nd * sr, a.shape[-1])

    outs = dict(gla_p=[], gla_s=[], cmp_p=[], cmp_s=[], slc_p=[], slc_s=[], win_p=[], win_s=[], mem_p=[])
    for i in range(depth):
        li = i // 2
        mem_kv_p = _norm_matmul(mem_x, mem_norm_g[i][None], w_mkv, i, TM_MEMKV, MEM_W)
        outs["mem_p"].append(mem_kv_p.reshape(nb, MEM_LEN, 2, N_MEM_HEADS, MEM_HEAD_DIM))
        mem_kv_p = mem_kv_p.reshape(nb, MEM_LEN, 2 * MEM_W)
        mem_kv_s = cache_mem_kv[i].reshape(nd, MEM_LEN, 2 * MEM_W)
        xp, xs = ffn_both(xp, xs, i, 0)
        g_mix = norm_g[i, 2][None]
        if i % 2 == 0:
            proj_p = _in_proj(xp, g_mix, wt_gla, GLA_MAIN, tail_gla, li, TM_PROJ)
            proj_s = _in_proj(xs, g_mix, wt_gla, GLA_MAIN, tail_gla, li, TM_SAMPLE)
            mem_col = GLA_MAIN // MEM_W
            b_a = b_gla_a[li][None]
            gn = gla_onorm_g[li][None]
            s0 = jnp.zeros((nb, GLA_HEADS, GLA_DK, GLA_DV), F32)
            tok_p, sp = _gla(proj_p, s0, wa_pad[li], b_a, gn, seq, TL_GLA, GLA_CHUNK, seq)
            tok_s, ss = _gla(per_seq(proj_s[:nd]), state_gla[li], wa_pad[li], b_a, gn, sr, sr, sr, 1)
            tok_s = _pad_rows(every(tok_s, sr), sr, 0)
            outs["gla_p"].append(sp)
            outs["gla_s"].append(ss)
        else:
            assert 2 * KV_W == TN_PROJ and NSA_KV_COL % TN_PROJ == 0
            kv_tiles = (NSA_KV_COL // TN_PROJ, NSA_GATE_COL // TN_PROJ)
            proj_p, *kv_lin = _in_proj(xp, g_mix, wt_nsa, NSA_GATE_COL, tail_nsa, li, TM_PROJ_KV, kv_tiles,
                                       linear_kv=True)
            proj_s = _in_proj(xs, g_mix, wt_nsa, NSA_GATE_COL, tail_nsa, li, TM_SAMPLE, kv_tiles)
            mem_col = NSA_MEM_COL // MEM_W
            b1 = cmp_b1[li][:, None, :]
            gate_b = nsa_gate_b[li]
            bct, bsn, bw = _prompt_bias_tables(rel_bias)
            proj3 = proj_p.reshape(nb, seq, NSA_N)
            cmp_kv = _cmp_prompt(proj3, cmp_pe[li], w1[li], b1, w2[li])
            tok_p = _nsa_prompt(proj_p, cmp_kv, _pad_rows(gate_b[None], LANE, 1), bct, bsn, bw, nb, seq)
            def kv_rows(branch):
                return jnp.swapaxes(kv_lin[branch].reshape(nb, seq, NSA_G, 2, DH), 2, 3)

            outs["cmp_p"].append(kv_rows(0).reshape(nb, seq // PAGE, PAGE, 2, NSA_G, DH))
            outs["slc_p"].append(kv_rows(1).reshape(nb, seq // PAGE, PAGE, 2, NSA_G, DH))
            outs["win_p"].append(kv_rows(2)[:, seq - n_win:])
            t_pos = past_len
            bias_cd, bias_sd, bias_wd = _decode_bias_tables(rel_bias, t_pos, n_past, n_win)
            kv_s = jnp.swapaxes(proj_s[:nd, NSA_KV_COL:NSA_GATE_COL].reshape(nd, 3, NSA_G, 2, DH), 2, 3)
            outs["cmp_s"].append(kv_s[:, None, 0])
            outs["slc_s"].append(kv_s[:, None, 1])
            outs["win_s"].append(jnp.concatenate([state_win_kv[li][:, 1:], kv_s[:, None, 2]], axis=1))
            q8 = _pad_rows(proj_s[:nd, :TOK_W].reshape(nd, NSA_G, NSA_R, DH), 8, 2)
            new_cmp = _pad_rows(jnp.moveaxis(kv_s[:, 0], 1, 0).reshape(2, nd * NSA_G, DH), 32, 1)
            cmp_kv_s, cmp_last = _cmp_decode(page_table, _linear_cache(cache_cmp_kv[li]), new_cmp,
                                             cmp_pe[li], w1[li], b1, w2[li])
            o_c, sel = _sel_decode(q8, cmp_kv_s, cmp_last, bias_cd, t_pos)
            sel_flat = sel[:, :, 0, :TOPN].reshape(-1)

            def row8(a):
                return _pad_rows(a[:, :, None, :], 8, 2)

            gate_l = _pad_rows(_pad_rows(proj_s[:nd, NSA_GATE_COL:NSA_GATE_COL + n_gate].reshape(nd, NSA_G, NSA_R, 3),
                                         8, 2), LANE, 3)
            gate_b8 = _pad_rows(_pad_rows(gate_b.reshape(NSA_G, NSA_R, 3), 8, 1), LANE, 2)
            tok_s = _slc_decode(sel_flat, page_table, q8, o_c, gate_l, gate_b8,
                                _linear_cache(cache_slc_kv[li]), bias_sd,
                                row8(kv_s[:, 1, 0]), row8(kv_s[:, 1, 1]),
                                _linear_cache(state_win_kv[li]),
                                row8(kv_s[:, 2, 0]), row8(kv_s[:, 2, 1]), bias_wd, t_pos)
            tok_s = _pad_rows(tok_s[:, :, :NSA_R].reshape(nd, TOK_W), sr, 0).astype(BF16)
        mem_o_p = _mem_attn(proj_p, mem_col, mem_kv_p, seq, TM_MEM_ATTN)
        q_s = jnp.broadcast_to(proj_s[:nd, None, mem_col * MEM_W:(mem_col + 1) * MEM_W],
                               (nd, sr, MEM_W)).reshape(nd * sr, MEM_W)
        mem_o_s = _pad_rows(every(_mem_attn(q_s, 0, mem_kv_s, sr, sr), sr), sr, 0)
        xp = _out_proj(xp, tok_p, mem_o_p, w_o, i, norm_g[i, 3][None], TM_OUT)
        xs = _out_proj(xs, tok_s, mem_o_s, w_o, i, norm_g[i, 3][None], sr)
        xp, xs = ffn_both(xp, xs, i, 1)

    y_prompt = xp.reshape(nb, seq, D_MODEL)
    y_sample = xs[:nd].reshape(nd, 1, D_MODEL)
    st = lambda k: jnp.stack(outs[k])
    return (y_prompt, y_sample, st("gla_p"), st("cmp_p"), st("slc_p"), st("win_p"), st("mem_p"),
            st("gla_s"), st("cmp_s"), st("slc_s"), st("win_s"))
```
